```python
import math
import jax, jax.numpy as jnp
from jax import lax
import numpy as np

D_MODEL = 1024
BATCH = 8
SEQ = 16384
DEPTH = 2

N_META = 16
MIX_WIDTH = D_MODEL
HEAD_DIM = 64
SB_WIDTH = MIX_WIDTH // 4
SB_HEADS = SB_WIDTH // HEAD_DIM
SB_BLOCK = 128
DN_WIDTH = MIX_WIDTH // 4
DN_HEADS = DN_WIDTH // HEAD_DIM
DN_CONV = 4
DN_CHUNK = 64
S5_WIDTH = MIX_WIDTH - SB_WIDTH - DN_WIDTH
S5_GROUP = 16
S5_GROUPS = S5_WIDTH // S5_GROUP
S5_STATE = 64
D_FF = ((8 * D_MODEL // 3 + 127) // 128) * 128
IN_SPLITS = (SB_WIDTH, SB_WIDTH, SB_WIDTH, 3 * DN_WIDTH, DN_WIDTH, DN_HEADS, DN_HEADS, S5_WIDTH)
IN_WIDTH = sum(IN_SPLITS)
EPS = 1e-6

kernel_name = 'hymba_sb_deltanet_s5_macaron'


def rmsnorm(x, g):
    xf = x.astype(jnp.float32)
    y = xf * lax.rsqrt(jnp.mean(xf * xf, axis=-1, keepdims=True) + EPS)
    return (y * g.astype(jnp.float32)).astype(x.dtype)


def l2norm(x):
    xf = x.astype(jnp.float32)
    return xf * lax.rsqrt(jnp.sum(xf * xf, axis=-1, keepdims=True) + EPS)


def swiglu(x, w_gate, w_up, w_down):
    return (jax.nn.silu(x @ w_gate) * (x @ w_up)) @ w_down


def front_pad(x, n):
    return jnp.pad(x, [(0, 0), (n, 0)] + [(0, 0)] * (x.ndim - 2))


def causal_depthwise_conv(x, w):
    k = w.shape[0]
    return lax.conv_general_dilated(
        x, w[:, None, :], window_strides=(1,), padding=[(k - 1, 0)],
        dimension_numbers=('NWC', 'WIO', 'NWC'), feature_group_count=x.shape[-1])


def stick_breaking_attention(q, k, v):
    bsz, length, nh, hd = q.shape
    pad = (-N_META) % SB_BLOCK
    q, k, v = (jnp.transpose(front_pad(t, pad), (0, 2, 1, 3)) for t in (q, k, v))
    lp = length + pad
    n_blocks = lp // SB_BLOCK
    key_pos = jnp.arange(lp)
    scale = hd ** -0.5

    def block(i):
        q_blk = lax.dynamic_slice_in_dim(q, i * SB_BLOCK, SB_BLOCK, axis=2)
        q_pos = i * SB_BLOCK + jnp.arange(SB_BLOCK)
        z = jnp.einsum('bhqd,bhkd->bhqk', q_blk, k, preferred_element_type=jnp.float32) * scale
        valid = (key_pos[None, :] < q_pos[:, None]) & (key_pos[None, :] >= pad)
        log_keep = jnp.where(valid, jax.nn.log_sigmoid(-z), 0.0)
        after = lax.cumsum(log_keep, axis=3, reverse=True) - log_keep
        w = jnp.where(valid, jnp.exp(jax.nn.log_sigmoid(z) + after), 0.0)
        return jnp.einsum('bhqk,bhkd->bhqd', w.astype(v.dtype), v)

    out = lax.map(block, jnp.arange(n_blocks))
    out = jnp.transpose(out, (1, 0, 3, 2, 4)).reshape(bsz, lp, nh, hd)
    return out[:, pad:]


def chunk_gated_delta_rule(q, k, v, g, beta):
    bsz, nh, length, dk = q.shape
    dv = v.shape[-1]
    c = DN_CHUNK
    n = length // c
    q = q * dk ** -0.5
    qc = q.reshape(bsz, nh, n, c, dk)
    kc = k.reshape(bsz, nh, n, c, dk)
    vc = v.reshape(bsz, nh, n, c, dv)
    bc = beta.reshape(bsz, nh, n, c, 1)
    gc = jnp.cumsum(g.reshape(bsz, nh, n, c), axis=-1)
    incl = jnp.tril(jnp.ones((c, c), dtype=bool))
    strict = jnp.tril(jnp.ones((c, c), dtype=bool), -1)
    decay = jnp.exp(jnp.where(incl, gc[..., :, None] - gc[..., None, :], -jnp.inf))
    kb = kc * bc
    lmat = jnp.where(strict, jnp.einsum('bhnid,bhnjd->bhnij', kb, kc) * decay, 0.0)
    eye = jnp.eye(c, dtype=jnp.float32)
    t_inv = lax.linalg.triangular_solve(lmat + eye, jnp.broadcast_to(eye, lmat.shape),
                                        left_side=True, lower=True, unit_diagonal=True)
    u = jnp.einsum('bhnij,bhnjd->bhnid', t_inv, vc * bc)
    w = jnp.einsum('bhnij,bhnjd->bhnid', t_inv, kb * jnp.exp(gc)[..., None])
    attn = jnp.where(incl, jnp.einsum('bhnid,bhnjd->bhnij', qc, kc) * decay, 0.0)

    def step(state, inp):
        q_i, k_i, u_i, w_i, a_i, g_i = inp
        v_new = u_i - jnp.einsum('bhcd,bhde->bhce', w_i, state)
        o_i = (jnp.einsum('bhcd,bhde->bhce', q_i * jnp.exp(g_i)[..., None], state)
               + jnp.einsum('bhij,bhje->bhie', a_i, v_new))
        g_last = g_i[..., -1:]
        state = (state * jnp.exp(g_last)[..., None]
                 + jnp.einsum('bhcd,bhce->bhde', k_i * jnp.exp(g_last - g_i)[..., None], v_new))
        return state, o_i

    chunks = tuple(jnp.moveaxis(t, 2, 0) for t in (qc, kc, u, w, attn, gc))
    state0 = jnp.zeros((bsz, nh, dk, dv), jnp.float32)
    _, out = lax.scan(step, state0, chunks)
    return jnp.moveaxis(out, 0, 2).reshape(bsz, nh, length, dv)


def gated_deltanet(qkv, z, b, a, conv_w, a_log, dt_bias, out_norm):
    bsz, length, _ = qkv.shape
    qkv = jax.nn.silu(causal_depthwise_conv(qkv, conv_w))
    q, k, v = (t.reshape(bsz, length, DN_HEADS, HEAD_DIM) for t in jnp.split(qkv, 3, axis=-1))
    q, k, v = l2norm(q), l2norm(k), v.astype(jnp.float32)
    beta = jax.nn.sigmoid(b.astype(jnp.float32))
    g = -jnp.exp(a_log.astype(jnp.float32)) * jax.nn.softplus(a.astype(jnp.float32) + dt_bias.astype(jnp.float32))
    pad = (-N_META) % DN_CHUNK
    q, k, v, g, beta = (front_pad(t, pad) for t in (q, k, v, g, beta))
    o = chunk_gated_delta_rule(jnp.moveaxis(q, 2, 1), jnp.moveaxis(k, 2, 1), jnp.moveaxis(v, 2, 1),
                               jnp.moveaxis(g, 2, 1), jnp.moveaxis(beta, 2, 1))
    o = jnp.moveaxis(o, 1, 2)[:, pad:]
    o = rmsnorm(o, out_norm) * jax.nn.silu(z.astype(jnp.float32).reshape(bsz, length, DN_HEADS, HEAD_DIM))
    return o.reshape(bsz, length, DN_WIDTH)


def s5_mixer(u, a_re, a_im, log_dt, b_re, b_im, c_re, c_im, d, w_glu, b_glu):
    bsz, length, _ = u.shape
    f32 = jnp.float32
    uf = u.astype(f32).reshape(bsz, length, S5_GROUPS, S5_GROUP)
    lam = lax.complex(a_re.astype(f32), a_im.astype(f32))
    dt = jnp.exp(log_dt.astype(f32))[:, None]
    log_abar = lam * dt
    abar = jnp.exp(log_abar)
    b_bar = ((abar - 1.0) / lam)[..., None] * lax.complex(b_re.astype(f32), b_im.astype(f32))
    bu = jnp.einsum('blgc,gpc->blgp', uf.astype(jnp.complex64), b_bar)
    steps = jnp.ones((1, length, 1, 1), f32)

    def combine(e1, e2):
        n1, x1 = e1
        n2, x2 = e2
        return n1 + n2, x1 * jnp.exp(n2 * log_abar) + x2

    _, states = lax.associative_scan(combine, (steps, bu), axis=1)
    c_cplx = lax.complex(c_re.astype(f32), c_im.astype(f32))
    y = jnp.real(jnp.einsum('blgp,gcp->blgc', states, c_cplx)) + d.astype(f32).reshape(S5_GROUPS, S5_GROUP) * uf
    y = jax.nn.gelu(y.reshape(bsz, length, S5_WIDTH))
    return y * jax.nn.sigmoid(y @ w_glu.astype(f32) + b_glu.astype(f32))


def hybrid_mixer(h, w_in, sb_out_norm, dn_conv_w, dn_a_log, dn_dt_bias, dn_out_norm,
                 s5_a_re, s5_a_im, s5_log_dt, s5_b_re, s5_b_im, s5_c_re, s5_c_im,
                 s5_d, s5_w_glu, s5_b_glu, s5_out_norm, w_out):
    bsz, length, _ = h.shape
    proj = h @ w_in
    sb_q, sb_k, sb_v, dn_qkv, dn_z, dn_b, dn_a, s5_u = jnp.split(
        proj, np.cumsum(IN_SPLITS)[:-1].tolist(), axis=-1)
    heads = lambda t: t.reshape(bsz, length, SB_HEADS, HEAD_DIM)
    o_sb = stick_breaking_attention(heads(sb_q), heads(sb_k), heads(sb_v))
    o_sb = rmsnorm(o_sb, sb_out_norm).reshape(bsz, length, SB_WIDTH)
    o_dn = gated_deltanet(dn_qkv, dn_z, dn_b, dn_a, dn_conv_w, dn_a_log, dn_dt_bias, dn_out_norm)
    o_s5 = rmsnorm(s5_mixer(s5_u, s5_a_re, s5_a_im, s5_log_dt, s5_b_re, s5_b_im, s5_c_re, s5_c_im,
                            s5_d, s5_w_glu, s5_b_glu), s5_out_norm)
    mixed = jnp.concatenate([o_sb.astype(h.dtype), o_dn.astype(h.dtype), o_s5.astype(h.dtype)], axis=-1)
    return mixed @ w_out


def _fwd_setup_inputs(seed: int = 0) -> dict:
    key = jax.random.key(seed)
    ks = iter(jax.random.split(key, 40))
    f32 = jnp.float32
    nrm = lambda shape, scale: scale * jax.random.normal(next(ks), shape, f32)
    gain = lambda shape: 1.0 + nrm(shape, 0.02)
    unif = lambda shape, lo, hi: jax.random.uniform(next(ks), shape, f32, minval=lo, maxval=hi)
    dn_dt = jnp.exp(unif((DEPTH, DN_HEADS), math.log(1e-3), math.log(1e-1)))
    return {
        'x': nrm((BATCH, SEQ, D_MODEL), 1.0),
        'meta_tokens': nrm((N_META, D_MODEL), 1.0),
        'ffn1_norm': gain((DEPTH, D_MODEL)),
        'ffn1_w_gate': nrm((DEPTH, D_MODEL, D_FF), D_MODEL ** -0.5),
        'ffn1_w_up': nrm((DEPTH, D_MODEL, D_FF), D_MODEL ** -0.5),
        'ffn1_w_down': nrm((DEPTH, D_FF, D_MODEL), D_FF ** -0.5),
        'mix_norm': gain((DEPTH, D_MODEL)),
        'w_in': nrm((DEPTH, D_MODEL, IN_WIDTH), D_MODEL ** -0.5),
        'sb_out_norm': gain((DEPTH, HEAD_DIM)),
        'dn_conv_w': nrm((DEPTH, DN_CONV, 3 * DN_WIDTH), DN_CONV ** -0.5),
        'dn_a_log': jnp.log(unif((DEPTH, DN_HEADS), 1.0, 16.0)),
        'dn_dt_bias': dn_dt + jnp.log(-jnp.expm1(-dn_dt)),
        'dn_out_norm': gain((DEPTH, HEAD_DIM)),
        's5_a_re': -0.5 + nrm((DEPTH, S5_GROUPS, S5_STATE), 0.01),
        's5_a_im': math.pi * jnp.arange(S5_STATE, dtype=f32) + nrm((DEPTH, S5_GROUPS, S5_STATE), 0.01),
        's5_log_dt': unif((DEPTH, S5_GROUPS), math.log(1e-3), math.log(1e-1)),
        's5_b_re': nrm((DEPTH, S5_GROUPS, S5_STATE, S5_GROUP), (2 * S5_GROUP) ** -0.5),
        's5_b_im': nrm((DEPTH, S5_GROUPS, S5_STATE, S5_GROUP), (2 * S5_GROUP) ** -0.5),
        's5_c_re': nrm((DEPTH, S5_GROUPS, S5_GROUP, S5_STATE), (2 * S5_STATE) ** -0.5),
        's5_c_im': nrm((DEPTH, S5_GROUPS, S5_GROUP, S5_STATE), (2 * S5_STATE) ** -0.5),
        's5_d': nrm((DEPTH, S5_WIDTH), 1.0),
        's5_w_glu': nrm((DEPTH, S5_WIDTH, S5_WIDTH), S5_WIDTH ** -0.5),
        's5_b_glu': nrm((DEPTH, S5_WIDTH), 0.02),
        's5_out_norm': gain((DEPTH, S5_WIDTH)),
        'w_out': nrm((DEPTH, MIX_WIDTH, D_MODEL), MIX_WIDTH ** -0.5),
        'ffn2_norm': gain((DEPTH, D_MODEL)),
        'ffn2_w_gate': nrm((DEPTH, D_MODEL, D_FF), D_MODEL ** -0.5),
        'ffn2_w_up': nrm((DEPTH, D_MODEL, D_FF), D_MODEL ** -0.5),
        'ffn2_w_down': nrm((DEPTH, D_FF, D_MODEL), D_FF ** -0.5),
        'final_norm': gain((D_MODEL,)),
    }


def _fwd_reference(x, meta_tokens, ffn1_norm, ffn1_w_gate, ffn1_w_up, ffn1_w_down, mix_norm, w_in,
              sb_out_norm, dn_conv_w, dn_a_log, dn_dt_bias, dn_out_norm,
              s5_a_re, s5_a_im, s5_log_dt, s5_b_re, s5_b_im, s5_c_re, s5_c_im,
              s5_d, s5_w_glu, s5_b_glu, s5_out_norm, w_out,
              ffn2_norm, ffn2_w_gate, ffn2_w_up, ffn2_w_down, final_norm):
    bsz = x.shape[0]
    meta = jnp.broadcast_to(meta_tokens[None].astype(x.dtype), (bsz, N_META, D_MODEL))
    h = jnp.concatenate([meta, x], axis=1)
    for l in range(DEPTH):
        h = h + 0.5 * swiglu(rmsnorm(h, ffn1_norm[l]), ffn1_w_gate[l], ffn1_w_up[l], ffn1_w_down[l])
        h = h + hybrid_mixer(rmsnorm(h, mix_norm[l]), w_in[l], sb_out_norm[l], dn_conv_w[l],
                             dn_a_log[l], dn_dt_bias[l], dn_out_norm[l],
                             s5_a_re[l], s5_a_im[l], s5_log_dt[l], s5_b_re[l], s5_b_im[l],
                             s5_c_re[l], s5_c_im[l], s5_d[l], s5_w_glu[l], s5_b_glu[l],
                             s5_out_norm[l], w_out[l])
        h = h + 0.5 * swiglu(rmsnorm(h, ffn2_norm[l]), ffn2_w_gate[l], ffn2_w_up[l], ffn2_w_down[l])
    return rmsnorm(h, final_norm)[:, N_META:]


import jax as _jax
import jax.numpy as _jnp

TWIN_FORMAT = 'train_step'
FWD_PARAMS = ['x', 'meta_tokens', 'ffn1_norm', 'ffn1_w_gate', 'ffn1_w_up', 'ffn1_w_down', 'mix_norm', 'w_in', 'sb_out_norm', 'dn_conv_w', 'dn_a_log', 'dn_dt_bias', 'dn_out_norm', 's5_a_re', 's5_a_im', 's5_log_dt', 's5_b_re', 's5_b_im', 's5_c_re', 's5_c_im', 's5_d', 's5_w_glu', 's5_b_glu', 's5_out_norm', 'w_out', 'ffn2_norm', 'ffn2_w_gate', 'ffn2_w_up', 'ffn2_w_down', 'final_norm']
TWIN_WEIGHTS = ['meta_tokens', 'ffn1_norm', 'ffn1_w_gate', 'ffn1_w_up', 'ffn1_w_down', 'mix_norm', 'w_in', 'sb_out_norm', 'dn_conv_w', 'dn_a_log', 'dn_dt_bias', 'dn_out_norm', 's5_a_re', 's5_a_im', 's5_log_dt', 's5_b_re', 's5_b_im', 's5_c_re', 's5_c_im', 's5_d', 's5_w_glu', 's5_b_glu', 's5_out_norm', 'w_out', 'ffn2_norm', 'ffn2_w_gate', 'ffn2_w_up', 'ffn2_w_down', 'final_norm']
TWIN_DIFF_INPUT = 'x'
TWIN_INPUTS = ['x', 'meta_tokens', 'ffn1_norm', 'ffn1_w_gate', 'ffn1_w_up', 'ffn1_w_down', 'mix_norm', 'w_in', 'sb_out_norm', 'dn_conv_w', 'dn_a_log', 'dn_dt_bias', 'dn_out_norm', 's5_a_re', 's5_a_im', 's5_log_dt', 's5_b_re', 's5_b_im', 's5_c_re', 's5_c_im', 's5_d', 's5_w_glu', 's5_b_glu', 's5_out_norm', 'w_out', 'ffn2_norm', 'ffn2_w_gate', 'ffn2_w_up', 'ffn2_w_down', 'final_norm', 'loss_target', 'm_meta_tokens', 'm_ffn1_norm', 'm_ffn1_w_gate', 'm_ffn1_w_up', 'm_ffn1_w_down', 'm_mix_norm', 'm_w_in', 'm_sb_out_norm', 'm_dn_conv_w', 'm_dn_a_log', 'm_dn_dt_bias', 'm_dn_out_norm', 'm_s5_a_re', 'm_s5_a_im', 'm_s5_log_dt', 'm_s5_b_re', 'm_s5_b_im', 'm_s5_c_re', 'm_s5_c_im', 'm_s5_d', 'm_s5_w_glu', 'm_s5_b_glu', 'm_s5_out_norm', 'm_w_out', 'm_ffn2_norm', 'm_ffn2_w_gate', 'm_ffn2_w_up', 'm_ffn2_w_down', 'm_final_norm', 'v_meta_tokens', 'v_ffn1_norm', 'v_ffn1_w_gate', 'v_ffn1_w_up', 'v_ffn1_w_down', 'v_mix_norm', 'v_w_in', 'v_sb_out_norm', 'v_dn_conv_w', 'v_dn_a_log', 'v_dn_dt_bias', 'v_dn_out_norm', 'v_s5_a_re', 'v_s5_a_im', 'v_s5_log_dt', 'v_s5_b_re', 'v_s5_b_im', 'v_s5_c_re', 'v_s5_c_im', 'v_s5_d', 'v_s5_w_glu', 'v_s5_b_glu', 'v_s5_out_norm', 'v_w_out', 'v_ffn2_norm', 'v_ffn2_w_gate', 'v_ffn2_w_up', 'v_ffn2_w_down', 'v_final_norm']
TWIN_OUTPUTS = ['loss', 'grad_x', 'grad_meta_tokens', 'grad_ffn1_norm', 'grad_ffn1_w_gate', 'grad_ffn1_w_up', 'grad_ffn1_w_down', 'grad_mix_norm', 'grad_w_in', 'grad_sb_out_norm', 'grad_dn_conv_w', 'grad_dn_a_log', 'grad_dn_dt_bias', 'grad_dn_out_norm', 'grad_s5_a_re', 'grad_s5_a_im', 'grad_s5_log_dt', 'grad_s5_b_re', 'grad_s5_b_im', 'grad_s5_c_re', 'grad_s5_c_im', 'grad_s5_d', 'grad_s5_w_glu', 'grad_s5_b_glu', 'grad_s5_out_norm', 'grad_w_out', 'grad_ffn2_norm', 'grad_ffn2_w_gate', 'grad_ffn2_w_up', 'grad_ffn2_w_down', 'grad_final_norm', 'delta_meta_tokens', 'delta_ffn1_norm', 'delta_ffn1_w_gate', 'delta_ffn1_w_up', 'delta_ffn1_w_down', 'delta_mix_norm', 'delta_w_in', 'delta_sb_out_norm', 'delta_dn_conv_w', 'delta_dn_a_log', 'delta_dn_dt_bias', 'delta_dn_out_norm', 'delta_s5_a_re', 'delta_s5_a_im', 'delta_s5_log_dt', 'delta_s5_b_re', 'delta_s5_b_im', 'delta_s5_c_re', 'delta_s5_c_im', 'delta_s5_d', 'delta_s5_w_glu', 'delta_s5_b_glu', 'delta_s5_out_norm', 'delta_w_out', 'delta_ffn2_norm', 'delta_ffn2_w_gate', 'delta_ffn2_w_up', 'delta_ffn2_w_down', 'delta_final_norm', 'new_m_meta_tokens', 'new_m_ffn1_norm', 'new_m_ffn1_w_gate', 'new_m_ffn1_w_up', 'new_m_ffn1_w_down', 'new_m_mix_norm', 'new_m_w_in', 'new_m_sb_out_norm', 'new_m_dn_conv_w', 'new_m_dn_a_log', 'new_m_dn_dt_bias', 'new_m_dn_out_norm', 'new_m_s5_a_re', 'new_m_s5_a_im', 'new_m_s5_log_dt', 'new_m_s5_b_re', 'new_m_s5_b_im', 'new_m_s5_c_re', 'new_m_s5_c_im', 'new_m_s5_d', 'new_m_s5_w_glu', 'new_m_s5_b_glu', 'new_m_s5_out_norm', 'new_m_w_out', 'new_m_ffn2_norm', 'new_m_ffn2_w_gate', 'new_m_ffn2_w_up', 'new_m_ffn2_w_down', 'new_m_final_norm', 'new_v_meta_tokens', 'new_v_ffn1_norm', 'new_v_ffn1_w_gate', 'new_v_ffn1_w_up', 'new_v_ffn1_w_down', 'new_v_mix_norm', 'new_v_w_in', 'new_v_sb_out_norm', 'new_v_dn_conv_w', 'new_v_dn_a_log', 'new_v_dn_dt_bias', 'new_v_dn_out_norm', 'new_v_s5_a_re', 'new_v_s5_a_im', 'new_v_s5_log_dt', 'new_v_s5_b_re', 'new_v_s5_b_im', 'new_v_s5_c_re', 'new_v_s5_c_im', 'new_v_s5_d', 'new_v_s5_w_glu', 'new_v_s5_b_glu', 'new_v_s5_out_norm', 'new_v_w_out', 'new_v_ffn2_norm', 'new_v_ffn2_w_gate', 'new_v_ffn2_w_up', 'new_v_ffn2_w_down', 'new_v_final_norm']
TWIN_LEAF_KINDS = {'loss': 'loss', 'grad_x': 'grad_x', 'grad_meta_tokens': 'grad_w', 'grad_ffn1_norm': 'grad_w', 'grad_ffn1_w_gate': 'grad_w', 'grad_ffn1_w_up': 'grad_w', 'grad_ffn1_w_down': 'grad_w', 'grad_mix_norm': 'grad_w', 'grad_w_in': 'grad_w', 'grad_sb_out_norm': 'grad_w', 'grad_dn_conv_w': 'grad_w', 'grad_dn_a_log': 'grad_w', 'grad_dn_dt_bias': 'grad_w', 'grad_dn_out_norm': 'grad_w', 'grad_s5_a_re': 'grad_w', 'grad_s5_a_im': 'grad_w', 'grad_s5_log_dt': 'grad_w', 'grad_s5_b_re': 'grad_w', 'grad_s5_b_im': 'grad_w', 'grad_s5_c_re': 'grad_w', 'grad_s5_c_im': 'grad_w', 'grad_s5_d': 'grad_w', 'grad_s5_w_glu': 'grad_w', 'grad_s5_b_glu': 'grad_w', 'grad_s5_out_norm': 'grad_w', 'grad_w_out': 'grad_w', 'grad_ffn2_norm': 'grad_w', 'grad_ffn2_w_gate': 'grad_w', 'grad_ffn2_w_up': 'grad_w', 'grad_ffn2_w_down': 'grad_w', 'grad_final_norm': 'grad_w', 'delta_meta_tokens': 'delta_w', 'delta_ffn1_norm': 'delta_w', 'delta_ffn1_w_gate': 'delta_w', 'delta_ffn1_w_up': 'delta_w', 'delta_ffn1_w_down': 'delta_w', 'delta_mix_norm': 'delta_w', 'delta_w_in': 'delta_w', 'delta_sb_out_norm': 'delta_w', 'delta_dn_conv_w': 'delta_w', 'delta_dn_a_log': 'delta_w', 'delta_dn_dt_bias': 'delta_w', 'delta_dn_out_norm': 'delta_w', 'delta_s5_a_re': 'delta_w', 'delta_s5_a_im': 'delta_w', 'delta_s5_log_dt': 'delta_w', 'delta_s5_b_re': 'delta_w', 'delta_s5_b_im': 'delta_w', 'delta_s5_c_re': 'delta_w', 'delta_s5_c_im': 'delta_w', 'delta_s5_d': 'delta_w', 'delta_s5_w_glu': 'delta_w', 'delta_s5_b_glu': 'delta_w', 'delta_s5_out_norm': 'delta_w', 'delta_w_out': 'delta_w', 'delta_ffn2_norm': 'delta_w', 'delta_ffn2_w_gate': 'delta_w', 'delta_ffn2_w_up': 'delta_w', 'delta_ffn2_w_down': 'delta_w', 'delta_final_norm': 'delta_w', 'new_m_meta_tokens': 'new_m', 'new_m_ffn1_norm': 'new_m', 'new_m_ffn1_w_gate': 'new_m', 'new_m_ffn1_w_up': 'new_m', 'new_m_ffn1_w_down': 'new_m', 'new_m_mix_norm': 'new_m', 'new_m_w_in': 'new_m', 'new_m_sb_out_norm': 'new_m', 'new_m_dn_conv_w': 'new_m', 'new_m_dn_a_log': 'new_m', 'new_m_dn_dt_bias': 'new_m', 'new_m_dn_out_norm': 'new_m', 'new_m_s5_a_re': 'new_m', 'new_m_s5_a_im': 'new_m', 'new_m_s5_log_dt': 'new_m', 'new_m_s5_b_re': 'new_m', 'new_m_s5_b_im': 'new_m', 'new_m_s5_c_re': 'new_m', 'new_m_s5_c_im': 'new_m', 'new_m_s5_d': 'new_m', 'new_m_s5_w_glu': 'new_m', 'new_m_s5_b_glu': 'new_m', 'new_m_s5_out_norm': 'new_m', 'new_m_w_out': 'new_m', 'new_m_ffn2_norm': 'new_m', 'new_m_ffn2_w_gate': 'new_m', 'new_m_ffn2_w_up': 'new_m', 'new_m_ffn2_w_down': 'new_m', 'new_m_final_norm': 'new_m', 'new_v_meta_tokens': 'new_v', 'new_v_ffn1_norm': 'new_v', 'new_v_ffn1_w_gate': 'new_v', 'new_v_ffn1_w_up': 'new_v', 'new_v_ffn1_w_down': 'new_v', 'new_v_mix_norm': 'new_v', 'new_v_w_in': 'new_v', 'new_v_sb_out_norm': 'new_v', 'new_v_dn_conv_w': 'new_v', 'new_v_dn_a_log': 'new_v', 'new_v_dn_dt_bias': 'new_v', 'new_v_dn_out_norm': 'new_v', 'new_v_s5_a_re': 'new_v', 'new_v_s5_a_im': 'new_v', 'new_v_s5_log_dt': 'new_v', 'new_v_s5_b_re': 'new_v', 'new_v_s5_b_im': 'new_v', 'new_v_s5_c_re': 'new_v', 'new_v_s5_c_im': 'new_v', 'new_v_s5_d': 'new_v', 'new_v_s5_w_glu': 'new_v', 'new_v_s5_b_glu': 'new_v', 'new_v_s5_out_norm': 'new_v', 'new_v_w_out': 'new_v', 'new_v_ffn2_norm': 'new_v', 'new_v_ffn2_w_gate': 'new_v', 'new_v_ffn2_w_up': 'new_v', 'new_v_ffn2_w_down': 'new_v', 'new_v_final_norm': 'new_v'}


def _forward(args):
    return _fwd_reference(*[args[k] for k in FWD_PARAMS])


def _output_shape():
    def fwd():
        inp = _fwd_setup_inputs(0)
        return _fwd_reference(*[inp[k] for k in FWD_PARAMS])
    out = _jax.eval_shape(fwd)
    return out.shape, out.dtype

N_MICROBATCH = 1
ADAM_LR = 0.001
ADAM_B1 = 0.9
ADAM_B2 = 0.999
ADAM_EPS = 1e-08
ADAM_WD = 0.01
ADAM_STEP = 10
PER_EXAMPLE_BATCH_AXIS = {'x': 0, 'loss_target': 0}
SHARED_INPUTS = []
_WEIGHT_DTYPES = {'meta_tokens': _jnp.float32, 'ffn1_norm': _jnp.float32, 'ffn1_w_gate': _jnp.float32, 'ffn1_w_up': _jnp.float32, 'ffn1_w_down': _jnp.float32, 'mix_norm': _jnp.float32, 'w_in': _jnp.float32, 'sb_out_norm': _jnp.float32, 'dn_conv_w': _jnp.float32, 'dn_a_log': _jnp.float32, 'dn_dt_bias': _jnp.float32, 'dn_out_norm': _jnp.float32, 's5_a_re': _jnp.float32, 's5_a_im': _jnp.float32, 's5_log_dt': _jnp.float32, 's5_b_re': _jnp.float32, 's5_b_im': _jnp.float32, 's5_c_re': _jnp.float32, 's5_c_im': _jnp.float32, 's5_d': _jnp.float32, 's5_w_glu': _jnp.float32, 's5_b_glu': _jnp.float32, 's5_out_norm': _jnp.float32, 'w_out': _jnp.float32, 'ffn2_norm': _jnp.float32, 'ffn2_w_gate': _jnp.float32, 'ffn2_w_up': _jnp.float32, 'ffn2_w_down': _jnp.float32, 'final_norm': _jnp.float32}
MOMENT_SCALE = {'meta_tokens': 8.682018e-03, 'ffn1_norm': 1.558759e-01, 'ffn1_w_gate': 6.585178e-02, 'ffn1_w_up': 6.382215e-02, 'ffn1_w_down': 1.060075e-01, 'mix_norm': 2.911148e-01, 'w_in': 1.930982e-01, 'sb_out_norm': 6.189932e-01, 'dn_conv_w': 1.448381e-01, 'dn_a_log': 3.901134e-01, 'dn_dt_bias': 3.839039e-01, 'dn_out_norm': 2.759346e-01, 's5_a_re': 2.021471e-02, 's5_a_im': 2.061273e-02, 's5_log_dt': 1.161264e+01, 's5_b_re': 1.280677e-02, 's5_b_im': 1.041507e-02, 's5_c_re': 2.237208e-02, 's5_c_im': 2.259570e-02, 's5_d': 3.755102e-01, 's5_w_glu': 7.585955e-02, 's5_b_glu': 1.391580e-01, 's5_out_norm': 3.554480e-01, 'w_out': 2.810839e-01, 'ffn2_norm': 1.052964e-01, 'ffn2_w_gate': 4.522748e-02, 'ffn2_w_up': 4.445478e-02, 'ffn2_w_down': 7.381940e-02, 'final_norm': 1.290911e+02}


def _to_microbatches(a, axis):
    t = _jnp.moveaxis(a, axis, 0)
    t = t.reshape((N_MICROBATCH, t.shape[0] // N_MICROBATCH) + t.shape[1:])
    return _jnp.moveaxis(t, 1, axis + 1)


def setup_inputs(seed: int = 0) -> dict:
    inp = _fwd_setup_inputs(seed)
    key = _jax.random.fold_in(_jax.random.key(seed), 7919)
    shape, _ = _output_shape()
    out = dict(inp)
    out["loss_target"] = _jax.random.normal(_jax.random.fold_in(key, 0), shape, _jnp.float32)
    for i, name in enumerate(TWIN_WEIGHTS):
        w = inp[name].astype(_jnp.float32)
        if MOMENT_SCALE is None:
            s = _jnp.sqrt(_jnp.mean(_jnp.square(w)) + 1e-30)
        else:
            s = MOMENT_SCALE[name]
        km, kv = _jax.random.split(_jax.random.fold_in(key, i + 1))
        out[name] = w
        out["m_" + name] = s * _jax.random.normal(km, w.shape, _jnp.float32)
        out["v_" + name] = (s * s) * _jax.random.uniform(kv, w.shape, _jnp.float32, 0.5, 1.5)
    if N_MICROBATCH > 1:
        for name, axis in PER_EXAMPLE_BATCH_AXIS.items():
            out[name] = _to_microbatches(out[name], axis)
    return {'x': out['x'], 'meta_tokens': out['meta_tokens'], 'ffn1_norm': out['ffn1_norm'], 'ffn1_w_gate': out['ffn1_w_gate'], 'ffn1_w_up': out['ffn1_w_up'], 'ffn1_w_down': out['ffn1_w_down'], 'mix_norm': out['mix_norm'], 'w_in': out['w_in'], 'sb_out_norm': out['sb_out_norm'], 'dn_conv_w': out['dn_conv_w'], 'dn_a_log': out['dn_a_log'], 'dn_dt_bias': out['dn_dt_bias'], 'dn_out_norm': out['dn_out_norm'], 's5_a_re': out['s5_a_re'], 's5_a_im': out['s5_a_im'], 's5_log_dt': out['s5_log_dt'], 's5_b_re': out['s5_b_re'], 's5_b_im': out['s5_b_im'], 's5_c_re': out['s5_c_re'], 's5_c_im': out['s5_c_im'], 's5_d': out['s5_d'], 's5_w_glu': out['s5_w_glu'], 's5_b_glu': out['s5_b_glu'], 's5_out_norm': out['s5_out_norm'], 'w_out': out['w_out'], 'ffn2_norm': out['ffn2_norm'], 'ffn2_w_gate': out['ffn2_w_gate'], 'ffn2_w_up': out['ffn2_w_up'], 'ffn2_w_down': out['ffn2_w_down'], 'final_norm': out['final_norm'], 'loss_target': out['loss_target'], 'm_meta_tokens': out['m_meta_tokens'], 'm_ffn1_norm': out['m_ffn1_norm'], 'm_ffn1_w_gate': out['m_ffn1_w_gate'], 'm_ffn1_w_up': out['m_ffn1_w_up'], 'm_ffn1_w_down': out['m_ffn1_w_down'], 'm_mix_norm': out['m_mix_norm'], 'm_w_in': out['m_w_in'], 'm_sb_out_norm': out['m_sb_out_norm'], 'm_dn_conv_w': out['m_dn_conv_w'], 'm_dn_a_log': out['m_dn_a_log'], 'm_dn_dt_bias': out['m_dn_dt_bias'], 'm_dn_out_norm': out['m_dn_out_norm'], 'm_s5_a_re': out['m_s5_a_re'], 'm_s5_a_im': out['m_s5_a_im'], 'm_s5_log_dt': out['m_s5_log_dt'], 'm_s5_b_re': out['m_s5_b_re'], 'm_s5_b_im': out['m_s5_b_im'], 'm_s5_c_re': out['m_s5_c_re'], 'm_s5_c_im': out['m_s5_c_im'], 'm_s5_d': out['m_s5_d'], 'm_s5_w_glu': out['m_s5_w_glu'], 'm_s5_b_glu': out['m_s5_b_glu'], 'm_s5_out_norm': out['m_s5_out_norm'], 'm_w_out': out['m_w_out'], 'm_ffn2_norm': out['m_ffn2_norm'], 'm_ffn2_w_gate': out['m_ffn2_w_gate'], 'm_ffn2_w_up': out['m_ffn2_w_up'], 'm_ffn2_w_down': out['m_ffn2_w_down'], 'm_final_norm': out['m_final_norm'], 'v_meta_tokens': out['v_meta_tokens'], 'v_ffn1_norm': out['v_ffn1_norm'], 'v_ffn1_w_gate': out['v_ffn1_w_gate'], 'v_ffn1_w_up': out['v_ffn1_w_up'], 'v_ffn1_w_down': out['v_ffn1_w_down'], 'v_mix_norm': out['v_mix_norm'], 'v_w_in': out['v_w_in'], 'v_sb_out_norm': out['v_sb_out_norm'], 'v_dn_conv_w': out['v_dn_conv_w'], 'v_dn_a_log': out['v_dn_a_log'], 'v_dn_dt_bias': out['v_dn_dt_bias'], 'v_dn_out_norm': out['v_dn_out_norm'], 'v_s5_a_re': out['v_s5_a_re'], 'v_s5_a_im': out['v_s5_a_im'], 'v_s5_log_dt': out['v_s5_log_dt'], 'v_s5_b_re': out['v_s5_b_re'], 'v_s5_b_im': out['v_s5_b_im'], 'v_s5_c_re': out['v_s5_c_re'], 'v_s5_c_im': out['v_s5_c_im'], 'v_s5_d': out['v_s5_d'], 'v_s5_w_glu': out['v_s5_w_glu'], 'v_s5_b_glu': out['v_s5_b_glu'], 'v_s5_out_norm': out['v_s5_out_norm'], 'v_w_out': out['v_w_out'], 'v_ffn2_norm': out['v_ffn2_norm'], 'v_ffn2_w_gate': out['v_ffn2_w_gate'], 'v_ffn2_w_up': out['v_ffn2_w_up'], 'v_ffn2_w_down': out['v_ffn2_w_down'], 'v_final_norm': out['v_final_norm']}


def _loss(weights, diff, rest, loss_target):
    with _jax.named_scope("forward"):
        args = {**rest, TWIN_DIFF_INPUT: diff, **{k: w.astype(_WEIGHT_DTYPES[k]) for k, w in weights.items()}}
        y = _forward(args)
    with _jax.named_scope("loss_head"):
        err = _jnp.square(y.astype(_jnp.float32) - loss_target)
        return 0.5 * _jnp.sum(_jnp.mean(err, axis=-1)) if err.ndim else 0.5 * err


def _adamw(w, g, m, v):
    m = ADAM_B1 * m + (1.0 - ADAM_B1) * g
    v = ADAM_B2 * v + (1.0 - ADAM_B2) * _jnp.square(g)
    m_hat = m / (1.0 - ADAM_B1 ** ADAM_STEP)
    v_hat = v / (1.0 - ADAM_B2 ** ADAM_STEP)
    delta = -ADAM_LR * (m_hat / (_jnp.sqrt(v_hat) + ADAM_EPS) + ADAM_WD * w)
    return delta, m, v


def reference(x, meta_tokens, ffn1_norm, ffn1_w_gate, ffn1_w_up, ffn1_w_down, mix_norm, w_in, sb_out_norm, dn_conv_w, dn_a_log, dn_dt_bias, dn_out_norm, s5_a_re, s5_a_im, s5_log_dt, s5_b_re, s5_b_im, s5_c_re, s5_c_im, s5_d, s5_w_glu, s5_b_glu, s5_out_norm, w_out, ffn2_norm, ffn2_w_gate, ffn2_w_up, ffn2_w_down, final_norm, loss_target, m_meta_tokens, m_ffn1_norm, m_ffn1_w_gate, m_ffn1_w_up, m_ffn1_w_down, m_mix_norm, m_w_in, m_sb_out_norm, m_dn_conv_w, m_dn_a_log, m_dn_dt_bias, m_dn_out_norm, m_s5_a_re, m_s5_a_im, m_s5_log_dt, m_s5_b_re, m_s5_b_im, m_s5_c_re, m_s5_c_im, m_s5_d, m_s5_w_glu, m_s5_b_glu, m_s5_out_norm, m_w_out, m_ffn2_norm, m_ffn2_w_gate, m_ffn2_w_up, m_ffn2_w_down, m_final_norm, v_meta_tokens, v_ffn1_norm, v_ffn1_w_gate, v_ffn1_w_up, v_ffn1_w_down, v_mix_norm, v_w_in, v_sb_out_norm, v_dn_conv_w, v_dn_a_log, v_dn_dt_bias, v_dn_out_norm, v_s5_a_re, v_s5_a_im, v_s5_log_dt, v_s5_b_re, v_s5_b_im, v_s5_c_re, v_s5_c_im, v_s5_d, v_s5_w_glu, v_s5_b_glu, v_s5_out_norm, v_w_out, v_ffn2_norm, v_ffn2_w_gate, v_ffn2_w_up, v_ffn2_w_down, v_final_norm):
    given = dict(x=x, meta_tokens=meta_tokens, ffn1_norm=ffn1_norm, ffn1_w_gate=ffn1_w_gate, ffn1_w_up=ffn1_w_up, ffn1_w_down=ffn1_w_down, mix_norm=mix_norm, w_in=w_in, sb_out_norm=sb_out_norm, dn_conv_w=dn_conv_w, dn_a_log=dn_a_log, dn_dt_bias=dn_dt_bias, dn_out_norm=dn_out_norm, s5_a_re=s5_a_re, s5_a_im=s5_a_im, s5_log_dt=s5_log_dt, s5_b_re=s5_b_re, s5_b_im=s5_b_im, s5_c_re=s5_c_re, s5_c_im=s5_c_im, s5_d=s5_d, s5_w_glu=s5_w_glu, s5_b_glu=s5_b_glu, s5_out_norm=s5_out_norm, w_out=w_out, ffn2_norm=ffn2_norm, ffn2_w_gate=ffn2_w_gate, ffn2_w_up=ffn2_w_up, ffn2_w_down=ffn2_w_down, final_norm=final_norm, loss_target=loss_target, m_meta_tokens=m_meta_tokens, m_ffn1_norm=m_ffn1_norm, m_ffn1_w_gate=m_ffn1_w_gate, m_ffn1_w_up=m_ffn1_w_up, m_ffn1_w_down=m_ffn1_w_down, m_mix_norm=m_mix_norm, m_w_in=m_w_in, m_sb_out_norm=m_sb_out_norm, m_dn_conv_w=m_dn_conv_w, m_dn_a_log=m_dn_a_log, m_dn_dt_bias=m_dn_dt_bias, m_dn_out_norm=m_dn_out_norm, m_s5_a_re=m_s5_a_re, m_s5_a_im=m_s5_a_im, m_s5_log_dt=m_s5_log_dt, m_s5_b_re=m_s5_b_re, m_s5_b_im=m_s5_b_im, m_s5_c_re=m_s5_c_re, m_s5_c_im=m_s5_c_im, m_s5_d=m_s5_d, m_s5_w_glu=m_s5_w_glu, m_s5_b_glu=m_s5_b_glu, m_s5_out_norm=m_s5_out_norm, m_w_out=m_w_out, m_ffn2_norm=m_ffn2_norm, m_ffn2_w_gate=m_ffn2_w_gate, m_ffn2_w_up=m_ffn2_w_up, m_ffn2_w_down=m_ffn2_w_down, m_final_norm=m_final_norm, v_meta_tokens=v_meta_tokens, v_ffn1_norm=v_ffn1_norm, v_ffn1_w_gate=v_ffn1_w_gate, v_ffn1_w_up=v_ffn1_w_up, v_ffn1_w_down=v_ffn1_w_down, v_mix_norm=v_mix_norm, v_w_in=v_w_in, v_sb_out_norm=v_sb_out_norm, v_dn_conv_w=v_dn_conv_w, v_dn_a_log=v_dn_a_log, v_dn_dt_bias=v_dn_dt_bias, v_dn_out_norm=v_dn_out_norm, v_s5_a_re=v_s5_a_re, v_s5_a_im=v_s5_a_im, v_s5_log_dt=v_s5_log_dt, v_s5_b_re=v_s5_b_re, v_s5_b_im=v_s5_b_im, v_s5_c_re=v_s5_c_re, v_s5_c_im=v_s5_c_im, v_s5_d=v_s5_d, v_s5_w_glu=v_s5_w_glu, v_s5_b_glu=v_s5_b_glu, v_s5_out_norm=v_s5_out_norm, v_w_out=v_w_out, v_ffn2_norm=v_ffn2_norm, v_ffn2_w_gate=v_ffn2_w_gate, v_ffn2_w_up=v_ffn2_w_up, v_ffn2_w_down=v_ffn2_w_down, v_final_norm=v_final_norm)
    weights = {n: given[n] for n in TWIN_WEIGHTS}
    shared = {n: given[n] for n in SHARED_INPUTS}
    per_example = {n: given[n] for n in ['x']}
    grad_fn = _jax.value_and_grad(_loss, argnums=(0, 1))

    def one_microbatch(ex, loss_target):
        ex = dict(ex)
        diff = ex.pop(TWIN_DIFF_INPUT)
        return grad_fn(weights, diff, {**shared, **ex}, loss_target)

    if N_MICROBATCH == 1:
        loss, (grad_w, grad_x) = one_microbatch(per_example, given["loss_target"])
    else:
        def body(carry, xs):
            loss_sum, grad_sum = carry
            l_k, (gw_k, gx_k) = one_microbatch(xs[0], xs[1])
            with _jax.named_scope("update"):
                return (loss_sum + l_k, _jax.tree.map(_jnp.add, grad_sum, gw_k)), gx_k

        init = (_jnp.zeros((), _jnp.float32), _jax.tree.map(_jnp.zeros_like, weights))
        (loss, grad_w), grad_x = _jax.lax.scan(body, init, (per_example, given["loss_target"]))
    with _jax.named_scope("update"):
        delta_w, new_m, new_v = {}, {}, {}
        for n in TWIN_WEIGHTS:
            delta_w[n], new_m[n], new_v[n] = _adamw(weights[n], grad_w[n], given["m_" + n], given["v_" + n])
    return (loss, grad_x, *[grad_w[n] for n in TWIN_WEIGHTS], *[delta_w[n] for n in TWIN_WEIGHTS],
            *[new_m[n] for n in TWIN_WEIGHTS], *[new_v[n] for n in TWIN_WEIGHTS])
```

```python
import functools
import math

import jax
import jax.numpy as jnp
from jax import lax
from jax.experimental import pallas as pl
from jax.experimental.pallas import tpu as pltpu

f32 = jnp.float32
MXU_DTYPE = jnp.bfloat16
HI = lax.Precision.HIGHEST
NN = (((1,), (0,)), ((), ()))
NT = (((1,), (1,)), ((), ()))
TN = (((0,), (0,)), ((), ()))

EPS = 1e-6
D_MODEL = 1024
N_META = 16
HEAD_DIM = 64
N_HEADS = 4
QW = N_HEADS * HEAD_DIM
DN_CONV = 4
DN_CHUNK = 64
S5_W = 512
S5_G = 32
S5_P = 64
S5_C = 16
S5_N = S5_G * S5_P
S5_SLABS = 4
N_CHIPS = 4
PROJ_W = 2432
IN_WIDTH = 2312
VMEM_LIMIT = 56 * 1024 * 1024

ADAM_LR, ADAM_B1, ADAM_B2, ADAM_EPS, ADAM_WD, ADAM_STEP = 0.001, 0.9, 0.999, 1e-08, 0.01, 10
FLAT_W = 1024
BIG_ROWS = 512


def _dot(a, b, dims=NN):
    return lax.dot_general(a.astype(MXU_DTYPE), b.astype(MXU_DTYPE), dims, preferred_element_type=f32)


def _dotx(a, b, dims=NN):
    return lax.dot_general(a, b, dims, precision=HI, preferred_element_type=f32)


def _split(x):
    if MXU_DTYPE == f32:
        return x, None
    hi = x.astype(MXU_DTYPE)
    return hi, (x - hi.astype(f32)).astype(MXU_DTYPE)


def _dot_split(hi, lo, u01):
    if lo is None:
        return _dotx(hi, u01)
    u = u01.astype(MXU_DTYPE)
    return (lax.dot_general(hi, u, NN, preferred_element_type=f32)
            + lax.dot_general(lo, u, NN, preferred_element_type=f32))


def _dot3(a, b, dims=NN):
    ah, al = _split(a)
    if al is None:
        return _dotx(a, b, dims)
    bh, bl = _split(b)
    d = lambda x, y: lax.dot_general(x, y, dims, preferred_element_type=f32)
    return d(ah, bh) + d(ah, bl) + d(al, bh)


BNN = (((2,), (1,)), ((0,), (0,)))
BNT = (((2,), (2,)), ((0,), (0,)))
BTN = (((1,), (1,)), ((0,), (0,)))


def _with_dot_vjp(dot, kind, batched=False):
    nn, nt, tn = (BNN, BNT, BTN) if batched else (NN, NT, TN)
    dims = {"nn": nn, "nt": nt, "tn": tn}[kind]

    @jax.custom_vjp
    def f(a, b):
        return dot(a, b, dims)

    def fwd(a, b):
        return dot(a, b, dims), (a, b)

    def bwd(res, dy):
        a, b = res
        if kind == "nn":
            return dot(dy, b, nt), dot(a, dy, tn)
        if kind == "nt":
            return dot(dy, b, nn), dot(dy, a, tn)
        return dot(b, dy, nt), dot(a, dy, nn)

    f.defvjp(fwd, bwd)
    return f


_mm = _with_dot_vjp(_dot, "nn")
_bmm = _with_dot_vjp(_dot, "nn", True)
_bmm_nt = _with_dot_vjp(_dot, "nt", True)
_bmm_tn = _with_dot_vjp(_dot, "tn", True)
_bmm3 = _with_dot_vjp(_dot3, "nn", True)


def _rms(x, g):
    return x * lax.rsqrt(jnp.mean(x * x, axis=-1, keepdims=True) + EPS) * g


def _sigmoid(x):
    return 1.0 / (1.0 + jnp.exp(-x))


def _silu(x):
    return x * _sigmoid(x)


def _softplus(x):
    return jnp.maximum(x, 0.0) + jnp.log(1.0 + jnp.exp(-jnp.abs(x)))


def _gelu_tanh(x):
    return 0.5 * x * (1.0 + jnp.tanh(math.sqrt(2.0 / math.pi) * (x + 0.044715 * x * x * x)))


def _iota2(shape, axis):
    return lax.broadcasted_iota(jnp.int32, shape, axis)


def _block_diag_ones(n, blk):
    return ((_iota2((n, n), 0) // blk) == (_iota2((n, n), 1) // blk)).astype(f32)


def _pc(body, name, grid, in_specs, out_specs, out_shape, scratch=(), vmem=VMEM_LIMIT):
    return pl.pallas_call(
        body, name=name, grid=grid, in_specs=in_specs, out_specs=out_specs, out_shape=out_shape,
        scratch_shapes=list(scratch),
        compiler_params=pltpu.CompilerParams(dimension_semantics=("arbitrary",) * len(grid), vmem_limit_bytes=vmem))


def _bs(shape, imap):
    return pl.BlockSpec(shape, imap)


def _sds(shape, dtype=f32):
    return jax.ShapeDtypeStruct(tuple(shape), dtype)


def _token_tile(lp, cap=640):
    for t in (640, 320, 256, 128, 64):
        if t <= cap and lp % t == 0:
            return t
    raise ValueError(lp)


def _padded_len(l):
    return -(-l // 1280) * 1280 if l > 4096 else -(-l // 256) * 256


def _ffn_fwd(h, g, wg, wu, wd):
    lp, d = h.shape
    nch, _, fc = wg.shape
    tm = _token_tile(lp)

    def body(h_ref, g_ref, wg_ref, wu_ref, wd_ref, o_ref, xn_ref, gate_ref, up_ref, xn_s, acc_s):
        j = pl.program_id(1)

        @pl.when(j == 0)
        def _():
            xn_s[...] = _rms(h_ref[...], g_ref[...]).astype(xn_s.dtype)
            acc_s[...] = jnp.zeros_like(acc_s)

        xn = xn_s[...]
        gate = _dot(xn, wg_ref[0])
        up = _dot(xn, wu_ref[0])
        gate_ref[0] = gate.astype(gate_ref.dtype)
        up_ref[0] = up.astype(up_ref.dtype)
        acc_s[...] += _dot(_silu(gate) * up, wd_ref[0])

        @pl.when(j == nch - 1)
        def _():
            o_ref[...] = h_ref[...] + 0.5 * acc_s[...]
            xn_ref[...] = xn_s[...]

    tok = _bs((tm, d), lambda i, j: (i, 0))
    chunk = _bs((1, tm, fc), lambda i, j: (j, i, 0))
    return _pc(
        body, "ffn_fwd", (lp // tm, nch),
        [tok, _bs((1, d), lambda i, j: (0, 0)),
         _bs((1, d, fc), lambda i, j: (j, 0, 0)), _bs((1, d, fc), lambda i, j: (j, 0, 0)),
         _bs((1, fc, d), lambda i, j: (j, 0, 0))],
        [tok, tok, chunk, chunk],
        [_sds((lp, d)), _sds((lp, d), MXU_DTYPE), _sds((nch, lp, fc), MXU_DTYPE), _sds((nch, lp, fc), MXU_DTYPE)],
        scratch=[pltpu.VMEM((tm, d), MXU_DTYPE), pltpu.VMEM((tm, d), f32)])(h, g, wg, wu, wd)


def _ffn_bwd_dx(h, g, wg, wu, wd, dy, gate_saved, up_saved):
    lp, d = h.shape
    nch, _, fc = wg.shape
    tm = _token_tile(lp)

    def body(h_ref, g_ref, wg_ref, wu_ref, wd_ref, dy_ref, gate_ref, up_ref, dh_ref, dgn_ref, dg_ref, du_ref, act_ref,
             dxn_s, dout_s):
        i, j = pl.program_id(0), pl.program_id(1)

        @pl.when((i == 0) & (j == 0))
        def _():
            dgn_ref[...] = jnp.zeros_like(dgn_ref)

        @pl.when(j == 0)
        def _():
            dxn_s[...] = jnp.zeros_like(dxn_s)
            dout_s[...] = (0.5 * dy_ref[...]).astype(dout_s.dtype)

        gate = gate_ref[0].astype(f32)
        up = up_ref[0].astype(f32)
        sig = _sigmoid(gate)
        sl = gate * sig
        dact = _dot(dout_s[...], wd_ref[0], NT)
        d_up = dact * sl
        d_gate = dact * up * sig * (1.0 + gate * (1.0 - sig))
        dg_ref[0] = d_gate.astype(dg_ref.dtype)
        du_ref[0] = d_up.astype(du_ref.dtype)
        act_ref[0] = (sl * up).astype(act_ref.dtype)
        dxn_s[...] += _dot(d_gate, wg_ref[0], NT) + _dot(d_up, wu_ref[0], NT)

        @pl.when(j == nch - 1)
        def _():
            _, vjp = jax.vjp(_rms, h_ref[...], g_ref[...])
            dx, dg = vjp(dxn_s[...])
            dh_ref[...] = dy_ref[...] + dx
            dgn_ref[...] += dg

    tok = _bs((tm, d), lambda i, j: (i, 0))
    chunk = _bs((1, tm, fc), lambda i, j: (j, i, 0))
    return _pc(
        body, "ffn_bwd_dx", (lp // tm, nch),
        [tok, _bs((1, d), lambda i, j: (0, 0)),
         _bs((1, d, fc), lambda i, j: (j, 0, 0)), _bs((1, d, fc), lambda i, j: (j, 0, 0)),
         _bs((1, fc, d), lambda i, j: (j, 0, 0)), tok, chunk, chunk],
        [tok, _bs((1, d), lambda i, j: (0, 0)), chunk, chunk, chunk],
        [_sds((lp, d)), _sds((1, d)),
         _sds((nch, lp, fc), MXU_DTYPE), _sds((nch, lp, fc), MXU_DTYPE), _sds((nch, lp, fc), MXU_DTYPE)],
        scratch=[pltpu.VMEM((tm, d), f32), pltpu.VMEM((tm, d), MXU_DTYPE)],
    )(h, g, wg, wu, wd, dy, gate_saved, up_saved)


def _ffn_bwd_dw(xn, dy, d_gate, d_up, act):
    lp, d = xn.shape
    nch, _, fc = d_gate.shape
    tm = _token_tile(lp)

    def body(xn_ref, dy_ref, dg_ref, du_ref, act_ref, dwg_ref, dwu_ref, dwd_ref):
        @pl.when(pl.program_id(1) == 0)
        def _():
            dwg_ref[...] = jnp.zeros_like(dwg_ref)
            dwu_ref[...] = jnp.zeros_like(dwu_ref)
            dwd_ref[...] = jnp.zeros_like(dwd_ref)

        xn_t = xn_ref[...]
        dwg_ref[0] += _dot(xn_t, dg_ref[0], TN)
        dwu_ref[0] += _dot(xn_t, du_ref[0], TN)
        dwd_ref[0] += _dot(act_ref[0], 0.5 * dy_ref[...], TN)

    tok = _bs((tm, d), lambda j, i: (i, 0))
    chunk = _bs((1, tm, fc), lambda j, i: (j, i, 0))
    return _pc(
        body, "ffn_bwd_dw", (nch, lp // tm), [tok, tok, chunk, chunk, chunk],
        [_bs((1, d, fc), lambda j, i: (j, 0, 0)), _bs((1, d, fc), lambda j, i: (j, 0, 0)),
         _bs((1, fc, d), lambda j, i: (j, 0, 0))],
        [_sds((nch, d, fc)), _sds((nch, d, fc)), _sds((nch, fc, d))])(xn, dy, d_gate, d_up, act)


_PROJ_SPLITS = (QW, QW, QW, 3 * QW, QW, S5_W, 128)


def _inproj_fwd(h, g, w):
    lp, d = h.shape
    tm = _token_tile(lp)

    def body(h_ref, g_ref, w_ref, *outs):
        proj = _dot(_rms(h_ref[...], g_ref[...]), w_ref[...])
        off = 0
        for ref, wd in zip(outs, _PROJ_SPLITS):
            ref[...] = proj[:, off:off + wd]
            off += wd

    return _pc(
        body, "inproj_fwd", (lp // tm,),
        [_bs((tm, d), lambda i: (i, 0)), _bs((1, d), lambda i: (0, 0)), _bs((d, PROJ_W), lambda i: (0, 0))],
        [_bs((tm, wd), lambda i: (i, 0)) for wd in _PROJ_SPLITS],
        [_sds((lp, wd)) for wd in _PROJ_SPLITS])(h, g, w)


def _inproj_bwd(h, g, w, dres, dq, dk, dv, ddn4, dz, du, dba):
    lp, d = h.shape
    tm = _token_tile(lp, 320)

    def body(h_ref, g_ref, w_ref, dres_ref, dq_ref, dk_ref, dv_ref, ddn_ref, dz_ref, du_ref, dba_ref,
             dh_ref, dgn_ref, dw_ref):
        @pl.when(pl.program_id(0) == 0)
        def _():
            dgn_ref[...] = jnp.zeros_like(dgn_ref)
            dw_ref[...] = jnp.zeros_like(dw_ref)

        ddn = ddn_ref[0] + ddn_ref[1] + ddn_ref[2] + ddn_ref[3]
        dproj = jnp.concatenate(
            [dq_ref[...], dk_ref[...], dv_ref[...], ddn, dz_ref[...], du_ref[...], dba_ref[...]], axis=1)
        xn, vjp = jax.vjp(_rms, h_ref[...], g_ref[...])
        dx, dg = vjp(_dot(dproj, w_ref[...], NT))
        dw_ref[...] += _dot(xn, dproj, TN)
        dh_ref[...] = dres_ref[...] + dx
        dgn_ref[...] += dg

    tok = lambda wd: _bs((tm, wd), lambda i: (i, 0))
    return _pc(
        body, "inproj_bwd", (lp // tm,),
        [tok(d), _bs((1, d), lambda i: (0, 0)), _bs((d, PROJ_W), lambda i: (0, 0)), tok(d),
         tok(QW), tok(QW), tok(QW), _bs((4, tm, 3 * QW), lambda i: (0, i, 0)), tok(QW), tok(S5_W), tok(128)],
        [tok(d), _bs((1, d), lambda i: (0, 0)), _bs((d, PROJ_W), lambda i: (0, 0))],
        [_sds((lp, d)), _sds((1, d)), _sds((d, PROJ_W))])(h, g, w, dres, dq, dk, dv, ddn4, dz, du, dba)


_SB_TQ = 256
_SB_ROWS = 32
_SB_GROUP = 4
_SB_ROWS_BWD = 32
_SB_GROUP_BWD = 4


def _sb_pieces(z, valid):
    t = jnp.exp(-jnp.abs(z))
    sp = jnp.maximum(z, 0.0) + jnp.log(1.0 + t)
    lk = -sp if valid is None else jnp.where(valid, -sp, 0.0)
    return t, sp, lk


def _cat_rows(parts):
    return parts[0] if len(parts) == 1 else jnp.concatenate(parts, axis=0)


def _sb_fwd(q, kt, vt):
    nh, lp, hd = q.shape
    tq = tk = min(_SB_TQ, lp)
    blocks = [slice(r, r + _SB_ROWS) for r in range(0, tq, _SB_ROWS)]

    def body(q_ref, k_ref, v_ref, o_ref, w_hbm, s_hbm, wbuf, sbuf, sems):
        head, qi = pl.program_id(0), pl.program_id(1)
        qv = q_ref[0]
        u_strict = (_iota2((tk, tk), 0) > _iota2((tk, tk), 1)).astype(f32)
        below = _iota2((tq, tk), 1) < _iota2((tq, tk), 0)
        rows = pl.ds(pl.multiple_of(qi * tq, tq), tq)

        def save(t, ks):
            return [pltpu.make_async_copy(wbuf.at[t], w_hbm.at[head, rows, ks], sems.at[0, t]),
                    pltpu.make_async_copy(sbuf.at[t], s_hbm.at[head, rows, ks], sems.at[1, t])]

        def drain(was_group):
            for cp in save(0, pl.ds(0, tk)):
                cp.wait()

            @pl.when(was_group)
            def _():
                for t in range(1, _SB_GROUP):
                    for cp in save(t, pl.ds(0, tk)):
                        cp.wait()

        def tiles(js, carry, masked=False, before=None):
            o_acc, c_after = carry
            kss = [pl.ds(pl.multiple_of(j * tk, tk), tk) for j in js]
            z_alls = [_dot(qv, k_ref[0, :, ks]) * (HEAD_DIM ** -0.5) for ks in kss]
            stage, afters = [], []
            for z_all in z_alls:
                his, los, logs, sums = [], [], [], []
                for rs in blocks:
                    z = z_all[rs]
                    _, sp, lk = _sb_pieces(z, below[rs] if masked else None)
                    hi, lo = _split(lk)
                    his.append(hi)
                    los.append(lo)
                    logs.append(z - sp)
                    sums.append(jnp.sum(lk, axis=1, keepdims=True))
                stage.append((logs, _cat_rows(sums)))
                afters.append(_dot_split(_cat_rows(his), None if los[0] is None else _cat_rows(los), u_strict))
            w_alls, sig_alls = [], []
            for (logs, sums), after_all in zip(stage, afters):
                ws, sigs = [], []
                for n, rs in enumerate(blocks):
                    w = jnp.exp(logs[n] + after_all[rs] + c_after[rs])
                    sig = jnp.exp(logs[n])
                    if masked:
                        w, sig = jnp.where(below[rs], w, 0.0), jnp.where(below[rs], sig, 0.0)
                    ws.append(w.astype(MXU_DTYPE))
                    sigs.append(sig.astype(MXU_DTYPE))
                w_alls.append(_cat_rows(ws))
                sig_alls.append(_cat_rows(sigs))
                c_after = c_after + sums
            if before is not None:
                before()
            for t, (w_all, sig_all, ks) in enumerate(zip(w_alls, sig_alls, kss)):
                wbuf[t] = w_all
                sbuf[t] = sig_all
                for cp in save(t, ks):
                    cp.start()
                o_acc = o_acc + _dot(w_all, v_ref[0, :, ks], NT)
            return o_acc, c_after

        n_groups, rest = qi // _SB_GROUP, qi % _SB_GROUP
        group = lambda g, c: tiles([qi - 1 - _SB_GROUP * g - n for n in range(_SB_GROUP)], c,
                                   before=lambda: drain(g > 0))
        single = lambda n, c: tiles([rest - 1 - n], c, before=lambda: drain((n == 0) & (n_groups > 0)))
        carry = tiles([qi], (jnp.zeros((tq, hd), f32), jnp.zeros((tq, 1), f32)), True)
        carry = lax.fori_loop(0, n_groups, group, carry)
        o_acc, _ = lax.fori_loop(0, rest, single, carry)
        drain((rest == 0) & (n_groups > 0))
        o_ref[0] = o_acc

    full_t = _bs((1, hd, lp), lambda h, i: (h, 0, 0))
    hbm = pl.BlockSpec(memory_space=pltpu.HBM)
    return _pc(
        body, "sb_fwd", (nh, lp // tq),
        [_bs((1, tq, hd), lambda h, i: (h, i, 0)), full_t, full_t],
        [_bs((1, tq, hd), lambda h, i: (h, i, 0)), hbm, hbm],
        [_sds((nh, lp, hd)), _sds((nh, lp, lp), MXU_DTYPE), _sds((nh, lp, lp), MXU_DTYPE)],
        scratch=[pltpu.VMEM((_SB_GROUP, tq, tk), MXU_DTYPE), pltpu.VMEM((_SB_GROUP, tq, tk), MXU_DTYPE),
                 pltpu.SemaphoreType.DMA((2, _SB_GROUP))])(q, kt, vt)


def _sb_bwd(q, kt, vt, w_saved, s_saved, do):
    nh, lp, hd = q.shape
    tq = tk = min(_SB_TQ, lp)
    blocks = [slice(r, r + _SB_ROWS_BWD) for r in range(0, tq, _SB_ROWS_BWD)]
    grp = _SB_GROUP_BWD

    def body(q_ref, k_ref, v_ref, w_hbm, s_hbm, do_ref, dq_ref, dk_ref, dv_ref, wbuf, sbuf, sems):
        head, qi = pl.program_id(0), pl.program_id(1)

        @pl.when(qi == 0)
        def _():
            dk_ref[...] = jnp.zeros_like(dk_ref)
            dv_ref[...] = jnp.zeros_like(dv_ref)

        qv, dov = q_ref[0], do_ref[0]
        u_excl = (_iota2((tk, tk), 0) < _iota2((tk, tk), 1)).astype(f32)
        scale = HEAD_DIM ** -0.5
        rows = pl.ds(pl.multiple_of(qi * tq, tq), tq)

        def loads(js, slot):
            out = []
            for t, j in enumerate(js):
                ks = pl.ds(pl.multiple_of(j * tk, tk), tk)
                out += [pltpu.make_async_copy(w_hbm.at[head, rows, ks], wbuf.at[slot, t], sems.at[0, slot, t]),
                        pltpu.make_async_copy(s_hbm.at[head, rows, ks], sbuf.at[slot, t], sems.at[1, slot, t])]
            return out

        def tiles(js, slot, carry):
            dq_acc, c_e = carry
            kss = [pl.ds(pl.multiple_of(j * tk, tk), tk) for j in js]
            dw_alls = [_dot(dov, v_ref[0, :, ks]) for ks in kss]
            stage, befores = [], []
            for t, dw_all in enumerate(dw_alls):
                es, ebs, esums = [], [], []
                for rs in blocks:
                    e = wbuf[slot, t, rs].astype(f32) * dw_all[rs]
                    es.append(e)
                    ebs.append(e.astype(MXU_DTYPE))
                    esums.append(jnp.sum(e, axis=1, keepdims=True))
                stage.append((es, _cat_rows(esums)))
                befores.append(_dot(_cat_rows(ebs), u_excl))
            for t, ((es, esums), before_all, ks) in enumerate(zip(stage, befores, kss)):
                dzs = []
                for n, rs in enumerate(blocks):
                    sig = sbuf[slot, t, rs].astype(f32)
                    dz = es[n] * (1.0 - sig) - sig * (c_e[rs] + before_all[rs])
                    dzs.append((dz * scale).astype(MXU_DTYPE))
                dz_all = _cat_rows(dzs)
                c_e = c_e + esums
                dk_ref[0, :, ks] += _dot(qv, dz_all, TN)
                dv_ref[0, :, ks] += _dot(dov, wbuf[slot, t], TN)
                dq_acc = dq_acc + _dot(dz_all, k_ref[0, :, ks], NT)
            return dq_acc, c_e

        n_tiles = qi + 1
        n_groups = n_tiles // grp
        group_js = lambda g: [grp * g + t for t in range(grp)]

        @pl.when(n_groups > 0)
        def _():
            for cp in loads(group_js(0), 0):
                cp.start()

        def group(g, carry):
            slot = g % 2

            @pl.when(g + 1 < n_groups)
            def _():
                for cp in loads(group_js(g + 1), 1 - slot):
                    cp.start()

            for cp in loads(group_js(g), slot):
                cp.wait()
            return tiles(group_js(g), slot, carry)

        def single(j, carry):
            for cp in loads([j], 0):
                cp.start()
            for cp in loads([j], 0):
                cp.wait()
            return tiles([j], 0, carry)

        carry = lax.fori_loop(0, n_groups, group, (jnp.zeros((tq, hd), f32), jnp.zeros((tq, 1), f32)))
        dq_acc, _ = lax.fori_loop(n_groups * grp, n_tiles, single, carry)
        dq_ref[0] = dq_acc

    tile_spec = _bs((1, tq, hd), lambda h, i: (h, i, 0))
    full_t = _bs((1, hd, lp), lambda h, i: (h, 0, 0))
    hbm = pl.BlockSpec(memory_space=pltpu.HBM)
    return _pc(
        body, "sb_bwd", (nh, lp // tq),
        [tile_spec, full_t, full_t, hbm, hbm, tile_spec],
        [tile_spec, full_t, full_t], [_sds((nh, lp, hd)), _sds((nh, hd, lp)), _sds((nh, hd, lp))],
        scratch=[pltpu.VMEM((2, grp, tq, tk), MXU_DTYPE), pltpu.VMEM((2, grp, tq, tk), MXU_DTYPE),
                 pltpu.SemaphoreType.DMA((2, 2, grp))])(q, kt, vt, w_saved, s_saved, do)


def _dn_pre_tile(xs, ba, cw, pv):
    conv = xs[0] * cw[3:4] + xs[1] * cw[2:3] + xs[2] * cw[1:2] + xs[3] * cw[0:1]
    s = _silu(conv)
    bd = _block_diag_ones(QW, HEAD_DIM)
    sq, sk, sv = s[:, :QW], s[:, QW:2 * QW], s[:, 2 * QW:]
    qn = sq * lax.rsqrt(_dotx(sq * sq, bd) + EPS)
    kn = sk * lax.rsqrt(_dotx(sk * sk, bd) + EPS)
    lane = _iota2(ba.shape, 1)
    beta = _sigmoid(ba)
    g = -jnp.exp(pv[0:1]) * _softplus(ba + pv[1:2])
    bg = jnp.where(lane < N_HEADS, beta, jnp.where(lane < 2 * N_HEADS, g, 0.0))
    return qn, kn, sv, bg


def _dn_pre_fwd(xs, ba, cw, pv):
    _, lp, w3 = xs.shape
    tm = _token_tile(lp, 320)

    def body(xs_ref, ba_ref, cw_ref, pv_ref, q_ref, k_ref, v_ref, bg_ref):
        qn, kn, sv, bg = _dn_pre_tile(xs_ref[...], ba_ref[...], cw_ref[...], pv_ref[...])
        q_ref[...], k_ref[...], v_ref[...], bg_ref[...] = qn, kn, sv, bg

    tok = lambda wd: _bs((tm, wd), lambda i: (i, 0))
    return _pc(
        body, "dn_pre_fwd", (lp // tm,),
        [_bs((4, tm, w3), lambda i: (0, i, 0)), tok(128), _bs((8, w3), lambda i: (0, 0)), _bs((8, 128), lambda i: (0, 0))],
        [tok(QW), tok(QW), tok(QW), tok(128)],
        [_sds((lp, QW)), _sds((lp, QW)), _sds((lp, QW)), _sds((lp, 128))])(xs, ba, cw, pv)


def _dn_pre_bwd(xs, ba, cw, pv, dq, dk, dv, dbg):
    _, lp, w3 = xs.shape
    tm = _token_tile(lp, 320)

    def body(xs_ref, ba_ref, cw_ref, pv_ref, dq_ref, dk_ref, dv_ref, dbg_ref, dxs_ref, dba_ref, dcw_ref, dpv_ref):
        @pl.when(pl.program_id(0) == 0)
        def _():
            dcw_ref[...] = jnp.zeros_like(dcw_ref)
            dpv_ref[...] = jnp.zeros_like(dpv_ref)

        _, vjp = jax.vjp(_dn_pre_tile, xs_ref[...], ba_ref[...], cw_ref[...], pv_ref[...])
        dxs, dba, dcw, dpv = vjp((dq_ref[...], dk_ref[...], dv_ref[...], dbg_ref[...]))
        dxs_ref[...] = dxs
        dba_ref[...] = dba
        dcw_ref[...] += dcw
        dpv_ref[...] += dpv

    tok = lambda wd: _bs((tm, wd), lambda i: (i, 0))
    xs_spec = _bs((4, tm, w3), lambda i: (0, i, 0))
    cw_spec, pv_spec = _bs((8, w3), lambda i: (0, 0)), _bs((8, 128), lambda i: (0, 0))
    return _pc(
        body, "dn_pre_bwd", (lp // tm,),
        [xs_spec, tok(128), cw_spec, pv_spec, tok(QW), tok(QW), tok(QW), tok(128)],
        [xs_spec, tok(128), cw_spec, pv_spec],
        [_sds((4, lp, w3)), _sds((lp, 128)), _sds((8, w3)), _sds((8, 128))])(xs, ba, cw, pv, dq, dk, dv, dbg)


def _dn_chunk(state, q, k, v, grow, brow):
    nh, c, _ = q.shape
    ii, jj = _iota2((c, c), 0), _iota2((c, c), 1)
    eye = ii == jj
    col = lambda row: jnp.sum(jnp.where(eye, jnp.broadcast_to(row, (nh, c, c)), 0.0), axis=2, keepdims=True)
    gc_row = _dotx(grow, jnp.broadcast_to((ii <= jj).astype(f32), (nh, c, c)), BNN)
    gc_col, b_col = col(gc_row), col(brow)
    decay = jnp.exp(jnp.where(ii >= jj, gc_col - gc_row, -1e30))
    kb = k * b_col
    p = -jnp.where(ii > jj, _bmm_nt(kb, k) * decay, 0.0)
    t_inv = eye.astype(f32) + p
    for _ in range(5):
        p = _bmm3(p, p)
        t_inv = t_inv + _bmm3(t_inv, p)
    egc = jnp.exp(gc_col)
    u = _bmm(t_inv, v * b_col)
    w = _bmm(t_inv, kb * egc)
    qs = q * (q.shape[2] ** -0.5)
    attn = jnp.where(ii >= jj, _bmm_nt(qs, k) * decay, 0.0)
    v_new = u - _bmm(w, state)
    o = _bmm(qs * egc, state) + _bmm(attn, v_new)
    g_last = gc_row[:, :, c - 1:c]
    new_state = state * jnp.exp(g_last) + _bmm_tn(k * jnp.exp(g_last - gc_col), v_new)
    return new_state, o


def _dn_scan_fwd(q, k, v, grow, brow):
    nh, lp, hd = q.shape
    c = DN_CHUNK
    n = lp // c

    def body(q_ref, k_ref, v_ref, g_ref, b_ref, o_ref, st_ref, state_s):
        @pl.when(pl.program_id(0) == 0)
        def _():
            state_s[...] = jnp.zeros_like(state_s)

        st_ref[:, 0] = state_s[...]
        state, o = _dn_chunk(state_s[...], q_ref[...], k_ref[...], v_ref[...], g_ref[:, 0], b_ref[:, 0])
        state_s[...] = state
        o_ref[...] = o

    seq = _bs((nh, c, hd), lambda i: (0, i, 0))
    row = _bs((nh, 1, 1, c), lambda i: (0, i, 0, 0))
    return _pc(body, "dn_scan_fwd", (n,), [seq, seq, seq, row, row],
               [seq, _bs((nh, 1, hd, hd), lambda i: (0, i, 0, 0))],
               [_sds((nh, lp, hd)), _sds((nh, n, hd, hd))],
               scratch=[pltpu.VMEM((nh, hd, hd), f32)])(q, k, v, grow, brow)


def _dn_scan_bwd(q, k, v, grow, brow, states, do):
    nh, lp, hd = q.shape
    c = DN_CHUNK
    n = lp // c

    def body(q_ref, k_ref, v_ref, g_ref, b_ref, st_ref, do_ref, dq_ref, dk_ref, dv_ref, dg_ref, db_ref, dstate_s):
        @pl.when(pl.program_id(0) == 0)
        def _():
            dstate_s[...] = jnp.zeros_like(dstate_s)

        _, vjp = jax.vjp(_dn_chunk, st_ref[:, 0], q_ref[...], k_ref[...], v_ref[...], g_ref[:, 0], b_ref[:, 0])
        dstate, dq, dk, dv, dg, db = vjp((dstate_s[...], do_ref[...]))
        dstate_s[...] = dstate
        dq_ref[...], dk_ref[...], dv_ref[...] = dq, dk, dv
        dg_ref[:, 0], db_ref[:, 0] = dg, db

    seq = _bs((nh, c, hd), lambda i: (0, n - 1 - i, 0))
    row = _bs((nh, 1, 1, c), lambda i: (0, n - 1 - i, 0, 0))
    return _pc(body, "dn_scan_bwd", (n,),
               [seq, seq, seq, row, row, _bs((nh, 1, hd, hd), lambda i: (0, n - 1 - i, 0, 0)), seq],
               [seq, seq, seq, row, row],
               [_sds((nh, lp, hd))] * 3 + [_sds((nh, n, 1, c))] * 2,
               scratch=[pltpu.VMEM((nh, hd, hd), f32)])(q, k, v, grow, brow, states, do)


def _s5_prep(a_re, a_im, log_dt, b_re, b_im):
    dt = jnp.exp(log_dt)
    mag = jnp.exp(a_re * dt)
    ar, ai = mag * jnp.cos(a_im * dt), mag * jnp.sin(a_im * dt)
    den = a_re * a_re + a_im * a_im
    cr = ((ar - 1.0) * a_re + ai * a_im) / den
    ci = (ai * a_re - (ar - 1.0) * a_im) / den
    cr3, ci3 = cr[:, None, :], ci[:, None, :]
    return ar, ai, cr3 * b_re - ci3 * b_im, cr3 * b_im + ci3 * b_re


def _s5_prep_fwd(a_re, a_im, log_dt, b_re, b_im):
    def body(ar_ref, ai_ref, dt_ref, br_ref, bi_ref, *outs):
        for ref, val in zip(outs, _s5_prep(ar_ref[...], ai_ref[...], dt_ref[...], br_ref[...], bi_ref[...])):
            ref[...] = val

    return pl.pallas_call(body, name="s5_prep_fwd",
                          out_shape=[_sds(a_re.shape), _sds(a_re.shape), _sds(b_re.shape), _sds(b_re.shape)],
                          )(a_re, a_im, log_dt, b_re, b_im)


def _s5_prep_bwd(a_re, a_im, log_dt, b_re, b_im, d_ar, d_ai, d_br, d_bi):
    def body(ar_ref, ai_ref, dt_ref, br_ref, bi_ref, g0, g1, g2, g3, *outs):
        _, vjp = jax.vjp(_s5_prep, ar_ref[...], ai_ref[...], dt_ref[...], br_ref[...], bi_ref[...])
        for ref, val in zip(outs, vjp((g0[...], g1[...], g2[...], g3[...]))):
            ref[...] = val

    return pl.pallas_call(body, name="s5_prep_bwd",
                          out_shape=[_sds(a_re.shape), _sds(a_re.shape), _sds(log_dt.shape), _sds(b_re.shape),
                                     _sds(b_re.shape)])(a_re, a_im, log_dt, b_re, b_im, d_ar, d_ai, d_br, d_bi)


def _s5_block_len(lp):
    return 128 if lp % 128 == 0 else 64


def _s5_powers(ar, ai, tb):
    out = []
    k = 1
    while k < tb:
        out.append((ar, ai))
        ar, ai = ar * ar - ai * ai, 2.0 * ar * ai
        k *= 2
    return out


def _s5_scan_rows(xr, xi, pows, reverse):
    tb = xr.shape[0]
    row = _iota2((tb, 1), 0)
    k = 1
    for pr, pi in pows:
        if reverse:
            keep = row < tb - k
            sr, si = pltpu.roll(xr, tb - k, 0), pltpu.roll(xi, tb - k, 0)
        else:
            keep = row >= k
            sr, si = pltpu.roll(xr, k, 0), pltpu.roll(xi, k, 0)
        sr, si = jnp.where(keep, sr, 0.0), jnp.where(keep, si, 0.0)
        xr, xi = xr + pr * sr - pi * si, xi + pr * si + pi * sr
        k *= 2
    return xr, xi


def _s5_slab_mm(x, w_ref, dims=NN):
    a = x.shape[1] // S5_SLABS
    return jnp.concatenate([_dot(x[:, j * a:(j + 1) * a], w_ref[j], dims) for j in range(S5_SLABS)], axis=1)


def _s5_power_table(ar, ai, tb, reverse):
    at = _iota2((tb, 1), 0) == (tb - 1 if reverse else 0)
    return _s5_scan_rows(jnp.where(at, ar, 0.0), jnp.where(at, ai, 0.0), _s5_powers(ar, ai, tb), reverse)


def _s5_states(u, carry_r, carry_i, ar, ai, b8r_ref, b8i_ref, pw_ref, tb):
    sr, si = _s5_scan_rows(_s5_slab_mm(u, b8r_ref), _s5_slab_mm(u, b8i_ref), _s5_powers(ar, ai, tb), False)
    pr, pi = pw_ref[0], pw_ref[1]
    return sr + pr * carry_r - pi * carry_i, si + pr * carry_i + pi * carry_r


def _s5_scan_fwd(u, ar, ai, b8r, b8i, c8r, c8i, dvec):
    lp = u.shape[0]
    tb = _s5_block_len(lp)
    nblk = lp // tb

    def body(u_ref, ar_ref, ai_ref, b8r_ref, b8i_ref, c8r_ref, c8i_ref, d_ref, y_ref, cin_ref, carry_s, pw_s):
        @pl.when(pl.program_id(0) == 0)
        def _():
            carry_s[...] = jnp.zeros_like(carry_s)
            pw_s[0], pw_s[1] = _s5_power_table(ar_ref[...], ai_ref[...], tb, False)

        cin_ref[0] = carry_s[...]
        uv = u_ref[...]
        sr, si = _s5_states(uv, carry_s[0:1], carry_s[1:2], ar_ref[...], ai_ref[...], b8r_ref, b8i_ref, pw_s, tb)
        carry_s[0:1] = sr[tb - 1:tb]
        carry_s[1:2] = si[tb - 1:tb]
        y_ref[...] = _s5_slab_mm(sr, c8r_ref) - _s5_slab_mm(si, c8i_ref) + d_ref[...] * uv

    const = lambda shape: _bs(shape, lambda i: (0,) * len(shape))
    return _pc(
        body, "s5_scan_fwd", (nblk,),
        [_bs((tb, S5_W), lambda i: (i, 0)), const((1, S5_N)), const((1, S5_N)),
         const((S5_SLABS, 128, 512)), const((S5_SLABS, 128, 512)),
         const((S5_SLABS, 512, 128)), const((S5_SLABS, 512, 128)), const((1, S5_W))],
        [_bs((tb, S5_W), lambda i: (i, 0)), _bs((1, 8, S5_N), lambda i: (i, 0, 0))],
        [_sds((lp, S5_W)), _sds((nblk, 8, S5_N))],
        scratch=[pltpu.VMEM((8, S5_N), f32), pltpu.VMEM((2, tb, S5_N), f32)])(u, ar, ai, b8r, b8i, c8r, c8i, dvec)


def _s5_scan_bwd(u, dy, cin, ar, ai, b8r, b8i, c8r, c8i, dvec):
    lp = u.shape[0]
    tb = _s5_block_len(lp)
    nblk = lp // tb

    def body(u_ref, dy_ref, cin_ref, ar_ref, ai_ref, b8r_ref, b8i_ref, c8r_ref, c8i_ref, d_ref,
             du_ref, dab_ref, db8r_ref, db8i_ref, dc8r_ref, dc8i_ref, dd_ref, lam_s, pw_s, qw_s):
        @pl.when(pl.program_id(0) == 0)
        def _():
            lam_s[...] = jnp.zeros_like(lam_s)
            for ref in (dab_ref, db8r_ref, db8i_ref, dc8r_ref, dc8i_ref, dd_ref):
                ref[...] = jnp.zeros_like(ref)
            pw_s[0], pw_s[1] = _s5_power_table(ar_ref[...], ai_ref[...], tb, False)
            qw_s[0], qw_s[1] = _s5_power_table(ar_ref[...], -ai_ref[...], tb, True)

        uv, dyv = u_ref[...], dy_ref[...]
        a_r, a_i = ar_ref[...], ai_ref[...]
        cin_r, cin_i = cin_ref[0, 0:1], cin_ref[0, 1:2]
        sr, si = _s5_states(uv, cin_r, cin_i, a_r, a_i, b8r_ref, b8i_ref, pw_s, tb)
        lr, li = _s5_scan_rows(_s5_slab_mm(dyv, c8r_ref, NT), -_s5_slab_mm(dyv, c8i_ref, NT),
                               _s5_powers(a_r, -a_i, tb), True)
        qr, qi = qw_s[0], qw_s[1]
        nr, ni = lam_s[0:1], lam_s[1:2]
        lr, li = lr + qr * nr - qi * ni, li + qr * ni + qi * nr
        lam_s[0:1] = lr[0:1]
        lam_s[1:2] = li[0:1]
        first = _iota2((tb, 1), 0) == 0
        pr = jnp.where(first, cin_r, pltpu.roll(sr, 1, 0))
        pi = jnp.where(first, cin_i, pltpu.roll(si, 1, 0))
        dab_ref[0:1] += jnp.sum(lr * pr + li * pi, axis=0, keepdims=True)
        dab_ref[1:2] += jnp.sum(li * pr - lr * pi, axis=0, keepdims=True)
        du_ref[...] = _s5_slab_mm(lr, b8r_ref, NT) + _s5_slab_mm(li, b8i_ref, NT) + d_ref[...] * dyv
        dd_ref[...] += jnp.sum(dyv * uv, axis=0, keepdims=True)
        for j in range(S5_SLABS):
            us, dys = uv[:, j * 128:(j + 1) * 128], dyv[:, j * 128:(j + 1) * 128]
            st = slice(j * 512, (j + 1) * 512)
            db8r_ref[j] += _dot(us, lr[:, st], TN)
            db8i_ref[j] += _dot(us, li[:, st], TN)
            dc8r_ref[j] += _dot(sr[:, st], dys, TN)
            dc8i_ref[j] -= _dot(si[:, st], dys, TN)

    const = lambda shape: _bs(shape, lambda i: (0,) * len(shape))
    rev = _bs((tb, S5_W), lambda i: (nblk - 1 - i, 0))
    return _pc(
        body, "s5_scan_bwd", (nblk,),
        [rev, rev, _bs((1, 8, S5_N), lambda i: (nblk - 1 - i, 0, 0)), const((1, S5_N)), const((1, S5_N)),
         const((S5_SLABS, 128, 512)), const((S5_SLABS, 128, 512)),
         const((S5_SLABS, 512, 128)), const((S5_SLABS, 512, 128)), const((1, S5_W))],
        [rev, const((8, S5_N)), const((S5_SLABS, 128, 512)), const((S5_SLABS, 128, 512)),
         const((S5_SLABS, 512, 128)), const((S5_SLABS, 512, 128)), const((1, S5_W))],
        [_sds((lp, S5_W)), _sds((8, S5_N)), _sds((S5_SLABS, 128, 512)), _sds((S5_SLABS, 128, 512)),
         _sds((S5_SLABS, 512, 128)), _sds((S5_SLABS, 512, 128)), _sds((1, S5_W))],
        scratch=[pltpu.VMEM((8, S5_N), f32), pltpu.VMEM((2, tb, S5_N), f32), pltpu.VMEM((2, tb, S5_N), f32)],
    )(u, dy, cin, ar, ai, b8r, b8i, c8r, c8i, dvec)


def _mix_tile(osb, odn, z, ys5, g_sb, g_dn, w_glu, b_glu, g_s5):
    bd = _block_diag_ones(QW, HEAD_DIM)
    tile4 = ((_iota2((HEAD_DIM, QW), 1) % HEAD_DIM) == _iota2((HEAD_DIM, QW), 0)).astype(f32)
    seg_rms = lambda x: x * lax.rsqrt(_dotx(x * x, bd) * (1.0 / HEAD_DIM) + EPS)
    sbn = seg_rms(osb) * _dotx(g_sb, tile4)
    dnn = seg_rms(odn) * _dotx(g_dn, tile4) * _silu(z)
    y = _gelu_tanh(ys5)
    glu = y * _sigmoid(_mm(y, w_glu) + b_glu)
    return jnp.concatenate([sbn, dnn, _rms(glu, g_s5)], axis=1)


def _mixout_fwd(h, osb, odn, z, ys5, g_sb, g_dn, w_glu, b_glu, g_s5, w_out):
    lp, d = h.shape
    tm = _token_tile(lp)

    def body(h_ref, osb_ref, odn_ref, z_ref, ys_ref, gsb_ref, gdn_ref, wg_ref, bg_ref, gs5_ref, wo_ref, o_ref):
        mixed = _mix_tile(osb_ref[...], odn_ref[...], z_ref[...], ys_ref[...], gsb_ref[...], gdn_ref[...],
                          wg_ref[...], bg_ref[...], gs5_ref[...])
        o_ref[...] = h_ref[...] + _dot(mixed, wo_ref[...])

    tok = lambda wd: _bs((tm, wd), lambda i: (i, 0))
    const = lambda shape: _bs(shape, lambda i: (0,) * len(shape))
    return _pc(
        body, "mixout_fwd", (lp // tm,),
        [tok(d), tok(QW), tok(QW), tok(QW), tok(S5_W), const((1, HEAD_DIM)), const((1, HEAD_DIM)),
         const((S5_W, S5_W)), const((1, S5_W)), const((1, S5_W)), const((d, d))],
        tok(d), _sds((lp, d)))(h, osb, odn, z, ys5, g_sb, g_dn, w_glu, b_glu, g_s5, w_out)


def _mixout_bwd(dh, osb, odn, z, ys5, g_sb, g_dn, w_glu, b_glu, g_s5, w_out):
    lp, d = dh.shape
    tm = _token_tile(lp)

    def body(dh_ref, osb_ref, odn_ref, z_ref, ys_ref, gsb_ref, gdn_ref, wg_ref, bg_ref, gs5_ref, wo_ref,
             dosb_ref, dodn_ref, dz_ref, dys_ref, dgsb_ref, dgdn_ref, dwg_ref, dbg_ref, dgs5_ref, dwo_ref):
        accs = (dgsb_ref, dgdn_ref, dwg_ref, dbg_ref, dgs5_ref)

        @pl.when(pl.program_id(0) == 0)
        def _():
            for ref in accs + (dwo_ref,):
                ref[...] = jnp.zeros_like(ref)

        mixed, vjp = jax.vjp(_mix_tile, osb_ref[...], odn_ref[...], z_ref[...], ys_ref[...], gsb_ref[...],
                             gdn_ref[...], wg_ref[...], bg_ref[...], gs5_ref[...])
        dhv = dh_ref[...]
        dwo_ref[...] += _dot(mixed, dhv, TN)
        grads = vjp(_dot(dhv, wo_ref[...], NT))
        for ref, val in zip((dosb_ref, dodn_ref, dz_ref, dys_ref), grads[:4]):
            ref[...] = val
        for ref, val in zip(accs, grads[4:]):
            ref[...] += val

    tok = lambda wd: _bs((tm, wd), lambda i: (i, 0))
    const = lambda shape: _bs(shape, lambda i: (0,) * len(shape))
    params = [const((1, HEAD_DIM)), const((1, HEAD_DIM)), const((S5_W, S5_W)), const((1, S5_W)), const((1, S5_W))]
    return _pc(
        body, "mixout_bwd", (lp // tm,),
        [tok(d), tok(QW), tok(QW), tok(QW), tok(S5_W)] + params + [const((d, d))],
        [tok(QW), tok(QW), tok(QW), tok(S5_W)] + params + [const((d, d))],
        [_sds((lp, QW))] * 3 + [_sds((lp, S5_W)), _sds((1, HEAD_DIM)), _sds((1, HEAD_DIM)), _sds((S5_W, S5_W)),
                                _sds((1, S5_W)), _sds((1, S5_W)), _sds((d, d))],
    )(dh, osb, odn, z, ys5, g_sb, g_dn, w_glu, b_glu, g_s5, w_out)


def _loss_fwd_bwd(h, g, target, n_real):
    lp, d = h.shape
    tm = _token_tile(lp)

    def body(h_ref, g_ref, t_ref, loss_ref, dh_ref, dg_ref):
        i = pl.program_id(0)

        @pl.when(i == 0)
        def _():
            loss_ref[...] = jnp.zeros_like(loss_ref)
            dg_ref[...] = jnp.zeros_like(dg_ref)

        pos = i * tm + _iota2((tm, 1), 0)
        real = ((pos >= N_META) & (pos < n_real)).astype(f32)
        y, vjp = jax.vjp(_rms, h_ref[...], g_ref[...])
        err = (y - t_ref[...]) * real
        loss_ref[...] += 0.5 * jnp.sum(jnp.mean(err * err, axis=1, keepdims=True))
        dx, dg = vjp(err * (1.0 / d))
        dh_ref[...] = dx
        dg_ref[...] += dg

    tok = _bs((tm, d), lambda i: (i, 0))
    return _pc(body, "loss_fwd_bwd", (lp // tm,), [tok, _bs((1, d), lambda i: (0, 0)), tok],
               [_bs((8, 128), lambda i: (0, 0)), tok, _bs((1, d), lambda i: (0, 0))],
               [_sds((8, 128)), _sds((lp, d)), _sds((1, d))])(h, g, target)


def _row_tile(rows):
    for t in (256, 128, 64, 32, 16, 8):
        if rows % t == 0:
            return t
    raise ValueError(rows)


def _adamw(w, g, m, v):
    rows = w.shape[0]
    tr = _row_tile(rows)

    def body(w_ref, g_ref, m_ref, v_ref, d_ref, mo_ref, vo_ref):
        gv = g_ref[...]
        m_new = ADAM_B1 * m_ref[...] + (1.0 - ADAM_B1) * gv
        v_new = ADAM_B2 * v_ref[...] + (1.0 - ADAM_B2) * (gv * gv)
        m_hat = m_new / (1.0 - ADAM_B1 ** ADAM_STEP)
        v_hat = v_new / (1.0 - ADAM_B2 ** ADAM_STEP)
        d_ref[...] = -ADAM_LR * (m_hat / (jnp.sqrt(v_hat) + ADAM_EPS) + ADAM_WD * w_ref[...])
        mo_ref[...] = m_new
        vo_ref[...] = v_new

    blk = _bs((tr, FLAT_W), lambda i: (i, 0))
    return _pc(body, "adamw", (rows // tr,), [blk] * 4, [blk] * 3, [_sds(w.shape)] * 3)(w, g, m, v)


def _sum_leading(x, name):
    n, rows, _ = x.shape
    tr = _row_tile(rows)

    def body(x_ref, o_ref):
        acc = x_ref[0]
        for k in range(1, n):
            acc = acc + x_ref[k]
        o_ref[...] = acc

    return _pc(body, name, (rows // tr,), [_bs((n, tr, FLAT_W), lambda i: (0, i, 0))],
               _bs((tr, FLAT_W), lambda i: (i, 0)), _sds((rows, FLAT_W)))(x)


def _pair_add(a, b, name):
    rows = a.shape[0]
    tr = _row_tile(rows)

    def body(a_ref, b_ref, o_ref):
        o_ref[...] = a_ref[...] + b_ref[...]

    blk = _bs((tr, FLAT_W), lambda i: (i, 0))
    return _pc(body, name, (rows // tr,), [blk, blk], blk, _sds(a.shape))(a, b)


_CHIP_FLIPS = ((1, 0, 0), (0, 1, 0), (1, 1, 0))
_ALL_FLIPS = tuple((a, b, c) for a in (0, 1) for b in (0, 1) for c in (0, 1) if (a, b, c) != (0, 0, 0))
_CORE_FLIP = ((0, 0, 1),)
_D2D_STREAMS = 4


def _exchange(name, arrays, out_shapes, flips, plan):
    n_in = len(arrays)

    def body(*refs):
        ins, outs = refs[:n_in], refs[n_in:n_in + len(out_shapes)]
        send_sems, recv_sems, local_sems = refs[n_in + len(out_shapes):]
        me = (lax.axis_index("x"), lax.axis_index("y"), lax.axis_index("c"))
        local = [pltpu.make_async_copy(s, d, local_sems.at[n]) for n, (s, d) in enumerate(plan(me, None, ins, outs))]
        for cp in local:
            cp.start()
        sent, k = [], 0
        for f in flips:
            peer = tuple(1 - m if fl else m for m, fl in zip(me, f))
            for s, d in plan(me, peer, ins, outs):
                cp = pltpu.make_async_remote_copy(src_ref=s, dst_ref=d, send_sem=send_sems.at[k],
                                                  recv_sem=recv_sems.at[k], device_id=peer,
                                                  device_id_type=pl.DeviceIdType.MESH)
                cp.start()
                sent.append(cp)
                k += 1
        for cp in sent:
            cp.wait_recv()
        for cp in sent:
            cp.wait_send()
        for cp in local:
            cp.wait()

    me0 = (0, 0, 0)
    n_remote = sum(len(_plan_count(plan, me0, f, arrays, out_shapes)) for f in flips)
    n_local = len(_plan_count(plan, me0, None, arrays, out_shapes))
    hbm = pl.BlockSpec(memory_space=pltpu.HBM)
    return pl.pallas_call(
        body, name=name, in_specs=[hbm] * n_in, out_specs=[hbm] * len(out_shapes), out_shape=list(out_shapes),
        scratch_shapes=[pltpu.SemaphoreType.DMA((n_remote,)), pltpu.SemaphoreType.DMA((n_remote,)),
                        pltpu.SemaphoreType.DMA((max(n_local, 1),))],
        compiler_params=pltpu.CompilerParams(has_side_effects=True))(*arrays)


class _FakeRef:
    def __init__(self):
        self.at = self

    def __getitem__(self, idx):
        return self


def _plan_count(plan, me, flip, arrays, out_shapes):
    peer = None if flip is None else me
    return plan(me, peer, [_FakeRef() for _ in arrays], [_FakeRef() for _ in out_shapes])


def _chip_index(dev):
    return 2 * dev[0] + dev[1]


def _all_gather_chips(name, arrays):
    n = len(arrays)

    def body(*refs):
        ins, outs = refs[:n], refs[n:2 * n]
        send_sems, recv_sems, local_sems = refs[2 * n:]
        x, y, c = lax.axis_index("x"), lax.axis_index("y"), lax.axis_index("c")
        sibling = (x, y, 1 - c)
        chips = [(1 - x, y), (x, 1 - y), (1 - x, 1 - y)]
        mine = 2 * x + y

        def copy(k, src, dst, to):
            return pltpu.make_async_remote_copy(src_ref=src, dst_ref=dst, send_sem=send_sems.at[k],
                                                recv_sem=recv_sems.at[k], device_id=to,
                                                device_id_type=pl.DeviceIdType.MESH)

        local = [pltpu.make_async_copy(ins[a], outs[a].at[mine], local_sems.at[a]) for a in range(n)]
        for cp in local:
            cp.start()
        first = [copy(j * n + a, ins[a].at[c], outs[a].at[mine, c], (*chip, c))
                 for j, chip in enumerate(chips) for a in range(n)]
        for cp in first:
            cp.start()
        passed = []
        for j, chip in enumerate(chips):
            for a in range(n):
                landed = outs[a].at[_chip_index(chip), c]
                copy(j * n + a, landed, landed, sibling).wait_recv()
                cp = copy(3 * n + j * n + a, landed, landed, sibling)
                cp.start()
                passed.append(cp)
        for j, chip in enumerate(chips):
            for a in range(n):
                other = outs[a].at[_chip_index(chip), 1 - c]
                copy(3 * n + j * n + a, other, other, sibling).wait_recv()
        for cp in first + passed:
            cp.wait_send()
        for cp in local:
            cp.wait()

    hbm = pl.BlockSpec(memory_space=pltpu.HBM)
    return pl.pallas_call(
        body, name=name, in_specs=[hbm] * n, out_specs=[hbm] * n,
        out_shape=[_sds((N_CHIPS,) + a.shape, a.dtype) for a in arrays],
        scratch_shapes=[pltpu.SemaphoreType.DMA((6 * n,)), pltpu.SemaphoreType.DMA((6 * n,)),
                        pltpu.SemaphoreType.DMA((n,))],
        compiler_params=pltpu.CompilerParams(has_side_effects=True))(*arrays)


def _all_gather_devices(name, arr):
    def plan(me, peer, ins, outs):
        return [(ins[0], outs[0].at[4 * me[0] + 2 * me[1] + me[2]])]

    return _exchange(name, [arr], [_sds((8,) + arr.shape, arr.dtype)], _ALL_FLIPS, plan)[0]


def _swap_half_with_sibling(name, g):
    step = g.shape[2] // _D2D_STREAMS

    def plan(me, peer, ins, outs):
        if peer is None:
            return []
        return [(ins[0].at[k, 1 - me[2], pl.ds(r * step, step)], outs[0].at[k, pl.ds(r * step, step)])
                for k in range(N_CHIPS) for r in range(_D2D_STREAMS)]

    return _exchange(name, [g], [_sds((N_CHIPS,) + g.shape[2:], g.dtype)], _CORE_FLIP, plan)[0]


def _scatter_to_chips(name, s):
    def plan(me, peer, ins, outs):
        to = me if peer is None else peer
        return [(ins[0].at[_chip_index(to)], outs[0].at[_chip_index(me)])]

    return _exchange(name, [s], [_sds(s.shape, s.dtype)], _CHIP_FLIPS, plan)[0]


def _share_with_sibling(name, r):
    step = r.shape[0] // (4 * _D2D_STREAMS)

    def plan(me, peer, ins, outs):
        return [(ins[0].at[pl.ds(n * step, step)], outs[0].at[me[2], pl.ds(n * step, step)])
                for n in range(4 * _D2D_STREAMS)]

    return _exchange(name, [r], [_sds((2,) + r.shape, r.dtype)], _CORE_FLIP, plan)[0]


def _to_heads(x):
    return x.reshape(x.shape[0], N_HEADS, HEAD_DIM).transpose(1, 0, 2)


def _from_heads(x):
    return x.transpose(1, 0, 2).reshape(x.shape[1], QW)


def _to_heads_t(x):
    return x.T.reshape(N_HEADS, HEAD_DIM, x.shape[0])


def _from_heads_t(x):
    return x.reshape(QW, x.shape[2]).T


def _shift_rows(x, k):
    if k == 0:
        return x
    z = jnp.zeros((abs(k),) + x.shape[1:], x.dtype)
    return jnp.concatenate([z, x[:-k]], axis=0) if k > 0 else jnp.concatenate([x[-k:], z], axis=0)


def _reorder_w_in(w):
    o = 3 * QW + 3 * QW + QW
    main = jnp.concatenate([w[:, :o], w[:, o + 2 * N_HEADS:]], axis=1)
    ba = jnp.pad(w[:, o:o + 2 * N_HEADS], ((0, 0), (0, 128 - 2 * N_HEADS)))
    return jnp.concatenate([main, ba], axis=1)


def _restore_w_in(w):
    o = 3 * QW + 3 * QW + QW
    return jnp.concatenate([w[:, :o], w[:, PROJ_W - 128:PROJ_W - 128 + 2 * N_HEADS], w[:, o:PROJ_W - 128]], axis=1)


def _slab_embed_b(b):
    x = b.reshape(S5_SLABS, 8, S5_C, S5_P)
    eye = jnp.eye(8, dtype=b.dtype)
    return (x[:, :, :, None, :] * eye[None, :, None, :, None]).reshape(S5_SLABS, 8 * S5_C, 8 * S5_P)


def _slab_extract_b(m):
    x = m.reshape(S5_SLABS, 8, S5_C, 8, S5_P)
    return jnp.stack([x[:, g, :, g, :] for g in range(8)], axis=1).reshape(S5_G, S5_C, S5_P)


def _slab_embed_c(c):
    x = c.reshape(S5_SLABS, 8, S5_C, S5_P).transpose(0, 1, 3, 2)
    eye = jnp.eye(8, dtype=c.dtype)
    return (x[:, :, :, None, :] * eye[None, :, None, :, None]).reshape(S5_SLABS, 8 * S5_P, 8 * S5_C)


def _slab_extract_c(m):
    x = m.reshape(S5_SLABS, 8, S5_P, 8, S5_C)
    return jnp.stack([x[:, g, :, g, :] for g in range(8)], axis=1).transpose(0, 1, 3, 2).reshape(S5_G, S5_C, S5_P)


def _piece_rows(shape):
    return -(-math.prod(shape) // (8 * FLAT_W)) * 8


def _pack_rows(parts, lead, row_align):
    rows = []
    for p in parts:
        flat = p.reshape(lead + (-1,))
        r = _piece_rows(p.shape[len(lead):])
        flat = jnp.pad(flat, [(0, 0)] * len(lead) + [(0, r * FLAT_W - flat.shape[-1])])
        rows.append(flat.reshape(lead + (r, FLAT_W)))
    total = sum(r.shape[-2] for r in rows)
    rows.append(jnp.zeros(lead + ((-total) % row_align, FLAT_W), parts[0].dtype))
    return jnp.concatenate(rows, axis=len(lead))


def _unpack_rows(flat, shapes):
    out, off = [], 0
    for s in shapes:
        r = _piece_rows(s)
        out.append(flat[off:off + r].reshape(-1)[:math.prod(s)].reshape(s))
        off += r
    return out


_SHARDED = ("ffn1_w_gate", "ffn1_w_up", "ffn1_w_down", "w_in", "s5_w_glu", "w_out",
            "ffn2_w_gate", "ffn2_w_up", "ffn2_w_down", "meta_tokens", "dn_conv_w")
_MATMUL_W = _SHARDED[:9]
_WEIGHTS = ("meta_tokens", "ffn1_norm", "ffn1_w_gate", "ffn1_w_up", "ffn1_w_down", "mix_norm", "w_in", "sb_out_norm",
            "dn_conv_w", "dn_a_log", "dn_dt_bias", "dn_out_norm", "s5_a_re", "s5_a_im", "s5_log_dt", "s5_b_re",
            "s5_b_im", "s5_c_re", "s5_c_im", "s5_d", "s5_w_glu", "s5_b_glu", "s5_out_norm", "w_out", "ffn2_norm",
            "ffn2_w_gate", "ffn2_w_up", "ffn2_w_down", "final_norm")
_REPLICATED = tuple(n for n in _WEIGHTS if n not in _SHARDED)


def _chip_major(name, g):
    if name in ("ffn1_w_gate", "ffn1_w_up", "ffn2_w_gate", "ffn2_w_up", "ffn1_w_down", "ffn2_w_down"):
        return g.transpose(1, 0, 2, 3)
    if name == "w_in":
        return g.reshape(2, D_MODEL, N_CHIPS, IN_WIDTH // N_CHIPS).transpose(2, 0, 1, 3)
    if name in ("w_out", "s5_w_glu"):
        return g.reshape(2, N_CHIPS, g.shape[1] // N_CHIPS, g.shape[2]).transpose(1, 0, 2, 3)
    if name == "meta_tokens":
        return g.reshape(N_META, N_CHIPS, D_MODEL // N_CHIPS).transpose(1, 0, 2)
    if name == "dn_conv_w":
        return g.reshape(2, DN_CONV, N_CHIPS, 3 * QW // N_CHIPS).transpose(2, 0, 1, 3)
    raise ValueError(name)


def _layer_forward(h, p):
    lp = h.shape[0]
    h1, *ffn1_saved = _ffn_fwd(h, p["ffn1_norm"], *p["ffn1"])
    q, k, v, dnx, z, u, ba = _inproj_fwd(h1, p["mix_norm"], p["w_in"].astype(MXU_DTYPE))
    qh, kh, vh = _to_heads(q), _to_heads_t(k), _to_heads_t(v)
    osb, sb_w, sb_sig = _sb_fwd(qh, kh, vh)
    xs = jnp.stack([_shift_rows(dnx, s) for s in range(DN_CONV)])
    dq_, dk_, dv_, bg = _dn_pre_fwd(xs, ba, p["cw"], p["pv"])
    dqh, dkh, dvh = _to_heads(dq_), _to_heads(dk_), _to_heads(dv_)
    brow = bg[:, :N_HEADS].T.reshape(N_HEADS, lp // DN_CHUNK, 1, DN_CHUNK)
    grow = bg[:, N_HEADS:2 * N_HEADS].T.reshape(N_HEADS, lp // DN_CHUNK, 1, DN_CHUNK)
    odn, states = _dn_scan_fwd(dqh, dkh, dvh, grow, brow)
    ar, ai, bre, bim = _s5_prep_fwd(p["s5_a_re"], p["s5_a_im"], p["s5_log_dt"], p["s5_b_re"], p["s5_b_im"])
    s5t = (ar.reshape(1, S5_N), ai.reshape(1, S5_N), _slab_embed_b(bre), _slab_embed_b(bim),
           _slab_embed_c(p["s5_c_re"]), _slab_embed_c(p["s5_c_im"]), p["s5_d"])
    ys5, cin = _s5_scan_fwd(u, *s5t)
    osb_t, odn_t = _from_heads(osb), _from_heads(odn)
    h2 = _mixout_fwd(h1, osb_t, odn_t, z, ys5, p["sb_out_norm"], p["dn_out_norm"], p["s5_w_glu"], p["s5_b_glu"],
                     p["s5_out_norm"], p["w_out"].astype(MXU_DTYPE))
    h3, *ffn2_saved = _ffn_fwd(h2, p["ffn2_norm"], *p["ffn2"])
    saved = dict(h0=h, h1=h1, h2=h2, ffn1=ffn1_saved, ffn2=ffn2_saved, qh=qh, kh=kh, vh=vh, sb_w=sb_w, sb_sig=sb_sig, xs=xs, ba=ba, dqh=dqh, dkh=dkh, dvh=dvh,
                 grow=grow, brow=brow, states=states, odn_t=odn_t, osb_t=osb_t, z=z, u=u, ys5=ys5, cin=cin, s5t=s5t)
    return h3, saved


def _ffn_backward(h, g, w3, dy, fwd_saved):
    xn, gate, up = fwd_saved
    dh, dgn, d_gate, d_up, act = _ffn_bwd_dx(h, g, *w3, dy, gate, up)
    dwg, dwu, dwd = _ffn_bwd_dw(xn, dy, d_gate, d_up, act)
    return dh, dgn, dwg, dwu, dwd


def _layer_backward(dh3, p, s):
    lp = dh3.shape[0]
    g = {}
    dh2, g["ffn2_norm"], g["ffn2_w_gate"], g["ffn2_w_up"], g["ffn2_w_down"] = _ffn_backward(
        s["h2"], p["ffn2_norm"], p["ffn2"], dh3, s["ffn2"])
    (dosb_t, dodn_t, dz, dys5, g["sb_out_norm"], g["dn_out_norm"], g["s5_w_glu"], g["s5_b_glu"], g["s5_out_norm"],
     g["w_out"]) = _mixout_bwd(dh2, s["osb_t"], s["odn_t"], s["z"], s["ys5"], p["sb_out_norm"], p["dn_out_norm"],
                               p["s5_w_glu"], p["s5_b_glu"], p["s5_out_norm"], p["w_out"].astype(MXU_DTYPE))
    du, dab, db8r, db8i, dc8r, dc8i, g["s5_d"] = _s5_scan_bwd(s["u"], dys5, s["cin"], *s["s5t"])
    g["s5_c_re"], g["s5_c_im"] = _slab_extract_c(dc8r), _slab_extract_c(dc8i)
    g["s5_a_re"], g["s5_a_im"], g["s5_log_dt"], g["s5_b_re"], g["s5_b_im"] = _s5_prep_bwd(
        p["s5_a_re"], p["s5_a_im"], p["s5_log_dt"], p["s5_b_re"], p["s5_b_im"],
        dab[0].reshape(S5_G, S5_P), dab[1].reshape(S5_G, S5_P), _slab_extract_b(db8r), _slab_extract_b(db8i))
    ddq, ddk, ddv, dgrow, dbrow = _dn_scan_bwd(s["dqh"], s["dkh"], s["dvh"], s["grow"], s["brow"], s["states"],
                                               _to_heads(dodn_t))
    dbg = jnp.concatenate([dbrow.reshape(N_HEADS, lp).T, dgrow.reshape(N_HEADS, lp).T,
                           jnp.zeros((lp, 128 - 2 * N_HEADS), f32)], axis=1)
    dxs, dba, g["cw"], g["pv"] = _dn_pre_bwd(s["xs"], s["ba"], p["cw"], p["pv"], _from_heads(ddq), _from_heads(ddk),
                                             _from_heads(ddv), dbg)
    ddn4 = jnp.stack([_shift_rows(dxs[k], -k) for k in range(DN_CONV)])
    dq, dk_t, dv_t = _sb_bwd(s["qh"], s["kh"], s["vh"], s["sb_w"], s["sb_sig"], _to_heads(dosb_t))
    dh1, g["mix_norm"], g["w_in"] = _inproj_bwd(s["h1"], p["mix_norm"], p["w_in"].astype(MXU_DTYPE), dh2,
                                                _from_heads(dq), _from_heads_t(dk_t), _from_heads_t(dv_t), ddn4,
                                                dz, du, dba)
    dh0, g["ffn1_norm"], g["ffn1_w_gate"], g["ffn1_w_up"], g["ffn1_w_down"] = _ffn_backward(
        s["h0"], p["ffn1_norm"], p["ffn1"], dh1, s["ffn1"])
    return dh0, g


def kernel(x, meta_tokens, ffn1_norm, ffn1_w_gate, ffn1_w_up, ffn1_w_down, mix_norm, w_in, sb_out_norm, dn_conv_w, dn_a_log, dn_dt_bias, dn_out_norm, s5_a_re, s5_a_im, s5_log_dt, s5_b_re, s5_b_im, s5_c_re, s5_c_im, s5_d, s5_w_glu, s5_b_glu, s5_out_norm, w_out, ffn2_norm, ffn2_w_gate, ffn2_w_up, ffn2_w_down, final_norm, loss_target, m_meta_tokens, m_ffn1_norm, m_ffn1_w_gate, m_ffn1_w_up, m_ffn1_w_down, m_mix_norm, m_w_in, m_sb_out_norm, m_dn_conv_w, m_dn_a_log, m_dn_dt_bias, m_dn_out_norm, m_s5_a_re, m_s5_a_im, m_s5_log_dt, m_s5_b_re, m_s5_b_im, m_s5_c_re, m_s5_c_im, m_s5_d, m_s5_w_glu, m_s5_b_glu, m_s5_out_norm, m_w_out, m_ffn2_norm, m_ffn2_w_gate, m_ffn2_w_up, m_ffn2_w_down, m_final_norm, v_meta_tokens, v_ffn1_norm, v_ffn1_w_gate, v_ffn1_w_up, v_ffn1_w_down, v_mix_norm, v_w_in, v_sb_out_norm, v_dn_conv_w, v_dn_a_log, v_dn_dt_bias, v_dn_out_norm, v_s5_a_re, v_s5_a_im, v_s5_log_dt, v_s5_b_re, v_s5_b_im, v_s5_c_re, v_s5_c_im, v_s5_d, v_s5_w_glu, v_s5_b_glu, v_s5_out_norm, v_w_out, v_ffn2_norm, v_ffn2_w_gate, v_ffn2_w_up, v_ffn2_w_down, v_final_norm):
    args = dict(locals())
    w = {n: args[n] for n in _WEIGHTS}
    m = {n: args["m_" + n] for n in _WEIGHTS}
    v = {n: args["v_" + n] for n in _WEIGHTS}
    depth = ffn1_norm.shape[0]
    seq = x.shape[1]
    n_real = N_META + seq
    lp = _padded_len(n_real)

    gathered = _all_gather_chips("gather_weights", [w[n].astype(MXU_DTYPE) for n in _MATMUL_W]
                                 + [w["meta_tokens"].reshape(2, N_META // 2, -1), w["dn_conv_w"]])
    full = dict(zip(_MATMUL_W + ("meta_tokens", "dn_conv_w"), gathered))
    meta_full = full["meta_tokens"].reshape(N_CHIPS, N_META, -1).transpose(1, 0, 2).reshape(N_META, D_MODEL)
    conv_full = full["dn_conv_w"].transpose(1, 2, 0, 3).reshape(depth, DN_CONV, 3 * QW)
    w_in_full = full["w_in"].transpose(1, 2, 0, 3).reshape(depth, D_MODEL, IN_WIDTH)
    w_out_full = full["w_out"].transpose(1, 0, 2, 3).reshape(depth, D_MODEL, D_MODEL)
    w_glu_full = full["s5_w_glu"].transpose(1, 0, 2, 3).reshape(depth, S5_W, S5_W)

    layers = []
    for l in range(depth):
        pv = jnp.pad(jnp.stack([dn_a_log[l], dn_dt_bias[l]]), ((0, 6), (N_HEADS, 128 - 2 * N_HEADS)))
        layers.append(dict(
            ffn1_norm=ffn1_norm[l][None], mix_norm=mix_norm[l][None], ffn2_norm=ffn2_norm[l][None],
            ffn1=(full["ffn1_w_gate"][:, l], full["ffn1_w_up"][:, l], full["ffn1_w_down"][:, l]),
            ffn2=(full["ffn2_w_gate"][:, l], full["ffn2_w_up"][:, l], full["ffn2_w_down"][:, l]),
            w_in=_reorder_w_in(w_in_full[l]), w_out=w_out_full[l], s5_w_glu=w_glu_full[l].astype(f32),
            cw=jnp.pad(conv_full[l], ((0, 8 - DN_CONV), (0, 0))), pv=pv,
            sb_out_norm=sb_out_norm[l][None], dn_out_norm=dn_out_norm[l][None],
            s5_a_re=s5_a_re[l], s5_a_im=s5_a_im[l], s5_log_dt=s5_log_dt[l][:, None],
            s5_b_re=s5_b_re[l].transpose(0, 2, 1), s5_b_im=s5_b_im[l].transpose(0, 2, 1),
            s5_c_re=s5_c_re[l], s5_c_im=s5_c_im[l], s5_d=s5_d[l][None], s5_b_glu=s5_b_glu[l][None],
            s5_out_norm=s5_out_norm[l][None]))

    tail = jnp.zeros((lp - n_real, D_MODEL), f32)
    h = jnp.concatenate([meta_full, x[0], tail], axis=0)
    target = jnp.concatenate([jnp.zeros((N_META, D_MODEL), f32), loss_target[0], tail], axis=0)
    saved = []
    for p in layers:
        h, s = _layer_forward(h, p)
        saved.append(s)
    loss_blk, dh, d_final = _loss_fwd_bwd(h, final_norm[None], target, n_real)
    grads = [None] * depth
    for l in reversed(range(depth)):
        dh, grads[l] = _layer_backward(dh, layers[l], saved[l])
    loss = lax.psum(loss_blk[0, 0], ("x", "y", "c"))
    grad_x = dh[N_META:n_real][None]

    stack = lambda name: jnp.stack([grads[l][name] for l in range(depth)])
    gfull = {n: stack(n) for n in ("ffn1_w_gate", "ffn1_w_up", "ffn1_w_down", "s5_w_glu", "w_out", "ffn2_w_gate",
                                   "ffn2_w_up", "ffn2_w_down")}
    gfull["w_in"] = jnp.stack([_restore_w_in(grads[l]["w_in"]) for l in range(depth)])
    gfull["meta_tokens"] = dh[:N_META]
    gfull["dn_conv_w"] = jnp.stack([grads[l]["cw"][:DN_CONV] for l in range(depth)])
    grep = {n: stack(n).reshape(w[n].shape) for n in ("ffn1_norm", "mix_norm", "sb_out_norm", "dn_out_norm", "s5_a_re",
                                                      "s5_a_im", "s5_log_dt", "s5_c_re", "s5_c_im", "s5_d", "s5_b_glu",
                                                      "s5_out_norm", "ffn2_norm")}
    grep["s5_b_re"] = jnp.stack([grads[l]["s5_b_re"].transpose(0, 2, 1) for l in range(depth)])
    grep["s5_b_im"] = jnp.stack([grads[l]["s5_b_im"].transpose(0, 2, 1) for l in range(depth)])
    grep["dn_a_log"] = jnp.stack([grads[l]["pv"][0, N_HEADS:2 * N_HEADS] for l in range(depth)])
    grep["dn_dt_bias"] = jnp.stack([grads[l]["pv"][1, N_HEADS:2 * N_HEADS] for l in range(depth)])
    grep["final_norm"] = d_final[0]

    shard_shapes = [w[n].shape for n in _SHARDED]
    g_big = _pack_rows([_chip_major(n, gfull[n]) for n in _SHARDED], (N_CHIPS,), BIG_ROWS)
    half_rows = g_big.shape[1] // 2
    g_big = g_big.reshape(N_CHIPS, 2, half_rows, FLAT_W)
    c = lax.axis_index("c")
    mine = lax.dynamic_index_in_dim(g_big, c, axis=1, keepdims=False)
    theirs = _swap_half_with_sibling("grad_pair_swap", g_big)
    pair = _pair_add(mine.reshape(-1, FLAT_W), theirs.reshape(-1, FLAT_W), "grad_pair_add")
    arrived = _scatter_to_chips("grad_scatter", pair.reshape(N_CHIPS, half_rows, FLAT_W))
    reduced = _sum_leading(arrived, "grad_chip_sum")
    g_shard = _share_with_sibling("grad_share", reduced).reshape(-1, FLAT_W)

    rep_shapes = [w[n].shape for n in _REPLICATED]
    g_small = _pack_rows([grep[n] for n in _REPLICATED], (), 64)
    g_rep = _sum_leading(_all_gather_devices("grad_small_gather", g_small), "grad_small_sum")

    pack = lambda d, names, align: _pack_rows([d[n] for n in names], (), align)
    out = {}
    for names, shapes, g_flat, align in ((_SHARDED, shard_shapes, g_shard, BIG_ROWS),
                                         (_REPLICATED, rep_shapes, g_rep, 64)):
        delta, m_new, v_new = _adamw(pack(w, names, align), g_flat, pack(m, names, align), pack(v, names, align))
        for kind, flat in (("grad", g_flat), ("delta", delta), ("new_m", m_new), ("new_v", v_new)):
            for n, a in zip(names, _unpack_rows(flat, shapes)):
                out[kind + "_" + n] = a
    return (loss, grad_x, *[out[k + "_" + n] for k in ("grad", "delta", "new_m", "new_v") for n in _WEIGHTS])
```

```python
import functools
import math

import jax
import jax.numpy as jnp
from jax import lax
from jax.experimental import pallas as pl
from jax.experimental.pallas import tpu as pltpu

f32 = jnp.float32
MXU_DTYPE = jnp.bfloat16
HI = lax.Precision.HIGHEST
NN = (((1,), (0,)), ((), ()))
NT = (((1,), (1,)), ((), ()))
TN = (((0,), (0,)), ((), ()))

EPS = 1e-6
D_MODEL = 1024
N_META = 16
HEAD_DIM = 64
N_HEADS = 4
QW = N_HEADS * HEAD_DIM
DN_CONV = 4
DN_CHUNK = 64
S5_W = 512
S5_G = 32
S5_P = 64
S5_C = 16
S5_N = S5_G * S5_P
S5_SLABS = 4
N_CHIPS = 4
PROJ_W = 2432
IN_WIDTH = 2312
VMEM_LIMIT = 56 * 1024 * 1024

ADAM_LR, ADAM_B1, ADAM_B2, ADAM_EPS, ADAM_WD, ADAM_STEP = 0.001, 0.9, 0.999, 1e-08, 0.01, 10
FLAT_W = 1024
BIG_ROWS = 512


def _dot(a, b, dims=NN):
    return lax.dot_general(a.astype(MXU_DTYPE), b.astype(MXU_DTYPE), dims, preferred_element_type=f32)


def _dotx(a, b, dims=NN):
    return lax.dot_general(a, b, dims, precision=HI, preferred_element_type=f32)


def _split(x):
    if MXU_DTYPE == f32:
        return x, None
    hi = x.astype(MXU_DTYPE)
    return hi, (x - hi.astype(f32)).astype(MXU_DTYPE)


def _dot_split(hi, lo, u01):
    if lo is None:
        return _dotx(hi, u01)
    u = u01.astype(MXU_DTYPE)
    return (lax.dot_general(hi, u, NN, preferred_element_type=f32)
            + lax.dot_general(lo, u, NN, preferred_element_type=f32))


def _dot3(a, b, dims=NN):
    ah, al = _split(a)
    if al is None:
        return _dotx(a, b, dims)
    bh, bl = _split(b)
    d = lambda x, y: lax.dot_general(x, y, dims, preferred_element_type=f32)
    return d(ah, bh) + d(ah, bl) + d(al, bh)


BNN = (((2,), (1,)), ((0,), (0,)))
BNT = (((2,), (2,)), ((0,), (0,)))
BTN = (((1,), (1,)), ((0,), (0,)))


def _with_dot_vjp(dot, kind, batched=False):
    nn, nt, tn = (BNN, BNT, BTN) if batched else (NN, NT, TN)
    dims = {"nn": nn, "nt": nt, "tn": tn}[kind]

    @jax.custom_vjp
    def f(a, b):
        return dot(a, b, dims)

    def fwd(a, b):
        return dot(a, b, dims), (a, b)

    def bwd(res, dy):
        a, b = res
        if kind == "nn":
            return dot(dy, b, nt), dot(a, dy, tn)
        if kind == "nt":
            return dot(dy, b, nn), dot(dy, a, tn)
        return dot(b, dy, nt), dot(a, dy, nn)

    f.defvjp(fwd, bwd)
    return f


_mm = _with_dot_vjp(_dot, "nn")
_bmm = _with_dot_vjp(_dot, "nn", True)
_bmm_nt = _with_dot_vjp(_dot, "nt", True)
_bmm_tn = _with_dot_vjp(_dot, "tn", True)
_bmm3 = _with_dot_vjp(_dot3, "nn", True)


def _rms(x, g):
    return x * lax.rsqrt(jnp.mean(x * x, axis=-1, keepdims=True) + EPS) * g


def _sigmoid(x):
    return 1.0 / (1.0 + jnp.exp(-x))


def _silu(x):
    return x * _sigmoid(x)


def _softplus(x):
    return jnp.maximum(x, 0.0) + jnp.log(1.0 + jnp.exp(-jnp.abs(x)))


def _gelu_tanh(x):
    return 0.5 * x * (1.0 + jnp.tanh(math.sqrt(2.0 / math.pi) * (x + 0.044715 * x * x * x)))


def _iota2(shape, axis):
    return lax.broadcasted_iota(jnp.int32, shape, axis)


def _block_diag_ones(n, blk):
    return ((_iota2((n, n), 0) // blk) == (_iota2((n, n), 1) // blk)).astype(f32)


def _pc(body, name, grid, in_specs, out_specs, out_shape, scratch=(), vmem=VMEM_LIMIT):
    return pl.pallas_call(
        body, name=name, grid=grid, in_specs=in_specs, out_specs=out_specs, out_shape=out_shape,
        scratch_shapes=list(scratch),
        compiler_params=pltpu.CompilerParams(dimension_semantics=("arbitrary",) * len(grid), vmem_limit_bytes=vmem))


def _bs(shape, imap):
    return pl.BlockSpec(shape, imap)


def _sds(shape, dtype=f32):
    return jax.ShapeDtypeStruct(tuple(shape), dtype)


def _token_tile(lp, cap=640):
    for t in (640, 320, 256, 128, 64):
        if t <= cap and lp % t == 0:
            return t
    raise ValueError(lp)


def _padded_len(l):
    return -(-l // 1280) * 1280 if l > 4096 else -(-l // 256) * 256


def _ffn_fwd(h, g, wg, wu, wd):
    lp, d = h.shape
    nch, _, fc = wg.shape
    tm = _token_tile(lp)

    def body(h_ref, g_ref, wg_ref, wu_ref, wd_ref, o_ref, xn_ref, gate_ref, up_ref, xn_s, acc_s):
        j = pl.program_id(1)

        @pl.when(j == 0)
        def _():
            xn_s[...] = _rms(h_ref[...], g_ref[...]).astype(xn_s.dtype)
            acc_s[...] = jnp.zeros_like(acc_s)

        xn = xn_s[...]
        gate = _dot(xn, wg_ref[0])
        up = _dot(xn, wu_ref[0])
        gate_ref[0] = gate.astype(gate_ref.dtype)
        up_ref[0] = up.astype(up_ref.dtype)
        acc_s[...] += _dot(_silu(gate) * up, wd_ref[0])

        @pl.when(j == nch - 1)
        def _():
            o_ref[...] = h_ref[...] + 0.5 * acc_s[...]
            xn_ref[...] = xn_s[...]

    tok = _bs((tm, d), lambda i, j: (i, 0))
    chunk = _bs((1, tm, fc), lambda i, j: (j, i, 0))
    return _pc(
        body, "ffn_fwd", (lp // tm, nch),
        [tok, _bs((1, d), lambda i, j: (0, 0)),
         _bs((1, d, fc), lambda i, j: (j, 0, 0)), _bs((1, d, fc), lambda i, j: (j, 0, 0)),
         _bs((1, fc, d), lambda i, j: (j, 0, 0))],
        [tok, tok, chunk, chunk],
        [_sds((lp, d)), _sds((lp, d), MXU_DTYPE), _sds((nch, lp, fc), MXU_DTYPE), _sds((nch, lp, fc), MXU_DTYPE)],
        scratch=[pltpu.VMEM((tm, d), MXU_DTYPE), pltpu.VMEM((tm, d), f32)])(h, g, wg, wu, wd)


def _ffn_bwd_dx(h, g, wg, wu, wd, dy, gate_saved, up_saved):
    lp, d = h.shape
    nch, _, fc = wg.shape
    tm = _token_tile(lp)

    def body(h_ref, g_ref, wg_ref, wu_ref, wd_ref, dy_ref, gate_ref, up_ref, dh_ref, dgn_ref, dg_ref, du_ref, act_ref,
             dxn_s, dout_s):
        i, j = pl.program_id(0), pl.program_id(1)

        @pl.when((i == 0) & (j == 0))
        def _():
            dgn_ref[...] = jnp.zeros_like(dgn_ref)

        @pl.when(j == 0)
        def _():
            dxn_s[...] = jnp.zeros_like(dxn_s)
            dout_s[...] = (0.5 * dy_ref[...]).astype(dout_s.dtype)

        gate = gate_ref[0].astype(f32)
        up = up_ref[0].astype(f32)
        sig = _sigmoid(gate)
        sl = gate * sig
        dact = _dot(dout_s[...], wd_ref[0], NT)
        d_up = dact * sl
        d_gate = dact * up * sig * (1.0 + gate * (1.0 - sig))
        dg_ref[0] = d_gate.astype(dg_ref.dtype)
        du_ref[0] = d_up.astype(du_ref.dtype)
        act_ref[0] = (sl * up).astype(act_ref.dtype)
        dxn_s[...] += _dot(d_gate, wg_ref[0], NT) + _dot(d_up, wu_ref[0], NT)

        @pl.when(j == nch - 1)
        def _():
            _, vjp = jax.vjp(_rms, h_ref[...], g_ref[...])
            dx, dg = vjp(dxn_s[...])
            dh_ref[...] = dy_ref[...] + dx
            dgn_ref[...] += dg

    tok = _bs((tm, d), lambda i, j: (i, 0))
    chunk = _bs((1, tm, fc), lambda i, j: (j, i, 0))
    return _pc(
        body, "ffn_bwd_dx", (lp // tm, nch),
        [tok, _bs((1, d), lambda i, j: (0, 0)),
         _bs((1, d, fc), lambda i, j: (j, 0, 0)), _bs((1, d, fc), lambda i, j: (j, 0, 0)),
         _bs((1, fc, d), lambda i, j: (j, 0, 0)), tok, chunk, chunk],
        [tok, _bs((1, d), lambda i, j: (0, 0)), chunk, chunk, chunk],
        [_sds((lp, d)), _sds((1, d)),
         _sds((nch, lp, fc), MXU_DTYPE), _sds((nch, lp, fc), MXU_DTYPE), _sds((nch, lp, fc), MXU_DTYPE)],
        scratch=[pltpu.VMEM((tm, d), f32), pltpu.VMEM((tm, d), MXU_DTYPE)],
    )(h, g, wg, wu, wd, dy, gate_saved, up_saved)


def _ffn_bwd_dw(xn, dy, d_gate, d_up, act):
    lp, d = xn.shape
    nch, _, fc = d_gate.shape
    tm = _token_tile(lp)

    def body(xn_ref, dy_ref, dg_ref, du_ref, act_ref, dwg_ref, dwu_ref, dwd_ref):
        @pl.when(pl.program_id(1) == 0)
        def _():
            dwg_ref[...] = jnp.zeros_like(dwg_ref)
            dwu_ref[...] = jnp.zeros_like(dwu_ref)
            dwd_ref[...] = jnp.zeros_like(dwd_ref)

        xn_t = xn_ref[...]
        dwg_ref[0] += _dot(xn_t, dg_ref[0], TN)
        dwu_ref[0] += _dot(xn_t, du_ref[0], TN)
        dwd_ref[0] += _dot(act_ref[0], 0.5 * dy_ref[...], TN)

    tok = _bs((tm, d), lambda j, i: (i, 0))
    chunk = _bs((1, tm, fc), lambda j, i: (j, i, 0))
    return _pc(
        body, "ffn_bwd_dw", (nch, lp // tm), [tok, tok, chunk, chunk, chunk],
        [_bs((1, d, fc), lambda j, i: (j, 0, 0)), _bs((1, d, fc), lambda j, i: (j, 0, 0)),
         _bs((1, fc, d), lambda j, i: (j, 0, 0))],
        [_sds((nch, d, fc)), _sds((nch, d, fc)), _sds((nch, fc, d))])(xn, dy, d_gate, d_up, act)


_PROJ_SPLITS = (QW, QW, QW, 3 * QW, QW, S5_W, 128)


def _inproj_fwd(h, g, w):
    lp, d = h.shape
    tm = _token_tile(lp)

    def body(h_ref, g_ref, w_ref, *outs):
        proj = _dot(_rms(h_ref[...], g_ref[...]), w_ref[...])
        off = 0
        for ref, wd in zip(outs, _PROJ_SPLITS):
            ref[...] = proj[:, off:off + wd]
            off += wd

    return _pc(
        body, "inproj_fwd", (lp // tm,),
        [_bs((tm, d), lambda i: (i, 0)), _bs((1, d), lambda i: (0, 0)), _bs((d, PROJ_W), lambda i: (0, 0))],
        [_bs((tm, wd), lambda i: (i, 0)) for wd in _PROJ_SPLITS],
        [_sds((lp, wd)) for wd in _PROJ_SPLITS])(h, g, w)


def _inproj_bwd(h, g, w, dres, dq, dk, dv, ddn4, dz, du, dba):
    lp, d = h.shape
    tm = _token_tile(lp, 320)

    def body(h_ref, g_ref, w_ref, dres_ref, dq_ref, dk_ref, dv_ref, ddn_ref, dz_ref, du_ref, dba_ref,
             dh_ref, dgn_ref, dw_ref):
        @pl.when(pl.program_id(0) == 0)
        def _():
            dgn_ref[...] = jnp.zeros_like(dgn_ref)
            dw_ref[...] = jnp.zeros_like(dw_ref)

        ddn = ddn_ref[0] + ddn_ref[1] + ddn_ref[2] + ddn_ref[3]
        dproj = jnp.concatenate(
            [dq_ref[...], dk_ref[...], dv_ref[...], ddn, dz_ref[...], du_ref[...], dba_ref[...]], axis=1)
        xn, vjp = jax.vjp(_rms, h_ref[...], g_ref[...])
        dx, dg = vjp(_dot(dproj, w_ref[...], NT))
        dw_ref[...] += _dot(xn, dproj, TN)
        dh_ref[...] = dres_ref[...] + dx
        dgn_ref[...] += dg

    tok = lambda wd: _bs((tm, wd), lambda i: (i, 0))
    return _pc(
        body, "inproj_bwd", (lp // tm,),
        [tok(d), _bs((1, d), lambda i: (0, 0)), _bs((d, PROJ_W), lambda i: (0, 0)), tok(d),
         tok(QW), tok(QW), tok(QW), _bs((4, tm, 3 * QW), lambda i: (0, i, 0)), tok(QW), tok(S5_W), tok(128)],
        [tok(d), _bs((1, d), lambda i: (0, 0)), _bs((d, PROJ_W), lambda i: (0, 0))],
        [_sds((lp, d)), _sds((1, d)), _sds((d, PROJ_W))])(h, g, w, dres, dq, dk, dv, ddn4, dz, du, dba)


_SB_TQ = 256
_SB_ROWS = 32
_SB_GROUP = 4
_SB_ROWS_BWD = 32
_SB_GROUP_BWD = 4


def _sb_pieces(z, valid):
    t = jnp.exp(-jnp.abs(z))
    sp = jnp.maximum(z, 0.0) + jnp.log(1.0 + t)
    lk = -sp if valid is None else jnp.where(valid, -sp, 0.0)
    return t, sp, lk


def _cat_rows(parts):
    return parts[0] if len(parts) == 1 else jnp.concatenate(parts, axis=0)


def _sb_fwd(q, kt, vt):
    nh, lp, hd = q.shape
    tq = tk = min(_SB_TQ, lp)
    blocks = [slice(r, r + _SB_ROWS) for r in range(0, tq, _SB_ROWS)]

    def body(q_ref, k_ref, v_ref, o_ref, w_hbm, s_hbm, wbuf, sbuf, sems):
        head, qi = pl.program_id(0), pl.program_id(1)
        qv = q_ref[0]
        u_strict = (_iota2((tk, tk), 0) > _iota2((tk, tk), 1)).astype(f32)
        below = _iota2((tq, tk), 1) < _iota2((tq, tk), 0)

        spare = lambda slot, t: lp // tk + slot * _SB_GROUP + t

        def save(slot, t, j):
            return [pltpu.make_async_copy(wbuf.at[slot, t], w_hbm.at[head, qi, j], sems.at[0, slot, t]),
                    pltpu.make_async_copy(sbuf.at[slot, t], s_hbm.at[head, qi, j], sems.at[1, slot, t])]

        def drain(slot):
            for t in range(_SB_GROUP):
                for cp in save(slot, t, spare(slot, t)):
                    cp.wait()

        def idle(slot, t):
            wbuf[slot, t] = jnp.zeros((tq, tk), MXU_DTYPE)
            sbuf[slot, t] = jnp.zeros((tq, tk), MXU_DTYPE)
            for cp in save(slot, t, spare(slot, t)):
                cp.start()

        def tiles(js, carry, slot, masked=False, live=None, first=False):
            if not first:
                drain(slot)
            o_acc, c_after = carry
            kss = [pl.ds(pl.multiple_of(j * tk, tk), tk) for j in js]
            z_alls = [_dot(qv, k_ref[0, :, ks]) * (HEAD_DIM ** -0.5) for ks in kss]
            stage, afters = [], []
            for t, z_all in enumerate(z_alls):
                his, los, logs, sums = [], [], [], []
                for rs in blocks:
                    z = z_all[rs]
                    _, sp, lk = _sb_pieces(z, below[rs] if masked else None)
                    if live is not None:
                        lk = lk * live[t]
                    hi, lo = _split(lk)
                    his.append(hi)
                    los.append(lo)
                    logs.append(z - sp)
                    sums.append(jnp.sum(lk, axis=1, keepdims=True))
                stage.append((logs, _cat_rows(sums)))
                afters.append(_dot_split(_cat_rows(his), None if los[0] is None else _cat_rows(los), u_strict))
            for t, ((logs, sums), after_all) in enumerate(zip(stage, afters)):
                ws, sigs = [], []
                for n, rs in enumerate(blocks):
                    w = jnp.exp(logs[n] + after_all[rs] + c_after[rs])
                    sig = jnp.exp(logs[n])
                    if masked:
                        w, sig = jnp.where(below[rs], w, 0.0), jnp.where(below[rs], sig, 0.0)
                    if live is not None:
                        w = w * live[t]
                    ws.append(w.astype(MXU_DTYPE))
                    sigs.append(sig.astype(MXU_DTYPE))
                w_all = _cat_rows(ws)
                wbuf[slot, t] = w_all
                sbuf[slot, t] = _cat_rows(sigs)
                for cp in save(slot, t, js[t] if live is None else jnp.where(live[t] > 0.0, js[t], spare(slot, t))):
                    cp.start()
                o_acc = o_acc + _dot(w_all, v_ref[0, :, kss[t]], NT)
                c_after = c_after + sums
            for t in range(len(js), _SB_GROUP):
                idle(slot, t)
            return o_acc, c_after

        n_groups, rest = qi // _SB_GROUP, qi % _SB_GROUP
        group = lambda g, c: tiles([qi - 1 - _SB_GROUP * g - n for n in range(_SB_GROUP)], c, (g + 1) % 2)

        def last_group(_, c):
            idx = [rest - 1 - n for n in range(_SB_GROUP)]
            return tiles([jnp.maximum(j, 0) for j in idx], c, (n_groups + 1) % 2,
                         live=[(j >= 0).astype(f32) for j in idx])

        for t in range(_SB_GROUP):
            idle(1, t)
        carry = tiles([qi], (jnp.zeros((tq, hd), f32), jnp.zeros((tq, 1), f32)), 0, masked=True, first=True)
        carry = lax.fori_loop(0, n_groups, group, carry)
        o_acc, _ = lax.fori_loop(0, jnp.minimum(rest, 1), last_group, carry)
        drain(0)
        drain(1)
        o_ref[0] = o_acc

    full_t = _bs((1, hd, lp), lambda h, i: (h, 0, 0))
    hbm = pl.BlockSpec(memory_space=pltpu.HBM)
    return _pc(
        body, "sb_fwd", (nh, lp // tq),
        [_bs((1, tq, hd), lambda h, i: (h, i, 0)), full_t, full_t],
        [_bs((1, tq, hd), lambda h, i: (h, i, 0)), hbm, hbm],
        [_sds((nh, lp, hd))] + [_sds((nh, lp // tq, lp // tk + 2 * _SB_GROUP, tq, tk), MXU_DTYPE)] * 2,
        scratch=[pltpu.VMEM((2, _SB_GROUP, tq, tk), MXU_DTYPE), pltpu.VMEM((2, _SB_GROUP, tq, tk), MXU_DTYPE),
                 pltpu.SemaphoreType.DMA((2, 2, _SB_GROUP))])(q, kt, vt)


def _sb_bwd(q, kt, vt, w_saved, s_saved, do):
    nh, lp, hd = q.shape
    tq = tk = min(_SB_TQ, lp)
    blocks = [slice(r, r + _SB_ROWS_BWD) for r in range(0, tq, _SB_ROWS_BWD)]
    grp = _SB_GROUP_BWD

    def body(q_ref, k_ref, v_ref, w_hbm, s_hbm, do_ref, dq_ref, dk_ref, dv_ref, wbuf, sbuf, sems):
        head, qi = pl.program_id(0), pl.program_id(1)

        @pl.when(qi == 0)
        def _():
            dk_ref[...] = jnp.zeros_like(dk_ref)
            dv_ref[...] = jnp.zeros_like(dv_ref)

        qv, dov = q_ref[0], do_ref[0]
        u_excl = (_iota2((tk, tk), 0) < _iota2((tk, tk), 1)).astype(f32)
        scale = HEAD_DIM ** -0.5

        def loads(js, slot):
            out = []
            for t, j in enumerate(js):
                out += [pltpu.make_async_copy(w_hbm.at[head, qi, j], wbuf.at[slot, t], sems.at[0, slot, t]),
                        pltpu.make_async_copy(s_hbm.at[head, qi, j], sbuf.at[slot, t], sems.at[1, slot, t])]
            return out

        def tiles(js, slot, carry):
            dq_acc, c_e = carry
            kss = [pl.ds(pl.multiple_of(j * tk, tk), tk) for j in js]
            dw_alls = [_dot(dov, v_ref[0, :, ks]) for ks in kss]
            stage, befores = [], []
            for t, dw_all in enumerate(dw_alls):
                es, ebs, esums = [], [], []
                for rs in blocks:
                    e = wbuf[slot, t, rs].astype(f32) * dw_all[rs]
                    es.append(e)
                    ebs.append(e.astype(MXU_DTYPE))
                    esums.append(jnp.sum(e, axis=1, keepdims=True))
                stage.append((es, _cat_rows(esums)))
                befores.append(_dot(_cat_rows(ebs), u_excl))
            for t, ((es, esums), before_all, ks) in enumerate(zip(stage, befores, kss)):
                dzs = []
                for n, rs in enumerate(blocks):
                    sig = sbuf[slot, t, rs].astype(f32)
                    dz = es[n] * (1.0 - sig) - sig * (c_e[rs] + before_all[rs])
                    dzs.append((dz * scale).astype(MXU_DTYPE))
                dz_all = _cat_rows(dzs)
                c_e = c_e + esums
                dk_ref[0, :, ks] += _dot(qv, dz_all, TN)
                dv_ref[0, :, ks] += _dot(dov, wbuf[slot, t], TN)
                dq_acc = dq_acc + _dot(dz_all, k_ref[0, :, ks], NT)
            return dq_acc, c_e

        n_tiles = qi + 1
        n_groups = n_tiles // grp
        group_js = lambda g: [grp * g + t for t in range(grp)]

        @pl.when(n_groups > 0)
        def _():
            for cp in loads(group_js(0), 0):
                cp.start()

        def group(g, carry):
            slot = g % 2

            @pl.when(g + 1 < n_groups)
            def _():
                for cp in loads(group_js(g + 1), 1 - slot):
                    cp.start()

            for cp in loads(group_js(g), slot):
                cp.wait()
            return tiles(group_js(g), slot, carry)

        def single(j, carry):
            for cp in loads([j], 0):
                cp.start()
            for cp in loads([j], 0):
                cp.wait()
            return tiles([j], 0, carry)

        carry = lax.fori_loop(0, n_groups, group, (jnp.zeros((tq, hd), f32), jnp.zeros((tq, 1), f32)))
        dq_acc, _ = lax.fori_loop(n_groups * grp, n_tiles, single, carry)
        dq_ref[0] = dq_acc

    tile_spec = _bs((1, tq, hd), lambda h, i: (h, i, 0))
    full_t = _bs((1, hd, lp), lambda h, i: (h, 0, 0))
    hbm = pl.BlockSpec(memory_space=pltpu.HBM)
    return _pc(
        body, "sb_bwd", (nh, lp // tq),
        [tile_spec, full_t, full_t, hbm, hbm, tile_spec],
        [tile_spec, full_t, full_t], [_sds((nh, lp, hd)), _sds((nh, hd, lp)), _sds((nh, hd, lp))],
        scratch=[pltpu.VMEM((2, grp, tq, tk), MXU_DTYPE), pltpu.VMEM((2, grp, tq, tk), MXU_DTYPE),
                 pltpu.SemaphoreType.DMA((2, 2, grp))])(q, kt, vt, w_saved, s_saved, do)


def _dn_pre_tile(xs, ba, cw, pv):
    conv = xs[0] * cw[3:4] + xs[1] * cw[2:3] + xs[2] * cw[1:2] + xs[3] * cw[0:1]
    s = _silu(conv)
    bd = _block_diag_ones(QW, HEAD_DIM)
    sq, sk, sv = s[:, :QW], s[:, QW:2 * QW], s[:, 2 * QW:]
    qn = sq * lax.rsqrt(_dotx(sq * sq, bd) + EPS)
    kn = sk * lax.rsqrt(_dotx(sk * sk, bd) + EPS)
    lane = _iota2(ba.shape, 1)
    beta = _sigmoid(ba)
    g = -jnp.exp(pv[0:1]) * _softplus(ba + pv[1:2])
    bg = jnp.where(lane < N_HEADS, beta, jnp.where(lane < 2 * N_HEADS, g, 0.0))
    return qn, kn, sv, bg


def _dn_pre_fwd(xs, ba, cw, pv):
    _, lp, w3 = xs.shape
    tm = _token_tile(lp, 320)

    def body(xs_ref, ba_ref, cw_ref, pv_ref, q_ref, k_ref, v_ref, bg_ref):
        qn, kn, sv, bg = _dn_pre_tile(xs_ref[...], ba_ref[...], cw_ref[...], pv_ref[...])
        q_ref[...], k_ref[...], v_ref[...], bg_ref[...] = qn, kn, sv, bg

    tok = lambda wd: _bs((tm, wd), lambda i: (i, 0))
    return _pc(
        body, "dn_pre_fwd", (lp // tm,),
        [_bs((4, tm, w3), lambda i: (0, i, 0)), tok(128), _bs((8, w3), lambda i: (0, 0)), _bs((8, 128), lambda i: (0, 0))],
        [tok(QW), tok(QW), tok(QW), tok(128)],
        [_sds((lp, QW)), _sds((lp, QW)), _sds((lp, QW)), _sds((lp, 128))])(xs, ba, cw, pv)


def _dn_pre_bwd(xs, ba, cw, pv, dq, dk, dv, dbg):
    _, lp, w3 = xs.shape
    tm = _token_tile(lp, 320)

    def body(xs_ref, ba_ref, cw_ref, pv_ref, dq_ref, dk_ref, dv_ref, dbg_ref, dxs_ref, dba_ref, dcw_ref, dpv_ref):
        @pl.when(pl.program_id(0) == 0)
        def _():
            dcw_ref[...] = jnp.zeros_like(dcw_ref)
            dpv_ref[...] = jnp.zeros_like(dpv_ref)

        _, vjp = jax.vjp(_dn_pre_tile, xs_ref[...], ba_ref[...], cw_ref[...], pv_ref[...])
        dxs, dba, dcw, dpv = vjp((dq_ref[...], dk_ref[...], dv_ref[...], dbg_ref[...]))
        dxs_ref[...] = dxs
        dba_ref[...] = dba
        dcw_ref[...] += dcw
        dpv_ref[...] += dpv

    tok = lambda wd: _bs((tm, wd), lambda i: (i, 0))
    xs_spec = _bs((4, tm, w3), lambda i: (0, i, 0))
    cw_spec, pv_spec = _bs((8, w3), lambda i: (0, 0)), _bs((8, 128), lambda i: (0, 0))
    return _pc(
        body, "dn_pre_bwd", (lp // tm,),
        [xs_spec, tok(128), cw_spec, pv_spec, tok(QW), tok(QW), tok(QW), tok(128)],
        [xs_spec, tok(128), cw_spec, pv_spec],
        [_sds((4, lp, w3)), _sds((lp, 128)), _sds((8, w3)), _sds((8, 128))])(xs, ba, cw, pv, dq, dk, dv, dbg)


def _dn_chunk(state, q, k, v, grow, brow):
    nh, c, _ = q.shape
    ii, jj = _iota2((c, c), 0), _iota2((c, c), 1)
    eye = ii == jj
    col = lambda row: jnp.sum(jnp.where(eye, jnp.broadcast_to(row, (nh, c, c)), 0.0), axis=2, keepdims=True)
    gc_row = _dotx(grow, jnp.broadcast_to((ii <= jj).astype(f32), (nh, c, c)), BNN)
    gc_col, b_col = col(gc_row), col(brow)
    decay = jnp.exp(jnp.where(ii >= jj, gc_col - gc_row, -1e30))
    kb = k * b_col
    p = -jnp.where(ii > jj, _bmm_nt(kb, k) * decay, 0.0)
    t_inv = eye.astype(f32) + p
    for _ in range(5):
        p = _bmm3(p, p)
        t_inv = t_inv + _bmm3(t_inv, p)
    egc = jnp.exp(gc_col)
    u = _bmm(t_inv, v * b_col)
    w = _bmm(t_inv, kb * egc)
    qs = q * (q.shape[2] ** -0.5)
    attn = jnp.where(ii >= jj, _bmm_nt(qs, k) * decay, 0.0)
    v_new = u - _bmm(w, state)
    o = _bmm(qs * egc, state) + _bmm(attn, v_new)
    g_last = gc_row[:, :, c - 1:c]
    new_state = state * jnp.exp(g_last) + _bmm_tn(k * jnp.exp(g_last - gc_col), v_new)
    return new_state, o


def _dn_scan_fwd(q, k, v, grow, brow):
    nh, lp, hd = q.shape
    c = DN_CHUNK
    n = lp // c

    def body(q_ref, k_ref, v_ref, g_ref, b_ref, o_ref, st_ref, state_s):
        @pl.when(pl.program_id(0) == 0)
        def _():
            state_s[...] = jnp.zeros_like(state_s)

        st_ref[:, 0] = state_s[...]
        state, o = _dn_chunk(state_s[...], q_ref[...], k_ref[...], v_ref[...], g_ref[:, 0], b_ref[:, 0])
        state_s[...] = state
        o_ref[...] = o

    seq = _bs((nh, c, hd), lambda i: (0, i, 0))
    row = _bs((nh, 1, 1, c), lambda i: (0, i, 0, 0))
    return _pc(body, "dn_scan_fwd", (n,), [seq, seq, seq, row, row],
               [seq, _bs((nh, 1, hd, hd), lambda i: (0, i, 0, 0))],
               [_sds((nh, lp, hd)), _sds((nh, n, hd, hd))],
               scratch=[pltpu.VMEM((nh, hd, hd), f32)])(q, k, v, grow, brow)


def _dn_scan_bwd(q, k, v, grow, brow, states, do):
    nh, lp, hd = q.shape
    c = DN_CHUNK
    n = lp // c

    def body(q_ref, k_ref, v_ref, g_ref, b_ref, st_ref, do_ref, dq_ref, dk_ref, dv_ref, dg_ref, db_ref, dstate_s):
        @pl.when(pl.program_id(0) == 0)
        def _():
            dstate_s[...] = jnp.zeros_like(dstate_s)

        _, vjp = jax.vjp(_dn_chunk, st_ref[:, 0], q_ref[...], k_ref[...], v_ref[...], g_ref[:, 0], b_ref[:, 0])
        dstate, dq, dk, dv, dg, db = vjp((dstate_s[...], do_ref[...]))
        dstate_s[...] = dstate
        dq_ref[...], dk_ref[...], dv_ref[...] = dq, dk, dv
        dg_ref[:, 0], db_ref[:, 0] = dg, db

    seq = _bs((nh, c, hd), lambda i: (0, n - 1 - i, 0))
    row = _bs((nh, 1, 1, c), lambda i: (0, n - 1 - i, 0, 0))
    return _pc(body, "dn_scan_bwd", (n,),
               [seq, seq, seq, row, row, _bs((nh, 1, hd, hd), lambda i: (0, n - 1 - i, 0, 0)), seq],
               [seq, seq, seq, row, row],
               [_sds((nh, lp, hd))] * 3 + [_sds((nh, n, 1, c))] * 2,
               scratch=[pltpu.VMEM((nh, hd, hd), f32)])(q, k, v, grow, brow, states, do)


def _s5_prep(a_re, a_im, log_dt, b_re, b_im):
    dt = jnp.exp(log_dt)
    mag = jnp.exp(a_re * dt)
    ar, ai = mag * jnp.cos(a_im * dt), mag * jnp.sin(a_im * dt)
    den = a_re * a_re + a_im * a_im
    cr = ((ar - 1.0) * a_re + ai * a_im) / den
    ci = (ai * a_re - (ar - 1.0) * a_im) / den
    cr3, ci3 = cr[:, None, :], ci[:, None, :]
    return ar, ai, cr3 * b_re - ci3 * b_im, cr3 * b_im + ci3 * b_re


def _s5_prep_fwd(a_re, a_im, log_dt, b_re, b_im):
    def body(ar_ref, ai_ref, dt_ref, br_ref, bi_ref, *outs):
        for ref, val in zip(outs, _s5_prep(ar_ref[...], ai_ref[...], dt_ref[...], br_ref[...], bi_ref[...])):
            ref[...] = val

    return pl.pallas_call(body, name="s5_prep_fwd",
                          out_shape=[_sds(a_re.shape), _sds(a_re.shape), _sds(b_re.shape), _sds(b_re.shape)],
                          )(a_re, a_im, log_dt, b_re, b_im)


def _s5_prep_bwd(a_re, a_im, log_dt, b_re, b_im, d_ar, d_ai, d_br, d_bi):
    def body(ar_ref, ai_ref, dt_ref, br_ref, bi_ref, g0, g1, g2, g3, *outs):
        _, vjp = jax.vjp(_s5_prep, ar_ref[...], ai_ref[...], dt_ref[...], br_ref[...], bi_ref[...])
        for ref, val in zip(outs, vjp((g0[...], g1[...], g2[...], g3[...]))):
            ref[...] = val

    return pl.pallas_call(body, name="s5_prep_bwd",
                          out_shape=[_sds(a_re.shape), _sds(a_re.shape), _sds(log_dt.shape), _sds(b_re.shape),
                                     _sds(b_re.shape)])(a_re, a_im, log_dt, b_re, b_im, d_ar, d_ai, d_br, d_bi)


def _s5_block_len(lp):
    return 128 if lp % 128 == 0 else 64


def _s5_powers(ar, ai, tb):
    out = []
    k = 1
    while k < tb:
        out.append((ar, ai))
        ar, ai = ar * ar - ai * ai, 2.0 * ar * ai
        k *= 2
    return out


def _s5_scan_rows(xr, xi, pows, reverse):
    tb = xr.shape[0]
    row = _iota2((tb, 1), 0)
    k = 1
    for pr, pi in pows:
        if reverse:
            keep = row < tb - k
            sr, si = pltpu.roll(xr, tb - k, 0), pltpu.roll(xi, tb - k, 0)
        else:
            keep = row >= k
            sr, si = pltpu.roll(xr, k, 0), pltpu.roll(xi, k, 0)
        sr, si = jnp.where(keep, sr, 0.0), jnp.where(keep, si, 0.0)
        xr, xi = xr + pr * sr - pi * si, xi + pr * si + pi * sr
        k *= 2
    return xr, xi


def _s5_slab_mm(x, w_ref, dims=NN):
    a = x.shape[1] // S5_SLABS
    return jnp.concatenate([_dot(x[:, j * a:(j + 1) * a], w_ref[j], dims) for j in range(S5_SLABS)], axis=1)


def _s5_power_table(ar, ai, tb, reverse):
    at = _iota2((tb, 1), 0) == (tb - 1 if reverse else 0)
    return _s5_scan_rows(jnp.where(at, ar, 0.0), jnp.where(at, ai, 0.0), _s5_powers(ar, ai, tb), reverse)


def _s5_states(u, carry_r, carry_i, ar, ai, b8r_ref, b8i_ref, pw_ref, tb):
    sr, si = _s5_scan_rows(_s5_slab_mm(u, b8r_ref), _s5_slab_mm(u, b8i_ref), _s5_powers(ar, ai, tb), False)
    pr, pi = pw_ref[0], pw_ref[1]
    return sr + pr * carry_r - pi * carry_i, si + pr * carry_i + pi * carry_r


def _s5_scan_fwd(u, ar, ai, b8r, b8i, c8r, c8i, dvec):
    lp = u.shape[0]
    tb = _s5_block_len(lp)
    nblk = lp // tb

    def body(u_ref, ar_ref, ai_ref, b8r_ref, b8i_ref, c8r_ref, c8i_ref, d_ref, y_ref, cin_ref, carry_s, pw_s):
        @pl.when(pl.program_id(0) == 0)
        def _():
            carry_s[...] = jnp.zeros_like(carry_s)
            pw_s[0], pw_s[1] = _s5_power_table(ar_ref[...], ai_ref[...], tb, False)

        cin_ref[0] = carry_s[...]
        uv = u_ref[...]
        sr, si = _s5_states(uv, carry_s[0:1], carry_s[1:2], ar_ref[...], ai_ref[...], b8r_ref, b8i_ref, pw_s, tb)
        carry_s[0:1] = sr[tb - 1:tb]
        carry_s[1:2] = si[tb - 1:tb]
        y_ref[...] = _s5_slab_mm(sr, c8r_ref) - _s5_slab_mm(si, c8i_ref) + d_ref[...] * uv

    const = lambda shape: _bs(shape, lambda i: (0,) * len(shape))
    return _pc(
        body, "s5_scan_fwd", (nblk,),
        [_bs((tb, S5_W), lambda i: (i, 0)), const((1, S5_N)), const((1, S5_N)),
         const((S5_SLABS, 128, 512)), const((S5_SLABS, 128, 512)),
         const((S5_SLABS, 512, 128)), const((S5_SLABS, 512, 128)), const((1, S5_W))],
        [_bs((tb, S5_W), lambda i: (i, 0)), _bs((1, 8, S5_N), lambda i: (i, 0, 0))],
        [_sds((lp, S5_W)), _sds((nblk, 8, S5_N))],
        scratch=[pltpu.VMEM((8, S5_N), f32), pltpu.VMEM((2, tb, S5_N), f32)])(u, ar, ai, b8r, b8i, c8r, c8i, dvec)


def _s5_scan_bwd(u, dy, cin, ar, ai, b8r, b8i, c8r, c8i, dvec):
    lp = u.shape[0]
    tb = _s5_block_len(lp)
    nblk = lp // tb

    def body(u_ref, dy_ref, cin_ref, ar_ref, ai_ref, b8r_ref, b8i_ref, c8r_ref, c8i_ref, d_ref,
             du_ref, dab_ref, db8r_ref, db8i_ref, dc8r_ref, dc8i_ref, dd_ref, lam_s, pw_s, qw_s):
        @pl.when(pl.program_id(0) == 0)
        def _():
            lam_s[...] = jnp.zeros_like(lam_s)
            for ref in (dab_ref, db8r_ref, db8i_ref, dc8r_ref, dc8i_ref, dd_ref):
                ref[...] = jnp.zeros_like(ref)
            pw_s[0], pw_s[1] = _s5_power_table(ar_ref[...], ai_ref[...], tb, False)
            qw_s[0], qw_s[1] = _s5_power_table(ar_ref[...], -ai_ref[...], tb, True)

        uv, dyv = u_ref[...], dy_ref[...]
        a_r, a_i = ar_ref[...], ai_ref[...]
        cin_r, cin_i = cin_ref[0, 0:1], cin_ref[0, 1:2]
        sr, si = _s5_states(uv, cin_r, cin_i, a_r, a_i, b8r_ref, b8i_ref, pw_s, tb)
        lr, li = _s5_scan_rows(_s5_slab_mm(dyv, c8r_ref, NT), -_s5_slab_mm(dyv, c8i_ref, NT),
                               _s5_powers(a_r, -a_i, tb), True)
        qr, qi = qw_s[0], qw_s[1]
        nr, ni = lam_s[0:1], lam_s[1:2]
        lr, li = lr + qr * nr - qi * ni, li + qr * ni + qi * nr
        lam_s[0:1] = lr[0:1]
        lam_s[1:2] = li[0:1]
        first = _iota2((tb, 1), 0) == 0
        pr = jnp.where(first, cin_r, pltpu.roll(sr, 1, 0))
        pi = jnp.where(first, cin_i, pltpu.roll(si, 1, 0))
        dab_ref[0:1] += jnp.sum(lr * pr + li * pi, axis=0, keepdims=True)
        dab_ref[1:2] += jnp.sum(li * pr - lr * pi, axis=0, keepdims=True)
        du_ref[...] = _s5_slab_mm(lr, b8r_ref, NT) + _s5_slab_mm(li, b8i_ref, NT) + d_ref[...] * dyv
        dd_ref[...] += jnp.sum(dyv * uv, axis=0, keepdims=True)
        for j in range(S5_SLABS):
            us, dys = uv[:, j * 128:(j + 1) * 128], dyv[:, j * 128:(j + 1) * 128]
            st = slice(j * 512, (j + 1) * 512)
            db8r_ref[j] += _dot(us, lr[:, st], TN)
            db8i_ref[j] += _dot(us, li[:, st], TN)
            dc8r_ref[j] += _dot(sr[:, st], dys, TN)
            dc8i_ref[j] -= _dot(si[:, st], dys, TN)

    const = lambda shape: _bs(shape, lambda i: (0,) * len(shape))
    rev = _bs((tb, S5_W), lambda i: (nblk - 1 - i, 0))
    return _pc(
        body, "s5_scan_bwd", (nblk,),
        [rev, rev, _bs((1, 8, S5_N), lambda i: (nblk - 1 - i, 0, 0)), const((1, S5_N)), const((1, S5_N)),
         const((S5_SLABS, 128, 512)), const((S5_SLABS, 128, 512)),
         const((S5_SLABS, 512, 128)), const((S5_SLABS, 512, 128)), const((1, S5_W))],
        [rev, const((8, S5_N)), const((S5_SLABS, 128, 512)), const((S5_SLABS, 128, 512)),
         const((S5_SLABS, 512, 128)), const((S5_SLABS, 512, 128)), const((1, S5_W))],
        [_sds((lp, S5_W)), _sds((8, S5_N)), _sds((S5_SLABS, 128, 512)), _sds((S5_SLABS, 128, 512)),
         _sds((S5_SLABS, 512, 128)), _sds((S5_SLABS, 512, 128)), _sds((1, S5_W))],
        scratch=[pltpu.VMEM((8, S5_N), f32), pltpu.VMEM((2, tb, S5_N), f32), pltpu.VMEM((2, tb, S5_N), f32)],
    )(u, dy, cin, ar, ai, b8r, b8i, c8r, c8i, dvec)


def _mix_tile(osb, odn, z, ys5, g_sb, g_dn, w_glu, b_glu, g_s5):
    bd = _block_diag_ones(QW, HEAD_DIM)
    tile4 = ((_iota2((HEAD_DIM, QW), 1) % HEAD_DIM) == _iota2((HEAD_DIM, QW), 0)).astype(f32)
    seg_rms = lambda x: x * lax.rsqrt(_dotx(x * x, bd) * (1.0 / HEAD_DIM) + EPS)
    sbn = seg_rms(osb) * _dotx(g_sb, tile4)
    dnn = seg_rms(odn) * _dotx(g_dn, tile4) * _silu(z)
    y = _gelu_tanh(ys5)
    glu = y * _sigmoid(_mm(y, w_glu) + b_glu)
    return jnp.concatenate([sbn, dnn, _rms(glu, g_s5)], axis=1)


def _mixout_fwd(h, osb, odn, z, ys5, g_sb, g_dn, w_glu, b_glu, g_s5, w_out):
    lp, d = h.shape
    tm = _token_tile(lp)

    def body(h_ref, osb_ref, odn_ref, z_ref, ys_ref, gsb_ref, gdn_ref, wg_ref, bg_ref, gs5_ref, wo_ref, o_ref):
        mixed = _mix_tile(osb_ref[...], odn_ref[...], z_ref[...], ys_ref[...], gsb_ref[...], gdn_ref[...],
                          wg_ref[...], bg_ref[...], gs5_ref[...])
        o_ref[...] = h_ref[...] + _dot(mixed, wo_ref[...])

    tok = lambda wd: _bs((tm, wd), lambda i: (i, 0))
    const = lambda shape: _bs(shape, lambda i: (0,) * len(shape))
    return _pc(
        body, "mixout_fwd", (lp // tm,),
        [tok(d), tok(QW), tok(QW), tok(QW), tok(S5_W), const((1, HEAD_DIM)), const((1, HEAD_DIM)),
         const((S5_W, S5_W)), const((1, S5_W)), const((1, S5_W)), const((d, d))],
        tok(d), _sds((lp, d)))(h, osb, odn, z, ys5, g_sb, g_dn, w_glu, b_glu, g_s5, w_out)


def _mixout_bwd(dh, osb, odn, z, ys5, g_sb, g_dn, w_glu, b_glu, g_s5, w_out):
    lp, d = dh.shape
    tm = _token_tile(lp)

    def body(dh_ref, osb_ref, odn_ref, z_ref, ys_ref, gsb_ref, gdn_ref, wg_ref, bg_ref, gs5_ref, wo_ref,
             dosb_ref, dodn_ref, dz_ref, dys_ref, dgsb_ref, dgdn_ref, dwg_ref, dbg_ref, dgs5_ref, dwo_ref):
        accs = (dgsb_ref, dgdn_ref, dwg_ref, dbg_ref, dgs5_ref)

        @pl.when(pl.program_id(0) == 0)
        def _():
            for ref in accs + (dwo_ref,):
                ref[...] = jnp.zeros_like(ref)

        mixed, vjp = jax.vjp(_mix_tile, osb_ref[...], odn_ref[...], z_ref[...], ys_ref[...], gsb_ref[...],
                             gdn_ref[...], wg_ref[...], bg_ref[...], gs5_ref[...])
        dhv = dh_ref[...]
        dwo_ref[...] += _dot(mixed, dhv, TN)
        grads = vjp(_dot(dhv, wo_ref[...], NT))
        for ref, val in zip((dosb_ref, dodn_ref, dz_ref, dys_ref), grads[:4]):
            ref[...] = val
        for ref, val in zip(accs, grads[4:]):
            ref[...] += val

    tok = lambda wd: _bs((tm, wd), lambda i: (i, 0))
    const = lambda shape: _bs(shape, lambda i: (0,) * len(shape))
    params = [const((1, HEAD_DIM)), const((1, HEAD_DIM)), const((S5_W, S5_W)), const((1, S5_W)), const((1, S5_W))]
    return _pc(
        body, "mixout_bwd", (lp // tm,),
        [tok(d), tok(QW), tok(QW), tok(QW), tok(S5_W)] + params + [const((d, d))],
        [tok(QW), tok(QW), tok(QW), tok(S5_W)] + params + [const((d, d))],
        [_sds((lp, QW))] * 3 + [_sds((lp, S5_W)), _sds((1, HEAD_DIM)), _sds((1, HEAD_DIM)), _sds((S5_W, S5_W)),
                                _sds((1, S5_W)), _sds((1, S5_W)), _sds((d, d))],
    )(dh, osb, odn, z, ys5, g_sb, g_dn, w_glu, b_glu, g_s5, w_out)


def _loss_fwd_bwd(h, g, target, n_real):
    lp, d = h.shape
    tm = _token_tile(lp)

    def body(h_ref, g_ref, t_ref, loss_ref, dh_ref, dg_ref):
        i = pl.program_id(0)

        @pl.when(i == 0)
        def _():
            loss_ref[...] = jnp.zeros_like(loss_ref)
            dg_ref[...] = jnp.zeros_like(dg_ref)

        pos = i * tm + _iota2((tm, 1), 0)
        real = ((pos >= N_META) & (pos < n_real)).astype(f32)
        y, vjp = jax.vjp(_rms, h_ref[...], g_ref[...])
        err = (y - t_ref[...]) * real
        loss_ref[...] += 0.5 * jnp.sum(jnp.mean(err * err, axis=1, keepdims=True))
        dx, dg = vjp(err * (1.0 / d))
        dh_ref[...] = dx
        dg_ref[...] += dg

    tok = _bs((tm, d), lambda i: (i, 0))
    return _pc(body, "loss_fwd_bwd", (lp // tm,), [tok, _bs((1, d), lambda i: (0, 0)), tok],
               [_bs((8, 128), lambda i: (0, 0)), tok, _bs((1, d), lambda i: (0, 0))],
               [_sds((8, 128)), _sds((lp, d)), _sds((1, d))])(h, g, target)


def _row_tile(rows):
    for t in (256, 128, 64, 32, 16, 8):
        if rows % t == 0:
            return t
    raise ValueError(rows)


def _adamw(w, g, m, v):
    rows = w.shape[0]
    tr = _row_tile(rows)

    def body(w_ref, g_ref, m_ref, v_ref, d_ref, mo_ref, vo_ref):
        gv = g_ref[...]
        m_new = ADAM_B1 * m_ref[...] + (1.0 - ADAM_B1) * gv
        v_new = ADAM_B2 * v_ref[...] + (1.0 - ADAM_B2) * (gv * gv)
        m_hat = m_new / (1.0 - ADAM_B1 ** ADAM_STEP)
        v_hat = v_new / (1.0 - ADAM_B2 ** ADAM_STEP)
        d_ref[...] = -ADAM_LR * (m_hat / (jnp.sqrt(v_hat) + ADAM_EPS) + ADAM_WD * w_ref[...])
        mo_ref[...] = m_new
        vo_ref[...] = v_new

    blk = _bs((tr, FLAT_W), lambda i: (i, 0))
    return _pc(body, "adamw", (rows // tr,), [blk] * 4, [blk] * 3, [_sds(w.shape)] * 3)(w, g, m, v)


def _sum_leading(x, name):
    n, rows, _ = x.shape
    tr = _row_tile(rows)

    def body(x_ref, o_ref):
        acc = x_ref[0]
        for k in range(1, n):
            acc = acc + x_ref[k]
        o_ref[...] = acc

    return _pc(body, name, (rows // tr,), [_bs((n, tr, FLAT_W), lambda i: (0, i, 0))],
               _bs((tr, FLAT_W), lambda i: (i, 0)), _sds((rows, FLAT_W)))(x)


def _pair_add(a, b, name):
    rows = a.shape[0]
    tr = _row_tile(rows)

    def body(a_ref, b_ref, o_ref):
        o_ref[...] = a_ref[...] + b_ref[...]

    blk = _bs((tr, FLAT_W), lambda i: (i, 0))
    return _pc(body, name, (rows // tr,), [blk, blk], blk, _sds(a.shape))(a, b)


_CHIP_FLIPS = ((1, 0, 0), (0, 1, 0), (1, 1, 0))
_ALL_FLIPS = tuple((a, b, c) for a in (0, 1) for b in (0, 1) for c in (0, 1) if (a, b, c) != (0, 0, 0))
_CORE_FLIP = ((0, 0, 1),)
_D2D_STREAMS = 4


def _exchange(name, arrays, out_shapes, flips, plan):
    n_in = len(arrays)

    def body(*refs):
        ins, outs = refs[:n_in], refs[n_in:n_in + len(out_shapes)]
        send_sems, recv_sems, local_sems = refs[n_in + len(out_shapes):]
        me = (lax.axis_index("x"), lax.axis_index("y"), lax.axis_index("c"))
        local = [pltpu.make_async_copy(s, d, local_sems.at[n]) for n, (s, d) in enumerate(plan(me, None, ins, outs))]
        for cp in local:
            cp.start()
        sent, k = [], 0
        for f in flips:
            peer = tuple(1 - m if fl else m for m, fl in zip(me, f))
            for s, d in plan(me, peer, ins, outs):
                cp = pltpu.make_async_remote_copy(src_ref=s, dst_ref=d, send_sem=send_sems.at[k],
                                                  recv_sem=recv_sems.at[k], device_id=peer,
                                                  device_id_type=pl.DeviceIdType.MESH)
                cp.start()
                sent.append(cp)
                k += 1
        for cp in sent:
            cp.wait_recv()
        for cp in sent:
            cp.wait_send()
        for cp in local:
            cp.wait()

    me0 = (0, 0, 0)
    n_remote = sum(len(_plan_count(plan, me0, f, arrays, out_shapes)) for f in flips)
    n_local = len(_plan_count(plan, me0, None, arrays, out_shapes))
    hbm = pl.BlockSpec(memory_space=pltpu.HBM)
    return pl.pallas_call(
        body, name=name, in_specs=[hbm] * n_in, out_specs=[hbm] * len(out_shapes), out_shape=list(out_shapes),
        scratch_shapes=[pltpu.SemaphoreType.DMA((n_remote,)), pltpu.SemaphoreType.DMA((n_remote,)),
                        pltpu.SemaphoreType.DMA((max(n_local, 1),))],
        compiler_params=pltpu.CompilerParams(has_side_effects=True))(*arrays)


class _FakeRef:
    def __init__(self):
        self.at = self

    def __getitem__(self, idx):
        return self


def _plan_count(plan, me, flip, arrays, out_shapes):
    peer = None if flip is None else me
    return plan(me, peer, [_FakeRef() for _ in arrays], [_FakeRef() for _ in out_shapes])


def _chip_index(dev):
    return 2 * dev[0] + dev[1]


def _all_gather_chips(name, arrays):
    n = len(arrays)

    def body(*refs):
        ins, outs = refs[:n], refs[n:2 * n]
        send_sems, recv_sems, local_sems = refs[2 * n:]
        x, y, c = lax.axis_index("x"), lax.axis_index("y"), lax.axis_index("c")
        sibling = (x, y, 1 - c)
        chips = [(1 - x, y), (x, 1 - y), (1 - x, 1 - y)]
        mine = 2 * x + y

        def copy(k, src, dst, to):
            return pltpu.make_async_remote_copy(src_ref=src, dst_ref=dst, send_sem=send_sems.at[k],
                                                recv_sem=recv_sems.at[k], device_id=to,
                                                device_id_type=pl.DeviceIdType.MESH)

        local = [pltpu.make_async_copy(ins[a], outs[a].at[mine], local_sems.at[a]) for a in range(n)]
        for cp in local:
            cp.start()
        first = [copy(j * n + a, ins[a].at[c], outs[a].at[mine, c], (*chip, c))
                 for j, chip in enumerate(chips) for a in range(n)]
        for cp in first:
            cp.start()
        passed = []
        for j, chip in enumerate(chips):
            for a in range(n):
                landed = outs[a].at[_chip_index(chip), c]
                copy(j * n + a, landed, landed, sibling).wait_recv()
                cp = copy(3 * n + j * n + a, landed, landed, sibling)
                cp.start()
                passed.append(cp)
        for j, chip in enumerate(chips):
            for a in range(n):
                other = outs[a].at[_chip_index(chip), 1 - c]
                copy(3 * n + j * n + a, other, other, sibling).wait_recv()
        for cp in first + passed:
            cp.wait_send()
        for cp in local:
            cp.wait()

    hbm = pl.BlockSpec(memory_space=pltpu.HBM)
    return pl.pallas_call(
        body, name=name, in_specs=[hbm] * n, out_specs=[hbm] * n,
        out_shape=[_sds((N_CHIPS,) + a.shape, a.dtype) for a in arrays],
        scratch_shapes=[pltpu.SemaphoreType.DMA((6 * n,)), pltpu.SemaphoreType.DMA((6 * n,)),
                        pltpu.SemaphoreType.DMA((n,))],
        compiler_params=pltpu.CompilerParams(has_side_effects=True))(*arrays)


def _all_gather_devices(name, arr):
    def plan(me, peer, ins, outs):
        return [(ins[0], outs[0].at[4 * me[0] + 2 * me[1] + me[2]])]

    return _exchange(name, [arr], [_sds((8,) + arr.shape, arr.dtype)], _ALL_FLIPS, plan)[0]


def _swap_half_with_sibling(name, g):
    step = g.shape[2] // _D2D_STREAMS

    def plan(me, peer, ins, outs):
        if peer is None:
            return []
        return [(ins[0].at[k, 1 - me[2], pl.ds(r * step, step)], outs[0].at[k, pl.ds(r * step, step)])
                for k in range(N_CHIPS) for r in range(_D2D_STREAMS)]

    return _exchange(name, [g], [_sds((N_CHIPS,) + g.shape[2:], g.dtype)], _CORE_FLIP, plan)[0]


def _scatter_to_chips(name, s):
    def plan(me, peer, ins, outs):
        to = me if peer is None else peer
        return [(ins[0].at[_chip_index(to)], outs[0].at[_chip_index(me)])]

    return _exchange(name, [s], [_sds(s.shape, s.dtype)], _CHIP_FLIPS, plan)[0]


def _share_with_sibling(name, r):
    step = r.shape[0] // (4 * _D2D_STREAMS)

    def plan(me, peer, ins, outs):
        return [(ins[0].at[pl.ds(n * step, step)], outs[0].at[me[2], pl.ds(n * step, step)])
                for n in range(4 * _D2D_STREAMS)]

    return _exchange(name, [r], [_sds((2,) + r.shape, r.dtype)], _CORE_FLIP, plan)[0]


def _to_heads(x):
    return x.reshape(x.shape[0], N_HEADS, HEAD_DIM).transpose(1, 0, 2)


def _from_heads(x):
    return x.transpose(1, 0, 2).reshape(x.shape[1], QW)


def _to_heads_t(x):
    return x.T.reshape(N_HEADS, HEAD_DIM, x.shape[0])


def _from_heads_t(x):
    return x.reshape(QW, x.shape[2]).T


def _shift_rows(x, k):
    if k == 0:
        return x
    z = jnp.zeros((abs(k),) + x.shape[1:], x.dtype)
    return jnp.concatenate([z, x[:-k]], axis=0) if k > 0 else jnp.concatenate([x[-k:], z], axis=0)


def _reorder_w_in(w):
    o = 3 * QW + 3 * QW + QW
    main = jnp.concatenate([w[:, :o], w[:, o + 2 * N_HEADS:]], axis=1)
    ba = jnp.pad(w[:, o:o + 2 * N_HEADS], ((0, 0), (0, 128 - 2 * N_HEADS)))
    return jnp.concatenate([main, ba], axis=1)


def _restore_w_in(w):
    o = 3 * QW + 3 * QW + QW
    return jnp.concatenate([w[:, :o], w[:, PROJ_W - 128:PROJ_W - 128 + 2 * N_HEADS], w[:, o:PROJ_W - 128]], axis=1)


def _slab_embed_b(b):
    x = b.reshape(S5_SLABS, 8, S5_C, S5_P)
    eye = jnp.eye(8, dtype=b.dtype)
    return (x[:, :, :, None, :] * eye[None, :, None, :, None]).reshape(S5_SLABS, 8 * S5_C, 8 * S5_P)


def _slab_extract_b(m):
    x = m.reshape(S5_SLABS, 8, S5_C, 8, S5_P)
    return jnp.stack([x[:, g, :, g, :] for g in range(8)], axis=1).reshape(S5_G, S5_C, S5_P)


def _slab_embed_c(c):
    x = c.reshape(S5_SLABS, 8, S5_C, S5_P).transpose(0, 1, 3, 2)
    eye = jnp.eye(8, dtype=c.dtype)
    return (x[:, :, :, None, :] * eye[None, :, None, :, None]).reshape(S5_SLABS, 8 * S5_P, 8 * S5_C)


def _slab_extract_c(m):
    x = m.reshape(S5_SLABS, 8, S5_P, 8, S5_C)
    return jnp.stack([x[:, g, :, g, :] for g in range(8)], axis=1).transpose(0, 1, 3, 2).reshape(S5_G, S5_C, S5_P)


def _piece_rows(shape):
    return -(-math.prod(shape) // (8 * FLAT_W)) * 8


def _pack_rows(parts, lead, row_align):
    rows = []
    for p in parts:
        flat = p.reshape(lead + (-1,))
        r = _piece_rows(p.shape[len(lead):])
        flat = jnp.pad(flat, [(0, 0)] * len(lead) + [(0, r * FLAT_W - flat.shape[-1])])
        rows.append(flat.reshape(lead + (r, FLAT_W)))
    total = sum(r.shape[-2] for r in rows)
    rows.append(jnp.zeros(lead + ((-total) % row_align, FLAT_W), parts[0].dtype))
    return jnp.concatenate(rows, axis=len(lead))


def _unpack_rows(flat, shapes):
    out, off = [], 0
    for s in shapes:
        r = _piece_rows(s)
        out.append(flat[off:off + r].reshape(-1)[:math.prod(s)].reshape(s))
        off += r
    return out


_SHARDED = ("ffn1_w_gate", "ffn1_w_up", "ffn1_w_down", "w_in", "s5_w_glu", "w_out",
            "ffn2_w_gate", "ffn2_w_up", "ffn2_w_down", "meta_tokens", "dn_conv_w")
_MATMUL_W = _SHARDED[:9]
_WEIGHTS = ("meta_tokens", "ffn1_norm", "ffn1_w_gate", "ffn1_w_up", "ffn1_w_down", "mix_norm", "w_in", "sb_out_norm",
            "dn_conv_w", "dn_a_log", "dn_dt_bias", "dn_out_norm", "s5_a_re", "s5_a_im", "s5_log_dt", "s5_b_re",
            "s5_b_im", "s5_c_re", "s5_c_im", "s5_d", "s5_w_glu", "s5_b_glu", "s5_out_norm", "w_out", "ffn2_norm",
            "ffn2_w_gate", "ffn2_w_up", "ffn2_w_down", "final_norm")
_REPLICATED = tuple(n for n in _WEIGHTS if n not in _SHARDED)


def _chip_major(name, g):
    if name in ("ffn1_w_gate", "ffn1_w_up", "ffn2_w_gate", "ffn2_w_up", "ffn1_w_down", "ffn2_w_down"):
        return g.transpose(1, 0, 2, 3)
    if name == "w_in":
        return g.reshape(2, D_MODEL, N_CHIPS, IN_WIDTH // N_CHIPS).transpose(2, 0, 1, 3)
    if name in ("w_out", "s5_w_glu"):
        return g.reshape(2, N_CHIPS, g.shape[1] // N_CHIPS, g.shape[2]).transpose(1, 0, 2, 3)
    if name == "meta_tokens":
        return g.reshape(N_META, N_CHIPS, D_MODEL // N_CHIPS).transpose(1, 0, 2)
    if name == "dn_conv_w":
        return g.reshape(2, DN_CONV, N_CHIPS, 3 * QW // N_CHIPS).transpose(2, 0, 1, 3)
    raise ValueError(name)


def _layer_forward(h, p):
    lp = h.shape[0]
    h1, *ffn1_saved = _ffn_fwd(h, p["ffn1_norm"], *p["ffn1"])
    q, k, v, dnx, z, u, ba = _inproj_fwd(h1, p["mix_norm"], p["w_in"].astype(MXU_DTYPE))
    qh, kh, vh = _to_heads(q), _to_heads_t(k), _to_heads_t(v)
    osb, sb_w, sb_sig = _sb_fwd(qh, kh, vh)
    xs = jnp.stack([_shift_rows(dnx, s) for s in range(DN_CONV)])
    dq_, dk_, dv_, bg = _dn_pre_fwd(xs, ba, p["cw"], p["pv"])
    dqh, dkh, dvh = _to_heads(dq_), _to_heads(dk_), _to_heads(dv_)
    brow = bg[:, :N_HEADS].T.reshape(N_HEADS, lp // DN_CHUNK, 1, DN_CHUNK)
    grow = bg[:, N_HEADS:2 * N_HEADS].T.reshape(N_HEADS, lp // DN_CHUNK, 1, DN_CHUNK)
    odn, states = _dn_scan_fwd(dqh, dkh, dvh, grow, brow)
    ar, ai, bre, bim = _s5_prep_fwd(p["s5_a_re"], p["s5_a_im"], p["s5_log_dt"], p["s5_b_re"], p["s5_b_im"])
    s5t = (ar.reshape(1, S5_N), ai.reshape(1, S5_N), _slab_embed_b(bre), _slab_embed_b(bim),
           _slab_embed_c(p["s5_c_re"]), _slab_embed_c(p["s5_c_im"]), p["s5_d"])
    ys5, cin = _s5_scan_fwd(u, *s5t)
    osb_t, odn_t = _from_heads(osb), _from_heads(odn)
    h2 = _mixout_fwd(h1, osb_t, odn_t, z, ys5, p["sb_out_norm"], p["dn_out_norm"], p["s5_w_glu"], p["s5_b_glu"],
                     p["s5_out_norm"], p["w_out"].astype(MXU_DTYPE))
    h3, *ffn2_saved = _ffn_fwd(h2, p["ffn2_norm"], *p["ffn2"])
    saved = dict(h0=h, h1=h1, h2=h2, ffn1=ffn1_saved, ffn2=ffn2_saved, qh=qh, kh=kh, vh=vh, sb_w=sb_w, sb_sig=sb_sig, xs=xs, ba=ba, dqh=dqh, dkh=dkh, dvh=dvh,
                 grow=grow, brow=brow, states=states, odn_t=odn_t, osb_t=osb_t, z=z, u=u, ys5=ys5, cin=cin, s5t=s5t)
    return h3, saved


def _ffn_backward(h, g, w3, dy, fwd_saved):
    xn, gate, up = fwd_saved
    dh, dgn, d_gate, d_up, act = _ffn_bwd_dx(h, g, *w3, dy, gate, up)
    dwg, dwu, dwd = _ffn_bwd_dw(xn, dy, d_gate, d_up, act)
    return dh, dgn, dwg, dwu, dwd


def _layer_backward(dh3, p, s):
    lp = dh3.shape[0]
    g = {}
    dh2, g["ffn2_norm"], g["ffn2_w_gate"], g["ffn2_w_up"], g["ffn2_w_down"] = _ffn_backward(
        s["h2"], p["ffn2_norm"], p["ffn2"], dh3, s["ffn2"])
    (dosb_t, dodn_t, dz, dys5, g["sb_out_norm"], g["dn_out_norm"], g["s5_w_glu"], g["s5_b_glu"], g["s5_out_norm"],
     g["w_out"]) = _mixout_bwd(dh2, s["osb_t"], s["odn_t"], s["z"], s["ys5"], p["sb_out_norm"], p["dn_out_norm"],
                               p["s5_w_glu"], p["s5_b_glu"], p["s5_out_norm"], p["w_out"].astype(MXU_DTYPE))
    du, dab, db8r, db8i, dc8r, dc8i, g["s5_d"] = _s5_scan_bwd(s["u"], dys5, s["cin"], *s["s5t"])
    g["s5_c_re"], g["s5_c_im"] = _slab_extract_c(dc8r), _slab_extract_c(dc8i)
    g["s5_a_re"], g["s5_a_im"], g["s5_log_dt"], g["s5_b_re"], g["s5_b_im"] = _s5_prep_bwd(
        p["s5_a_re"], p["s5_a_im"], p["s5_log_dt"], p["s5_b_re"], p["s5_b_im"],
        dab[0].reshape(S5_G, S5_P), dab[1].reshape(S5_G, S5_P), _slab_extract_b(db8r), _slab_extract_b(db8i))
    ddq, ddk, ddv, dgrow, dbrow = _dn_scan_bwd(s["dqh"], s["dkh"], s["dvh"], s["grow"], s["brow"], s["states"],
                                               _to_heads(dodn_t))
    dbg = jnp.concatenate([dbrow.reshape(N_HEADS, lp).T, dgrow.reshape(N_HEADS, lp).T,
                           jnp.zeros((lp, 128 - 2 * N_HEADS), f32)], axis=1)
    dxs, dba, g["cw"], g["pv"] = _dn_pre_bwd(s["xs"], s["ba"], p["cw"], p["pv"], _from_heads(ddq), _from_heads(ddk),
                                             _from_heads(ddv), dbg)
    ddn4 = jnp.stack([_shift_rows(dxs[k], -k) for k in range(DN_CONV)])
    dq, dk_t, dv_t = _sb_bwd(s["qh"], s["kh"], s["vh"], s["sb_w"], s["sb_sig"], _to_heads(dosb_t))
    dh1, g["mix_norm"], g["w_in"] = _inproj_bwd(s["h1"], p["mix_norm"], p["w_in"].astype(MXU_DTYPE), dh2,
                                                _from_heads(dq), _from_heads_t(dk_t), _from_heads_t(dv_t), ddn4,
                                                dz, du, dba)
    dh0, g["ffn1_norm"], g["ffn1_w_gate"], g["ffn1_w_up"], g["ffn1_w_down"] = _ffn_backward(
        s["h0"], p["ffn1_norm"], p["ffn1"], dh1, s["ffn1"])
    return dh0, g


def kernel(x, meta_tokens, ffn1_norm, ffn1_w_gate, ffn1_w_up, ffn1_w_down, mix_norm, w_in, sb_out_norm, dn_conv_w, dn_a_log, dn_dt_bias, dn_out_norm, s5_a_re, s5_a_im, s5_log_dt, s5_b_re, s5_b_im, s5_c_re, s5_c_im, s5_d, s5_w_glu, s5_b_glu, s5_out_norm, w_out, ffn2_norm, ffn2_w_gate, ffn2_w_up, ffn2_w_down, final_norm, loss_target, m_meta_tokens, m_ffn1_norm, m_ffn1_w_gate, m_ffn1_w_up, m_ffn1_w_down, m_mix_norm, m_w_in, m_sb_out_norm, m_dn_conv_w, m_dn_a_log, m_dn_dt_bias, m_dn_out_norm, m_s5_a_re, m_s5_a_im, m_s5_log_dt, m_s5_b_re, m_s5_b_im, m_s5_c_re, m_s5_c_im, m_s5_d, m_s5_w_glu, m_s5_b_glu, m_s5_out_norm, m_w_out, m_ffn2_norm, m_ffn2_w_gate, m_ffn2_w_up, m_ffn2_w_down, m_final_norm, v_meta_tokens, v_ffn1_norm, v_ffn1_w_gate, v_ffn1_w_up, v_ffn1_w_down, v_mix_norm, v_w_in, v_sb_out_norm, v_dn_conv_w, v_dn_a_log, v_dn_dt_bias, v_dn_out_norm, v_s5_a_re, v_s5_a_im, v_s5_log_dt, v_s5_b_re, v_s5_b_im, v_s5_c_re, v_s5_c_im, v_s5_d, v_s5_w_glu, v_s5_b_glu, v_s5_out_norm, v_w_out, v_ffn2_norm, v_ffn2_w_gate, v_ffn2_w_up, v_ffn2_w_down, v_final_norm):
    args = dict(locals())
    w = {n: args[n] for n in _WEIGHTS}
    m = {n: args["m_" + n] for n in _WEIGHTS}
    v = {n: args["v_" + n] for n in _WEIGHTS}
    depth = ffn1_norm.shape[0]
    seq = x.shape[1]
    n_real = N_META + seq
    lp = _padded_len(n_real)

    gathered = _all_gather_chips("gather_weights", [w[n].astype(MXU_DTYPE) for n in _MATMUL_W]
                                 + [w["meta_tokens"].reshape(2, N_META // 2, -1), w["dn_conv_w"]])
    full = dict(zip(_MATMUL_W + ("meta_tokens", "dn_conv_w"), gathered))
    meta_full = full["meta_tokens"].reshape(N_CHIPS, N_META, -1).transpose(1, 0, 2).reshape(N_META, D_MODEL)
    conv_full = full["dn_conv_w"].transpose(1, 2, 0, 3).reshape(depth, DN_CONV, 3 * QW)
    w_in_full = full["w_in"].transpose(1, 2, 0, 3).reshape(depth, D_MODEL, IN_WIDTH)
    w_out_full = full["w_out"].transpose(1, 0, 2, 3).reshape(depth, D_MODEL, D_MODEL)
    w_glu_full = full["s5_w_glu"].transpose(1, 0, 2, 3).reshape(depth, S5_W, S5_W)

    layers = []
    for l in range(depth):
        pv = jnp.pad(jnp.stack([dn_a_log[l], dn_dt_bias[l]]), ((0, 6), (N_HEADS, 128 - 2 * N_HEADS)))
        layers.append(dict(
            ffn1_norm=ffn1_norm[l][None], mix_norm=mix_norm[l][None], ffn2_norm=ffn2_norm[l][None],
            ffn1=(full["ffn1_w_gate"][:, l], full["ffn1_w_up"][:, l], full["ffn1_w_down"][:, l]),
            ffn2=(full["ffn2_w_gate"][:, l], full["ffn2_w_up"][:, l], full["ffn2_w_down"][:, l]),
            w_in=_reorder_w_in(w_in_full[l]), w_out=w_out_full[l], s5_w_glu=w_glu_full[l].astype(f32),
            cw=jnp.pad(conv_full[l], ((0, 8 - DN_CONV), (0, 0))), pv=pv,
            sb_out_norm=sb_out_norm[l][None], dn_out_norm=dn_out_norm[l][None],
            s5_a_re=s5_a_re[l], s5_a_im=s5_a_im[l], s5_log_dt=s5_log_dt[l][:, None],
            s5_b_re=s5_b_re[l].transpose(0, 2, 1), s5_b_im=s5_b_im[l].transpose(0, 2, 1),
            s5_c_re=s5_c_re[l], s5_c_im=s5_c_im[l], s5_d=s5_d[l][None], s5_b_glu=s5_b_glu[l][None],
            s5_out_norm=s5_out_norm[l][None]))

    tail = jnp.zeros((lp - n_real, D_MODEL), f32)
    h = jnp.concatenate([meta_full, x[0], tail], axis=0)
    target = jnp.concatenate([jnp.zeros((N_META, D_MODEL), f32), loss_target[0], tail], axis=0)
    saved = []
    for p in layers:
        h, s = _layer_forward(h, p)
        saved.append(s)
    loss_blk, dh, d_final = _loss_fwd_bwd(h, final_norm[None], target, n_real)
    grads = [None] * depth
    for l in reversed(range(depth)):
        dh, grads[l] = _layer_backward(dh, layers[l], saved[l])
    loss = lax.psum(loss_blk[0, 0], ("x", "y", "c"))
    grad_x = dh[N_META:n_real][None]

    stack = lambda name: jnp.stack([grads[l][name] for l in range(depth)])
    gfull = {n: stack(n) for n in ("ffn1_w_gate", "ffn1_w_up", "ffn1_w_down", "s5_w_glu", "w_out", "ffn2_w_gate",
                                   "ffn2_w_up", "ffn2_w_down")}
    gfull["w_in"] = jnp.stack([_restore_w_in(grads[l]["w_in"]) for l in range(depth)])
    gfull["meta_tokens"] = dh[:N_META]
    gfull["dn_conv_w"] = jnp.stack([grads[l]["cw"][:DN_CONV] for l in range(depth)])
    grep = {n: stack(n).reshape(w[n].shape) for n in ("ffn1_norm", "mix_norm", "sb_out_norm", "dn_out_norm", "s5_a_re",
                                                      "s5_a_im", "s5_log_dt", "s5_c_re", "s5_c_im", "s5_d", "s5_b_glu",
                                                      "s5_out_norm", "ffn2_norm")}
    grep["s5_b_re"] = jnp.stack([grads[l]["s5_b_re"].transpose(0, 2, 1) for l in range(depth)])
    grep["s5_b_im"] = jnp.stack([grads[l]["s5_b_im"].transpose(0, 2, 1) for l in range(depth)])
    grep["dn_a_log"] = jnp.stack([grads[l]["pv"][0, N_HEADS:2 * N_HEADS] for l in range(depth)])
    grep["dn_dt_bias"] = jnp.stack([grads[l]["pv"][1, N_HEADS:2 * N_HEADS] for l in range(depth)])
    grep["final_norm"] = d_final[0]

    shard_shapes = [w[n].shape for n in _SHARDED]
    g_big = _pack_rows([_chip_major(n, gfull[n]) for n in _SHARDED], (N_CHIPS,), BIG_ROWS)
    half_rows = g_big.shape[1] // 2
    g_big = g_big.reshape(N_CHIPS, 2, half_rows, FLAT_W)
    c = lax.axis_index("c")
    mine = lax.dynamic_index_in_dim(g_big, c, axis=1, keepdims=False)
    theirs = _swap_half_with_sibling("grad_pair_swap", g_big)
    pair = _pair_add(mine.reshape(-1, FLAT_W), theirs.reshape(-1, FLAT_W), "grad_pair_add")
    arrived = _scatter_to_chips("grad_scatter", pair.reshape(N_CHIPS, half_rows, FLAT_W))
    reduced = _sum_leading(arrived, "grad_chip_sum")
    g_shard = _share_with_sibling("grad_share", reduced).reshape(-1, FLAT_W)

    rep_shapes = [w[n].shape for n in _REPLICATED]
    g_small = _pack_rows([grep[n] for n in _REPLICATED], (), 64)
    g_rep = _sum_leading(_all_gather_devices("grad_small_gather", g_small), "grad_small_sum")

    pack = lambda d, names, align: _pack_rows([d[n] for n in names], (), align)
    out = {}
    for names, shapes, g_flat, align in ((_SHARDED, shard_shapes, g_shard, BIG_ROWS),
                                         (_REPLICATED, rep_shapes, g_rep, 64)):
        delta, m_new, v_new = _adamw(pack(w, names, align), g_flat, pack(m, names, align), pack(v, names, align))
        for kind, flat in (("grad", g_flat), ("delta", delta), ("new_m", m_new), ("new_v", v_new)):
            for n, a in zip(names, _unpack_rows(flat, shapes)):
                out[kind + "_" + n] = a
    return (loss, grad_x, *[out[k + "_" + n] for k in ("grad", "delta", "new_m", "new_v") for n in _WEIGHTS])
```

```python
import functools
import math

import jax
import jax.numpy as jnp
from jax import lax
from jax.experimental import pallas as pl
from jax.experimental.pallas import tpu as pltpu

f32 = jnp.float32
MXU_DTYPE = jnp.bfloat16
HI = lax.Precision.HIGHEST
NN = (((1,), (0,)), ((), ()))
NT = (((1,), (1,)), ((), ()))
TN = (((0,), (0,)), ((), ()))

EPS = 1e-6
D_MODEL = 1024
N_META = 16
HEAD_DIM = 64
N_HEADS = 4
QW = N_HEADS * HEAD_DIM
DN_CONV = 4
DN_CHUNK = 64
S5_W = 512
S5_G = 32
S5_P = 64
S5_C = 16
S5_N = S5_G * S5_P
S5_SLABS = 4
N_CHIPS = 4
PROJ_W = 2432
IN_WIDTH = 2312
VMEM_LIMIT = 56 * 1024 * 1024

ADAM_LR, ADAM_B1, ADAM_B2, ADAM_EPS, ADAM_WD, ADAM_STEP = 0.001, 0.9, 0.999, 1e-08, 0.01, 10
FLAT_W = 1024
BIG_ROWS = 512


def _dot(a, b, dims=NN):
    return lax.dot_general(a.astype(MXU_DTYPE), b.astype(MXU_DTYPE), dims, preferred_element_type=f32)


def _dotx(a, b, dims=NN):
    return lax.dot_general(a, b, dims, precision=HI, preferred_element_type=f32)


def _split(x):
    if MXU_DTYPE == f32:
        return x, None
    hi = x.astype(MXU_DTYPE)
    return hi, (x - hi.astype(f32)).astype(MXU_DTYPE)


def _dot_split(hi, lo, u01):
    if lo is None:
        return _dotx(hi, u01)
    u = u01.astype(MXU_DTYPE)
    return (lax.dot_general(hi, u, NN, preferred_element_type=f32)
            + lax.dot_general(lo, u, NN, preferred_element_type=f32))


def _dot3(a, b, dims=NN):
    ah, al = _split(a)
    if al is None:
        return _dotx(a, b, dims)
    bh, bl = _split(b)
    d = lambda x, y: lax.dot_general(x, y, dims, preferred_element_type=f32)
    return d(ah, bh) + d(ah, bl) + d(al, bh)


BNN = (((2,), (1,)), ((0,), (0,)))
BNT = (((2,), (2,)), ((0,), (0,)))
BTN = (((1,), (1,)), ((0,), (0,)))


def _with_dot_vjp(dot, kind, batched=False):
    nn, nt, tn = (BNN, BNT, BTN) if batched else (NN, NT, TN)
    dims = {"nn": nn, "nt": nt, "tn": tn}[kind]

    @jax.custom_vjp
    def f(a, b):
        return dot(a, b, dims)

    def fwd(a, b):
        return dot(a, b, dims), (a, b)

    def bwd(res, dy):
        a, b = res
        if kind == "nn":
            return dot(dy, b, nt), dot(a, dy, tn)
        if kind == "nt":
            return dot(dy, b, nn), dot(dy, a, tn)
        return dot(b, dy, nt), dot(a, dy, nn)

    f.defvjp(fwd, bwd)
    return f


_mm = _with_dot_vjp(_dot, "nn")
_bmm = _with_dot_vjp(_dot, "nn", True)
_bmm_nt = _with_dot_vjp(_dot, "nt", True)
_bmm_tn = _with_dot_vjp(_dot, "tn", True)
_bmm3 = _with_dot_vjp(_dot3, "nn", True)


def _rms(x, g):
    return x * lax.rsqrt(jnp.mean(x * x, axis=-1, keepdims=True) + EPS) * g


def _sigmoid(x):
    return 1.0 / (1.0 + jnp.exp(-x))


def _silu(x):
    return x * _sigmoid(x)


def _softplus(x):
    return jnp.maximum(x, 0.0) + jnp.log(1.0 + jnp.exp(-jnp.abs(x)))


def _gelu_tanh(x):
    return 0.5 * x * (1.0 + jnp.tanh(math.sqrt(2.0 / math.pi) * (x + 0.044715 * x * x * x)))


def _iota2(shape, axis):
    return lax.broadcasted_iota(jnp.int32, shape, axis)


def _block_diag_ones(n, blk):
    return ((_iota2((n, n), 0) // blk) == (_iota2((n, n), 1) // blk)).astype(f32)


def _pc(body, name, grid, in_specs, out_specs, out_shape, scratch=(), vmem=VMEM_LIMIT):
    return pl.pallas_call(
        body, name=name, grid=grid, in_specs=in_specs, out_specs=out_specs, out_shape=out_shape,
        scratch_shapes=list(scratch),
        compiler_params=pltpu.CompilerParams(dimension_semantics=("arbitrary",) * len(grid), vmem_limit_bytes=vmem))


def _bs(shape, imap):
    return pl.BlockSpec(shape, imap)


def _sds(shape, dtype=f32):
    return jax.ShapeDtypeStruct(tuple(shape), dtype)


def _token_tile(lp, cap=640):
    for t in (640, 320, 256, 128, 64):
        if t <= cap and lp % t == 0:
            return t
    raise ValueError(lp)


def _padded_len(l):
    return -(-l // 1280) * 1280 if l > 4096 else -(-l // 256) * 256


def _ffn_fwd(h, g, wg, wu, wd):
    lp, d = h.shape
    nch, _, fc = wg.shape
    tm = _token_tile(lp)

    def body(h_ref, g_ref, wg_ref, wu_ref, wd_ref, o_ref, xn_ref, gate_ref, up_ref, xn_s, acc_s):
        j = pl.program_id(1)

        @pl.when(j == 0)
        def _():
            xn_s[...] = _rms(h_ref[...], g_ref[...]).astype(xn_s.dtype)
            acc_s[...] = jnp.zeros_like(acc_s)

        xn = xn_s[...]
        gate = _dot(xn, wg_ref[0])
        up = _dot(xn, wu_ref[0])
        gate_ref[0] = gate.astype(gate_ref.dtype)
        up_ref[0] = up.astype(up_ref.dtype)
        acc_s[...] += _dot(_silu(gate) * up, wd_ref[0])

        @pl.when(j == nch - 1)
        def _():
            o_ref[...] = h_ref[...] + 0.5 * acc_s[...]
            xn_ref[...] = xn_s[...]

    tok = _bs((tm, d), lambda i, j: (i, 0))
    chunk = _bs((1, tm, fc), lambda i, j: (j, i, 0))
    return _pc(
        body, "ffn_fwd", (lp // tm, nch),
        [tok, _bs((1, d), lambda i, j: (0, 0)),
         _bs((1, d, fc), lambda i, j: (j, 0, 0)), _bs((1, d, fc), lambda i, j: (j, 0, 0)),
         _bs((1, fc, d), lambda i, j: (j, 0, 0))],
        [tok, tok, chunk, chunk],
        [_sds((lp, d)), _sds((lp, d), MXU_DTYPE), _sds((nch, lp, fc), MXU_DTYPE), _sds((nch, lp, fc), MXU_DTYPE)],
        scratch=[pltpu.VMEM((tm, d), MXU_DTYPE), pltpu.VMEM((tm, d), f32)])(h, g, wg, wu, wd)


def _ffn_bwd_dx(h, g, wg, wu, wd, dy, gate_saved, up_saved):
    lp, d = h.shape
    nch, _, fc = wg.shape
    tm = _token_tile(lp)

    def body(h_ref, g_ref, wg_ref, wu_ref, wd_ref, dy_ref, gate_ref, up_ref, dh_ref, dgn_ref, dg_ref, du_ref, act_ref,
             dxn_s, dout_s):
        i, j = pl.program_id(0), pl.program_id(1)

        @pl.when((i == 0) & (j == 0))
        def _():
            dgn_ref[...] = jnp.zeros_like(dgn_ref)

        @pl.when(j == 0)
        def _():
            dxn_s[...] = jnp.zeros_like(dxn_s)
            dout_s[...] = (0.5 * dy_ref[...]).astype(dout_s.dtype)

        gate = gate_ref[0].astype(f32)
        up = up_ref[0].astype(f32)
        sig = _sigmoid(gate)
        sl = gate * sig
        dact = _dot(dout_s[...], wd_ref[0], NT)
        d_up = dact * sl
        d_gate = dact * up * sig * (1.0 + gate * (1.0 - sig))
        dg_ref[0] = d_gate.astype(dg_ref.dtype)
        du_ref[0] = d_up.astype(du_ref.dtype)
        act_ref[0] = (sl * up).astype(act_ref.dtype)
        dxn_s[...] += _dot(d_gate, wg_ref[0], NT) + _dot(d_up, wu_ref[0], NT)

        @pl.when(j == nch - 1)
        def _():
            _, vjp = jax.vjp(_rms, h_ref[...], g_ref[...])
            dx, dg = vjp(dxn_s[...])
            dh_ref[...] = dy_ref[...] + dx
            dgn_ref[...] += dg

    tok = _bs((tm, d), lambda i, j: (i, 0))
    chunk = _bs((1, tm, fc), lambda i, j: (j, i, 0))
    return _pc(
        body, "ffn_bwd_dx", (lp // tm, nch),
        [tok, _bs((1, d), lambda i, j: (0, 0)),
         _bs((1, d, fc), lambda i, j: (j, 0, 0)), _bs((1, d, fc), lambda i, j: (j, 0, 0)),
         _bs((1, fc, d), lambda i, j: (j, 0, 0)), tok, chunk, chunk],
        [tok, _bs((1, d), lambda i, j: (0, 0)), chunk, chunk, chunk],
        [_sds((lp, d)), _sds((1, d)),
         _sds((nch, lp, fc), MXU_DTYPE), _sds((nch, lp, fc), MXU_DTYPE), _sds((nch, lp, fc), MXU_DTYPE)],
        scratch=[pltpu.VMEM((tm, d), f32), pltpu.VMEM((tm, d), MXU_DTYPE)],
    )(h, g, wg, wu, wd, dy, gate_saved, up_saved)


def _ffn_bwd_dw(xn, dy, d_gate, d_up, act):
    lp, d = xn.shape
    nch, _, fc = d_gate.shape
    tm = _token_tile(lp)

    def body(xn_ref, dy_ref, dg_ref, du_ref, act_ref, dwg_ref, dwu_ref, dwd_ref):
        @pl.when(pl.program_id(1) == 0)
        def _():
            dwg_ref[...] = jnp.zeros_like(dwg_ref)
            dwu_ref[...] = jnp.zeros_like(dwu_ref)
            dwd_ref[...] = jnp.zeros_like(dwd_ref)

        xn_t = xn_ref[...]
        dwg_ref[0] += _dot(xn_t, dg_ref[0], TN)
        dwu_ref[0] += _dot(xn_t, du_ref[0], TN)
        dwd_ref[0] += _dot(act_ref[0], 0.5 * dy_ref[...], TN)

    tok = _bs((tm, d), lambda j, i: (i, 0))
    chunk = _bs((1, tm, fc), lambda j, i: (j, i, 0))
    return _pc(
        body, "ffn_bwd_dw", (nch, lp // tm), [tok, tok, chunk, chunk, chunk],
        [_bs((1, d, fc), lambda j, i: (j, 0, 0)), _bs((1, d, fc), lambda j, i: (j, 0, 0)),
         _bs((1, fc, d), lambda j, i: (j, 0, 0))],
        [_sds((nch, d, fc)), _sds((nch, d, fc)), _sds((nch, fc, d))])(xn, dy, d_gate, d_up, act)


_PROJ_SPLITS = (QW, QW, QW, 3 * QW, QW, S5_W, 128)


def _inproj_fwd(h, g, w):
    lp, d = h.shape
    tm = _token_tile(lp)

    def body(h_ref, g_ref, w_ref, *outs):
        proj = _dot(_rms(h_ref[...], g_ref[...]), w_ref[...])
        off = 0
        for ref, wd in zip(outs, _PROJ_SPLITS):
            ref[...] = proj[:, off:off + wd]
            off += wd

    return _pc(
        body, "inproj_fwd", (lp // tm,),
        [_bs((tm, d), lambda i: (i, 0)), _bs((1, d), lambda i: (0, 0)), _bs((d, PROJ_W), lambda i: (0, 0))],
        [_bs((tm, wd), lambda i: (i, 0)) for wd in _PROJ_SPLITS],
        [_sds((lp, wd)) for wd in _PROJ_SPLITS])(h, g, w)


def _inproj_bwd(h, g, w, dres, dq, dk, dv, ddn4, dz, du, dba):
    lp, d = h.shape
    tm = _token_tile(lp, 320)

    def body(h_ref, g_ref, w_ref, dres_ref, dq_ref, dk_ref, dv_ref, ddn_ref, dz_ref, du_ref, dba_ref,
             dh_ref, dgn_ref, dw_ref):
        @pl.when(pl.program_id(0) == 0)
        def _():
            dgn_ref[...] = jnp.zeros_like(dgn_ref)
            dw_ref[...] = jnp.zeros_like(dw_ref)

        ddn = ddn_ref[0] + ddn_ref[1] + ddn_ref[2] + ddn_ref[3]
        dproj = jnp.concatenate(
            [dq_ref[...], dk_ref[...], dv_ref[...], ddn, dz_ref[...], du_ref[...], dba_ref[...]], axis=1)
        xn, vjp = jax.vjp(_rms, h_ref[...], g_ref[...])
        dx, dg = vjp(_dot(dproj, w_ref[...], NT))
        dw_ref[...] += _dot(xn, dproj, TN)
        dh_ref[...] = dres_ref[...] + dx
        dgn_ref[...] += dg

    tok = lambda wd: _bs((tm, wd), lambda i: (i, 0))
    return _pc(
        body, "inproj_bwd", (lp // tm,),
        [tok(d), _bs((1, d), lambda i: (0, 0)), _bs((d, PROJ_W), lambda i: (0, 0)), tok(d),
         tok(QW), tok(QW), tok(QW), _bs((4, tm, 3 * QW), lambda i: (0, i, 0)), tok(QW), tok(S5_W), tok(128)],
        [tok(d), _bs((1, d), lambda i: (0, 0)), _bs((d, PROJ_W), lambda i: (0, 0))],
        [_sds((lp, d)), _sds((1, d)), _sds((d, PROJ_W))])(h, g, w, dres, dq, dk, dv, ddn4, dz, du, dba)


_SB_TQ = 256
_SB_ROWS = 32
_SB_GROUP = 4
_SB_ROWS_BWD = 32
_SB_GROUP_BWD = 4


def _sb_pieces(z, valid):
    t = jnp.exp(-jnp.abs(z))
    sp = jnp.maximum(z, 0.0) + jnp.log(1.0 + t)
    lk = -sp if valid is None else jnp.where(valid, -sp, 0.0)
    return t, sp, lk


def _cat_rows(parts):
    return parts[0] if len(parts) == 1 else jnp.concatenate(parts, axis=0)


def _sb_fwd(q, kt, vt):
    nh, lp, hd = q.shape
    tq = tk = min(_SB_TQ, lp)
    blocks = [slice(r, r + _SB_ROWS) for r in range(0, tq, _SB_ROWS)]

    def body(q_ref, k_ref, v_ref, o_ref, w_hbm, s_hbm, wbuf, sbuf, sems):
        head, qi = pl.program_id(0), pl.program_id(1)
        qv = q_ref[0]
        u_strict = (_iota2((tk, tk), 0) > _iota2((tk, tk), 1)).astype(f32)
        below = _iota2((tq, tk), 1) < _iota2((tq, tk), 0)

        spare = lambda slot, t: lp // tk + slot * _SB_GROUP + t

        def save(slot, t, j):
            return [pltpu.make_async_copy(wbuf.at[slot, t], w_hbm.at[head, qi, j], sems.at[0, slot, t]),
                    pltpu.make_async_copy(sbuf.at[slot, t], s_hbm.at[head, qi, j], sems.at[1, slot, t])]

        def drain(slot):
            for t in range(_SB_GROUP):
                for cp in save(slot, t, spare(slot, t)):
                    cp.wait()

        def idle(slot, t):
            wbuf[slot, t] = jnp.zeros((tq, tk), MXU_DTYPE)
            sbuf[slot, t] = jnp.zeros((tq, tk), MXU_DTYPE)
            for cp in save(slot, t, spare(slot, t)):
                cp.start()

        def tiles(js, carry, slot, masked=False, live=None, first=False):
            if not first:
                drain(slot)
            o_acc, c_after = carry
            kss = [pl.ds(pl.multiple_of(j * tk, tk), tk) for j in js]
            z_alls = [_dot(qv, k_ref[0, :, ks]) * (HEAD_DIM ** -0.5) for ks in kss]
            stage, afters = [], []
            for t, z_all in enumerate(z_alls):
                his, los, logs, sums = [], [], [], []
                for rs in blocks:
                    z = z_all[rs]
                    _, sp, lk = _sb_pieces(z, below[rs] if masked else None)
                    if live is not None:
                        lk = lk * live[t]
                    hi, lo = _split(lk)
                    his.append(hi)
                    los.append(lo)
                    logs.append(z - sp)
                    sums.append(jnp.sum(lk, axis=1, keepdims=True))
                stage.append((logs, _cat_rows(sums)))
                afters.append(_dot_split(_cat_rows(his), None if los[0] is None else _cat_rows(los), u_strict))
            for t, ((logs, sums), after_all) in enumerate(zip(stage, afters)):
                ws, sigs = [], []
                for n, rs in enumerate(blocks):
                    w = jnp.exp(logs[n] + after_all[rs] + c_after[rs])
                    sig = jnp.exp(logs[n])
                    if masked:
                        w, sig = jnp.where(below[rs], w, 0.0), jnp.where(below[rs], sig, 0.0)
                    if live is not None:
                        w = w * live[t]
                    ws.append(w.astype(MXU_DTYPE))
                    sigs.append(sig.astype(MXU_DTYPE))
                w_all = _cat_rows(ws)
                wbuf[slot, t] = w_all
                sbuf[slot, t] = _cat_rows(sigs)
                for cp in save(slot, t, js[t] if live is None else jnp.where(live[t] > 0.0, js[t], spare(slot, t))):
                    cp.start()
                o_acc = o_acc + _dot(w_all, v_ref[0, :, kss[t]], NT)
                c_after = c_after + sums
            for t in range(len(js), _SB_GROUP):
                idle(slot, t)
            return o_acc, c_after

        n_groups, rest = qi // _SB_GROUP, qi % _SB_GROUP
        group = lambda g, c: tiles([qi - 1 - _SB_GROUP * g - n for n in range(_SB_GROUP)], c, (g + 1) % 2)

        def last_group(_, c):
            idx = [rest - 1 - n for n in range(_SB_GROUP)]
            return tiles([jnp.maximum(j, 0) for j in idx], c, (n_groups + 1) % 2,
                         live=[(j >= 0).astype(f32) for j in idx])

        for t in range(_SB_GROUP):
            idle(1, t)
        carry = tiles([qi], (jnp.zeros((tq, hd), f32), jnp.zeros((tq, 1), f32)), 0, masked=True, first=True)
        carry = lax.fori_loop(0, n_groups, group, carry)
        o_acc, _ = lax.fori_loop(0, jnp.minimum(rest, 1), last_group, carry)
        drain(0)
        drain(1)
        o_ref[0] = o_acc

    full_t = _bs((1, hd, lp), lambda h, i: (h, 0, 0))
    hbm = pl.BlockSpec(memory_space=pltpu.HBM)
    return _pc(
        body, "sb_fwd", (nh, lp // tq),
        [_bs((1, tq, hd), lambda h, i: (h, i, 0)), full_t, full_t],
        [_bs((1, tq, hd), lambda h, i: (h, i, 0)), hbm, hbm],
        [_sds((nh, lp, hd))] + [_sds((nh, lp // tq, lp // tk + 2 * _SB_GROUP, tq, tk), MXU_DTYPE)] * 2,
        scratch=[pltpu.VMEM((2, _SB_GROUP, tq, tk), MXU_DTYPE), pltpu.VMEM((2, _SB_GROUP, tq, tk), MXU_DTYPE),
                 pltpu.SemaphoreType.DMA((2, 2, _SB_GROUP))])(q, kt, vt)


def _sb_bwd(q, kt, vt, w_saved, s_saved, do):
    nh, lp, hd = q.shape
    tq = tk = min(_SB_TQ, lp)
    blocks = [slice(r, r + _SB_ROWS_BWD) for r in range(0, tq, _SB_ROWS_BWD)]
    grp = _SB_GROUP_BWD

    def body(q_ref, k_ref, v_ref, w_hbm, s_hbm, do_ref, dq_ref, dk_ref, dv_ref, wbuf, sbuf, sems):
        head, qi = pl.program_id(0), pl.program_id(1)

        @pl.when(qi == 0)
        def _():
            dk_ref[...] = jnp.zeros_like(dk_ref)
            dv_ref[...] = jnp.zeros_like(dv_ref)

        qv, dov = q_ref[0], do_ref[0]
        u_excl = (_iota2((tk, tk), 0) < _iota2((tk, tk), 1)).astype(f32)
        scale = HEAD_DIM ** -0.5

        def loads(js, slot):
            out = []
            for t, j in enumerate(js):
                out += [pltpu.make_async_copy(w_hbm.at[head, qi, j], wbuf.at[slot, t], sems.at[0, slot, t]),
                        pltpu.make_async_copy(s_hbm.at[head, qi, j], sbuf.at[slot, t], sems.at[1, slot, t])]
            return out

        def tiles(js, slot, carry, live=None):
            dq_acc, c_e = carry
            kss = [pl.ds(pl.multiple_of(j * tk, tk), tk) for j in js]
            dw_alls = [_dot(dov, v_ref[0, :, ks]) for ks in kss]
            stage, befores = [], []
            for t, dw_all in enumerate(dw_alls):
                es, ebs, esums = [], [], []
                for rs in blocks:
                    e = wbuf[slot, t, rs].astype(f32) * dw_all[rs]
                    if live is not None:
                        e = e * live[t]
                    es.append(e)
                    ebs.append(e.astype(MXU_DTYPE))
                    esums.append(jnp.sum(e, axis=1, keepdims=True))
                stage.append((es, _cat_rows(esums)))
                befores.append(_dot(_cat_rows(ebs), u_excl))
            for t, ((es, esums), before_all, ks) in enumerate(zip(stage, befores, kss)):
                dzs = []
                for n, rs in enumerate(blocks):
                    sig = sbuf[slot, t, rs].astype(f32)
                    if live is not None:
                        sig = sig * live[t]
                    dz = es[n] * (1.0 - sig) - sig * (c_e[rs] + before_all[rs])
                    dzs.append((dz * scale).astype(MXU_DTYPE))
                dz_all = _cat_rows(dzs)
                w_all = wbuf[slot, t] if live is None else wbuf[slot, t] * live[t].astype(MXU_DTYPE)
                c_e = c_e + esums
                dk_ref[0, :, ks] += _dot(qv, dz_all, TN)
                dv_ref[0, :, ks] += _dot(dov, w_all, TN)
                dq_acc = dq_acc + _dot(dz_all, k_ref[0, :, ks], NT)
            return dq_acc, c_e

        n_tiles = qi + 1
        n_groups, rest = n_tiles // grp, n_tiles % grp
        n_passes = n_groups + jnp.minimum(rest, 1)
        group_js = lambda g: [jnp.minimum(grp * g + t, qi) for t in range(grp)]

        for cp in loads(group_js(0), 0):
            cp.start()

        def fetch_next_and_wait(g):
            slot = g % 2

            @pl.when(g + 1 < n_passes)
            def _():
                for cp in loads(group_js(g + 1), 1 - slot):
                    cp.start()

            for cp in loads(group_js(g), slot):
                cp.wait()
            return slot

        def group(g, carry):
            slot = fetch_next_and_wait(g)
            return tiles(group_js(g), slot, carry)

        def last_group(_, carry):
            slot = fetch_next_and_wait(n_groups)
            live = [(grp * n_groups + t <= qi).astype(f32) for t in range(grp)]
            return tiles(group_js(n_groups), slot, carry, live)

        carry = lax.fori_loop(0, n_groups, group, (jnp.zeros((tq, hd), f32), jnp.zeros((tq, 1), f32)))
        dq_acc, _ = lax.fori_loop(0, jnp.minimum(rest, 1), last_group, carry)
        dq_ref[0] = dq_acc

    tile_spec = _bs((1, tq, hd), lambda h, i: (h, i, 0))
    full_t = _bs((1, hd, lp), lambda h, i: (h, 0, 0))
    hbm = pl.BlockSpec(memory_space=pltpu.HBM)
    return _pc(
        body, "sb_bwd", (nh, lp // tq),
        [tile_spec, full_t, full_t, hbm, hbm, tile_spec],
        [tile_spec, full_t, full_t], [_sds((nh, lp, hd)), _sds((nh, hd, lp)), _sds((nh, hd, lp))],
        scratch=[pltpu.VMEM((2, grp, tq, tk), MXU_DTYPE), pltpu.VMEM((2, grp, tq, tk), MXU_DTYPE),
                 pltpu.SemaphoreType.DMA((2, 2, grp))])(q, kt, vt, w_saved, s_saved, do)


def _dn_pre_tile(xs, ba, cw, pv):
    conv = xs[0] * cw[3:4] + xs[1] * cw[2:3] + xs[2] * cw[1:2] + xs[3] * cw[0:1]
    s = _silu(conv)
    bd = _block_diag_ones(QW, HEAD_DIM)
    sq, sk, sv = s[:, :QW], s[:, QW:2 * QW], s[:, 2 * QW:]
    qn = sq * lax.rsqrt(_dotx(sq * sq, bd) + EPS)
    kn = sk * lax.rsqrt(_dotx(sk * sk, bd) + EPS)
    lane = _iota2(ba.shape, 1)
    beta = _sigmoid(ba)
    g = -jnp.exp(pv[0:1]) * _softplus(ba + pv[1:2])
    bg = jnp.where(lane < N_HEADS, beta, jnp.where(lane < 2 * N_HEADS, g, 0.0))
    return qn, kn, sv, bg


def _dn_shifted(cur, prev, first):
    row = _iota2((cur.shape[0], 1), 0)
    out = [cur]
    for k in range(1, DN_CONV):
        head_rows = jnp.where(first, 0.0, pltpu.roll(prev, k, 0))
        out.append(jnp.where(row >= k, pltpu.roll(cur, k, 0), head_rows))
    return tuple(out)


def _dn_pre_fwd(x, ba, cw, pv):
    lp, w3 = x.shape
    tm = _token_tile(lp, 320)

    def body(x_ref, xp_ref, ba_ref, cw_ref, pv_ref, q_ref, k_ref, v_ref, bg_ref):
        xs = _dn_shifted(x_ref[...], xp_ref[...], pl.program_id(0) == 0)
        qn, kn, sv, bg = _dn_pre_tile(xs, ba_ref[...], cw_ref[...], pv_ref[...])
        q_ref[...], k_ref[...], v_ref[...], bg_ref[...] = qn, kn, sv, bg

    tok = lambda wd: _bs((tm, wd), lambda i: (i, 0))
    return _pc(
        body, "dn_pre_fwd", (lp // tm,),
        [tok(w3), _bs((tm, w3), lambda i: (jnp.maximum(i - 1, 0), 0)), tok(128),
         _bs((8, w3), lambda i: (0, 0)), _bs((8, 128), lambda i: (0, 0))],
        [tok(QW), tok(QW), tok(QW), tok(128)],
        [_sds((lp, QW)), _sds((lp, QW)), _sds((lp, QW)), _sds((lp, 128))])(x, x, ba, cw, pv)


def _dn_pre_bwd(x, ba, cw, pv, dq, dk, dv, dbg):
    lp, w3 = x.shape
    tm = _token_tile(lp, 320)

    def body(x_ref, xp_ref, ba_ref, cw_ref, pv_ref, dq_ref, dk_ref, dv_ref, dbg_ref, dxs_ref, dba_ref, dcw_ref, dpv_ref):
        @pl.when(pl.program_id(0) == 0)
        def _():
            dcw_ref[...] = jnp.zeros_like(dcw_ref)
            dpv_ref[...] = jnp.zeros_like(dpv_ref)

        xs = _dn_shifted(x_ref[...], xp_ref[...], pl.program_id(0) == 0)
        _, vjp = jax.vjp(_dn_pre_tile, xs, ba_ref[...], cw_ref[...], pv_ref[...])
        dxs, dba, dcw, dpv = vjp((dq_ref[...], dk_ref[...], dv_ref[...], dbg_ref[...]))
        for k in range(DN_CONV):
            dxs_ref[k] = dxs[k]
        dba_ref[...] = dba
        dcw_ref[...] += dcw
        dpv_ref[...] += dpv

    tok = lambda wd: _bs((tm, wd), lambda i: (i, 0))
    cw_spec, pv_spec = _bs((8, w3), lambda i: (0, 0)), _bs((8, 128), lambda i: (0, 0))
    return _pc(
        body, "dn_pre_bwd", (lp // tm,),
        [tok(w3), _bs((tm, w3), lambda i: (jnp.maximum(i - 1, 0), 0)), tok(128), cw_spec, pv_spec,
         tok(QW), tok(QW), tok(QW), tok(128)],
        [_bs((4, tm, w3), lambda i: (0, i, 0)), tok(128), cw_spec, pv_spec],
        [_sds((4, lp, w3)), _sds((lp, 128)), _sds((8, w3)), _sds((8, 128))])(x, x, ba, cw, pv, dq, dk, dv, dbg)


def _dn_chunk(state, q, k, v, grow, brow):
    nh, c, _ = q.shape
    ii, jj = _iota2((c, c), 0), _iota2((c, c), 1)
    eye = ii == jj
    col = lambda row: jnp.sum(jnp.where(eye, jnp.broadcast_to(row, (nh, c, c)), 0.0), axis=2, keepdims=True)
    gc_row = _dotx(grow, jnp.broadcast_to((ii <= jj).astype(f32), (nh, c, c)), BNN)
    gc_col, b_col = col(gc_row), col(brow)
    decay = jnp.exp(jnp.where(ii >= jj, gc_col - gc_row, -1e30))
    kb = k * b_col
    p = -jnp.where(ii > jj, _bmm_nt(kb, k) * decay, 0.0)
    t_inv = eye.astype(f32) + p
    for _ in range(5):
        p = _bmm3(p, p)
        t_inv = t_inv + _bmm3(t_inv, p)
    egc = jnp.exp(gc_col)
    u = _bmm(t_inv, v * b_col)
    w = _bmm(t_inv, kb * egc)
    qs = q * (q.shape[2] ** -0.5)
    attn = jnp.where(ii >= jj, _bmm_nt(qs, k) * decay, 0.0)
    v_new = u - _bmm(w, state)
    o = _bmm(qs * egc, state) + _bmm(attn, v_new)
    g_last = gc_row[:, :, c - 1:c]
    new_state = state * jnp.exp(g_last) + _bmm_tn(k * jnp.exp(g_last - gc_col), v_new)
    return new_state, o


def _dn_scan_fwd(q, k, v, grow, brow):
    nh, lp, hd = q.shape
    c = DN_CHUNK
    n = lp // c

    def body(q_ref, k_ref, v_ref, g_ref, b_ref, o_ref, st_ref, state_s):
        @pl.when(pl.program_id(0) == 0)
        def _():
            state_s[...] = jnp.zeros_like(state_s)

        st_ref[:, 0] = state_s[...]
        state, o = _dn_chunk(state_s[...], q_ref[...], k_ref[...], v_ref[...], g_ref[:, 0], b_ref[:, 0])
        state_s[...] = state
        o_ref[...] = o

    seq = _bs((nh, c, hd), lambda i: (0, i, 0))
    row = _bs((nh, 1, 1, c), lambda i: (0, i, 0, 0))
    return _pc(body, "dn_scan_fwd", (n,), [seq, seq, seq, row, row],
               [seq, _bs((nh, 1, hd, hd), lambda i: (0, i, 0, 0))],
               [_sds((nh, lp, hd)), _sds((nh, n, hd, hd))],
               scratch=[pltpu.VMEM((nh, hd, hd), f32)])(q, k, v, grow, brow)


def _dn_scan_bwd(q, k, v, grow, brow, states, do):
    nh, lp, hd = q.shape
    c = DN_CHUNK
    n = lp // c

    def body(q_ref, k_ref, v_ref, g_ref, b_ref, st_ref, do_ref, dq_ref, dk_ref, dv_ref, dg_ref, db_ref, dstate_s):
        @pl.when(pl.program_id(0) == 0)
        def _():
            dstate_s[...] = jnp.zeros_like(dstate_s)

        _, vjp = jax.vjp(_dn_chunk, st_ref[:, 0], q_ref[...], k_ref[...], v_ref[...], g_ref[:, 0], b_ref[:, 0])
        dstate, dq, dk, dv, dg, db = vjp((dstate_s[...], do_ref[...]))
        dstate_s[...] = dstate
        dq_ref[...], dk_ref[...], dv_ref[...] = dq, dk, dv
        dg_ref[:, 0], db_ref[:, 0] = dg, db

    seq = _bs((nh, c, hd), lambda i: (0, n - 1 - i, 0))
    row = _bs((nh, 1, 1, c), lambda i: (0, n - 1 - i, 0, 0))
    return _pc(body, "dn_scan_bwd", (n,),
               [seq, seq, seq, row, row, _bs((nh, 1, hd, hd), lambda i: (0, n - 1 - i, 0, 0)), seq],
               [seq, seq, seq, row, row],
               [_sds((nh, lp, hd))] * 3 + [_sds((nh, n, 1, c))] * 2,
               scratch=[pltpu.VMEM((nh, hd, hd), f32)])(q, k, v, grow, brow, states, do)


def _s5_prep(a_re, a_im, log_dt, b_re, b_im):
    dt = jnp.exp(log_dt)
    mag = jnp.exp(a_re * dt)
    ar, ai = mag * jnp.cos(a_im * dt), mag * jnp.sin(a_im * dt)
    den = a_re * a_re + a_im * a_im
    cr = ((ar - 1.0) * a_re + ai * a_im) / den
    ci = (ai * a_re - (ar - 1.0) * a_im) / den
    cr3, ci3 = cr[:, None, :], ci[:, None, :]
    return ar, ai, cr3 * b_re - ci3 * b_im, cr3 * b_im + ci3 * b_re


def _s5_prep_fwd(a_re, a_im, log_dt, b_re, b_im):
    def body(ar_ref, ai_ref, dt_ref, br_ref, bi_ref, *outs):
        for ref, val in zip(outs, _s5_prep(ar_ref[...], ai_ref[...], dt_ref[...], br_ref[...], bi_ref[...])):
            ref[...] = val

    return pl.pallas_call(body, name="s5_prep_fwd",
                          out_shape=[_sds(a_re.shape), _sds(a_re.shape), _sds(b_re.shape), _sds(b_re.shape)],
                          )(a_re, a_im, log_dt, b_re, b_im)


def _s5_prep_bwd(a_re, a_im, log_dt, b_re, b_im, d_ar, d_ai, d_br, d_bi):
    def body(ar_ref, ai_ref, dt_ref, br_ref, bi_ref, g0, g1, g2, g3, *outs):
        _, vjp = jax.vjp(_s5_prep, ar_ref[...], ai_ref[...], dt_ref[...], br_ref[...], bi_ref[...])
        for ref, val in zip(outs, vjp((g0[...], g1[...], g2[...], g3[...]))):
            ref[...] = val

    return pl.pallas_call(body, name="s5_prep_bwd",
                          out_shape=[_sds(a_re.shape), _sds(a_re.shape), _sds(log_dt.shape), _sds(b_re.shape),
                                     _sds(b_re.shape)])(a_re, a_im, log_dt, b_re, b_im, d_ar, d_ai, d_br, d_bi)


def _s5_block_len(lp):
    return 128 if lp % 128 == 0 else 64


def _s5_powers(ar, ai, tb):
    out = []
    k = 1
    while k < tb:
        out.append((ar, ai))
        ar, ai = ar * ar - ai * ai, 2.0 * ar * ai
        k *= 2
    return out


def _s5_scan_rows(xr, xi, pows, reverse):
    tb = xr.shape[0]
    row = _iota2((tb, 1), 0)
    k = 1
    for pr, pi in pows:
        if reverse:
            keep = row < tb - k
            sr, si = pltpu.roll(xr, tb - k, 0), pltpu.roll(xi, tb - k, 0)
        else:
            keep = row >= k
            sr, si = pltpu.roll(xr, k, 0), pltpu.roll(xi, k, 0)
        sr, si = jnp.where(keep, sr, 0.0), jnp.where(keep, si, 0.0)
        xr, xi = xr + pr * sr - pi * si, xi + pr * si + pi * sr
        k *= 2
    return xr, xi


def _s5_slab_mm(x, w_ref, dims=NN):
    a = x.shape[1] // S5_SLABS
    return jnp.concatenate([_dot(x[:, j * a:(j + 1) * a], w_ref[j], dims) for j in range(S5_SLABS)], axis=1)


def _s5_power_table(ar, ai, tb, reverse):
    at = _iota2((tb, 1), 0) == (tb - 1 if reverse else 0)
    return _s5_scan_rows(jnp.where(at, ar, 0.0), jnp.where(at, ai, 0.0), _s5_powers(ar, ai, tb), reverse)


def _s5_states(u, carry_r, carry_i, ar, ai, b8r_ref, b8i_ref, pw_ref, tb):
    sr, si = _s5_scan_rows(_s5_slab_mm(u, b8r_ref), _s5_slab_mm(u, b8i_ref), _s5_powers(ar, ai, tb), False)
    pr, pi = pw_ref[0], pw_ref[1]
    return sr + pr * carry_r - pi * carry_i, si + pr * carry_i + pi * carry_r


def _s5_scan_fwd(u, ar, ai, b8r, b8i, c8r, c8i, dvec):
    lp = u.shape[0]
    tb = _s5_block_len(lp)
    nblk = lp // tb

    def body(u_ref, ar_ref, ai_ref, b8r_ref, b8i_ref, c8r_ref, c8i_ref, d_ref, y_ref, cin_ref, carry_s, pw_s):
        @pl.when(pl.program_id(0) == 0)
        def _():
            carry_s[...] = jnp.zeros_like(carry_s)
            pw_s[0], pw_s[1] = _s5_power_table(ar_ref[...], ai_ref[...], tb, False)

        cin_ref[0] = carry_s[...]
        uv = u_ref[...]
        sr, si = _s5_states(uv, carry_s[0:1], carry_s[1:2], ar_ref[...], ai_ref[...], b8r_ref, b8i_ref, pw_s, tb)
        carry_s[0:1] = sr[tb - 1:tb]
        carry_s[1:2] = si[tb - 1:tb]
        y_ref[...] = _s5_slab_mm(sr, c8r_ref) - _s5_slab_mm(si, c8i_ref) + d_ref[...] * uv

    const = lambda shape: _bs(shape, lambda i: (0,) * len(shape))
    return _pc(
        body, "s5_scan_fwd", (nblk,),
        [_bs((tb, S5_W), lambda i: (i, 0)), const((1, S5_N)), const((1, S5_N)),
         const((S5_SLABS, 128, 512)), const((S5_SLABS, 128, 512)),
         const((S5_SLABS, 512, 128)), const((S5_SLABS, 512, 128)), const((1, S5_W))],
        [_bs((tb, S5_W), lambda i: (i, 0)), _bs((1, 8, S5_N), lambda i: (i, 0, 0))],
        [_sds((lp, S5_W)), _sds((nblk, 8, S5_N))],
        scratch=[pltpu.VMEM((8, S5_N), f32), pltpu.VMEM((2, tb, S5_N), f32)])(u, ar, ai, b8r, b8i, c8r, c8i, dvec)


def _s5_scan_bwd(u, dy, cin, ar, ai, b8r, b8i, c8r, c8i, dvec):
    lp = u.shape[0]
    tb = _s5_block_len(lp)
    nblk = lp // tb

    def body(u_ref, dy_ref, cin_ref, ar_ref, ai_ref, b8r_ref, b8i_ref, c8r_ref, c8i_ref, d_ref,
             du_ref, dab_ref, db8r_ref, db8i_ref, dc8r_ref, dc8i_ref, dd_ref, lam_s, pw_s, qw_s):
        @pl.when(pl.program_id(0) == 0)
        def _():
            lam_s[...] = jnp.zeros_like(lam_s)
            for ref in (dab_ref, db8r_ref, db8i_ref, dc8r_ref, dc8i_ref, dd_ref):
                ref[...] = jnp.zeros_like(ref)
            pw_s[0], pw_s[1] = _s5_power_table(ar_ref[...], ai_ref[...], tb, False)
            qw_s[0], qw_s[1] = _s5_power_table(ar_ref[...], -ai_ref[...], tb, True)

        uv, dyv = u_ref[...], dy_ref[...]
        a_r, a_i = ar_ref[...], ai_ref[...]
        cin_r, cin_i = cin_ref[0, 0:1], cin_ref[0, 1:2]
        sr, si = _s5_states(uv, cin_r, cin_i, a_r, a_i, b8r_ref, b8i_ref, pw_s, tb)
        lr, li = _s5_scan_rows(_s5_slab_mm(dyv, c8r_ref, NT), -_s5_slab_mm(dyv, c8i_ref, NT),
                               _s5_powers(a_r, -a_i, tb), True)
        qr, qi = qw_s[0], qw_s[1]
        nr, ni = lam_s[0:1], lam_s[1:2]
        lr, li = lr + qr * nr - qi * ni, li + qr * ni + qi * nr
        lam_s[0:1] = lr[0:1]
        lam_s[1:2] = li[0:1]
        first = _iota2((tb, 1), 0) == 0
        pr = jnp.where(first, cin_r, pltpu.roll(sr, 1, 0))
        pi = jnp.where(first, cin_i, pltpu.roll(si, 1, 0))
        dab_ref[0:1] += jnp.sum(lr * pr + li * pi, axis=0, keepdims=True)
        dab_ref[1:2] += jnp.sum(li * pr - lr * pi, axis=0, keepdims=True)
        du_ref[...] = _s5_slab_mm(lr, b8r_ref, NT) + _s5_slab_mm(li, b8i_ref, NT) + d_ref[...] * dyv
        dd_ref[...] += jnp.sum(dyv * uv, axis=0, keepdims=True)
        for j in range(S5_SLABS):
            us, dys = uv[:, j * 128:(j + 1) * 128], dyv[:, j * 128:(j + 1) * 128]
            st = slice(j * 512, (j + 1) * 512)
            db8r_ref[j] += _dot(us, lr[:, st], TN)
            db8i_ref[j] += _dot(us, li[:, st], TN)
            dc8r_ref[j] += _dot(sr[:, st], dys, TN)
            dc8i_ref[j] -= _dot(si[:, st], dys, TN)

    const = lambda shape: _bs(shape, lambda i: (0,) * len(shape))
    rev = _bs((tb, S5_W), lambda i: (nblk - 1 - i, 0))
    return _pc(
        body, "s5_scan_bwd", (nblk,),
        [rev, rev, _bs((1, 8, S5_N), lambda i: (nblk - 1 - i, 0, 0)), const((1, S5_N)), const((1, S5_N)),
         const((S5_SLABS, 128, 512)), const((S5_SLABS, 128, 512)),
         const((S5_SLABS, 512, 128)), const((S5_SLABS, 512, 128)), const((1, S5_W))],
        [rev, const((8, S5_N)), const((S5_SLABS, 128, 512)), const((S5_SLABS, 128, 512)),
         const((S5_SLABS, 512, 128)), const((S5_SLABS, 512, 128)), const((1, S5_W))],
        [_sds((lp, S5_W)), _sds((8, S5_N)), _sds((S5_SLABS, 128, 512)), _sds((S5_SLABS, 128, 512)),
         _sds((S5_SLABS, 512, 128)), _sds((S5_SLABS, 512, 128)), _sds((1, S5_W))],
        scratch=[pltpu.VMEM((8, S5_N), f32), pltpu.VMEM((2, tb, S5_N), f32), pltpu.VMEM((2, tb, S5_N), f32)],
    )(u, dy, cin, ar, ai, b8r, b8i, c8r, c8i, dvec)


def _mix_tile(osb, odn, z, ys5, g_sb, g_dn, w_glu, b_glu, g_s5):
    bd = _block_diag_ones(QW, HEAD_DIM)
    tile4 = ((_iota2((HEAD_DIM, QW), 1) % HEAD_DIM) == _iota2((HEAD_DIM, QW), 0)).astype(f32)
    seg_rms = lambda x: x * lax.rsqrt(_dotx(x * x, bd) * (1.0 / HEAD_DIM) + EPS)
    sbn = seg_rms(osb) * _dotx(g_sb, tile4)
    dnn = seg_rms(odn) * _dotx(g_dn, tile4) * _silu(z)
    y = _gelu_tanh(ys5)
    glu = y * _sigmoid(_mm(y, w_glu) + b_glu)
    return jnp.concatenate([sbn, dnn, _rms(glu, g_s5)], axis=1)


def _mixout_fwd(h, osb, odn, z, ys5, g_sb, g_dn, w_glu, b_glu, g_s5, w_out):
    lp, d = h.shape
    tm = _token_tile(lp)

    def body(h_ref, osb_ref, odn_ref, z_ref, ys_ref, gsb_ref, gdn_ref, wg_ref, bg_ref, gs5_ref, wo_ref, o_ref):
        mixed = _mix_tile(osb_ref[...], odn_ref[...], z_ref[...], ys_ref[...], gsb_ref[...], gdn_ref[...],
                          wg_ref[...], bg_ref[...], gs5_ref[...])
        o_ref[...] = h_ref[...] + _dot(mixed, wo_ref[...])

    tok = lambda wd: _bs((tm, wd), lambda i: (i, 0))
    const = lambda shape: _bs(shape, lambda i: (0,) * len(shape))
    return _pc(
        body, "mixout_fwd", (lp // tm,),
        [tok(d), tok(QW), tok(QW), tok(QW), tok(S5_W), const((1, HEAD_DIM)), const((1, HEAD_DIM)),
         const((S5_W, S5_W)), const((1, S5_W)), const((1, S5_W)), const((d, d))],
        tok(d), _sds((lp, d)))(h, osb, odn, z, ys5, g_sb, g_dn, w_glu, b_glu, g_s5, w_out)


def _mixout_bwd(dh, osb, odn, z, ys5, g_sb, g_dn, w_glu, b_glu, g_s5, w_out):
    lp, d = dh.shape
    tm = _token_tile(lp)

    def body(dh_ref, osb_ref, odn_ref, z_ref, ys_ref, gsb_ref, gdn_ref, wg_ref, bg_ref, gs5_ref, wo_ref,
             dosb_ref, dodn_ref, dz_ref, dys_ref, dgsb_ref, dgdn_ref, dwg_ref, dbg_ref, dgs5_ref, dwo_ref):
        accs = (dgsb_ref, dgdn_ref, dwg_ref, dbg_ref, dgs5_ref)

        @pl.when(pl.program_id(0) == 0)
        def _():
            for ref in accs + (dwo_ref,):
                ref[...] = jnp.zeros_like(ref)

        mixed, vjp = jax.vjp(_mix_tile, osb_ref[...], odn_ref[...], z_ref[...], ys_ref[...], gsb_ref[...],
                             gdn_ref[...], wg_ref[...], bg_ref[...], gs5_ref[...])
        dhv = dh_ref[...]
        dwo_ref[...] += _dot(mixed, dhv, TN)
        grads = vjp(_dot(dhv, wo_ref[...], NT))
        for ref, val in zip((dosb_ref, dodn_ref, dz_ref, dys_ref), grads[:4]):
            ref[...] = val
        for ref, val in zip(accs, grads[4:]):
            ref[...] += val

    tok = lambda wd: _bs((tm, wd), lambda i: (i, 0))
    const = lambda shape: _bs(shape, lambda i: (0,) * len(shape))
    params = [const((1, HEAD_DIM)), const((1, HEAD_DIM)), const((S5_W, S5_W)), const((1, S5_W)), const((1, S5_W))]
    return _pc(
        body, "mixout_bwd", (lp // tm,),
        [tok(d), tok(QW), tok(QW), tok(QW), tok(S5_W)] + params + [const((d, d))],
        [tok(QW), tok(QW), tok(QW), tok(S5_W)] + params + [const((d, d))],
        [_sds((lp, QW))] * 3 + [_sds((lp, S5_W)), _sds((1, HEAD_DIM)), _sds((1, HEAD_DIM)), _sds((S5_W, S5_W)),
                                _sds((1, S5_W)), _sds((1, S5_W)), _sds((d, d))],
    )(dh, osb, odn, z, ys5, g_sb, g_dn, w_glu, b_glu, g_s5, w_out)


def _loss_fwd_bwd(h, g, target, n_real):
    lp, d = h.shape
    tm = _token_tile(lp)

    def body(h_ref, g_ref, t_ref, loss_ref, dh_ref, dg_ref):
        i = pl.program_id(0)

        @pl.when(i == 0)
        def _():
            loss_ref[...] = jnp.zeros_like(loss_ref)
            dg_ref[...] = jnp.zeros_like(dg_ref)

        pos = i * tm + _iota2((tm, 1), 0)
        real = ((pos >= N_META) & (pos < n_real)).astype(f32)
        y, vjp = jax.vjp(_rms, h_ref[...], g_ref[...])
        err = (y - t_ref[...]) * real
        loss_ref[...] += 0.5 * jnp.sum(jnp.mean(err * err, axis=1, keepdims=True))
        dx, dg = vjp(err * (1.0 / d))
        dh_ref[...] = dx
        dg_ref[...] += dg

    tok = _bs((tm, d), lambda i: (i, 0))
    return _pc(body, "loss_fwd_bwd", (lp // tm,), [tok, _bs((1, d), lambda i: (0, 0)), tok],
               [_bs((8, 128), lambda i: (0, 0)), tok, _bs((1, d), lambda i: (0, 0))],
               [_sds((8, 128)), _sds((lp, d)), _sds((1, d))])(h, g, target)


def _row_tile(rows):
    for t in (256, 128, 64, 32, 16, 8):
        if rows % t == 0:
            return t
    raise ValueError(rows)


def _adamw(w, g, m, v):
    rows = w.shape[0]
    tr = _row_tile(rows)

    def body(w_ref, g_ref, m_ref, v_ref, d_ref, mo_ref, vo_ref):
        gv = g_ref[...]
        m_new = ADAM_B1 * m_ref[...] + (1.0 - ADAM_B1) * gv
        v_new = ADAM_B2 * v_ref[...] + (1.0 - ADAM_B2) * (gv * gv)
        m_hat = m_new / (1.0 - ADAM_B1 ** ADAM_STEP)
        v_hat = v_new / (1.0 - ADAM_B2 ** ADAM_STEP)
        d_ref[...] = -ADAM_LR * (m_hat / (jnp.sqrt(v_hat) + ADAM_EPS) + ADAM_WD * w_ref[...])
        mo_ref[...] = m_new
        vo_ref[...] = v_new

    blk = _bs((tr, FLAT_W), lambda i: (i, 0))
    return _pc(body, "adamw", (rows // tr,), [blk] * 4, [blk] * 3, [_sds(w.shape)] * 3)(w, g, m, v)


def _sum_leading(x, name):
    n, rows, _ = x.shape
    tr = _row_tile(rows)

    def body(x_ref, o_ref):
        acc = x_ref[0]
        for k in range(1, n):
            acc = acc + x_ref[k]
        o_ref[...] = acc

    return _pc(body, name, (rows // tr,), [_bs((n, tr, FLAT_W), lambda i: (0, i, 0))],
               _bs((tr, FLAT_W), lambda i: (i, 0)), _sds((rows, FLAT_W)))(x)


def _pair_add(a, b, name):
    rows = a.shape[0]
    tr = _row_tile(rows)

    def body(a_ref, b_ref, o_ref):
        o_ref[...] = a_ref[...] + b_ref[...]

    blk = _bs((tr, FLAT_W), lambda i: (i, 0))
    return _pc(body, name, (rows // tr,), [blk, blk], blk, _sds(a.shape))(a, b)


_CHIP_FLIPS = ((1, 0, 0), (0, 1, 0), (1, 1, 0))
_ALL_FLIPS = tuple((a, b, c) for a in (0, 1) for b in (0, 1) for c in (0, 1) if (a, b, c) != (0, 0, 0))
_CORE_FLIP = ((0, 0, 1),)
_D2D_STREAMS = 4


def _exchange(name, arrays, out_shapes, flips, plan):
    n_in = len(arrays)

    def body(*refs):
        ins, outs = refs[:n_in], refs[n_in:n_in + len(out_shapes)]
        send_sems, recv_sems, local_sems = refs[n_in + len(out_shapes):]
        me = (lax.axis_index("x"), lax.axis_index("y"), lax.axis_index("c"))
        local = [pltpu.make_async_copy(s, d, local_sems.at[n]) for n, (s, d) in enumerate(plan(me, None, ins, outs))]
        for cp in local:
            cp.start()
        sent, k = [], 0
        for f in flips:
            peer = tuple(1 - m if fl else m for m, fl in zip(me, f))
            for s, d in plan(me, peer, ins, outs):
                cp = pltpu.make_async_remote_copy(src_ref=s, dst_ref=d, send_sem=send_sems.at[k],
                                                  recv_sem=recv_sems.at[k], device_id=peer,
                                                  device_id_type=pl.DeviceIdType.MESH)
                cp.start()
                sent.append(cp)
                k += 1
        for cp in sent:
            cp.wait_recv()
        for cp in sent:
            cp.wait_send()
        for cp in local:
            cp.wait()

    me0 = (0, 0, 0)
    n_remote = sum(len(_plan_count(plan, me0, f, arrays, out_shapes)) for f in flips)
    n_local = len(_plan_count(plan, me0, None, arrays, out_shapes))
    hbm = pl.BlockSpec(memory_space=pltpu.HBM)
    return pl.pallas_call(
        body, name=name, in_specs=[hbm] * n_in, out_specs=[hbm] * len(out_shapes), out_shape=list(out_shapes),
        scratch_shapes=[pltpu.SemaphoreType.DMA((n_remote,)), pltpu.SemaphoreType.DMA((n_remote,)),
                        pltpu.SemaphoreType.DMA((max(n_local, 1),))],
        compiler_params=pltpu.CompilerParams(has_side_effects=True))(*arrays)


class _FakeRef:
    def __init__(self):
        self.at = self

    def __getitem__(self, idx):
        return self


def _plan_count(plan, me, flip, arrays, out_shapes):
    peer = None if flip is None else me
    return plan(me, peer, [_FakeRef() for _ in arrays], [_FakeRef() for _ in out_shapes])


def _chip_index(dev):
    return 2 * dev[0] + dev[1]


def _all_gather_chips(name, arrays):
    n = len(arrays)

    def body(*refs):
        ins, outs = refs[:n], refs[n:2 * n]
        send_sems, recv_sems, local_sems = refs[2 * n:]
        x, y, c = lax.axis_index("x"), lax.axis_index("y"), lax.axis_index("c")
        sibling = (x, y, 1 - c)
        chips = [(1 - x, y), (x, 1 - y), (1 - x, 1 - y)]
        mine = 2 * x + y

        def copy(k, src, dst, to):
            return pltpu.make_async_remote_copy(src_ref=src, dst_ref=dst, send_sem=send_sems.at[k],
                                                recv_sem=recv_sems.at[k], device_id=to,
                                                device_id_type=pl.DeviceIdType.MESH)

        local = [pltpu.make_async_copy(ins[a], outs[a].at[mine], local_sems.at[a]) for a in range(n)]
        for cp in local:
            cp.start()
        first = [copy(j * n + a, ins[a].at[c], outs[a].at[mine, c], (*chip, c))
                 for j, chip in enumerate(chips) for a in range(n)]
        for cp in first:
            cp.start()
        passed = []
        for j, chip in enumerate(chips):
            for a in range(n):
                landed = outs[a].at[_chip_index(chip), c]
                copy(j * n + a, landed, landed, sibling).wait_recv()
                cp = copy(3 * n + j * n + a, landed, landed, sibling)
                cp.start()
                passed.append(cp)
        for j, chip in enumerate(chips):
            for a in range(n):
                other = outs[a].at[_chip_index(chip), 1 - c]
                copy(3 * n + j * n + a, other, other, sibling).wait_recv()
        for cp in first + passed:
            cp.wait_send()
        for cp in local:
            cp.wait()

    hbm = pl.BlockSpec(memory_space=pltpu.HBM)
    return pl.pallas_call(
        body, name=name, in_specs=[hbm] * n, out_specs=[hbm] * n,
        out_shape=[_sds((N_CHIPS,) + a.shape, a.dtype) for a in arrays],
        scratch_shapes=[pltpu.SemaphoreType.DMA((6 * n,)), pltpu.SemaphoreType.DMA((6 * n,)),
                        pltpu.SemaphoreType.DMA((n,))],
        compiler_params=pltpu.CompilerParams(has_side_effects=True))(*arrays)


def _all_gather_devices(name, arr):
    def plan(me, peer, ins, outs):
        return [(ins[0], outs[0].at[4 * me[0] + 2 * me[1] + me[2]])]

    return _exchange(name, [arr], [_sds((8,) + arr.shape, arr.dtype)], _ALL_FLIPS, plan)[0]


def _swap_half_with_sibling(name, g):
    step = g.shape[2] // _D2D_STREAMS

    def plan(me, peer, ins, outs):
        if peer is None:
            return []
        return [(ins[0].at[k, 1 - me[2], pl.ds(r * step, step)], outs[0].at[k, pl.ds(r * step, step)])
                for k in range(N_CHIPS) for r in range(_D2D_STREAMS)]

    return _exchange(name, [g], [_sds((N_CHIPS,) + g.shape[2:], g.dtype)], _CORE_FLIP, plan)[0]


def _scatter_to_chips(name, s):
    def plan(me, peer, ins, outs):
        to = me if peer is None else peer
        return [(ins[0].at[_chip_index(to)], outs[0].at[_chip_index(me)])]

    return _exchange(name, [s], [_sds(s.shape, s.dtype)], _CHIP_FLIPS, plan)[0]


def _share_with_sibling(name, r):
    step = r.shape[0] // (4 * _D2D_STREAMS)

    def plan(me, peer, ins, outs):
        return [(ins[0].at[pl.ds(n * step, step)], outs[0].at[me[2], pl.ds(n * step, step)])
                for n in range(4 * _D2D_STREAMS)]

    return _exchange(name, [r], [_sds((2,) + r.shape, r.dtype)], _CORE_FLIP, plan)[0]


def _to_heads(x):
    return x.reshape(x.shape[0], N_HEADS, HEAD_DIM).transpose(1, 0, 2)


def _from_heads(x):
    return x.transpose(1, 0, 2).reshape(x.shape[1], QW)


def _to_heads_t(x):
    return x.T.reshape(N_HEADS, HEAD_DIM, x.shape[0])


def _from_heads_t(x):
    return x.reshape(QW, x.shape[2]).T


def _shift_rows(x, k):
    if k == 0:
        return x
    z = jnp.zeros((abs(k),) + x.shape[1:], x.dtype)
    return jnp.concatenate([z, x[:-k]], axis=0) if k > 0 else jnp.concatenate([x[-k:], z], axis=0)


def _reorder_w_in(w):
    o = 3 * QW + 3 * QW + QW
    main = jnp.concatenate([w[:, :o], w[:, o + 2 * N_HEADS:]], axis=1)
    ba = jnp.pad(w[:, o:o + 2 * N_HEADS], ((0, 0), (0, 128 - 2 * N_HEADS)))
    return jnp.concatenate([main, ba], axis=1)


def _restore_w_in(w):
    o = 3 * QW + 3 * QW + QW
    return jnp.concatenate([w[:, :o], w[:, PROJ_W - 128:PROJ_W - 128 + 2 * N_HEADS], w[:, o:PROJ_W - 128]], axis=1)


def _slab_embed_b(b):
    x = b.reshape(S5_SLABS, 8, S5_C, S5_P)
    eye = jnp.eye(8, dtype=b.dtype)
    return (x[:, :, :, None, :] * eye[None, :, None, :, None]).reshape(S5_SLABS, 8 * S5_C, 8 * S5_P)


def _slab_extract_b(m):
    x = m.reshape(S5_SLABS, 8, S5_C, 8, S5_P)
    return jnp.stack([x[:, g, :, g, :] for g in range(8)], axis=1).reshape(S5_G, S5_C, S5_P)


def _slab_embed_c(c):
    x = c.reshape(S5_SLABS, 8, S5_C, S5_P).transpose(0, 1, 3, 2)
    eye = jnp.eye(8, dtype=c.dtype)
    return (x[:, :, :, None, :] * eye[None, :, None, :, None]).reshape(S5_SLABS, 8 * S5_P, 8 * S5_C)


def _slab_extract_c(m):
    x = m.reshape(S5_SLABS, 8, S5_P, 8, S5_C)
    return jnp.stack([x[:, g, :, g, :] for g in range(8)], axis=1).transpose(0, 1, 3, 2).reshape(S5_G, S5_C, S5_P)


def _piece_rows(shape):
    return -(-math.prod(shape) // (8 * FLAT_W)) * 8


def _pack_rows(parts, lead, row_align):
    rows = []
    for p in parts:
        flat = p.reshape(lead + (-1,))
        r = _piece_rows(p.shape[len(lead):])
        flat = jnp.pad(flat, [(0, 0)] * len(lead) + [(0, r * FLAT_W - flat.shape[-1])])
        rows.append(flat.reshape(lead + (r, FLAT_W)))
    total = sum(r.shape[-2] for r in rows)
    rows.append(jnp.zeros(lead + ((-total) % row_align, FLAT_W), parts[0].dtype))
    return jnp.concatenate(rows, axis=len(lead))


def _unpack_rows(flat, shapes):
    out, off = [], 0
    for s in shapes:
        r = _piece_rows(s)
        out.append(flat[off:off + r].reshape(-1)[:math.prod(s)].reshape(s))
        off += r
    return out


_SHARDED = ("ffn1_w_gate", "ffn1_w_up", "ffn1_w_down", "w_in", "s5_w_glu", "w_out",
            "ffn2_w_gate", "ffn2_w_up", "ffn2_w_down", "meta_tokens", "dn_conv_w")
_MATMUL_W = _SHARDED[:9]
_WEIGHTS = ("meta_tokens", "ffn1_norm", "ffn1_w_gate", "ffn1_w_up", "ffn1_w_down", "mix_norm", "w_in", "sb_out_norm",
            "dn_conv_w", "dn_a_log", "dn_dt_bias", "dn_out_norm", "s5_a_re", "s5_a_im", "s5_log_dt", "s5_b_re",
            "s5_b_im", "s5_c_re", "s5_c_im", "s5_d", "s5_w_glu", "s5_b_glu", "s5_out_norm", "w_out", "ffn2_norm",
            "ffn2_w_gate", "ffn2_w_up", "ffn2_w_down", "final_norm")
_REPLICATED = tuple(n for n in _WEIGHTS if n not in _SHARDED)


def _chip_major(name, g):
    if name in ("ffn1_w_gate", "ffn1_w_up", "ffn2_w_gate", "ffn2_w_up", "ffn1_w_down", "ffn2_w_down"):
        return g.transpose(1, 0, 2, 3)
    if name == "w_in":
        return g.reshape(2, D_MODEL, N_CHIPS, IN_WIDTH // N_CHIPS).transpose(2, 0, 1, 3)
    if name in ("w_out", "s5_w_glu"):
        return g.reshape(2, N_CHIPS, g.shape[1] // N_CHIPS, g.shape[2]).transpose(1, 0, 2, 3)
    if name == "meta_tokens":
        return g.reshape(N_META, N_CHIPS, D_MODEL // N_CHIPS).transpose(1, 0, 2)
    if name == "dn_conv_w":
        return g.reshape(2, DN_CONV, N_CHIPS, 3 * QW // N_CHIPS).transpose(2, 0, 1, 3)
    raise ValueError(name)


def _layer_forward(h, p):
    lp = h.shape[0]
    h1, *ffn1_saved = _ffn_fwd(h, p["ffn1_norm"], *p["ffn1"])
    q, k, v, dnx, z, u, ba = _inproj_fwd(h1, p["mix_norm"], p["w_in"].astype(MXU_DTYPE))
    qh, kh, vh = _to_heads(q), _to_heads_t(k), _to_heads_t(v)
    osb, sb_w, sb_sig = _sb_fwd(qh, kh, vh)
    dq_, dk_, dv_, bg = _dn_pre_fwd(dnx, ba, p["cw"], p["pv"])
    dqh, dkh, dvh = _to_heads(dq_), _to_heads(dk_), _to_heads(dv_)
    brow = bg[:, :N_HEADS].T.reshape(N_HEADS, lp // DN_CHUNK, 1, DN_CHUNK)
    grow = bg[:, N_HEADS:2 * N_HEADS].T.reshape(N_HEADS, lp // DN_CHUNK, 1, DN_CHUNK)
    odn, states = _dn_scan_fwd(dqh, dkh, dvh, grow, brow)
    ar, ai, bre, bim = _s5_prep_fwd(p["s5_a_re"], p["s5_a_im"], p["s5_log_dt"], p["s5_b_re"], p["s5_b_im"])
    s5t = (ar.reshape(1, S5_N), ai.reshape(1, S5_N), _slab_embed_b(bre), _slab_embed_b(bim),
           _slab_embed_c(p["s5_c_re"]), _slab_embed_c(p["s5_c_im"]), p["s5_d"])
    ys5, cin = _s5_scan_fwd(u, *s5t)
    osb_t, odn_t = _from_heads(osb), _from_heads(odn)
    h2 = _mixout_fwd(h1, osb_t, odn_t, z, ys5, p["sb_out_norm"], p["dn_out_norm"], p["s5_w_glu"], p["s5_b_glu"],
                     p["s5_out_norm"], p["w_out"].astype(MXU_DTYPE))
    h3, *ffn2_saved = _ffn_fwd(h2, p["ffn2_norm"], *p["ffn2"])
    saved = dict(h0=h, h1=h1, h2=h2, ffn1=ffn1_saved, ffn2=ffn2_saved, qh=qh, kh=kh, vh=vh, sb_w=sb_w, sb_sig=sb_sig, dnx=dnx, ba=ba, dqh=dqh, dkh=dkh, dvh=dvh,
                 grow=grow, brow=brow, states=states, odn_t=odn_t, osb_t=osb_t, z=z, u=u, ys5=ys5, cin=cin, s5t=s5t)
    return h3, saved


def _ffn_backward(h, g, w3, dy, fwd_saved):
    xn, gate, up = fwd_saved
    dh, dgn, d_gate, d_up, act = _ffn_bwd_dx(h, g, *w3, dy, gate, up)
    dwg, dwu, dwd = _ffn_bwd_dw(xn, dy, d_gate, d_up, act)
    return dh, dgn, dwg, dwu, dwd


def _layer_backward(dh3, p, s):
    lp = dh3.shape[0]
    g = {}
    dh2, g["ffn2_norm"], g["ffn2_w_gate"], g["ffn2_w_up"], g["ffn2_w_down"] = _ffn_backward(
        s["h2"], p["ffn2_norm"], p["ffn2"], dh3, s["ffn2"])
    (dosb_t, dodn_t, dz, dys5, g["sb_out_norm"], g["dn_out_norm"], g["s5_w_glu"], g["s5_b_glu"], g["s5_out_norm"],
     g["w_out"]) = _mixout_bwd(dh2, s["osb_t"], s["odn_t"], s["z"], s["ys5"], p["sb_out_norm"], p["dn_out_norm"],
                               p["s5_w_glu"], p["s5_b_glu"], p["s5_out_norm"], p["w_out"].astype(MXU_DTYPE))
    du, dab, db8r, db8i, dc8r, dc8i, g["s5_d"] = _s5_scan_bwd(s["u"], dys5, s["cin"], *s["s5t"])
    g["s5_c_re"], g["s5_c_im"] = _slab_extract_c(dc8r), _slab_extract_c(dc8i)
    g["s5_a_re"], g["s5_a_im"], g["s5_log_dt"], g["s5_b_re"], g["s5_b_im"] = _s5_prep_bwd(
        p["s5_a_re"], p["s5_a_im"], p["s5_log_dt"], p["s5_b_re"], p["s5_b_im"],
        dab[0].reshape(S5_G, S5_P), dab[1].reshape(S5_G, S5_P), _slab_extract_b(db8r), _slab_extract_b(db8i))
    ddq, ddk, ddv, dgrow, dbrow = _dn_scan_bwd(s["dqh"], s["dkh"], s["dvh"], s["grow"], s["brow"], s["states"],
                                               _to_heads(dodn_t))
    dbg = jnp.concatenate([dbrow.reshape(N_HEADS, lp).T, dgrow.reshape(N_HEADS, lp).T,
                           jnp.zeros((lp, 128 - 2 * N_HEADS), f32)], axis=1)
    dxs, dba, g["cw"], g["pv"] = _dn_pre_bwd(s["dnx"], s["ba"], p["cw"], p["pv"], _from_heads(ddq), _from_heads(ddk),
                                             _from_heads(ddv), dbg)
    ddn4 = jnp.stack([_shift_rows(dxs[k], -k) for k in range(DN_CONV)])
    dq, dk_t, dv_t = _sb_bwd(s["qh"], s["kh"], s["vh"], s["sb_w"], s["sb_sig"], _to_heads(dosb_t))
    dh1, g["mix_norm"], g["w_in"] = _inproj_bwd(s["h1"], p["mix_norm"], p["w_in"].astype(MXU_DTYPE), dh2,
                                                _from_heads(dq), _from_heads_t(dk_t), _from_heads_t(dv_t), ddn4,
                                                dz, du, dba)
    dh0, g["ffn1_norm"], g["ffn1_w_gate"], g["ffn1_w_up"], g["ffn1_w_down"] = _ffn_backward(
        s["h0"], p["ffn1_norm"], p["ffn1"], dh1, s["ffn1"])
    return dh0, g


def kernel(x, meta_tokens, ffn1_norm, ffn1_w_gate, ffn1_w_up, ffn1_w_down, mix_norm, w_in, sb_out_norm, dn_conv_w, dn_a_log, dn_dt_bias, dn_out_norm, s5_a_re, s5_a_im, s5_log_dt, s5_b_re, s5_b_im, s5_c_re, s5_c_im, s5_d, s5_w_glu, s5_b_glu, s5_out_norm, w_out, ffn2_norm, ffn2_w_gate, ffn2_w_up, ffn2_w_down, final_norm, loss_target, m_meta_tokens, m_ffn1_norm, m_ffn1_w_gate, m_ffn1_w_up, m_ffn1_w_down, m_mix_norm, m_w_in, m_sb_out_norm, m_dn_conv_w, m_dn_a_log, m_dn_dt_bias, m_dn_out_norm, m_s5_a_re, m_s5_a_im, m_s5_log_dt, m_s5_b_re, m_s5_b_im, m_s5_c_re, m_s5_c_im, m_s5_d, m_s5_w_glu, m_s5_b_glu, m_s5_out_norm, m_w_out, m_ffn2_norm, m_ffn2_w_gate, m_ffn2_w_up, m_ffn2_w_down, m_final_norm, v_meta_tokens, v_ffn1_norm, v_ffn1_w_gate, v_ffn1_w_up, v_ffn1_w_down, v_mix_norm, v_w_in, v_sb_out_norm, v_dn_conv_w, v_dn_a_log, v_dn_dt_bias, v_dn_out_norm, v_s5_a_re, v_s5_a_im, v_s5_log_dt, v_s5_b_re, v_s5_b_im, v_s5_c_re, v_s5_c_im, v_s5_d, v_s5_w_glu, v_s5_b_glu, v_s5_out_norm, v_w_out, v_ffn2_norm, v_ffn2_w_gate, v_ffn2_w_up, v_ffn2_w_down, v_final_norm):
    args = dict(locals())
    w = {n: args[n] for n in _WEIGHTS}
    m = {n: args["m_" + n] for n in _WEIGHTS}
    v = {n: args["v_" + n] for n in _WEIGHTS}
    depth = ffn1_norm.shape[0]
    seq = x.shape[1]
    n_real = N_META + seq
    lp = _padded_len(n_real)

    gathered = _all_gather_chips("gather_weights", [w[n].astype(MXU_DTYPE) for n in _MATMUL_W]
                                 + [w["meta_tokens"].reshape(2, N_META // 2, -1), w["dn_conv_w"]])
    full = dict(zip(_MATMUL_W + ("meta_tokens", "dn_conv_w"), gathered))
    meta_full = full["meta_tokens"].reshape(N_CHIPS, N_META, -1).transpose(1, 0, 2).reshape(N_META, D_MODEL)
    conv_full = full["dn_conv_w"].transpose(1, 2, 0, 3).reshape(depth, DN_CONV, 3 * QW)
    w_in_full = full["w_in"].transpose(1, 2, 0, 3).reshape(depth, D_MODEL, IN_WIDTH)
    w_out_full = full["w_out"].transpose(1, 0, 2, 3).reshape(depth, D_MODEL, D_MODEL)
    w_glu_full = full["s5_w_glu"].transpose(1, 0, 2, 3).reshape(depth, S5_W, S5_W)

    layers = []
    for l in range(depth):
        pv = jnp.pad(jnp.stack([dn_a_log[l], dn_dt_bias[l]]), ((0, 6), (N_HEADS, 128 - 2 * N_HEADS)))
        layers.append(dict(
            ffn1_norm=ffn1_norm[l][None], mix_norm=mix_norm[l][None], ffn2_norm=ffn2_norm[l][None],
            ffn1=(full["ffn1_w_gate"][:, l], full["ffn1_w_up"][:, l], full["ffn1_w_down"][:, l]),
            ffn2=(full["ffn2_w_gate"][:, l], full["ffn2_w_up"][:, l], full["ffn2_w_down"][:, l]),
            w_in=_reorder_w_in(w_in_full[l]), w_out=w_out_full[l], s5_w_glu=w_glu_full[l].astype(f32),
            cw=jnp.pad(conv_full[l], ((0, 8 - DN_CONV), (0, 0))), pv=pv,
            sb_out_norm=sb_out_norm[l][None], dn_out_norm=dn_out_norm[l][None],
            s5_a_re=s5_a_re[l], s5_a_im=s5_a_im[l], s5_log_dt=s5_log_dt[l][:, None],
            s5_b_re=s5_b_re[l].transpose(0, 2, 1), s5_b_im=s5_b_im[l].transpose(0, 2, 1),
            s5_c_re=s5_c_re[l], s5_c_im=s5_c_im[l], s5_d=s5_d[l][None], s5_b_glu=s5_b_glu[l][None],
            s5_out_norm=s5_out_norm[l][None]))

    tail = jnp.zeros((lp - n_real, D_MODEL), f32)
    h = jnp.concatenate([meta_full, x[0], tail], axis=0)
    target = jnp.concatenate([jnp.zeros((N_META, D_MODEL), f32), loss_target[0], tail], axis=0)
    saved = []
    for p in layers:
        h, s = _layer_forward(h, p)
        saved.append(s)
    loss_blk, dh, d_final = _loss_fwd_bwd(h, final_norm[None], target, n_real)
    grads = [None] * depth
    for l in reversed(range(depth)):
        dh, grads[l] = _layer_backward(dh, layers[l], saved[l])
    loss = lax.psum(loss_blk[0, 0], ("x", "y", "c"))
    grad_x = dh[N_META:n_real][None]

    stack = lambda name: jnp.stack([grads[l][name] for l in range(depth)])
    gfull = {n: stack(n) for n in ("ffn1_w_gate", "ffn1_w_up", "ffn1_w_down", "s5_w_glu", "w_out", "ffn2_w_gate",
                                   "ffn2_w_up", "ffn2_w_down")}
    gfull["w_in"] = jnp.stack([_restore_w_in(grads[l]["w_in"]) for l in range(depth)])
    gfull["meta_tokens"] = dh[:N_META]
    gfull["dn_conv_w"] = jnp.stack([grads[l]["cw"][:DN_CONV] for l in range(depth)])
    grep = {n: stack(n).reshape(w[n].shape) for n in ("ffn1_norm", "mix_norm", "sb_out_norm", "dn_out_norm", "s5_a_re",
                                                      "s5_a_im", "s5_log_dt", "s5_c_re", "s5_c_im", "s5_d", "s5_b_glu",
                                                      "s5_out_norm", "ffn2_norm")}
    grep["s5_b_re"] = jnp.stack([grads[l]["s5_b_re"].transpose(0, 2, 1) for l in range(depth)])
    grep["s5_b_im"] = jnp.stack([grads[l]["s5_b_im"].transpose(0, 2, 1) for l in range(depth)])
    grep["dn_a_log"] = jnp.stack([grads[l]["pv"][0, N_HEADS:2 * N_HEADS] for l in range(depth)])
    grep["dn_dt_bias"] = jnp.stack([grads[l]["pv"][1, N_HEADS:2 * N_HEADS] for l in range(depth)])
    grep["final_norm"] = d_final[0]

    shard_shapes = [w[n].shape for n in _SHARDED]
    g_big = _pack_rows([_chip_major(n, gfull[n]) for n in _SHARDED], (N_CHIPS,), BIG_ROWS)
    half_rows = g_big.shape[1] // 2
    g_big = g_big.reshape(N_CHIPS, 2, half_rows, FLAT_W)
    c = lax.axis_index("c")
    mine = lax.dynamic_index_in_dim(g_big, c, axis=1, keepdims=False)
    theirs = _swap_half_with_sibling("grad_pair_swap", g_big)
    pair = _pair_add(mine.reshape(-1, FLAT_W), theirs.reshape(-1, FLAT_W), "grad_pair_add")
    arrived = _scatter_to_chips("grad_scatter", pair.reshape(N_CHIPS, half_rows, FLAT_W))
    reduced = _sum_leading(arrived, "grad_chip_sum")
    g_shard = _share_with_sibling("grad_share", reduced).reshape(-1, FLAT_W)

    rep_shapes = [w[n].shape for n in _REPLICATED]
    g_small = _pack_rows([grep[n] for n in _REPLICATED], (), 64)
    g_rep = _sum_leading(_all_gather_devices("grad_small_gather", g_small), "grad_small_sum")

    pack = lambda d, names, align: _pack_rows([d[n] for n in names], (), align)
    out = {}
    for names, shapes, g_flat, align in ((_SHARDED, shard_shapes, g_shard, BIG_ROWS),
                                         (_REPLICATED, rep_shapes, g_rep, 64)):
        delta, m_new, v_new = _adamw(pack(w, names, align), g_flat, pack(m, names, align), pack(v, names, align))
        for kind, flat in (("grad", g_flat), ("delta", delta), ("new_m", m_new), ("new_v", v_new)):
            for n, a in zip(names, _unpack_rows(flat, shapes)):
                out[kind + "_" + n] = a
    return (loss, grad_x, *[out[k + "_" + n] for k in ("grad", "delta", "new_m", "new_v") for n in _WEIGHTS])
```

```python
import functools
import math

import jax
import jax.numpy as jnp
from jax import lax
from jax.experimental import pallas as pl
from jax.experimental.pallas import tpu as pltpu

f32 = jnp.float32
MXU_DTYPE = jnp.bfloat16
HI = lax.Precision.HIGHEST
NN = (((1,), (0,)), ((), ()))
NT = (((1,), (1,)), ((), ()))
TN = (((0,), (0,)), ((), ()))

EPS = 1e-6
D_MODEL = 1024
N_META = 16
HEAD_DIM = 64
N_HEADS = 4
QW = N_HEADS * HEAD_DIM
DN_CONV = 4
DN_CHUNK = 64
S5_W = 512
S5_G = 32
S5_P = 64
S5_C = 16
S5_N = S5_G * S5_P
S5_SLABS = 4
N_CHIPS = 4
PROJ_W = 2432
IN_WIDTH = 2312
VMEM_LIMIT = 56 * 1024 * 1024

ADAM_LR, ADAM_B1, ADAM_B2, ADAM_EPS, ADAM_WD, ADAM_STEP = 0.001, 0.9, 0.999, 1e-08, 0.01, 10
FLAT_W = 1024
BIG_ROWS = 512


def _dot(a, b, dims=NN):
    return lax.dot_general(a.astype(MXU_DTYPE), b.astype(MXU_DTYPE), dims, preferred_element_type=f32)


def _dotx(a, b, dims=NN):
    return lax.dot_general(a, b, dims, precision=HI, preferred_element_type=f32)


def _split(x):
    if MXU_DTYPE == f32:
        return x, None
    hi = x.astype(MXU_DTYPE)
    return hi, (x - hi.astype(f32)).astype(MXU_DTYPE)


def _dot_split(hi, lo, u01):
    if lo is None:
        return _dotx(hi, u01)
    u = u01.astype(MXU_DTYPE)
    return (lax.dot_general(hi, u, NN, preferred_element_type=f32)
            + lax.dot_general(lo, u, NN, preferred_element_type=f32))


def _dot3(a, b, dims=NN):
    ah, al = _split(a)
    if al is None:
        return _dotx(a, b, dims)
    bh, bl = _split(b)
    d = lambda x, y: lax.dot_general(x, y, dims, preferred_element_type=f32)
    return d(ah, bh) + d(ah, bl) + d(al, bh)


BNN = (((2,), (1,)), ((0,), (0,)))
BNT = (((2,), (2,)), ((0,), (0,)))
BTN = (((1,), (1,)), ((0,), (0,)))


def _with_dot_vjp(dot, kind, batched=False):
    nn, nt, tn = (BNN, BNT, BTN) if batched else (NN, NT, TN)
    dims = {"nn": nn, "nt": nt, "tn": tn}[kind]

    @jax.custom_vjp
    def f(a, b):
        return dot(a, b, dims)

    def fwd(a, b):
        return dot(a, b, dims), (a, b)

    def bwd(res, dy):
        a, b = res
        if kind == "nn":
            return dot(dy, b, nt), dot(a, dy, tn)
        if kind == "nt":
            return dot(dy, b, nn), dot(dy, a, tn)
        return dot(b, dy, nt), dot(a, dy, nn)

    f.defvjp(fwd, bwd)
    return f


_mm = _with_dot_vjp(_dot, "nn")
_bmm = _with_dot_vjp(_dot, "nn", True)
_bmm_nt = _with_dot_vjp(_dot, "nt", True)
_bmm_tn = _with_dot_vjp(_dot, "tn", True)
_bmm3 = _with_dot_vjp(_dot3, "nn", True)


def _rms(x, g):
    return x * lax.rsqrt(jnp.mean(x * x, axis=-1, keepdims=True) + EPS) * g


def _sigmoid(x):
    return 1.0 / (1.0 + jnp.exp(-x))


def _silu(x):
    return x * _sigmoid(x)


def _softplus(x):
    return jnp.maximum(x, 0.0) + jnp.log(1.0 + jnp.exp(-jnp.abs(x)))


def _gelu_tanh(x):
    return 0.5 * x * (1.0 + jnp.tanh(math.sqrt(2.0 / math.pi) * (x + 0.044715 * x * x * x)))


def _iota2(shape, axis):
    return lax.broadcasted_iota(jnp.int32, shape, axis)


def _block_diag_ones(n, blk):
    return ((_iota2((n, n), 0) // blk) == (_iota2((n, n), 1) // blk)).astype(f32)


def _pc(body, name, grid, in_specs, out_specs, out_shape, scratch=(), vmem=VMEM_LIMIT):
    return pl.pallas_call(
        body, name=name, grid=grid, in_specs=in_specs, out_specs=out_specs, out_shape=out_shape,
        scratch_shapes=list(scratch),
        compiler_params=pltpu.CompilerParams(dimension_semantics=("arbitrary",) * len(grid), vmem_limit_bytes=vmem))


def _bs(shape, imap):
    return pl.BlockSpec(shape, imap)


def _sds(shape, dtype=f32):
    return jax.ShapeDtypeStruct(tuple(shape), dtype)


def _token_tile(lp, cap=640):
    for t in (640, 320, 256, 128, 64):
        if t <= cap and lp % t == 0:
            return t
    raise ValueError(lp)


def _padded_len(l):
    return -(-l // 1280) * 1280 if l > 4096 else -(-l // 256) * 256


def _ffn_fwd(h, g, wg, wu, wd):
    lp, d = h.shape
    nch, _, fc = wg.shape
    tm = _token_tile(lp)

    def body(h_ref, g_ref, wg_ref, wu_ref, wd_ref, o_ref, xn_ref, gate_ref, up_ref, xn_s, acc_s):
        j = pl.program_id(1)

        @pl.when(j == 0)
        def _():
            xn_s[...] = _rms(h_ref[...], g_ref[...]).astype(xn_s.dtype)
            acc_s[...] = jnp.zeros_like(acc_s)

        xn = xn_s[...]
        gate = _dot(xn, wg_ref[0])
        up = _dot(xn, wu_ref[0])
        gate_ref[0] = gate.astype(gate_ref.dtype)
        up_ref[0] = up.astype(up_ref.dtype)
        acc_s[...] += _dot(_silu(gate) * up, wd_ref[0])

        @pl.when(j == nch - 1)
        def _():
            o_ref[...] = h_ref[...] + 0.5 * acc_s[...]
            xn_ref[...] = xn_s[...]

    tok = _bs((tm, d), lambda i, j: (i, 0))
    chunk = _bs((1, tm, fc), lambda i, j: (j, i, 0))
    return _pc(
        body, "ffn_fwd", (lp // tm, nch),
        [tok, _bs((1, d), lambda i, j: (0, 0)),
         _bs((1, d, fc), lambda i, j: (j, 0, 0)), _bs((1, d, fc), lambda i, j: (j, 0, 0)),
         _bs((1, fc, d), lambda i, j: (j, 0, 0))],
        [tok, tok, chunk, chunk],
        [_sds((lp, d)), _sds((lp, d), MXU_DTYPE), _sds((nch, lp, fc), MXU_DTYPE), _sds((nch, lp, fc), MXU_DTYPE)],
        scratch=[pltpu.VMEM((tm, d), MXU_DTYPE), pltpu.VMEM((tm, d), f32)])(h, g, wg, wu, wd)


def _ffn_bwd_dx(h, g, wg, wu, wd, dy, gate_saved, up_saved):
    lp, d = h.shape
    nch, _, fc = wg.shape
    tm = _token_tile(lp)

    def body(h_ref, g_ref, wg_ref, wu_ref, wd_ref, dy_ref, gate_ref, up_ref, dh_ref, dgn_ref, dg_ref, du_ref, act_ref,
             dxn_s, dout_s):
        i, j = pl.program_id(0), pl.program_id(1)

        @pl.when((i == 0) & (j == 0))
        def _():
            dgn_ref[...] = jnp.zeros_like(dgn_ref)

        @pl.when(j == 0)
        def _():
            dxn_s[...] = jnp.zeros_like(dxn_s)
            dout_s[...] = (0.5 * dy_ref[...]).astype(dout_s.dtype)

        gate = gate_ref[0].astype(f32)
        up = up_ref[0].astype(f32)
        sig = _sigmoid(gate)
        sl = gate * sig
        dact = _dot(dout_s[...], wd_ref[0], NT)
        d_up = dact * sl
        d_gate = dact * up * sig * (1.0 + gate * (1.0 - sig))
        dg_ref[0] = d_gate.astype(dg_ref.dtype)
        du_ref[0] = d_up.astype(du_ref.dtype)
        act_ref[0] = (sl * up).astype(act_ref.dtype)
        dxn_s[...] += _dot(d_gate, wg_ref[0], NT) + _dot(d_up, wu_ref[0], NT)

        @pl.when(j == nch - 1)
        def _():
            _, vjp = jax.vjp(_rms, h_ref[...], g_ref[...])
            dx, dg = vjp(dxn_s[...])
            dh_ref[...] = dy_ref[...] + dx
            dgn_ref[...] += dg

    tok = _bs((tm, d), lambda i, j: (i, 0))
    chunk = _bs((1, tm, fc), lambda i, j: (j, i, 0))
    return _pc(
        body, "ffn_bwd_dx", (lp // tm, nch),
        [tok, _bs((1, d), lambda i, j: (0, 0)),
         _bs((1, d, fc), lambda i, j: (j, 0, 0)), _bs((1, d, fc), lambda i, j: (j, 0, 0)),
         _bs((1, fc, d), lambda i, j: (j, 0, 0)), tok, chunk, chunk],
        [tok, _bs((1, d), lambda i, j: (0, 0)), chunk, chunk, chunk],
        [_sds((lp, d)), _sds((1, d)),
         _sds((nch, lp, fc), MXU_DTYPE), _sds((nch, lp, fc), MXU_DTYPE), _sds((nch, lp, fc), MXU_DTYPE)],
        scratch=[pltpu.VMEM((tm, d), f32), pltpu.VMEM((tm, d), MXU_DTYPE)],
    )(h, g, wg, wu, wd, dy, gate_saved, up_saved)


def _ffn_bwd_dw(xn, dy, d_gate, d_up, act):
    lp, d = xn.shape
    nch, _, fc = d_gate.shape
    tm = _token_tile(lp)

    def body(xn_ref, dy_ref, dg_ref, du_ref, act_ref, dwg_ref, dwu_ref, dwd_ref):
        @pl.when(pl.program_id(1) == 0)
        def _():
            dwg_ref[...] = jnp.zeros_like(dwg_ref)
            dwu_ref[...] = jnp.zeros_like(dwu_ref)
            dwd_ref[...] = jnp.zeros_like(dwd_ref)

        xn_t = xn_ref[...]
        dwg_ref[0] += _dot(xn_t, dg_ref[0], TN)
        dwu_ref[0] += _dot(xn_t, du_ref[0], TN)
        dwd_ref[0] += _dot(act_ref[0], 0.5 * dy_ref[...], TN)

    tok = _bs((tm, d), lambda j, i: (i, 0))
    chunk = _bs((1, tm, fc), lambda j, i: (j, i, 0))
    return _pc(
        body, "ffn_bwd_dw", (nch, lp // tm), [tok, tok, chunk, chunk, chunk],
        [_bs((1, d, fc), lambda j, i: (j, 0, 0)), _bs((1, d, fc), lambda j, i: (j, 0, 0)),
         _bs((1, fc, d), lambda j, i: (j, 0, 0))],
        [_sds((nch, d, fc)), _sds((nch, d, fc)), _sds((nch, fc, d))])(xn, dy, d_gate, d_up, act)


_PROJ_SPLITS = (QW, QW, QW, 3 * QW, QW, S5_W, 128)


def _inproj_fwd(h, g, w):
    lp, d = h.shape
    tm = _token_tile(lp)

    def body(h_ref, g_ref, w_ref, *outs):
        proj = _dot(_rms(h_ref[...], g_ref[...]), w_ref[...])
        off = 0
        for ref, wd in zip(outs, _PROJ_SPLITS):
            ref[...] = proj[:, off:off + wd]
            off += wd

    return _pc(
        body, "inproj_fwd", (lp // tm,),
        [_bs((tm, d), lambda i: (i, 0)), _bs((1, d), lambda i: (0, 0)), _bs((d, PROJ_W), lambda i: (0, 0))],
        [_bs((tm, wd), lambda i: (i, 0)) for wd in _PROJ_SPLITS],
        [_sds((lp, wd)) for wd in _PROJ_SPLITS])(h, g, w)


def _inproj_bwd(h, g, w, dres, dq, dk, dv, dxs, dz, du, dba):
    lp, d = h.shape
    tm = _token_tile(lp, 320)
    n_tiles = lp // tm

    def body(h_ref, g_ref, w_ref, dres_ref, dq_ref, dk_ref, dv_ref, dxs_ref, nxt_ref, dz_ref, du_ref, dba_ref,
             dh_ref, dgn_ref, dw_ref):
        @pl.when(pl.program_id(0) == 0)
        def _():
            dgn_ref[...] = jnp.zeros_like(dgn_ref)
            dw_ref[...] = jnp.zeros_like(dw_ref)

        row, row8 = _iota2((tm, 1), 0), _iota2((8, 1), 0)
        more = (pl.program_id(0) < n_tiles - 1).astype(f32)
        ddn, tail = dxs_ref[0], jnp.zeros((8, 3 * QW), f32)
        for k in range(1, DN_CONV):
            ddn = ddn + jnp.where(row < tm - k, pltpu.roll(dxs_ref[k], tm - k, 0), 0.0)
            tail = tail + jnp.where(row8 >= 8 - k, pltpu.roll(nxt_ref[k], 8 - k, 0), 0.0)
        ddn = jnp.concatenate([ddn[:tm - 8], ddn[tm - 8:] + more * tail], axis=0)
        dproj = jnp.concatenate(
            [dq_ref[...], dk_ref[...], dv_ref[...], ddn, dz_ref[...], du_ref[...], dba_ref[...]], axis=1)
        xn, vjp = jax.vjp(_rms, h_ref[...], g_ref[...])
        dx, dg = vjp(_dot(dproj, w_ref[...], NT))
        dw_ref[...] += _dot(xn, dproj, TN)
        dh_ref[...] = dres_ref[...] + dx
        dgn_ref[...] += dg

    tok = lambda wd: _bs((tm, wd), lambda i: (i, 0))
    return _pc(
        body, "inproj_bwd", (lp // tm,),
        [tok(d), _bs((1, d), lambda i: (0, 0)), _bs((d, PROJ_W), lambda i: (0, 0)), tok(d),
         tok(QW), tok(QW), tok(QW), _bs((4, tm, 3 * QW), lambda i: (0, i, 0)),
         _bs((4, 8, 3 * QW), lambda i: (0, jnp.minimum((i + 1) * (tm // 8), lp // 8 - 1), 0)),
         tok(QW), tok(S5_W), tok(128)],
        [tok(d), _bs((1, d), lambda i: (0, 0)), _bs((d, PROJ_W), lambda i: (0, 0))],
        [_sds((lp, d)), _sds((1, d)), _sds((d, PROJ_W))])(h, g, w, dres, dq, dk, dv, dxs, dxs, dz, du, dba)


_SB_TQ = 256
_SB_ROWS = 32
_SB_GROUP = 4
_SB_ROWS_BWD = 32
_SB_GROUP_BWD = 4


def _sb_pieces(z, valid):
    t = jnp.exp(-jnp.abs(z))
    sp = jnp.maximum(z, 0.0) + jnp.log(1.0 + t)
    lk = -sp if valid is None else jnp.where(valid, -sp, 0.0)
    return t, sp, lk


def _cat_rows(parts):
    return parts[0] if len(parts) == 1 else jnp.concatenate(parts, axis=0)


def _sb_fwd(q, kt, vt):
    nh, lp, hd = q.shape
    tq = tk = min(_SB_TQ, lp)
    blocks = [slice(r, r + _SB_ROWS) for r in range(0, tq, _SB_ROWS)]

    def body(q_ref, k_ref, v_ref, o_ref, w_hbm, s_hbm, wbuf, sbuf, sems):
        head, qi = pl.program_id(0), pl.program_id(1)
        qv = q_ref[0]
        u_strict = (_iota2((tk, tk), 0) > _iota2((tk, tk), 1)).astype(f32)
        below = _iota2((tq, tk), 1) < _iota2((tq, tk), 0)

        spare = lambda slot, t: lp // tk + slot * _SB_GROUP + t

        def save(slot, t, j):
            return [pltpu.make_async_copy(wbuf.at[slot, t], w_hbm.at[head, qi, j], sems.at[0, slot, t]),
                    pltpu.make_async_copy(sbuf.at[slot, t], s_hbm.at[head, qi, j], sems.at[1, slot, t])]

        def drain(slot):
            for t in range(_SB_GROUP):
                for cp in save(slot, t, spare(slot, t)):
                    cp.wait()

        def idle(slot, t):
            wbuf[slot, t] = jnp.zeros((tq, tk), MXU_DTYPE)
            sbuf[slot, t] = jnp.zeros((tq, tk), MXU_DTYPE)
            for cp in save(slot, t, spare(slot, t)):
                cp.start()

        def tiles(js, carry, slot, masked=False, live=None, first=False):
            if not first:
                drain(slot)
            o_acc, c_after = carry
            kss = [pl.ds(pl.multiple_of(j * tk, tk), tk) for j in js]
            z_alls = [_dot(qv, k_ref[0, :, ks]) * (HEAD_DIM ** -0.5) for ks in kss]
            stage, afters = [], []
            for t, z_all in enumerate(z_alls):
                his, los, logs, sums = [], [], [], []
                for rs in blocks:
                    z = z_all[rs]
                    _, sp, lk = _sb_pieces(z, below[rs] if masked else None)
                    if live is not None:
                        lk = lk * live[t]
                    hi, lo = _split(lk)
                    his.append(hi)
                    los.append(lo)
                    logs.append(z - sp)
                    sums.append(jnp.sum(lk, axis=1, keepdims=True))
                stage.append((logs, _cat_rows(sums)))
                afters.append(_dot_split(_cat_rows(his), None if los[0] is None else _cat_rows(los), u_strict))
            for t, ((logs, sums), after_all) in enumerate(zip(stage, afters)):
                ws, sigs = [], []
                for n, rs in enumerate(blocks):
                    w = jnp.exp(logs[n] + after_all[rs] + c_after[rs])
                    sig = jnp.exp(logs[n])
                    if masked:
                        w, sig = jnp.where(below[rs], w, 0.0), jnp.where(below[rs], sig, 0.0)
                    if live is not None:
                        w = w * live[t]
                    ws.append(w.astype(MXU_DTYPE))
                    sigs.append(sig.astype(MXU_DTYPE))
                w_all = _cat_rows(ws)
                wbuf[slot, t] = w_all
                sbuf[slot, t] = _cat_rows(sigs)
                for cp in save(slot, t, js[t] if live is None else jnp.where(live[t] > 0.0, js[t], spare(slot, t))):
                    cp.start()
                o_acc = o_acc + _dot(w_all, v_ref[0, :, kss[t]], NT)
                c_after = c_after + sums
            for t in range(len(js), _SB_GROUP):
                idle(slot, t)
            return o_acc, c_after

        n_groups, rest = qi // _SB_GROUP, qi % _SB_GROUP
        group = lambda g, c: tiles([qi - 1 - _SB_GROUP * g - n for n in range(_SB_GROUP)], c, (g + 1) % 2)

        def last_group(_, c):
            idx = [rest - 1 - n for n in range(_SB_GROUP)]
            return tiles([jnp.maximum(j, 0) for j in idx], c, (n_groups + 1) % 2,
                         live=[(j >= 0).astype(f32) for j in idx])

        for t in range(_SB_GROUP):
            idle(1, t)
        carry = tiles([qi], (jnp.zeros((tq, hd), f32), jnp.zeros((tq, 1), f32)), 0, masked=True, first=True)
        carry = lax.fori_loop(0, n_groups, group, carry)
        o_acc, _ = lax.fori_loop(0, jnp.minimum(rest, 1), last_group, carry)
        drain(0)
        drain(1)
        o_ref[0] = o_acc

    full_t = _bs((1, hd, lp), lambda h, i: (h, 0, 0))
    hbm = pl.BlockSpec(memory_space=pltpu.HBM)
    return _pc(
        body, "sb_fwd", (nh, lp // tq),
        [_bs((1, tq, hd), lambda h, i: (h, i, 0)), full_t, full_t],
        [_bs((1, tq, hd), lambda h, i: (h, i, 0)), hbm, hbm],
        [_sds((nh, lp, hd))] + [_sds((nh, lp // tq, lp // tk + 2 * _SB_GROUP, tq, tk), MXU_DTYPE)] * 2,
        scratch=[pltpu.VMEM((2, _SB_GROUP, tq, tk), MXU_DTYPE), pltpu.VMEM((2, _SB_GROUP, tq, tk), MXU_DTYPE),
                 pltpu.SemaphoreType.DMA((2, 2, _SB_GROUP))])(q, kt, vt)


def _sb_bwd(q, kt, vt, w_saved, s_saved, do):
    nh, lp, hd = q.shape
    tq = tk = min(_SB_TQ, lp)
    blocks = [slice(r, r + _SB_ROWS_BWD) for r in range(0, tq, _SB_ROWS_BWD)]
    grp = _SB_GROUP_BWD

    def body(q_ref, k_ref, v_ref, w_hbm, s_hbm, do_ref, dq_ref, dk_ref, dv_ref, wbuf, sbuf, sems):
        head, qi = pl.program_id(0), pl.program_id(1)

        @pl.when(qi == 0)
        def _():
            dk_ref[...] = jnp.zeros_like(dk_ref)
            dv_ref[...] = jnp.zeros_like(dv_ref)

        qv, dov = q_ref[0], do_ref[0]
        u_excl = (_iota2((tk, tk), 0) < _iota2((tk, tk), 1)).astype(f32)
        scale = HEAD_DIM ** -0.5

        def loads(js, slot):
            out = []
            for t, j in enumerate(js):
                out += [pltpu.make_async_copy(w_hbm.at[head, qi, j], wbuf.at[slot, t], sems.at[0, slot, t]),
                        pltpu.make_async_copy(s_hbm.at[head, qi, j], sbuf.at[slot, t], sems.at[1, slot, t])]
            return out

        def tiles(js, slot, carry, live=None):
            dq_acc, c_e = carry
            kss = [pl.ds(pl.multiple_of(j * tk, tk), tk) for j in js]
            dw_alls = [_dot(dov, v_ref[0, :, ks]) for ks in kss]
            stage, befores = [], []
            for t, dw_all in enumerate(dw_alls):
                es, ebs, esums = [], [], []
                for rs in blocks:
                    e = wbuf[slot, t, rs].astype(f32) * dw_all[rs]
                    if live is not None:
                        e = e * live[t]
                    es.append(e)
                    ebs.append(e.astype(MXU_DTYPE))
                    esums.append(jnp.sum(e, axis=1, keepdims=True))
                stage.append((es, _cat_rows(esums)))
                befores.append(_dot(_cat_rows(ebs), u_excl))
            for t, ((es, esums), before_all, ks) in enumerate(zip(stage, befores, kss)):
                dzs = []
                for n, rs in enumerate(blocks):
                    sig = sbuf[slot, t, rs].astype(f32)
                    if live is not None:
                        sig = sig * live[t]
                    dz = es[n] * (1.0 - sig) - sig * (c_e[rs] + before_all[rs])
                    dzs.append((dz * scale).astype(MXU_DTYPE))
                dz_all = _cat_rows(dzs)
                w_all = wbuf[slot, t] if live is None else wbuf[slot, t] * live[t].astype(MXU_DTYPE)
                c_e = c_e + esums
                dk_ref[0, :, ks] += _dot(qv, dz_all, TN)
                dv_ref[0, :, ks] += _dot(dov, w_all, TN)
                dq_acc = dq_acc + _dot(dz_all, k_ref[0, :, ks], NT)
            return dq_acc, c_e

        n_tiles = qi + 1
        n_groups, rest = n_tiles // grp, n_tiles % grp
        n_passes = n_groups + jnp.minimum(rest, 1)
        group_js = lambda g: [jnp.minimum(grp * g + t, qi) for t in range(grp)]

        for cp in loads(group_js(0), 0):
            cp.start()

        def fetch_next_and_wait(g):
            slot = g % 2

            @pl.when(g + 1 < n_passes)
            def _():
                for cp in loads(group_js(g + 1), 1 - slot):
                    cp.start()

            for cp in loads(group_js(g), slot):
                cp.wait()
            return slot

        def group(g, carry):
            slot = fetch_next_and_wait(g)
            return tiles(group_js(g), slot, carry)

        def last_group(_, carry):
            slot = fetch_next_and_wait(n_groups)
            live = [(grp * n_groups + t <= qi).astype(f32) for t in range(grp)]
            return tiles(group_js(n_groups), slot, carry, live)

        carry = lax.fori_loop(0, n_groups, group, (jnp.zeros((tq, hd), f32), jnp.zeros((tq, 1), f32)))
        dq_acc, _ = lax.fori_loop(0, jnp.minimum(rest, 1), last_group, carry)
        dq_ref[0] = dq_acc

    tile_spec = _bs((1, tq, hd), lambda h, i: (h, i, 0))
    full_t = _bs((1, hd, lp), lambda h, i: (h, 0, 0))
    hbm = pl.BlockSpec(memory_space=pltpu.HBM)
    return _pc(
        body, "sb_bwd", (nh, lp // tq),
        [tile_spec, full_t, full_t, hbm, hbm, tile_spec],
        [tile_spec, full_t, full_t], [_sds((nh, lp, hd)), _sds((nh, hd, lp)), _sds((nh, hd, lp))],
        scratch=[pltpu.VMEM((2, grp, tq, tk), MXU_DTYPE), pltpu.VMEM((2, grp, tq, tk), MXU_DTYPE),
                 pltpu.SemaphoreType.DMA((2, 2, grp))])(q, kt, vt, w_saved, s_saved, do)


def _dn_pre_tile(xs, ba, cw, pv):
    conv = xs[0] * cw[3:4] + xs[1] * cw[2:3] + xs[2] * cw[1:2] + xs[3] * cw[0:1]
    s = _silu(conv)
    bd = _block_diag_ones(QW, HEAD_DIM)
    sq, sk, sv = s[:, :QW], s[:, QW:2 * QW], s[:, 2 * QW:]
    qn = sq * lax.rsqrt(_dotx(sq * sq, bd) + EPS)
    kn = sk * lax.rsqrt(_dotx(sk * sk, bd) + EPS)
    lane = _iota2(ba.shape, 1)
    beta = _sigmoid(ba)
    g = -jnp.exp(pv[0:1]) * _softplus(ba + pv[1:2])
    bg = jnp.where(lane < N_HEADS, beta, jnp.where(lane < 2 * N_HEADS, g, 0.0))
    return qn, kn, sv, bg


def _dn_shifted(cur, prev, first):
    row = _iota2((cur.shape[0], 1), 0)
    out = [cur]
    for k in range(1, DN_CONV):
        head_rows = jnp.where(first, 0.0, pltpu.roll(prev, k, 0))
        out.append(jnp.where(row >= k, pltpu.roll(cur, k, 0), head_rows))
    return tuple(out)


def _dn_pre_fwd(x, ba, cw, pv):
    lp, w3 = x.shape
    tm = _token_tile(lp, 320)

    def body(x_ref, xp_ref, ba_ref, cw_ref, pv_ref, q_ref, k_ref, v_ref, bg_ref):
        xs = _dn_shifted(x_ref[...], xp_ref[...], pl.program_id(0) == 0)
        qn, kn, sv, bg = _dn_pre_tile(xs, ba_ref[...], cw_ref[...], pv_ref[...])
        q_ref[...], k_ref[...], v_ref[...], bg_ref[...] = qn, kn, sv, bg

    tok = lambda wd: _bs((tm, wd), lambda i: (i, 0))
    return _pc(
        body, "dn_pre_fwd", (lp // tm,),
        [tok(w3), _bs((tm, w3), lambda i: (jnp.maximum(i - 1, 0), 0)), tok(128),
         _bs((8, w3), lambda i: (0, 0)), _bs((8, 128), lambda i: (0, 0))],
        [tok(QW), tok(QW), tok(QW), tok(128)],
        [_sds((lp, QW)), _sds((lp, QW)), _sds((lp, QW)), _sds((lp, 128))])(x, x, ba, cw, pv)


def _dn_pre_bwd(x, ba, cw, pv, dq, dk, dv, dbg):
    lp, w3 = x.shape
    tm = _token_tile(lp, 320)

    def body(x_ref, xp_ref, ba_ref, cw_ref, pv_ref, dq_ref, dk_ref, dv_ref, dbg_ref, dxs_ref, dba_ref, dcw_ref, dpv_ref):
        @pl.when(pl.program_id(0) == 0)
        def _():
            dcw_ref[...] = jnp.zeros_like(dcw_ref)
            dpv_ref[...] = jnp.zeros_like(dpv_ref)

        xs = _dn_shifted(x_ref[...], xp_ref[...], pl.program_id(0) == 0)
        _, vjp = jax.vjp(_dn_pre_tile, xs, ba_ref[...], cw_ref[...], pv_ref[...])
        dxs, dba, dcw, dpv = vjp((dq_ref[...], dk_ref[...], dv_ref[...], dbg_ref[...]))
        for k in range(DN_CONV):
            dxs_ref[k] = dxs[k]
        dba_ref[...] = dba
        dcw_ref[...] += dcw
        dpv_ref[...] += dpv

    tok = lambda wd: _bs((tm, wd), lambda i: (i, 0))
    cw_spec, pv_spec = _bs((8, w3), lambda i: (0, 0)), _bs((8, 128), lambda i: (0, 0))
    return _pc(
        body, "dn_pre_bwd", (lp // tm,),
        [tok(w3), _bs((tm, w3), lambda i: (jnp.maximum(i - 1, 0), 0)), tok(128), cw_spec, pv_spec,
         tok(QW), tok(QW), tok(QW), tok(128)],
        [_bs((4, tm, w3), lambda i: (0, i, 0)), tok(128), cw_spec, pv_spec],
        [_sds((4, lp, w3)), _sds((lp, 128)), _sds((8, w3)), _sds((8, 128))])(x, x, ba, cw, pv, dq, dk, dv, dbg)


def _dn_chunk(state, q, k, v, grow, brow):
    nh, c, _ = q.shape
    ii, jj = _iota2((c, c), 0), _iota2((c, c), 1)
    eye = ii == jj
    col = lambda row: jnp.sum(jnp.where(eye, jnp.broadcast_to(row, (nh, c, c)), 0.0), axis=2, keepdims=True)
    gc_row = _dotx(grow, jnp.broadcast_to((ii <= jj).astype(f32), (nh, c, c)), BNN)
    gc_col, b_col = col(gc_row), col(brow)
    decay = jnp.exp(jnp.where(ii >= jj, gc_col - gc_row, -1e30))
    kb = k * b_col
    p = -jnp.where(ii > jj, _bmm_nt(kb, k) * decay, 0.0)
    t_inv = eye.astype(f32) + p
    for _ in range(5):
        p = _bmm3(p, p)
        t_inv = t_inv + _bmm3(t_inv, p)
    egc = jnp.exp(gc_col)
    u = _bmm(t_inv, v * b_col)
    w = _bmm(t_inv, kb * egc)
    qs = q * (q.shape[2] ** -0.5)
    attn = jnp.where(ii >= jj, _bmm_nt(qs, k) * decay, 0.0)
    v_new = u - _bmm(w, state)
    o = _bmm(qs * egc, state) + _bmm(attn, v_new)
    g_last = gc_row[:, :, c - 1:c]
    new_state = state * jnp.exp(g_last) + _bmm_tn(k * jnp.exp(g_last - gc_col), v_new)
    return new_state, o


def _dn_scan_fwd(q, k, v, grow, brow):
    nh, lp, hd = q.shape
    c = DN_CHUNK
    n = lp // c

    def body(q_ref, k_ref, v_ref, g_ref, b_ref, o_ref, st_ref, state_s):
        @pl.when(pl.program_id(0) == 0)
        def _():
            state_s[...] = jnp.zeros_like(state_s)

        st_ref[:, 0] = state_s[...]
        state, o = _dn_chunk(state_s[...], q_ref[...], k_ref[...], v_ref[...], g_ref[:, 0], b_ref[:, 0])
        state_s[...] = state
        o_ref[...] = o

    seq = _bs((nh, c, hd), lambda i: (0, i, 0))
    row = _bs((nh, 1, 1, c), lambda i: (0, i, 0, 0))
    return _pc(body, "dn_scan_fwd", (n,), [seq, seq, seq, row, row],
               [seq, _bs((nh, 1, hd, hd), lambda i: (0, i, 0, 0))],
               [_sds((nh, lp, hd)), _sds((nh, n, hd, hd))],
               scratch=[pltpu.VMEM((nh, hd, hd), f32)])(q, k, v, grow, brow)


def _dn_scan_bwd(q, k, v, grow, brow, states, do):
    nh, lp, hd = q.shape
    c = DN_CHUNK
    n = lp // c

    def body(q_ref, k_ref, v_ref, g_ref, b_ref, st_ref, do_ref, dq_ref, dk_ref, dv_ref, dg_ref, db_ref, dstate_s):
        @pl.when(pl.program_id(0) == 0)
        def _():
            dstate_s[...] = jnp.zeros_like(dstate_s)

        _, vjp = jax.vjp(_dn_chunk, st_ref[:, 0], q_ref[...], k_ref[...], v_ref[...], g_ref[:, 0], b_ref[:, 0])
        dstate, dq, dk, dv, dg, db = vjp((dstate_s[...], do_ref[...]))
        dstate_s[...] = dstate
        dq_ref[...], dk_ref[...], dv_ref[...] = dq, dk, dv
        dg_ref[:, 0], db_ref[:, 0] = dg, db

    seq = _bs((nh, c, hd), lambda i: (0, n - 1 - i, 0))
    row = _bs((nh, 1, 1, c), lambda i: (0, n - 1 - i, 0, 0))
    return _pc(body, "dn_scan_bwd", (n,),
               [seq, seq, seq, row, row, _bs((nh, 1, hd, hd), lambda i: (0, n - 1 - i, 0, 0)), seq],
               [seq, seq, seq, row, row],
               [_sds((nh, lp, hd))] * 3 + [_sds((nh, n, 1, c))] * 2,
               scratch=[pltpu.VMEM((nh, hd, hd), f32)])(q, k, v, grow, brow, states, do)


def _s5_prep(a_re, a_im, log_dt, b_re, b_im):
    dt = jnp.exp(log_dt)
    mag = jnp.exp(a_re * dt)
    ar, ai = mag * jnp.cos(a_im * dt), mag * jnp.sin(a_im * dt)
    den = a_re * a_re + a_im * a_im
    cr = ((ar - 1.0) * a_re + ai * a_im) / den
    ci = (ai * a_re - (ar - 1.0) * a_im) / den
    cr3, ci3 = cr[:, None, :], ci[:, None, :]
    return ar, ai, cr3 * b_re - ci3 * b_im, cr3 * b_im + ci3 * b_re


def _s5_prep_fwd(a_re, a_im, log_dt, b_re, b_im):
    def body(ar_ref, ai_ref, dt_ref, br_ref, bi_ref, *outs):
        for ref, val in zip(outs, _s5_prep(ar_ref[...], ai_ref[...], dt_ref[...], br_ref[...], bi_ref[...])):
            ref[...] = val

    return pl.pallas_call(body, name="s5_prep_fwd",
                          out_shape=[_sds(a_re.shape), _sds(a_re.shape), _sds(b_re.shape), _sds(b_re.shape)],
                          )(a_re, a_im, log_dt, b_re, b_im)


def _s5_prep_bwd(a_re, a_im, log_dt, b_re, b_im, d_ar, d_ai, d_br, d_bi):
    def body(ar_ref, ai_ref, dt_ref, br_ref, bi_ref, g0, g1, g2, g3, *outs):
        _, vjp = jax.vjp(_s5_prep, ar_ref[...], ai_ref[...], dt_ref[...], br_ref[...], bi_ref[...])
        for ref, val in zip(outs, vjp((g0[...], g1[...], g2[...], g3[...]))):
            ref[...] = val

    return pl.pallas_call(body, name="s5_prep_bwd",
                          out_shape=[_sds(a_re.shape), _sds(a_re.shape), _sds(log_dt.shape), _sds(b_re.shape),
                                     _sds(b_re.shape)])(a_re, a_im, log_dt, b_re, b_im, d_ar, d_ai, d_br, d_bi)


def _s5_block_len(lp):
    return 128 if lp % 128 == 0 else 64


def _s5_powers(ar, ai, tb):
    out = []
    k = 1
    while k < tb:
        out.append((ar, ai))
        ar, ai = ar * ar - ai * ai, 2.0 * ar * ai
        k *= 2
    return out


def _s5_scan_rows(xr, xi, pows, reverse):
    tb = xr.shape[0]
    row = _iota2((tb, 1), 0)
    k = 1
    for pr, pi in pows:
        if reverse:
            keep = row < tb - k
            sr, si = pltpu.roll(xr, tb - k, 0), pltpu.roll(xi, tb - k, 0)
        else:
            keep = row >= k
            sr, si = pltpu.roll(xr, k, 0), pltpu.roll(xi, k, 0)
        sr, si = jnp.where(keep, sr, 0.0), jnp.where(keep, si, 0.0)
        xr, xi = xr + pr * sr - pi * si, xi + pr * si + pi * sr
        k *= 2
    return xr, xi


def _s5_slab_mm(x, w_ref, dims=NN):
    a = x.shape[1] // S5_SLABS
    return jnp.concatenate([_dot(x[:, j * a:(j + 1) * a], w_ref[j], dims) for j in range(S5_SLABS)], axis=1)


def _s5_power_table(ar, ai, tb, reverse):
    at = _iota2((tb, 1), 0) == (tb - 1 if reverse else 0)
    return _s5_scan_rows(jnp.where(at, ar, 0.0), jnp.where(at, ai, 0.0), _s5_powers(ar, ai, tb), reverse)


def _s5_states(u, carry_r, carry_i, ar, ai, b8r_ref, b8i_ref, pw_ref, tb):
    sr, si = _s5_scan_rows(_s5_slab_mm(u, b8r_ref), _s5_slab_mm(u, b8i_ref), _s5_powers(ar, ai, tb), False)
    pr, pi = pw_ref[0], pw_ref[1]
    return sr + pr * carry_r - pi * carry_i, si + pr * carry_i + pi * carry_r


def _s5_scan_fwd(u, ar, ai, b8r, b8i, c8r, c8i, dvec):
    lp = u.shape[0]
    tb = _s5_block_len(lp)
    nblk = lp // tb

    def body(u_ref, ar_ref, ai_ref, b8r_ref, b8i_ref, c8r_ref, c8i_ref, d_ref, y_ref, cin_ref, carry_s, pw_s):
        @pl.when(pl.program_id(0) == 0)
        def _():
            carry_s[...] = jnp.zeros_like(carry_s)
            pw_s[0], pw_s[1] = _s5_power_table(ar_ref[...], ai_ref[...], tb, False)

        cin_ref[0] = carry_s[...]
        uv = u_ref[...]
        sr, si = _s5_states(uv, carry_s[0:1], carry_s[1:2], ar_ref[...], ai_ref[...], b8r_ref, b8i_ref, pw_s, tb)
        carry_s[0:1] = sr[tb - 1:tb]
        carry_s[1:2] = si[tb - 1:tb]
        y_ref[...] = _s5_slab_mm(sr, c8r_ref) - _s5_slab_mm(si, c8i_ref) + d_ref[...] * uv

    const = lambda shape: _bs(shape, lambda i: (0,) * len(shape))
    return _pc(
        body, "s5_scan_fwd", (nblk,),
        [_bs((tb, S5_W), lambda i: (i, 0)), const((1, S5_N)), const((1, S5_N)),
         const((S5_SLABS, 128, 512)), const((S5_SLABS, 128, 512)),
         const((S5_SLABS, 512, 128)), const((S5_SLABS, 512, 128)), const((1, S5_W))],
        [_bs((tb, S5_W), lambda i: (i, 0)), _bs((1, 8, S5_N), lambda i: (i, 0, 0))],
        [_sds((lp, S5_W)), _sds((nblk, 8, S5_N))],
        scratch=[pltpu.VMEM((8, S5_N), f32), pltpu.VMEM((2, tb, S5_N), f32)])(u, ar, ai, b8r, b8i, c8r, c8i, dvec)


def _s5_scan_bwd(u, dy, cin, ar, ai, b8r, b8i, c8r, c8i, dvec):
    lp = u.shape[0]
    tb = _s5_block_len(lp)
    nblk = lp // tb

    def body(u_ref, dy_ref, cin_ref, ar_ref, ai_ref, b8r_ref, b8i_ref, c8r_ref, c8i_ref, d_ref,
             du_ref, dab_ref, db8r_ref, db8i_ref, dc8r_ref, dc8i_ref, dd_ref, lam_s, pw_s, qw_s):
        @pl.when(pl.program_id(0) == 0)
        def _():
            lam_s[...] = jnp.zeros_like(lam_s)
            for ref in (dab_ref, db8r_ref, db8i_ref, dc8r_ref, dc8i_ref, dd_ref):
                ref[...] = jnp.zeros_like(ref)
            pw_s[0], pw_s[1] = _s5_power_table(ar_ref[...], ai_ref[...], tb, False)
            qw_s[0], qw_s[1] = _s5_power_table(ar_ref[...], -ai_ref[...], tb, True)

        uv, dyv = u_ref[...], dy_ref[...]
        a_r, a_i = ar_ref[...], ai_ref[...]
        cin_r, cin_i = cin_ref[0, 0:1], cin_ref[0, 1:2]
        sr, si = _s5_states(uv, cin_r, cin_i, a_r, a_i, b8r_ref, b8i_ref, pw_s, tb)
        lr, li = _s5_scan_rows(_s5_slab_mm(dyv, c8r_ref, NT), -_s5_slab_mm(dyv, c8i_ref, NT),
                               _s5_powers(a_r, -a_i, tb), True)
        qr, qi = qw_s[0], qw_s[1]
        nr, ni = lam_s[0:1], lam_s[1:2]
        lr, li = lr + qr * nr - qi * ni, li + qr * ni + qi * nr
        lam_s[0:1] = lr[0:1]
        lam_s[1:2] = li[0:1]
        first = _iota2((tb, 1), 0) == 0
        pr = jnp.where(first, cin_r, pltpu.roll(sr, 1, 0))
        pi = jnp.where(first, cin_i, pltpu.roll(si, 1, 0))
        dab_ref[0:1] += jnp.sum(lr * pr + li * pi, axis=0, keepdims=True)
        dab_ref[1:2] += jnp.sum(li * pr - lr * pi, axis=0, keepdims=True)
        du_ref[...] = _s5_slab_mm(lr, b8r_ref, NT) + _s5_slab_mm(li, b8i_ref, NT) + d_ref[...] * dyv
        dd_ref[...] += jnp.sum(dyv * uv, axis=0, keepdims=True)
        for j in range(S5_SLABS):
            us, dys = uv[:, j * 128:(j + 1) * 128], dyv[:, j * 128:(j + 1) * 128]
            st = slice(j * 512, (j + 1) * 512)
            db8r_ref[j] += _dot(us, lr[:, st], TN)
            db8i_ref[j] += _dot(us, li[:, st], TN)
            dc8r_ref[j] += _dot(sr[:, st], dys, TN)
            dc8i_ref[j] -= _dot(si[:, st], dys, TN)

    const = lambda shape: _bs(shape, lambda i: (0,) * len(shape))
    rev = _bs((tb, S5_W), lambda i: (nblk - 1 - i, 0))
    return _pc(
        body, "s5_scan_bwd", (nblk,),
        [rev, rev, _bs((1, 8, S5_N), lambda i: (nblk - 1 - i, 0, 0)), const((1, S5_N)), const((1, S5_N)),
         const((S5_SLABS, 128, 512)), const((S5_SLABS, 128, 512)),
         const((S5_SLABS, 512, 128)), const((S5_SLABS, 512, 128)), const((1, S5_W))],
        [rev, const((8, S5_N)), const((S5_SLABS, 128, 512)), const((S5_SLABS, 128, 512)),
         const((S5_SLABS, 512, 128)), const((S5_SLABS, 512, 128)), const((1, S5_W))],
        [_sds((lp, S5_W)), _sds((8, S5_N)), _sds((S5_SLABS, 128, 512)), _sds((S5_SLABS, 128, 512)),
         _sds((S5_SLABS, 512, 128)), _sds((S5_SLABS, 512, 128)), _sds((1, S5_W))],
        scratch=[pltpu.VMEM((8, S5_N), f32), pltpu.VMEM((2, tb, S5_N), f32), pltpu.VMEM((2, tb, S5_N), f32)],
    )(u, dy, cin, ar, ai, b8r, b8i, c8r, c8i, dvec)


def _mix_tile(osb, odn, z, ys5, g_sb, g_dn, w_glu, b_glu, g_s5):
    bd = _block_diag_ones(QW, HEAD_DIM)
    tile4 = ((_iota2((HEAD_DIM, QW), 1) % HEAD_DIM) == _iota2((HEAD_DIM, QW), 0)).astype(f32)
    seg_rms = lambda x: x * lax.rsqrt(_dotx(x * x, bd) * (1.0 / HEAD_DIM) + EPS)
    sbn = seg_rms(osb) * _dotx(g_sb, tile4)
    dnn = seg_rms(odn) * _dotx(g_dn, tile4) * _silu(z)
    y = _gelu_tanh(ys5)
    glu = y * _sigmoid(_mm(y, w_glu) + b_glu)
    return jnp.concatenate([sbn, dnn, _rms(glu, g_s5)], axis=1)


def _mixout_fwd(h, osb, odn, z, ys5, g_sb, g_dn, w_glu, b_glu, g_s5, w_out):
    lp, d = h.shape
    tm = _token_tile(lp)

    def body(h_ref, osb_ref, odn_ref, z_ref, ys_ref, gsb_ref, gdn_ref, wg_ref, bg_ref, gs5_ref, wo_ref, o_ref):
        mixed = _mix_tile(osb_ref[...], odn_ref[...], z_ref[...], ys_ref[...], gsb_ref[...], gdn_ref[...],
                          wg_ref[...], bg_ref[...], gs5_ref[...])
        o_ref[...] = h_ref[...] + _dot(mixed, wo_ref[...])

    tok = lambda wd: _bs((tm, wd), lambda i: (i, 0))
    const = lambda shape: _bs(shape, lambda i: (0,) * len(shape))
    return _pc(
        body, "mixout_fwd", (lp // tm,),
        [tok(d), tok(QW), tok(QW), tok(QW), tok(S5_W), const((1, HEAD_DIM)), const((1, HEAD_DIM)),
         const((S5_W, S5_W)), const((1, S5_W)), const((1, S5_W)), const((d, d))],
        tok(d), _sds((lp, d)))(h, osb, odn, z, ys5, g_sb, g_dn, w_glu, b_glu, g_s5, w_out)


def _mixout_bwd(dh, osb, odn, z, ys5, g_sb, g_dn, w_glu, b_glu, g_s5, w_out):
    lp, d = dh.shape
    tm = _token_tile(lp)

    def body(dh_ref, osb_ref, odn_ref, z_ref, ys_ref, gsb_ref, gdn_ref, wg_ref, bg_ref, gs5_ref, wo_ref,
             dosb_ref, dodn_ref, dz_ref, dys_ref, dgsb_ref, dgdn_ref, dwg_ref, dbg_ref, dgs5_ref, dwo_ref):
        accs = (dgsb_ref, dgdn_ref, dwg_ref, dbg_ref, dgs5_ref)

        @pl.when(pl.program_id(0) == 0)
        def _():
            for ref in accs + (dwo_ref,):
                ref[...] = jnp.zeros_like(ref)

        mixed, vjp = jax.vjp(_mix_tile, osb_ref[...], odn_ref[...], z_ref[...], ys_ref[...], gsb_ref[...],
                             gdn_ref[...], wg_ref[...], bg_ref[...], gs5_ref[...])
        dhv = dh_ref[...]
        dwo_ref[...] += _dot(mixed, dhv, TN)
        grads = vjp(_dot(dhv, wo_ref[...], NT))
        for ref, val in zip((dosb_ref, dodn_ref, dz_ref, dys_ref), grads[:4]):
            ref[...] = val
        for ref, val in zip(accs, grads[4:]):
            ref[...] += val

    tok = lambda wd: _bs((tm, wd), lambda i: (i, 0))
    const = lambda shape: _bs(shape, lambda i: (0,) * len(shape))
    params = [const((1, HEAD_DIM)), const((1, HEAD_DIM)), const((S5_W, S5_W)), const((1, S5_W)), const((1, S5_W))]
    return _pc(
        body, "mixout_bwd", (lp // tm,),
        [tok(d), tok(QW), tok(QW), tok(QW), tok(S5_W)] + params + [const((d, d))],
        [tok(QW), tok(QW), tok(QW), tok(S5_W)] + params + [const((d, d))],
        [_sds((lp, QW))] * 3 + [_sds((lp, S5_W)), _sds((1, HEAD_DIM)), _sds((1, HEAD_DIM)), _sds((S5_W, S5_W)),
                                _sds((1, S5_W)), _sds((1, S5_W)), _sds((d, d))],
    )(dh, osb, odn, z, ys5, g_sb, g_dn, w_glu, b_glu, g_s5, w_out)


def _loss_fwd_bwd(h, g, target, n_real):
    lp, d = h.shape
    tm = _token_tile(lp)

    def body(h_ref, g_ref, t_ref, loss_ref, dh_ref, dg_ref):
        i = pl.program_id(0)

        @pl.when(i == 0)
        def _():
            loss_ref[...] = jnp.zeros_like(loss_ref)
            dg_ref[...] = jnp.zeros_like(dg_ref)

        pos = i * tm + _iota2((tm, 1), 0)
        real = ((pos >= N_META) & (pos < n_real)).astype(f32)
        y, vjp = jax.vjp(_rms, h_ref[...], g_ref[...])
        err = (y - t_ref[...]) * real
        loss_ref[...] += 0.5 * jnp.sum(jnp.mean(err * err, axis=1, keepdims=True))
        dx, dg = vjp(err * (1.0 / d))
        dh_ref[...] = dx
        dg_ref[...] += dg

    tok = _bs((tm, d), lambda i: (i, 0))
    return _pc(body, "loss_fwd_bwd", (lp // tm,), [tok, _bs((1, d), lambda i: (0, 0)), tok],
               [_bs((8, 128), lambda i: (0, 0)), tok, _bs((1, d), lambda i: (0, 0))],
               [_sds((8, 128)), _sds((lp, d)), _sds((1, d))])(h, g, target)


def _row_tile(rows):
    for t in (256, 128, 64, 32, 16, 8):
        if rows % t == 0:
            return t
    raise ValueError(rows)


def _adamw(w, g, m, v):
    shape = w.shape
    w, g, m, v = (a.reshape(-1, shape[-1]) for a in (w, g, m, v))
    rows, width = w.shape
    tr = _row_tile(rows)

    def body(w_ref, g_ref, m_ref, v_ref, d_ref, mo_ref, vo_ref):
        gv = g_ref[...]
        m_new = ADAM_B1 * m_ref[...] + (1.0 - ADAM_B1) * gv
        v_new = ADAM_B2 * v_ref[...] + (1.0 - ADAM_B2) * (gv * gv)
        m_hat = m_new / (1.0 - ADAM_B1 ** ADAM_STEP)
        v_hat = v_new / (1.0 - ADAM_B2 ** ADAM_STEP)
        d_ref[...] = -ADAM_LR * (m_hat / (jnp.sqrt(v_hat) + ADAM_EPS) + ADAM_WD * w_ref[...])
        mo_ref[...] = m_new
        vo_ref[...] = v_new

    blk = _bs((tr, width), lambda i: (i, 0))
    outs = _pc(body, "adamw", (rows // tr,), [blk] * 4, [blk] * 3, [_sds(w.shape)] * 3)(w, g, m, v)
    return tuple(o.reshape(shape) for o in outs)


def _sum_leading(x, name):
    n, rows, _ = x.shape
    tr = _row_tile(rows)

    def body(x_ref, o_ref):
        acc = x_ref[0]
        for k in range(1, n):
            acc = acc + x_ref[k]
        o_ref[...] = acc

    return _pc(body, name, (rows // tr,), [_bs((n, tr, FLAT_W), lambda i: (0, i, 0))],
               _bs((tr, FLAT_W), lambda i: (i, 0)), _sds((rows, FLAT_W)))(x)


def _pair_add(a, b, name):
    rows = a.shape[0]
    tr = _row_tile(rows)

    def body(a_ref, b_ref, o_ref):
        o_ref[...] = a_ref[...] + b_ref[...]

    blk = _bs((tr, FLAT_W), lambda i: (i, 0))
    return _pc(body, name, (rows // tr,), [blk, blk], blk, _sds(a.shape))(a, b)


_CHIP_FLIPS = ((1, 0, 0), (0, 1, 0), (1, 1, 0))
_ALL_FLIPS = tuple((a, b, c) for a in (0, 1) for b in (0, 1) for c in (0, 1) if (a, b, c) != (0, 0, 0))
_CORE_FLIP = ((0, 0, 1),)
_D2D_STREAMS = 4


def _exchange(name, arrays, out_shapes, flips, plan):
    n_in = len(arrays)

    def body(*refs):
        ins, outs = refs[:n_in], refs[n_in:n_in + len(out_shapes)]
        send_sems, recv_sems, local_sems = refs[n_in + len(out_shapes):]
        me = (lax.axis_index("x"), lax.axis_index("y"), lax.axis_index("c"))
        local = [pltpu.make_async_copy(s, d, local_sems.at[n]) for n, (s, d) in enumerate(plan(me, None, ins, outs))]
        for cp in local:
            cp.start()
        sent, k = [], 0
        for f in flips:
            peer = tuple(1 - m if fl else m for m, fl in zip(me, f))
            for s, d in plan(me, peer, ins, outs):
                cp = pltpu.make_async_remote_copy(src_ref=s, dst_ref=d, send_sem=send_sems.at[k],
                                                  recv_sem=recv_sems.at[k], device_id=peer,
                                                  device_id_type=pl.DeviceIdType.MESH)
                cp.start()
                sent.append(cp)
                k += 1
        for cp in sent:
            cp.wait_recv()
        for cp in sent:
            cp.wait_send()
        for cp in local:
            cp.wait()

    me0 = (0, 0, 0)
    n_remote = sum(len(_plan_count(plan, me0, f, arrays, out_shapes)) for f in flips)
    n_local = len(_plan_count(plan, me0, None, arrays, out_shapes))
    hbm = pl.BlockSpec(memory_space=pltpu.HBM)
    return pl.pallas_call(
        body, name=name, in_specs=[hbm] * n_in, out_specs=[hbm] * len(out_shapes), out_shape=list(out_shapes),
        scratch_shapes=[pltpu.SemaphoreType.DMA((n_remote,)), pltpu.SemaphoreType.DMA((n_remote,)),
                        pltpu.SemaphoreType.DMA((max(n_local, 1),))],
        compiler_params=pltpu.CompilerParams(has_side_effects=True))(*arrays)


class _FakeRef:
    def __init__(self):
        self.at = self

    def __getitem__(self, idx):
        return self


def _plan_count(plan, me, flip, arrays, out_shapes):
    peer = None if flip is None else me
    return plan(me, peer, [_FakeRef() for _ in arrays], [_FakeRef() for _ in out_shapes])


def _chip_index(dev):
    return 2 * dev[0] + dev[1]


def _all_gather_chips(name, arrays):
    n = len(arrays)

    def body(*refs):
        ins, outs = refs[:n], refs[n:2 * n]
        send_sems, recv_sems, local_sems = refs[2 * n:]
        x, y, c = lax.axis_index("x"), lax.axis_index("y"), lax.axis_index("c")
        sibling = (x, y, 1 - c)
        chips = [(1 - x, y), (x, 1 - y), (1 - x, 1 - y)]
        mine = 2 * x + y

        def copy(k, src, dst, to):
            return pltpu.make_async_remote_copy(src_ref=src, dst_ref=dst, send_sem=send_sems.at[k],
                                                recv_sem=recv_sems.at[k], device_id=to,
                                                device_id_type=pl.DeviceIdType.MESH)

        local = [pltpu.make_async_copy(ins[a], outs[a].at[mine], local_sems.at[a]) for a in range(n)]
        for cp in local:
            cp.start()
        first = [copy(j * n + a, ins[a].at[c], outs[a].at[mine, c], (*chip, c))
                 for j, chip in enumerate(chips) for a in range(n)]
        for cp in first:
            cp.start()
        passed = []
        for j, chip in enumerate(chips):
            for a in range(n):
                landed = outs[a].at[_chip_index(chip), c]
                copy(j * n + a, landed, landed, sibling).wait_recv()
                cp = copy(3 * n + j * n + a, landed, landed, sibling)
                cp.start()
                passed.append(cp)
        for j, chip in enumerate(chips):
            for a in range(n):
                other = outs[a].at[_chip_index(chip), 1 - c]
                copy(3 * n + j * n + a, other, other, sibling).wait_recv()
        for cp in first + passed:
            cp.wait_send()
        for cp in local:
            cp.wait()

    hbm = pl.BlockSpec(memory_space=pltpu.HBM)
    return pl.pallas_call(
        body, name=name, in_specs=[hbm] * n, out_specs=[hbm] * n,
        out_shape=[_sds((N_CHIPS,) + a.shape, a.dtype) for a in arrays],
        scratch_shapes=[pltpu.SemaphoreType.DMA((6 * n,)), pltpu.SemaphoreType.DMA((6 * n,)),
                        pltpu.SemaphoreType.DMA((n,))],
        compiler_params=pltpu.CompilerParams(has_side_effects=True))(*arrays)


def _all_gather_devices(name, arr):
    def plan(me, peer, ins, outs):
        return [(ins[0], outs[0].at[4 * me[0] + 2 * me[1] + me[2]])]

    return _exchange(name, [arr], [_sds((8,) + arr.shape, arr.dtype)], _ALL_FLIPS, plan)[0]


def _swap_half_with_sibling(name, g):
    step = g.shape[2] // _D2D_STREAMS

    def plan(me, peer, ins, outs):
        if peer is None:
            return []
        return [(ins[0].at[k, 1 - me[2], pl.ds(r * step, step)], outs[0].at[k, pl.ds(r * step, step)])
                for k in range(N_CHIPS) for r in range(_D2D_STREAMS)]

    return _exchange(name, [g], [_sds((N_CHIPS,) + g.shape[2:], g.dtype)], _CORE_FLIP, plan)[0]


def _scatter_to_chips(name, s):
    def plan(me, peer, ins, outs):
        to = me if peer is None else peer
        return [(ins[0].at[_chip_index(to)], outs[0].at[_chip_index(me)])]

    return _exchange(name, [s], [_sds(s.shape, s.dtype)], _CHIP_FLIPS, plan)[0]


def _share_with_sibling(name, r):
    step = r.shape[0] // (4 * _D2D_STREAMS)

    def plan(me, peer, ins, outs):
        return [(ins[0].at[pl.ds(n * step, step)], outs[0].at[me[2], pl.ds(n * step, step)])
                for n in range(4 * _D2D_STREAMS)]

    return _exchange(name, [r], [_sds((2,) + r.shape, r.dtype)], _CORE_FLIP, plan)[0]


def _to_heads(x):
    return x.reshape(x.shape[0], N_HEADS, HEAD_DIM).transpose(1, 0, 2)


def _from_heads(x):
    return x.transpose(1, 0, 2).reshape(x.shape[1], QW)


def _to_heads_t(x):
    return x.T.reshape(N_HEADS, HEAD_DIM, x.shape[0])


def _from_heads_t(x):
    return x.reshape(QW, x.shape[2]).T


def _reorder_w_in(w):
    o = 3 * QW + 3 * QW + QW
    main = jnp.concatenate([w[:, :o], w[:, o + 2 * N_HEADS:]], axis=1)
    ba = jnp.pad(w[:, o:o + 2 * N_HEADS], ((0, 0), (0, 128 - 2 * N_HEADS)))
    return jnp.concatenate([main, ba], axis=1)


def _restore_w_in(w):
    o = 3 * QW + 3 * QW + QW
    return jnp.concatenate([w[:, :o], w[:, PROJ_W - 128:PROJ_W - 128 + 2 * N_HEADS], w[:, o:PROJ_W - 128]], axis=1)


def _slab_embed_b(b):
    x = b.reshape(S5_SLABS, 8, S5_C, S5_P)
    eye = jnp.eye(8, dtype=b.dtype)
    return (x[:, :, :, None, :] * eye[None, :, None, :, None]).reshape(S5_SLABS, 8 * S5_C, 8 * S5_P)


def _slab_extract_b(m):
    x = m.reshape(S5_SLABS, 8, S5_C, 8, S5_P)
    return jnp.stack([x[:, g, :, g, :] for g in range(8)], axis=1).reshape(S5_G, S5_C, S5_P)


def _slab_embed_c(c):
    x = c.reshape(S5_SLABS, 8, S5_C, S5_P).transpose(0, 1, 3, 2)
    eye = jnp.eye(8, dtype=c.dtype)
    return (x[:, :, :, None, :] * eye[None, :, None, :, None]).reshape(S5_SLABS, 8 * S5_P, 8 * S5_C)


def _slab_extract_c(m):
    x = m.reshape(S5_SLABS, 8, S5_P, 8, S5_C)
    return jnp.stack([x[:, g, :, g, :] for g in range(8)], axis=1).transpose(0, 1, 3, 2).reshape(S5_G, S5_C, S5_P)


def _piece_rows(shape):
    return -(-math.prod(shape) // (8 * FLAT_W)) * 8


def _pack_rows(parts, lead, row_align):
    rows = []
    for p in parts:
        flat = p.reshape(lead + (-1,))
        r = _piece_rows(p.shape[len(lead):])
        flat = jnp.pad(flat, [(0, 0)] * len(lead) + [(0, r * FLAT_W - flat.shape[-1])])
        rows.append(flat.reshape(lead + (r, FLAT_W)))
    total = sum(r.shape[-2] for r in rows)
    rows.append(jnp.zeros(lead + ((-total) % row_align, FLAT_W), parts[0].dtype))
    return jnp.concatenate(rows, axis=len(lead))


def _unpack_rows(flat, shapes):
    out, off = [], 0
    for s in shapes:
        r = _piece_rows(s)
        out.append(flat[off:off + r].reshape(-1)[:math.prod(s)].reshape(s))
        off += r
    return out


_SHARDED = ("ffn1_w_gate", "ffn1_w_up", "ffn1_w_down", "w_in", "s5_w_glu", "w_out",
            "ffn2_w_gate", "ffn2_w_up", "ffn2_w_down", "meta_tokens", "dn_conv_w")
_MATMUL_W = _SHARDED[:9]
_WEIGHTS = ("meta_tokens", "ffn1_norm", "ffn1_w_gate", "ffn1_w_up", "ffn1_w_down", "mix_norm", "w_in", "sb_out_norm",
            "dn_conv_w", "dn_a_log", "dn_dt_bias", "dn_out_norm", "s5_a_re", "s5_a_im", "s5_log_dt", "s5_b_re",
            "s5_b_im", "s5_c_re", "s5_c_im", "s5_d", "s5_w_glu", "s5_b_glu", "s5_out_norm", "w_out", "ffn2_norm",
            "ffn2_w_gate", "ffn2_w_up", "ffn2_w_down", "final_norm")
_REPLICATED = tuple(n for n in _WEIGHTS if n not in _SHARDED)


def _chip_major(name, g):
    if name in ("ffn1_w_gate", "ffn1_w_up", "ffn2_w_gate", "ffn2_w_up", "ffn1_w_down", "ffn2_w_down"):
        return g.transpose(1, 0, 2, 3)
    if name == "w_in":
        return g.reshape(2, D_MODEL, N_CHIPS, IN_WIDTH // N_CHIPS).transpose(2, 0, 1, 3)
    if name in ("w_out", "s5_w_glu"):
        return g.reshape(2, N_CHIPS, g.shape[1] // N_CHIPS, g.shape[2]).transpose(1, 0, 2, 3)
    if name == "meta_tokens":
        return g.reshape(N_META, N_CHIPS, D_MODEL // N_CHIPS).transpose(1, 0, 2)
    if name == "dn_conv_w":
        return g.reshape(2, DN_CONV, N_CHIPS, 3 * QW // N_CHIPS).transpose(2, 0, 1, 3)
    raise ValueError(name)


def _layer_forward(h, p):
    lp = h.shape[0]
    h1, *ffn1_saved = _ffn_fwd(h, p["ffn1_norm"], *p["ffn1"])
    q, k, v, dnx, z, u, ba = _inproj_fwd(h1, p["mix_norm"], p["w_in"].astype(MXU_DTYPE))
    qh, kh, vh = _to_heads(q), _to_heads_t(k), _to_heads_t(v)
    osb, sb_w, sb_sig = _sb_fwd(qh, kh, vh)
    dq_, dk_, dv_, bg = _dn_pre_fwd(dnx, ba, p["cw"], p["pv"])
    dqh, dkh, dvh = _to_heads(dq_), _to_heads(dk_), _to_heads(dv_)
    brow = bg[:, :N_HEADS].T.reshape(N_HEADS, lp // DN_CHUNK, 1, DN_CHUNK)
    grow = bg[:, N_HEADS:2 * N_HEADS].T.reshape(N_HEADS, lp // DN_CHUNK, 1, DN_CHUNK)
    odn, states = _dn_scan_fwd(dqh, dkh, dvh, grow, brow)
    ar, ai, bre, bim = _s5_prep_fwd(p["s5_a_re"], p["s5_a_im"], p["s5_log_dt"], p["s5_b_re"], p["s5_b_im"])
    s5t = (ar.reshape(1, S5_N), ai.reshape(1, S5_N), _slab_embed_b(bre), _slab_embed_b(bim),
           _slab_embed_c(p["s5_c_re"]), _slab_embed_c(p["s5_c_im"]), p["s5_d"])
    ys5, cin = _s5_scan_fwd(u, *s5t)
    osb_t, odn_t = _from_heads(osb), _from_heads(odn)
    h2 = _mixout_fwd(h1, osb_t, odn_t, z, ys5, p["sb_out_norm"], p["dn_out_norm"], p["s5_w_glu"], p["s5_b_glu"],
                     p["s5_out_norm"], p["w_out"].astype(MXU_DTYPE))
    h3, *ffn2_saved = _ffn_fwd(h2, p["ffn2_norm"], *p["ffn2"])
    saved = dict(h0=h, h1=h1, h2=h2, ffn1=ffn1_saved, ffn2=ffn2_saved, qh=qh, kh=kh, vh=vh, sb_w=sb_w, sb_sig=sb_sig, dnx=dnx, ba=ba, dqh=dqh, dkh=dkh, dvh=dvh,
                 grow=grow, brow=brow, states=states, odn_t=odn_t, osb_t=osb_t, z=z, u=u, ys5=ys5, cin=cin, s5t=s5t)
    return h3, saved


def _ffn_backward(h, g, w3, dy, fwd_saved):
    xn, gate, up = fwd_saved
    dh, dgn, d_gate, d_up, act = _ffn_bwd_dx(h, g, *w3, dy, gate, up)
    dwg, dwu, dwd = _ffn_bwd_dw(xn, dy, d_gate, d_up, act)
    return dh, dgn, dwg, dwu, dwd


def _layer_backward(dh3, p, s):
    lp = dh3.shape[0]
    g = {}
    dh2, g["ffn2_norm"], g["ffn2_w_gate"], g["ffn2_w_up"], g["ffn2_w_down"] = _ffn_backward(
        s["h2"], p["ffn2_norm"], p["ffn2"], dh3, s["ffn2"])
    (dosb_t, dodn_t, dz, dys5, g["sb_out_norm"], g["dn_out_norm"], g["s5_w_glu"], g["s5_b_glu"], g["s5_out_norm"],
     g["w_out"]) = _mixout_bwd(dh2, s["osb_t"], s["odn_t"], s["z"], s["ys5"], p["sb_out_norm"], p["dn_out_norm"],
                               p["s5_w_glu"], p["s5_b_glu"], p["s5_out_norm"], p["w_out"].astype(MXU_DTYPE))
    du, dab, db8r, db8i, dc8r, dc8i, g["s5_d"] = _s5_scan_bwd(s["u"], dys5, s["cin"], *s["s5t"])
    g["s5_c_re"], g["s5_c_im"] = _slab_extract_c(dc8r), _slab_extract_c(dc8i)
    g["s5_a_re"], g["s5_a_im"], g["s5_log_dt"], g["s5_b_re"], g["s5_b_im"] = _s5_prep_bwd(
        p["s5_a_re"], p["s5_a_im"], p["s5_log_dt"], p["s5_b_re"], p["s5_b_im"],
        dab[0].reshape(S5_G, S5_P), dab[1].reshape(S5_G, S5_P), _slab_extract_b(db8r), _slab_extract_b(db8i))
    ddq, ddk, ddv, dgrow, dbrow = _dn_scan_bwd(s["dqh"], s["dkh"], s["dvh"], s["grow"], s["brow"], s["states"],
                                               _to_heads(dodn_t))
    dbg = jnp.concatenate([dbrow.reshape(N_HEADS, lp).T, dgrow.reshape(N_HEADS, lp).T,
                           jnp.zeros((lp, 128 - 2 * N_HEADS), f32)], axis=1)
    dxs, dba, g["cw"], g["pv"] = _dn_pre_bwd(s["dnx"], s["ba"], p["cw"], p["pv"], _from_heads(ddq), _from_heads(ddk),
                                             _from_heads(ddv), dbg)
    dq, dk_t, dv_t = _sb_bwd(s["qh"], s["kh"], s["vh"], s["sb_w"], s["sb_sig"], _to_heads(dosb_t))
    dh1, g["mix_norm"], g["w_in"] = _inproj_bwd(s["h1"], p["mix_norm"], p["w_in"].astype(MXU_DTYPE), dh2,
                                                _from_heads(dq), _from_heads_t(dk_t), _from_heads_t(dv_t), dxs,
                                                dz, du, dba)
    dh0, g["ffn1_norm"], g["ffn1_w_gate"], g["ffn1_w_up"], g["ffn1_w_down"] = _ffn_backward(
        s["h0"], p["ffn1_norm"], p["ffn1"], dh1, s["ffn1"])
    return dh0, g


def kernel(x, meta_tokens, ffn1_norm, ffn1_w_gate, ffn1_w_up, ffn1_w_down, mix_norm, w_in, sb_out_norm, dn_conv_w, dn_a_log, dn_dt_bias, dn_out_norm, s5_a_re, s5_a_im, s5_log_dt, s5_b_re, s5_b_im, s5_c_re, s5_c_im, s5_d, s5_w_glu, s5_b_glu, s5_out_norm, w_out, ffn2_norm, ffn2_w_gate, ffn2_w_up, ffn2_w_down, final_norm, loss_target, m_meta_tokens, m_ffn1_norm, m_ffn1_w_gate, m_ffn1_w_up, m_ffn1_w_down, m_mix_norm, m_w_in, m_sb_out_norm, m_dn_conv_w, m_dn_a_log, m_dn_dt_bias, m_dn_out_norm, m_s5_a_re, m_s5_a_im, m_s5_log_dt, m_s5_b_re, m_s5_b_im, m_s5_c_re, m_s5_c_im, m_s5_d, m_s5_w_glu, m_s5_b_glu, m_s5_out_norm, m_w_out, m_ffn2_norm, m_ffn2_w_gate, m_ffn2_w_up, m_ffn2_w_down, m_final_norm, v_meta_tokens, v_ffn1_norm, v_ffn1_w_gate, v_ffn1_w_up, v_ffn1_w_down, v_mix_norm, v_w_in, v_sb_out_norm, v_dn_conv_w, v_dn_a_log, v_dn_dt_bias, v_dn_out_norm, v_s5_a_re, v_s5_a_im, v_s5_log_dt, v_s5_b_re, v_s5_b_im, v_s5_c_re, v_s5_c_im, v_s5_d, v_s5_w_glu, v_s5_b_glu, v_s5_out_norm, v_w_out, v_ffn2_norm, v_ffn2_w_gate, v_ffn2_w_up, v_ffn2_w_down, v_final_norm):
    args = dict(locals())
    w = {n: args[n] for n in _WEIGHTS}
    m = {n: args["m_" + n] for n in _WEIGHTS}
    v = {n: args["v_" + n] for n in _WEIGHTS}
    depth = ffn1_norm.shape[0]
    seq = x.shape[1]
    n_real = N_META + seq
    lp = _padded_len(n_real)

    gathered = _all_gather_chips("gather_weights", [w[n].astype(MXU_DTYPE) for n in _MATMUL_W]
                                 + [w["meta_tokens"].reshape(2, N_META // 2, -1), w["dn_conv_w"]])
    full = dict(zip(_MATMUL_W + ("meta_tokens", "dn_conv_w"), gathered))
    meta_full = full["meta_tokens"].reshape(N_CHIPS, N_META, -1).transpose(1, 0, 2).reshape(N_META, D_MODEL)
    conv_full = full["dn_conv_w"].transpose(1, 2, 0, 3).reshape(depth, DN_CONV, 3 * QW)
    w_in_full = full["w_in"].transpose(1, 2, 0, 3).reshape(depth, D_MODEL, IN_WIDTH)
    w_out_full = full["w_out"].transpose(1, 0, 2, 3).reshape(depth, D_MODEL, D_MODEL)
    w_glu_full = full["s5_w_glu"].transpose(1, 0, 2, 3).reshape(depth, S5_W, S5_W)

    layers = []
    for l in range(depth):
        pv = jnp.pad(jnp.stack([dn_a_log[l], dn_dt_bias[l]]), ((0, 6), (N_HEADS, 128 - 2 * N_HEADS)))
        layers.append(dict(
            ffn1_norm=ffn1_norm[l][None], mix_norm=mix_norm[l][None], ffn2_norm=ffn2_norm[l][None],
            ffn1=(full["ffn1_w_gate"][:, l], full["ffn1_w_up"][:, l], full["ffn1_w_down"][:, l]),
            ffn2=(full["ffn2_w_gate"][:, l], full["ffn2_w_up"][:, l], full["ffn2_w_down"][:, l]),
            w_in=_reorder_w_in(w_in_full[l]), w_out=w_out_full[l], s5_w_glu=w_glu_full[l].astype(f32),
            cw=jnp.pad(conv_full[l], ((0, 8 - DN_CONV), (0, 0))), pv=pv,
            sb_out_norm=sb_out_norm[l][None], dn_out_norm=dn_out_norm[l][None],
            s5_a_re=s5_a_re[l], s5_a_im=s5_a_im[l], s5_log_dt=s5_log_dt[l][:, None],
            s5_b_re=s5_b_re[l].transpose(0, 2, 1), s5_b_im=s5_b_im[l].transpose(0, 2, 1),
            s5_c_re=s5_c_re[l], s5_c_im=s5_c_im[l], s5_d=s5_d[l][None], s5_b_glu=s5_b_glu[l][None],
            s5_out_norm=s5_out_norm[l][None]))

    tail = jnp.zeros((lp - n_real, D_MODEL), f32)
    h = jnp.concatenate([meta_full, x[0], tail], axis=0)
    target = jnp.concatenate([jnp.zeros((N_META, D_MODEL), f32), loss_target[0], tail], axis=0)
    saved = []
    for p in layers:
        h, s = _layer_forward(h, p)
        saved.append(s)
    loss_blk, dh, d_final = _loss_fwd_bwd(h, final_norm[None], target, n_real)
    grads = [None] * depth
    for l in reversed(range(depth)):
        dh, grads[l] = _layer_backward(dh, layers[l], saved[l])
    loss = lax.psum(loss_blk[0, 0], ("x", "y", "c"))
    grad_x = dh[N_META:n_real][None]

    stack = lambda name: jnp.stack([grads[l][name] for l in range(depth)])
    gfull = {n: stack(n) for n in ("ffn1_w_gate", "ffn1_w_up", "ffn1_w_down", "s5_w_glu", "w_out", "ffn2_w_gate",
                                   "ffn2_w_up", "ffn2_w_down")}
    gfull["w_in"] = jnp.stack([_restore_w_in(grads[l]["w_in"]) for l in range(depth)])
    gfull["meta_tokens"] = dh[:N_META]
    gfull["dn_conv_w"] = jnp.stack([grads[l]["cw"][:DN_CONV] for l in range(depth)])
    grep = {n: stack(n).reshape(w[n].shape) for n in ("ffn1_norm", "mix_norm", "sb_out_norm", "dn_out_norm", "s5_a_re",
                                                      "s5_a_im", "s5_log_dt", "s5_c_re", "s5_c_im", "s5_d", "s5_b_glu",
                                                      "s5_out_norm", "ffn2_norm")}
    grep["s5_b_re"] = jnp.stack([grads[l]["s5_b_re"].transpose(0, 2, 1) for l in range(depth)])
    grep["s5_b_im"] = jnp.stack([grads[l]["s5_b_im"].transpose(0, 2, 1) for l in range(depth)])
    grep["dn_a_log"] = jnp.stack([grads[l]["pv"][0, N_HEADS:2 * N_HEADS] for l in range(depth)])
    grep["dn_dt_bias"] = jnp.stack([grads[l]["pv"][1, N_HEADS:2 * N_HEADS] for l in range(depth)])
    grep["final_norm"] = d_final[0]

    shard_shapes = [w[n].shape for n in _SHARDED]
    g_big = _pack_rows([_chip_major(n, gfull[n]) for n in _SHARDED], (N_CHIPS,), BIG_ROWS)
    half_rows = g_big.shape[1] // 2
    g_big = g_big.reshape(N_CHIPS, 2, half_rows, FLAT_W)
    c = lax.axis_index("c")
    mine = lax.dynamic_index_in_dim(g_big, c, axis=1, keepdims=False)
    theirs = _swap_half_with_sibling("grad_pair_swap", g_big)
    pair = _pair_add(mine.reshape(-1, FLAT_W), theirs.reshape(-1, FLAT_W), "grad_pair_add")
    arrived = _scatter_to_chips("grad_scatter", pair.reshape(N_CHIPS, half_rows, FLAT_W))
    reduced = _sum_leading(arrived, "grad_chip_sum")
    g_shard = _share_with_sibling("grad_share", reduced).reshape(-1, FLAT_W)

    rep_shapes = [w[n].shape for n in _REPLICATED]
    g_small = _pack_rows([grep[n] for n in _REPLICATED], (), 64)
    g_rep = _sum_leading(_all_gather_devices("grad_small_gather", g_small), "grad_small_sum")

    out = {}
    for n, g_n in zip(_SHARDED, _unpack_rows(g_shard, shard_shapes)):
        out["grad_" + n] = g_n
        out["delta_" + n], out["new_m_" + n], out["new_v_" + n] = _adamw(w[n], g_n, m[n], v[n])
    pack = lambda d: _pack_rows([d[n] for n in _REPLICATED], (), 64)
    delta, m_new, v_new = _adamw(pack(w), g_rep, pack(m), pack(v))
    for kind, flat in (("grad", g_rep), ("delta", delta), ("new_m", m_new), ("new_v", v_new)):
        for n, a in zip(_REPLICATED, _unpack_rows(flat, rep_shapes)):
            out[kind + "_" + n] = a
    return (loss, grad_x, *[out[k + "_" + n] for k in ("grad", "delta", "new_m", "new_v") for n in _WEIGHTS])
```

```python
import functools
import math

import jax
import jax.numpy as jnp
from jax import lax
from jax.experimental import pallas as pl
from jax.experimental.pallas import tpu as pltpu

f32 = jnp.float32
MXU_DTYPE = jnp.bfloat16
HI = lax.Precision.HIGHEST
NN = (((1,), (0,)), ((), ()))
NT = (((1,), (1,)), ((), ()))
TN = (((0,), (0,)), ((), ()))

EPS = 1e-6
D_MODEL = 1024
N_META = 16
HEAD_DIM = 64
N_HEADS = 4
QW = N_HEADS * HEAD_DIM
DN_CONV = 4
DN_CHUNK = 64
S5_W = 512
S5_G = 32
S5_P = 64
S5_C = 16
S5_N = S5_G * S5_P
S5_SLABS = 4
N_CHIPS = 4
PROJ_W = 2432
IN_WIDTH = 2312
VMEM_LIMIT = 56 * 1024 * 1024

ADAM_LR, ADAM_B1, ADAM_B2, ADAM_EPS, ADAM_WD, ADAM_STEP = 0.001, 0.9, 0.999, 1e-08, 0.01, 10
FLAT_W = 1024
BIG_ROWS = 512


def _dot(a, b, dims=NN):
    return lax.dot_general(a.astype(MXU_DTYPE), b.astype(MXU_DTYPE), dims, preferred_element_type=f32)


def _dotx(a, b, dims=NN):
    return lax.dot_general(a, b, dims, precision=HI, preferred_element_type=f32)


def _split(x):
    if MXU_DTYPE == f32:
        return x, None
    hi = x.astype(MXU_DTYPE)
    return hi, (x - hi.astype(f32)).astype(MXU_DTYPE)


def _dot_split(hi, lo, u01):
    if lo is None:
        return _dotx(hi, u01)
    u = u01.astype(MXU_DTYPE)
    return (lax.dot_general(hi, u, NN, preferred_element_type=f32)
            + lax.dot_general(lo, u, NN, preferred_element_type=f32))


def _dot3(a, b, dims=NN):
    ah, al = _split(a)
    if al is None:
        return _dotx(a, b, dims)
    bh, bl = _split(b)
    d = lambda x, y: lax.dot_general(x, y, dims, preferred_element_type=f32)
    return d(ah, bh) + d(ah, bl) + d(al, bh)


BNN = (((2,), (1,)), ((0,), (0,)))
BNT = (((2,), (2,)), ((0,), (0,)))
BTN = (((1,), (1,)), ((0,), (0,)))


def _with_dot_vjp(dot, kind, batched=False):
    nn, nt, tn = (BNN, BNT, BTN) if batched else (NN, NT, TN)
    dims = {"nn": nn, "nt": nt, "tn": tn}[kind]

    @jax.custom_vjp
    def f(a, b):
        return dot(a, b, dims)

    def fwd(a, b):
        return dot(a, b, dims), (a, b)

    def bwd(res, dy):
        a, b = res
        if kind == "nn":
            return dot(dy, b, nt), dot(a, dy, tn)
        if kind == "nt":
            return dot(dy, b, nn), dot(dy, a, tn)
        return dot(b, dy, nt), dot(a, dy, nn)

    f.defvjp(fwd, bwd)
    return f


_mm = _with_dot_vjp(_dot, "nn")
_bmm = _with_dot_vjp(_dot, "nn", True)
_bmm_nt = _with_dot_vjp(_dot, "nt", True)
_bmm_tn = _with_dot_vjp(_dot, "tn", True)
_bmm3 = _with_dot_vjp(_dot3, "nn", True)


def _rms(x, g):
    return x * lax.rsqrt(jnp.mean(x * x, axis=-1, keepdims=True) + EPS) * g


def _sigmoid(x):
    return 1.0 / (1.0 + jnp.exp(-x))


def _silu(x):
    return x * _sigmoid(x)


def _softplus(x):
    return jnp.maximum(x, 0.0) + jnp.log(1.0 + jnp.exp(-jnp.abs(x)))


def _gelu_tanh(x):
    return 0.5 * x * (1.0 + jnp.tanh(math.sqrt(2.0 / math.pi) * (x + 0.044715 * x * x * x)))


def _iota2(shape, axis):
    return lax.broadcasted_iota(jnp.int32, shape, axis)


def _block_diag_ones(n, blk):
    return ((_iota2((n, n), 0) // blk) == (_iota2((n, n), 1) // blk)).astype(f32)


def _pc(body, name, grid, in_specs, out_specs, out_shape, scratch=(), vmem=VMEM_LIMIT):
    return pl.pallas_call(
        body, name=name, grid=grid, in_specs=in_specs, out_specs=out_specs, out_shape=out_shape,
        scratch_shapes=list(scratch),
        compiler_params=pltpu.CompilerParams(dimension_semantics=("arbitrary",) * len(grid), vmem_limit_bytes=vmem))


def _bs(shape, imap):
    return pl.BlockSpec(shape, imap)


def _sds(shape, dtype=f32):
    return jax.ShapeDtypeStruct(tuple(shape), dtype)


def _token_tile(lp, cap=640):
    for t in (640, 320, 256, 128, 64):
        if t <= cap and lp % t == 0:
            return t
    raise ValueError(lp)


def _padded_len(l):
    return -(-l // 1280) * 1280 if l > 4096 else -(-l // 256) * 256


def _ffn_fwd(h, g, wg, wu, wd):
    lp, d = h.shape
    nch, _, fc = wg.shape
    tm = _token_tile(lp)

    def body(h_ref, g_ref, wg_ref, wu_ref, wd_ref, o_ref, xn_ref, gate_ref, up_ref, xn_s, acc_s):
        j = pl.program_id(1)

        @pl.when(j == 0)
        def _():
            xn_s[...] = _rms(h_ref[...], g_ref[...]).astype(xn_s.dtype)
            acc_s[...] = jnp.zeros_like(acc_s)

        xn = xn_s[...]
        gate = _dot(xn, wg_ref[0])
        up = _dot(xn, wu_ref[0])
        gate_ref[0] = gate.astype(gate_ref.dtype)
        up_ref[0] = up.astype(up_ref.dtype)
        acc_s[...] += _dot(_silu(gate) * up, wd_ref[0])

        @pl.when(j == nch - 1)
        def _():
            o_ref[...] = h_ref[...] + 0.5 * acc_s[...]
            xn_ref[...] = xn_s[...]

    tok = _bs((tm, d), lambda i, j: (i, 0))
    chunk = _bs((1, tm, fc), lambda i, j: (j, i, 0))
    return _pc(
        body, "ffn_fwd", (lp // tm, nch),
        [tok, _bs((1, d), lambda i, j: (0, 0)),
         _bs((1, d, fc), lambda i, j: (j, 0, 0)), _bs((1, d, fc), lambda i, j: (j, 0, 0)),
         _bs((1, fc, d), lambda i, j: (j, 0, 0))],
        [tok, tok, chunk, chunk],
        [_sds((lp, d)), _sds((lp, d), MXU_DTYPE), _sds((nch, lp, fc), MXU_DTYPE), _sds((nch, lp, fc), MXU_DTYPE)],
        scratch=[pltpu.VMEM((tm, d), MXU_DTYPE), pltpu.VMEM((tm, d), f32)])(h, g, wg, wu, wd)


def _ffn_bwd_dx(h, g, wg, wu, wd, dy, gate_saved, up_saved):
    lp, d = h.shape
    nch, _, fc = wg.shape
    tm = _token_tile(lp)

    def body(h_ref, g_ref, wg_ref, wu_ref, wd_ref, dy_ref, gate_ref, up_ref, dh_ref, dgn_ref, dg_ref, du_ref, act_ref,
             dxn_s, dout_s):
        i, j = pl.program_id(0), pl.program_id(1)

        @pl.when((i == 0) & (j == 0))
        def _():
            dgn_ref[...] = jnp.zeros_like(dgn_ref)

        @pl.when(j == 0)
        def _():
            dxn_s[...] = jnp.zeros_like(dxn_s)
            dout_s[...] = (0.5 * dy_ref[...]).astype(dout_s.dtype)

        gate = gate_ref[0].astype(f32)
        up = up_ref[0].astype(f32)
        sig = _sigmoid(gate)
        sl = gate * sig
        dact = _dot(dout_s[...], wd_ref[0], NT)
        d_up = dact * sl
        d_gate = dact * up * sig * (1.0 + gate * (1.0 - sig))
        dg_ref[0] = d_gate.astype(dg_ref.dtype)
        du_ref[0] = d_up.astype(du_ref.dtype)
        act_ref[0] = (sl * up).astype(act_ref.dtype)
        dxn_s[...] += _dot(d_gate, wg_ref[0], NT) + _dot(d_up, wu_ref[0], NT)

        @pl.when(j == nch - 1)
        def _():
            _, vjp = jax.vjp(_rms, h_ref[...], g_ref[...])
            dx, dg = vjp(dxn_s[...])
            dh_ref[...] = dy_ref[...] + dx
            dgn_ref[...] += dg

    tok = _bs((tm, d), lambda i, j: (i, 0))
    chunk = _bs((1, tm, fc), lambda i, j: (j, i, 0))
    return _pc(
        body, "ffn_bwd_dx", (lp // tm, nch),
        [tok, _bs((1, d), lambda i, j: (0, 0)),
         _bs((1, d, fc), lambda i, j: (j, 0, 0)), _bs((1, d, fc), lambda i, j: (j, 0, 0)),
         _bs((1, fc, d), lambda i, j: (j, 0, 0)), tok, chunk, chunk],
        [tok, _bs((1, d), lambda i, j: (0, 0)), chunk, chunk, chunk],
        [_sds((lp, d)), _sds((1, d)),
         _sds((nch, lp, fc), MXU_DTYPE), _sds((nch, lp, fc), MXU_DTYPE), _sds((nch, lp, fc), MXU_DTYPE)],
        scratch=[pltpu.VMEM((tm, d), f32), pltpu.VMEM((tm, d), MXU_DTYPE)],
    )(h, g, wg, wu, wd, dy, gate_saved, up_saved)


def _ffn_bwd_dw(xn, dy, d_gate, d_up, act):
    lp, d = xn.shape
    nch, _, fc = d_gate.shape
    tm = _token_tile(lp)

    def body(xn_ref, dy_ref, dg_ref, du_ref, act_ref, dwg_ref, dwu_ref, dwd_ref):
        @pl.when(pl.program_id(1) == 0)
        def _():
            dwg_ref[...] = jnp.zeros_like(dwg_ref)
            dwu_ref[...] = jnp.zeros_like(dwu_ref)
            dwd_ref[...] = jnp.zeros_like(dwd_ref)

        xn_t = xn_ref[...]
        dwg_ref[0] += _dot(xn_t, dg_ref[0], TN)
        dwu_ref[0] += _dot(xn_t, du_ref[0], TN)
        dwd_ref[0] += _dot(act_ref[0], 0.5 * dy_ref[...], TN)

    tok = _bs((tm, d), lambda j, i: (i, 0))
    chunk = _bs((1, tm, fc), lambda j, i: (j, i, 0))
    return _pc(
        body, "ffn_bwd_dw", (nch, lp // tm), [tok, tok, chunk, chunk, chunk],
        [_bs((1, d, fc), lambda j, i: (j, 0, 0)), _bs((1, d, fc), lambda j, i: (j, 0, 0)),
         _bs((1, fc, d), lambda j, i: (j, 0, 0))],
        [_sds((nch, d, fc)), _sds((nch, d, fc)), _sds((nch, fc, d))])(xn, dy, d_gate, d_up, act)


_PROJ_SPLITS = (QW, QW, QW, 3 * QW, QW, S5_W, 128)


def _inproj_fwd(h, g, w):
    lp, d = h.shape
    tm = _token_tile(lp)

    def body(h_ref, g_ref, w_ref, *outs):
        proj = _dot(_rms(h_ref[...], g_ref[...]), w_ref[...])
        off = 0
        for ref, wd in zip(outs, _PROJ_SPLITS):
            ref[...] = proj[:, off:off + wd]
            off += wd

    return _pc(
        body, "inproj_fwd", (lp // tm,),
        [_bs((tm, d), lambda i: (i, 0)), _bs((1, d), lambda i: (0, 0)), _bs((d, PROJ_W), lambda i: (0, 0))],
        [_bs((tm, wd), lambda i: (i, 0)) for wd in _PROJ_SPLITS],
        [_sds((lp, wd)) for wd in _PROJ_SPLITS])(h, g, w)


def _inproj_bwd(h, g, w, dres, dq, dk, dv, dxs, dz, du, dba):
    lp, d = h.shape
    tm = _token_tile(lp, 320)
    n_tiles = lp // tm

    def body(h_ref, g_ref, w_ref, dres_ref, dq_ref, dk_ref, dv_ref, dxs_ref, nxt_ref, dz_ref, du_ref, dba_ref,
             dh_ref, dgn_ref, dw_ref):
        @pl.when(pl.program_id(0) == 0)
        def _():
            dgn_ref[...] = jnp.zeros_like(dgn_ref)
            dw_ref[...] = jnp.zeros_like(dw_ref)

        row, row8 = _iota2((tm, 1), 0), _iota2((8, 1), 0)
        more = (pl.program_id(0) < n_tiles - 1).astype(f32)
        ddn, tail = dxs_ref[0], jnp.zeros((8, 3 * QW), f32)
        for k in range(1, DN_CONV):
            ddn = ddn + jnp.where(row < tm - k, pltpu.roll(dxs_ref[k], tm - k, 0), 0.0)
            tail = tail + jnp.where(row8 >= 8 - k, pltpu.roll(nxt_ref[k], 8 - k, 0), 0.0)
        ddn = jnp.concatenate([ddn[:tm - 8], ddn[tm - 8:] + more * tail], axis=0)
        dproj = jnp.concatenate(
            [dq_ref[...], dk_ref[...], dv_ref[...], ddn, dz_ref[...], du_ref[...], dba_ref[...]], axis=1)
        xn, vjp = jax.vjp(_rms, h_ref[...], g_ref[...])
        dx, dg = vjp(_dot(dproj, w_ref[...], NT))
        dw_ref[...] += _dot(xn, dproj, TN)
        dh_ref[...] = dres_ref[...] + dx
        dgn_ref[...] += dg

    tok = lambda wd: _bs((tm, wd), lambda i: (i, 0))
    return _pc(
        body, "inproj_bwd", (lp // tm,),
        [tok(d), _bs((1, d), lambda i: (0, 0)), _bs((d, PROJ_W), lambda i: (0, 0)), tok(d),
         tok(QW), tok(QW), tok(QW), _bs((4, tm, 3 * QW), lambda i: (0, i, 0)),
         _bs((4, 8, 3 * QW), lambda i: (0, jnp.minimum((i + 1) * (tm // 8), lp // 8 - 1), 0)),
         tok(QW), tok(S5_W), tok(128)],
        [tok(d), _bs((1, d), lambda i: (0, 0)), _bs((d, PROJ_W), lambda i: (0, 0))],
        [_sds((lp, d)), _sds((1, d)), _sds((d, PROJ_W))])(h, g, w, dres, dq, dk, dv, dxs, dxs, dz, du, dba)


_SB_TQ = 256
_SB_ROWS = 32
_SB_GROUP = 4
_SB_ROWS_BWD = 32
_SB_GROUP_BWD = 4


def _sb_pieces(z, valid):
    t = jnp.exp(-jnp.abs(z))
    sp = jnp.maximum(z, 0.0) + jnp.log(1.0 + t)
    lk = -sp if valid is None else jnp.where(valid, -sp, 0.0)
    return t, sp, lk


def _cat_rows(parts):
    return parts[0] if len(parts) == 1 else jnp.concatenate(parts, axis=0)


def _sb_fwd(q, kt, vt):
    nh, lp, hd = q.shape
    tq = tk = min(_SB_TQ, lp)
    blocks = [slice(r, r + _SB_ROWS) for r in range(0, tq, _SB_ROWS)]

    def body(q_ref, k_ref, v_ref, o_ref, w_hbm, s_hbm, wbuf, sbuf, sems):
        head, qi = pl.program_id(0), pl.program_id(1)
        qv = q_ref[0]
        u_strict = (_iota2((tk, tk), 0) > _iota2((tk, tk), 1)).astype(f32)
        below = _iota2((tq, tk), 1) < _iota2((tq, tk), 0)

        spare = lambda slot, t: lp // tk + slot * _SB_GROUP + t

        def save(slot, t, j):
            return [pltpu.make_async_copy(wbuf.at[slot, t], w_hbm.at[head, qi, j], sems.at[0, slot, t]),
                    pltpu.make_async_copy(sbuf.at[slot, t], s_hbm.at[head, qi, j], sems.at[1, slot, t])]

        def drain(slot):
            for t in range(_SB_GROUP):
                for cp in save(slot, t, spare(slot, t)):
                    cp.wait()

        def idle(slot, t):
            wbuf[slot, t] = jnp.zeros((tq, tk), MXU_DTYPE)
            sbuf[slot, t] = jnp.zeros((tq, tk), MXU_DTYPE)
            for cp in save(slot, t, spare(slot, t)):
                cp.start()

        def tiles(js, carry, slot, masked=False, live=None, first=False):
            if not first:
                drain(slot)
            o_acc, c_after = carry
            kss = [pl.ds(pl.multiple_of(j * tk, tk), tk) for j in js]
            z_alls = [_dot(qv, k_ref[0, :, ks]) * (HEAD_DIM ** -0.5) for ks in kss]
            stage, afters = [], []
            for t, z_all in enumerate(z_alls):
                his, los, logs, sums = [], [], [], []
                for rs in blocks:
                    z = z_all[rs]
                    _, sp, lk = _sb_pieces(z, below[rs] if masked else None)
                    if live is not None:
                        lk = lk * live[t]
                    hi, lo = _split(lk)
                    his.append(hi)
                    los.append(lo)
                    logs.append(z - sp)
                    sums.append(jnp.sum(lk, axis=1, keepdims=True))
                stage.append((logs, _cat_rows(sums)))
                afters.append(_dot_split(_cat_rows(his), None if los[0] is None else _cat_rows(los), u_strict))
            for t, ((logs, sums), after_all) in enumerate(zip(stage, afters)):
                ws, sigs = [], []
                for n, rs in enumerate(blocks):
                    w = jnp.exp(logs[n] + after_all[rs] + c_after[rs])
                    sig = jnp.exp(logs[n])
                    if masked:
                        w, sig = jnp.where(below[rs], w, 0.0), jnp.where(below[rs], sig, 0.0)
                    if live is not None:
                        w = w * live[t]
                    ws.append(w.astype(MXU_DTYPE))
                    sigs.append(sig.astype(MXU_DTYPE))
                w_all = _cat_rows(ws)
                wbuf[slot, t] = w_all
                sbuf[slot, t] = _cat_rows(sigs)
                for cp in save(slot, t, js[t] if live is None else jnp.where(live[t] > 0.0, js[t], spare(slot, t))):
                    cp.start()
                o_acc = o_acc + _dot(w_all, v_ref[0, :, kss[t]], NT)
                c_after = c_after + sums
            for t in range(len(js), _SB_GROUP):
                idle(slot, t)
            return o_acc, c_after

        n_groups, rest = qi // _SB_GROUP, qi % _SB_GROUP
        group = lambda g, c: tiles([qi - 1 - _SB_GROUP * g - n for n in range(_SB_GROUP)], c, (g + 1) % 2)

        def last_group(_, c):
            idx = [rest - 1 - n for n in range(_SB_GROUP)]
            return tiles([jnp.maximum(j, 0) for j in idx], c, (n_groups + 1) % 2,
                         live=[(j >= 0).astype(f32) for j in idx])

        for t in range(_SB_GROUP):
            idle(1, t)
        carry = tiles([qi], (jnp.zeros((tq, hd), f32), jnp.zeros((tq, 1), f32)), 0, masked=True, first=True)
        carry = lax.fori_loop(0, n_groups, group, carry)
        o_acc, _ = lax.fori_loop(0, jnp.minimum(rest, 1), last_group, carry)
        drain(0)
        drain(1)
        o_ref[0] = o_acc

    full_t = _bs((1, hd, lp), lambda h, i: (h, 0, 0))
    hbm = pl.BlockSpec(memory_space=pltpu.HBM)
    return _pc(
        body, "sb_fwd", (nh, lp // tq),
        [_bs((1, tq, hd), lambda h, i: (h, i, 0)), full_t, full_t],
        [_bs((1, tq, hd), lambda h, i: (h, i, 0)), hbm, hbm],
        [_sds((nh, lp, hd))] + [_sds((nh, lp // tq, lp // tk + 2 * _SB_GROUP, tq, tk), MXU_DTYPE)] * 2,
        scratch=[pltpu.VMEM((2, _SB_GROUP, tq, tk), MXU_DTYPE), pltpu.VMEM((2, _SB_GROUP, tq, tk), MXU_DTYPE),
                 pltpu.SemaphoreType.DMA((2, 2, _SB_GROUP))])(q, kt, vt)


def _sb_bwd(q, kt, vt, w_saved, s_saved, do):
    nh, lp, hd = q.shape
    tq = tk = min(_SB_TQ, lp)
    blocks = [slice(r, r + _SB_ROWS_BWD) for r in range(0, tq, _SB_ROWS_BWD)]
    grp = _SB_GROUP_BWD

    def body(q_ref, k_ref, v_ref, w_hbm, s_hbm, do_ref, dq_ref, dk_ref, dv_ref, wbuf, sbuf, sems):
        head, qi = pl.program_id(0), pl.program_id(1)

        @pl.when(qi == 0)
        def _():
            dk_ref[...] = jnp.zeros_like(dk_ref)
            dv_ref[...] = jnp.zeros_like(dv_ref)

        qv, dov = q_ref[0], do_ref[0]
        u_excl = (_iota2((tk, tk), 0) < _iota2((tk, tk), 1)).astype(f32)
        scale = HEAD_DIM ** -0.5

        def loads(js, slot):
            out = []
            for t, j in enumerate(js):
                out += [pltpu.make_async_copy(w_hbm.at[head, qi, j], wbuf.at[slot, t], sems.at[0, slot, t]),
                        pltpu.make_async_copy(s_hbm.at[head, qi, j], sbuf.at[slot, t], sems.at[1, slot, t])]
            return out

        def tiles(js, slot, carry, live=None):
            dq_acc, c_e = carry
            kss = [pl.ds(pl.multiple_of(j * tk, tk), tk) for j in js]
            dw_alls = [_dot(dov, v_ref[0, :, ks]) for ks in kss]
            stage, befores = [], []
            for t, dw_all in enumerate(dw_alls):
                es, ebs, esums = [], [], []
                for rs in blocks:
                    e = wbuf[slot, t, rs].astype(f32) * dw_all[rs]
                    if live is not None:
                        e = e * live[t]
                    es.append(e)
                    ebs.append(e.astype(MXU_DTYPE))
                    esums.append(jnp.sum(e, axis=1, keepdims=True))
                stage.append((es, _cat_rows(esums)))
                befores.append(_dot(_cat_rows(ebs), u_excl))
            for t, ((es, esums), before_all, ks) in enumerate(zip(stage, befores, kss)):
                dzs = []
                for n, rs in enumerate(blocks):
                    sig = sbuf[slot, t, rs].astype(f32)
                    if live is not None:
                        sig = sig * live[t]
                    dz = es[n] * (1.0 - sig) - sig * (c_e[rs] + before_all[rs])
                    dzs.append((dz * scale).astype(MXU_DTYPE))
                dz_all = _cat_rows(dzs)
                w_all = wbuf[slot, t] if live is None else wbuf[slot, t] * live[t].astype(MXU_DTYPE)
                c_e = c_e + esums
                dk_ref[0, :, ks] += _dot(qv, dz_all, TN)
                dv_ref[0, :, ks] += _dot(dov, w_all, TN)
                dq_acc = dq_acc + _dot(dz_all, k_ref[0, :, ks], NT)
            return dq_acc, c_e

        n_tiles = qi + 1
        n_groups, rest = n_tiles // grp, n_tiles % grp
        n_passes = n_groups + jnp.minimum(rest, 1)
        group_js = lambda g: [jnp.minimum(grp * g + t, qi) for t in range(grp)]

        for cp in loads(group_js(0), 0):
            cp.start()

        def fetch_next_and_wait(g):
            slot = g % 2

            @pl.when(g + 1 < n_passes)
            def _():
                for cp in loads(group_js(g + 1), 1 - slot):
                    cp.start()

            for cp in loads(group_js(g), slot):
                cp.wait()
            return slot

        def group(g, carry):
            slot = fetch_next_and_wait(g)
            return tiles(group_js(g), slot, carry)

        def last_group(_, carry):
            slot = fetch_next_and_wait(n_groups)
            live = [(grp * n_groups + t <= qi).astype(f32) for t in range(grp)]
            return tiles(group_js(n_groups), slot, carry, live)

        carry = lax.fori_loop(0, n_groups, group, (jnp.zeros((tq, hd), f32), jnp.zeros((tq, 1), f32)))
        dq_acc, _ = lax.fori_loop(0, jnp.minimum(rest, 1), last_group, carry)
        dq_ref[0] = dq_acc

    tile_spec = _bs((1, tq, hd), lambda h, i: (h, i, 0))
    full_t = _bs((1, hd, lp), lambda h, i: (h, 0, 0))
    hbm = pl.BlockSpec(memory_space=pltpu.HBM)
    return _pc(
        body, "sb_bwd", (nh, lp // tq),
        [tile_spec, full_t, full_t, hbm, hbm, tile_spec],
        [tile_spec, full_t, full_t], [_sds((nh, lp, hd)), _sds((nh, hd, lp)), _sds((nh, hd, lp))],
        scratch=[pltpu.VMEM((2, grp, tq, tk), MXU_DTYPE), pltpu.VMEM((2, grp, tq, tk), MXU_DTYPE),
                 pltpu.SemaphoreType.DMA((2, 2, grp))])(q, kt, vt, w_saved, s_saved, do)


def _dn_pre_tile(xs, ba, cw, pv):
    conv = xs[0] * cw[3:4] + xs[1] * cw[2:3] + xs[2] * cw[1:2] + xs[3] * cw[0:1]
    s = _silu(conv)
    bd = _block_diag_ones(QW, HEAD_DIM)
    sq, sk, sv = s[:, :QW], s[:, QW:2 * QW], s[:, 2 * QW:]
    qn = sq * lax.rsqrt(_dotx(sq * sq, bd) + EPS)
    kn = sk * lax.rsqrt(_dotx(sk * sk, bd) + EPS)
    lane = _iota2(ba.shape, 1)
    beta = _sigmoid(ba)
    g = -jnp.exp(pv[0:1]) * _softplus(ba + pv[1:2])
    bg = jnp.where(lane < N_HEADS, beta, jnp.where(lane < 2 * N_HEADS, g, 0.0))
    return qn, kn, sv, bg


def _dn_shifted(cur, prev, first):
    row = _iota2((cur.shape[0], 1), 0)
    out = [cur]
    for k in range(1, DN_CONV):
        head_rows = jnp.where(first, 0.0, pltpu.roll(prev, k, 0))
        out.append(jnp.where(row >= k, pltpu.roll(cur, k, 0), head_rows))
    return tuple(out)


def _dn_pre_fwd(x, ba, cw, pv):
    lp, w3 = x.shape
    tm = _token_tile(lp, 320)

    def body(x_ref, xp_ref, ba_ref, cw_ref, pv_ref, q_ref, k_ref, v_ref, bg_ref):
        xs = _dn_shifted(x_ref[...], xp_ref[...], pl.program_id(0) == 0)
        qn, kn, sv, bg = _dn_pre_tile(xs, ba_ref[...], cw_ref[...], pv_ref[...])
        q_ref[...], k_ref[...], v_ref[...], bg_ref[...] = qn, kn, sv, bg

    tok = lambda wd: _bs((tm, wd), lambda i: (i, 0))
    return _pc(
        body, "dn_pre_fwd", (lp // tm,),
        [tok(w3), _bs((tm, w3), lambda i: (jnp.maximum(i - 1, 0), 0)), tok(128),
         _bs((8, w3), lambda i: (0, 0)), _bs((8, 128), lambda i: (0, 0))],
        [tok(QW), tok(QW), tok(QW), tok(128)],
        [_sds((lp, QW)), _sds((lp, QW)), _sds((lp, QW)), _sds((lp, 128))])(x, x, ba, cw, pv)


def _dn_pre_bwd(x, ba, cw, pv, dq, dk, dv, dbg):
    lp, w3 = x.shape
    tm = _token_tile(lp, 320)

    def body(x_ref, xp_ref, ba_ref, cw_ref, pv_ref, dq_ref, dk_ref, dv_ref, dbg_ref, dxs_ref, dba_ref, dcw_ref, dpv_ref):
        @pl.when(pl.program_id(0) == 0)
        def _():
            dcw_ref[...] = jnp.zeros_like(dcw_ref)
            dpv_ref[...] = jnp.zeros_like(dpv_ref)

        xs = _dn_shifted(x_ref[...], xp_ref[...], pl.program_id(0) == 0)
        _, vjp = jax.vjp(_dn_pre_tile, xs, ba_ref[...], cw_ref[...], pv_ref[...])
        dxs, dba, dcw, dpv = vjp((dq_ref[...], dk_ref[...], dv_ref[...], dbg_ref[...]))
        for k in range(DN_CONV):
            dxs_ref[k] = dxs[k]
        dba_ref[...] = dba
        dcw_ref[...] += dcw
        dpv_ref[...] += dpv

    tok = lambda wd: _bs((tm, wd), lambda i: (i, 0))
    cw_spec, pv_spec = _bs((8, w3), lambda i: (0, 0)), _bs((8, 128), lambda i: (0, 0))
    return _pc(
        body, "dn_pre_bwd", (lp // tm,),
        [tok(w3), _bs((tm, w3), lambda i: (jnp.maximum(i - 1, 0), 0)), tok(128), cw_spec, pv_spec,
         tok(QW), tok(QW), tok(QW), tok(128)],
        [_bs((4, tm, w3), lambda i: (0, i, 0)), tok(128), cw_spec, pv_spec],
        [_sds((4, lp, w3)), _sds((lp, 128)), _sds((8, w3)), _sds((8, 128))])(x, x, ba, cw, pv, dq, dk, dv, dbg)


def _dn_chunk(state, q, k, v, grow, brow):
    nh, c, _ = q.shape
    ii, jj = _iota2((c, c), 0), _iota2((c, c), 1)
    eye = ii == jj
    col = lambda row: jnp.sum(jnp.where(eye, jnp.broadcast_to(row, (nh, c, c)), 0.0), axis=2, keepdims=True)
    gc_row = _dotx(grow, jnp.broadcast_to((ii <= jj).astype(f32), (nh, c, c)), BNN)
    gc_col, b_col = col(gc_row), col(brow)
    decay = jnp.exp(jnp.where(ii >= jj, gc_col - gc_row, -1e30))
    kb = k * b_col
    p = -jnp.where(ii > jj, _bmm_nt(kb, k) * decay, 0.0)
    t_inv = eye.astype(f32) + p
    for _ in range(5):
        p = _bmm3(p, p)
        t_inv = t_inv + _bmm3(t_inv, p)
    egc = jnp.exp(gc_col)
    u = _bmm(t_inv, v * b_col)
    w = _bmm(t_inv, kb * egc)
    qs = q * (q.shape[2] ** -0.5)
    attn = jnp.where(ii >= jj, _bmm_nt(qs, k) * decay, 0.0)
    v_new = u - _bmm(w, state)
    o = _bmm(qs * egc, state) + _bmm(attn, v_new)
    g_last = gc_row[:, :, c - 1:c]
    new_state = state * jnp.exp(g_last) + _bmm_tn(k * jnp.exp(g_last - gc_col), v_new)
    return new_state, o


def _dn_scan_fwd(q, k, v, grow, brow):
    nh, lp, hd = q.shape
    c = DN_CHUNK
    n = lp // c

    def body(q_ref, k_ref, v_ref, g_ref, b_ref, o_ref, st_ref, state_s):
        @pl.when(pl.program_id(0) == 0)
        def _():
            state_s[...] = jnp.zeros_like(state_s)

        st_ref[:, 0] = state_s[...]
        state, o = _dn_chunk(state_s[...], q_ref[...], k_ref[...], v_ref[...], g_ref[:, 0], b_ref[:, 0])
        state_s[...] = state
        o_ref[...] = o

    seq = _bs((nh, c, hd), lambda i: (0, i, 0))
    row = _bs((nh, 1, 1, c), lambda i: (0, i, 0, 0))
    return _pc(body, "dn_scan_fwd", (n,), [seq, seq, seq, row, row],
               [seq, _bs((nh, 1, hd, hd), lambda i: (0, i, 0, 0))],
               [_sds((nh, lp, hd)), _sds((nh, n, hd, hd))],
               scratch=[pltpu.VMEM((nh, hd, hd), f32)])(q, k, v, grow, brow)


def _dn_scan_bwd(q, k, v, grow, brow, states, do):
    nh, lp, hd = q.shape
    c = DN_CHUNK
    n = lp // c

    def body(q_ref, k_ref, v_ref, g_ref, b_ref, st_ref, do_ref, dq_ref, dk_ref, dv_ref, dg_ref, db_ref, dstate_s):
        @pl.when(pl.program_id(0) == 0)
        def _():
            dstate_s[...] = jnp.zeros_like(dstate_s)

        _, vjp = jax.vjp(_dn_chunk, st_ref[:, 0], q_ref[...], k_ref[...], v_ref[...], g_ref[:, 0], b_ref[:, 0])
        dstate, dq, dk, dv, dg, db = vjp((dstate_s[...], do_ref[...]))
        dstate_s[...] = dstate
        dq_ref[...], dk_ref[...], dv_ref[...] = dq, dk, dv
        dg_ref[:, 0], db_ref[:, 0] = dg, db

    seq = _bs((nh, c, hd), lambda i: (0, n - 1 - i, 0))
    row = _bs((nh, 1, 1, c), lambda i: (0, n - 1 - i, 0, 0))
    return _pc(body, "dn_scan_bwd", (n,),
               [seq, seq, seq, row, row, _bs((nh, 1, hd, hd), lambda i: (0, n - 1 - i, 0, 0)), seq],
               [seq, seq, seq, row, row],
               [_sds((nh, lp, hd))] * 3 + [_sds((nh, n, 1, c))] * 2,
               scratch=[pltpu.VMEM((nh, hd, hd), f32)])(q, k, v, grow, brow, states, do)


def _s5_prep(a_re, a_im, log_dt, b_re, b_im):
    dt = jnp.exp(log_dt)
    mag = jnp.exp(a_re * dt)
    ar, ai = mag * jnp.cos(a_im * dt), mag * jnp.sin(a_im * dt)
    den = a_re * a_re + a_im * a_im
    cr = ((ar - 1.0) * a_re + ai * a_im) / den
    ci = (ai * a_re - (ar - 1.0) * a_im) / den
    cr3, ci3 = cr[:, None, :], ci[:, None, :]
    return ar, ai, cr3 * b_re - ci3 * b_im, cr3 * b_im + ci3 * b_re


def _s5_prep_fwd(a_re, a_im, log_dt, b_re, b_im):
    def body(ar_ref, ai_ref, dt_ref, br_ref, bi_ref, *outs):
        for ref, val in zip(outs, _s5_prep(ar_ref[...], ai_ref[...], dt_ref[...], br_ref[...], bi_ref[...])):
            ref[...] = val

    return pl.pallas_call(body, name="s5_prep_fwd",
                          out_shape=[_sds(a_re.shape), _sds(a_re.shape), _sds(b_re.shape), _sds(b_re.shape)],
                          )(a_re, a_im, log_dt, b_re, b_im)


def _s5_prep_bwd(a_re, a_im, log_dt, b_re, b_im, d_ar, d_ai, d_br, d_bi):
    def body(ar_ref, ai_ref, dt_ref, br_ref, bi_ref, g0, g1, g2, g3, *outs):
        _, vjp = jax.vjp(_s5_prep, ar_ref[...], ai_ref[...], dt_ref[...], br_ref[...], bi_ref[...])
        for ref, val in zip(outs, vjp((g0[...], g1[...], g2[...], g3[...]))):
            ref[...] = val

    return pl.pallas_call(body, name="s5_prep_bwd",
                          out_shape=[_sds(a_re.shape), _sds(a_re.shape), _sds(log_dt.shape), _sds(b_re.shape),
                                     _sds(b_re.shape)])(a_re, a_im, log_dt, b_re, b_im, d_ar, d_ai, d_br, d_bi)


def _s5_block_len(lp):
    return 128 if lp % 128 == 0 else 64


def _s5_powers(ar, ai, tb):
    out = []
    k = 1
    while k < tb:
        out.append((ar, ai))
        ar, ai = ar * ar - ai * ai, 2.0 * ar * ai
        k *= 2
    return out


def _s5_scan_rows(xr, xi, pows, reverse, period=None):
    tb = xr.shape[0]
    span = tb if period is None else period
    row = _iota2((tb, 1), 0)
    if period is not None:
        row = jnp.bitwise_and(row, period - 1)
    k = 1
    for pr, pi in pows:
        if reverse:
            keep = row < span - k
            sr, si = pltpu.roll(xr, tb - k, 0), pltpu.roll(xi, tb - k, 0)
        else:
            keep = row >= k
            sr, si = pltpu.roll(xr, k, 0), pltpu.roll(xi, k, 0)
        sr, si = jnp.where(keep, sr, 0.0), jnp.where(keep, si, 0.0)
        xr, xi = xr + pr * sr - pi * si, xi + pr * si + pi * sr
        k *= 2
    return xr, xi


def _s5_slab_mm(x, w_ref, dims=NN):
    a = x.shape[1] // S5_SLABS
    return jnp.concatenate([_dot(x[:, j * a:(j + 1) * a], w_ref[j], dims) for j in range(S5_SLABS)], axis=1)


def _s5_power_table(ar, ai, tb, reverse):
    at = _iota2((tb, 1), 0) == (tb - 1 if reverse else 0)
    return _s5_scan_rows(jnp.where(at, ar, 0.0), jnp.where(at, ai, 0.0), _s5_powers(ar, ai, tb), reverse)


_S5_GROUP = 8


def _s5_scan_block(xr, xi, ar, ai, carry_r, carry_i, reverse):
    tb = xr.shape[0]
    lr, li = _s5_scan_rows(xr, xi, _s5_powers(ar, ai, _S5_GROUP), reverse, period=_S5_GROUP)
    wr, wi = _s5_power_table(ar, ai, _S5_GROUP, reverse)
    n = tb // _S5_GROUP
    out_r, out_i = [None] * n, [None] * n
    for g in (range(n - 1, -1, -1) if reverse else range(n)):
        rs = slice(_S5_GROUP * g, _S5_GROUP * (g + 1))
        sr = lr[rs] + wr * carry_r - wi * carry_i
        si = li[rs] + wr * carry_i + wi * carry_r
        out_r[g], out_i[g] = sr, si
        edge = slice(0, 1) if reverse else slice(_S5_GROUP - 1, _S5_GROUP)
        carry_r, carry_i = sr[edge], si[edge]
    return jnp.concatenate(out_r, axis=0), jnp.concatenate(out_i, axis=0), carry_r, carry_i


def _s5_scan_fwd(u, ar, ai, b8r, b8i, c8r, c8i, dvec):
    lp = u.shape[0]
    tb = _s5_block_len(lp)
    nblk = lp // tb

    def body(u_ref, ar_ref, ai_ref, b8r_ref, b8i_ref, c8r_ref, c8i_ref, d_ref, y_ref, cin_ref, carry_s):
        @pl.when(pl.program_id(0) == 0)
        def _():
            carry_s[...] = jnp.zeros_like(carry_s)

        cin_ref[0] = carry_s[...]
        uv = u_ref[...]
        sr, si, out_r, out_i = _s5_scan_block(_s5_slab_mm(uv, b8r_ref), _s5_slab_mm(uv, b8i_ref), ar_ref[...],
                                              ai_ref[...], carry_s[0:1], carry_s[1:2], False)
        carry_s[0:1] = out_r
        carry_s[1:2] = out_i
        y_ref[...] = _s5_slab_mm(sr, c8r_ref) - _s5_slab_mm(si, c8i_ref) + d_ref[...] * uv

    const = lambda shape: _bs(shape, lambda i: (0,) * len(shape))
    return _pc(
        body, "s5_scan_fwd", (nblk,),
        [_bs((tb, S5_W), lambda i: (i, 0)), const((1, S5_N)), const((1, S5_N)),
         const((S5_SLABS, 128, 512)), const((S5_SLABS, 128, 512)),
         const((S5_SLABS, 512, 128)), const((S5_SLABS, 512, 128)), const((1, S5_W))],
        [_bs((tb, S5_W), lambda i: (i, 0)), _bs((1, 8, S5_N), lambda i: (i, 0, 0))],
        [_sds((lp, S5_W)), _sds((nblk, 8, S5_N))],
        scratch=[pltpu.VMEM((8, S5_N), f32)])(u, ar, ai, b8r, b8i, c8r, c8i, dvec)


def _s5_scan_bwd(u, dy, cin, ar, ai, b8r, b8i, c8r, c8i, dvec):
    lp = u.shape[0]
    tb = _s5_block_len(lp)
    nblk = lp // tb

    def body(u_ref, dy_ref, cin_ref, ar_ref, ai_ref, b8r_ref, b8i_ref, c8r_ref, c8i_ref, d_ref,
             du_ref, dab_ref, db8r_ref, db8i_ref, dc8r_ref, dc8i_ref, dd_ref, lam_s):
        @pl.when(pl.program_id(0) == 0)
        def _():
            lam_s[...] = jnp.zeros_like(lam_s)
            for ref in (dab_ref, db8r_ref, db8i_ref, dc8r_ref, dc8i_ref, dd_ref):
                ref[...] = jnp.zeros_like(ref)

        uv, dyv = u_ref[...], dy_ref[...]
        a_r, a_i = ar_ref[...], ai_ref[...]
        cin_r, cin_i = cin_ref[0, 0:1], cin_ref[0, 1:2]
        sr, si, _, _ = _s5_scan_block(_s5_slab_mm(uv, b8r_ref), _s5_slab_mm(uv, b8i_ref), a_r, a_i, cin_r, cin_i, False)
        lr, li, top_r, top_i = _s5_scan_block(_s5_slab_mm(dyv, c8r_ref, NT), -_s5_slab_mm(dyv, c8i_ref, NT),
                                              a_r, -a_i, lam_s[0:1], lam_s[1:2], True)
        lam_s[0:1] = top_r
        lam_s[1:2] = top_i
        first = _iota2((tb, 1), 0) == 0
        pr = jnp.where(first, cin_r, pltpu.roll(sr, 1, 0))
        pi = jnp.where(first, cin_i, pltpu.roll(si, 1, 0))
        dab_ref[0:1] += jnp.sum(lr * pr + li * pi, axis=0, keepdims=True)
        dab_ref[1:2] += jnp.sum(li * pr - lr * pi, axis=0, keepdims=True)
        du_ref[...] = _s5_slab_mm(lr, b8r_ref, NT) + _s5_slab_mm(li, b8i_ref, NT) + d_ref[...] * dyv
        dd_ref[...] += jnp.sum(dyv * uv, axis=0, keepdims=True)
        for j in range(S5_SLABS):
            us, dys = uv[:, j * 128:(j + 1) * 128], dyv[:, j * 128:(j + 1) * 128]
            st = slice(j * 512, (j + 1) * 512)
            db8r_ref[j] += _dot(us, lr[:, st], TN)
            db8i_ref[j] += _dot(us, li[:, st], TN)
            dc8r_ref[j] += _dot(sr[:, st], dys, TN)
            dc8i_ref[j] -= _dot(si[:, st], dys, TN)

    const = lambda shape: _bs(shape, lambda i: (0,) * len(shape))
    rev = _bs((tb, S5_W), lambda i: (nblk - 1 - i, 0))
    return _pc(
        body, "s5_scan_bwd", (nblk,),
        [rev, rev, _bs((1, 8, S5_N), lambda i: (nblk - 1 - i, 0, 0)), const((1, S5_N)), const((1, S5_N)),
         const((S5_SLABS, 128, 512)), const((S5_SLABS, 128, 512)),
         const((S5_SLABS, 512, 128)), const((S5_SLABS, 512, 128)), const((1, S5_W))],
        [rev, const((8, S5_N)), const((S5_SLABS, 128, 512)), const((S5_SLABS, 128, 512)),
         const((S5_SLABS, 512, 128)), const((S5_SLABS, 512, 128)), const((1, S5_W))],
        [_sds((lp, S5_W)), _sds((8, S5_N)), _sds((S5_SLABS, 128, 512)), _sds((S5_SLABS, 128, 512)),
         _sds((S5_SLABS, 512, 128)), _sds((S5_SLABS, 512, 128)), _sds((1, S5_W))],
        scratch=[pltpu.VMEM((8, S5_N), f32)])(u, dy, cin, ar, ai, b8r, b8i, c8r, c8i, dvec)


def _mix_tile(osb, odn, z, ys5, g_sb, g_dn, w_glu, b_glu, g_s5):
    bd = _block_diag_ones(QW, HEAD_DIM)
    tile4 = ((_iota2((HEAD_DIM, QW), 1) % HEAD_DIM) == _iota2((HEAD_DIM, QW), 0)).astype(f32)
    seg_rms = lambda x: x * lax.rsqrt(_dotx(x * x, bd) * (1.0 / HEAD_DIM) + EPS)
    sbn = seg_rms(osb) * _dotx(g_sb, tile4)
    dnn = seg_rms(odn) * _dotx(g_dn, tile4) * _silu(z)
    y = _gelu_tanh(ys5)
    glu = y * _sigmoid(_mm(y, w_glu) + b_glu)
    return jnp.concatenate([sbn, dnn, _rms(glu, g_s5)], axis=1)


def _mixout_fwd(h, osb, odn, z, ys5, g_sb, g_dn, w_glu, b_glu, g_s5, w_out):
    lp, d = h.shape
    tm = _token_tile(lp)

    def body(h_ref, osb_ref, odn_ref, z_ref, ys_ref, gsb_ref, gdn_ref, wg_ref, bg_ref, gs5_ref, wo_ref, o_ref):
        mixed = _mix_tile(osb_ref[...], odn_ref[...], z_ref[...], ys_ref[...], gsb_ref[...], gdn_ref[...],
                          wg_ref[...], bg_ref[...], gs5_ref[...])
        o_ref[...] = h_ref[...] + _dot(mixed, wo_ref[...])

    tok = lambda wd: _bs((tm, wd), lambda i: (i, 0))
    const = lambda shape: _bs(shape, lambda i: (0,) * len(shape))
    return _pc(
        body, "mixout_fwd", (lp // tm,),
        [tok(d), tok(QW), tok(QW), tok(QW), tok(S5_W), const((1, HEAD_DIM)), const((1, HEAD_DIM)),
         const((S5_W, S5_W)), const((1, S5_W)), const((1, S5_W)), const((d, d))],
        tok(d), _sds((lp, d)))(h, osb, odn, z, ys5, g_sb, g_dn, w_glu, b_glu, g_s5, w_out)


def _mixout_bwd(dh, osb, odn, z, ys5, g_sb, g_dn, w_glu, b_glu, g_s5, w_out):
    lp, d = dh.shape
    tm = _token_tile(lp)

    def body(dh_ref, osb_ref, odn_ref, z_ref, ys_ref, gsb_ref, gdn_ref, wg_ref, bg_ref, gs5_ref, wo_ref,
             dosb_ref, dodn_ref, dz_ref, dys_ref, dgsb_ref, dgdn_ref, dwg_ref, dbg_ref, dgs5_ref, dwo_ref):
        accs = (dgsb_ref, dgdn_ref, dwg_ref, dbg_ref, dgs5_ref)

        @pl.when(pl.program_id(0) == 0)
        def _():
            for ref in accs + (dwo_ref,):
                ref[...] = jnp.zeros_like(ref)

        mixed, vjp = jax.vjp(_mix_tile, osb_ref[...], odn_ref[...], z_ref[...], ys_ref[...], gsb_ref[...],
                             gdn_ref[...], wg_ref[...], bg_ref[...], gs5_ref[...])
        dhv = dh_ref[...]
        dwo_ref[...] += _dot(mixed, dhv, TN)
        grads = vjp(_dot(dhv, wo_ref[...], NT))
        for ref, val in zip((dosb_ref, dodn_ref, dz_ref, dys_ref), grads[:4]):
            ref[...] = val
        for ref, val in zip(accs, grads[4:]):
            ref[...] += val

    tok = lambda wd: _bs((tm, wd), lambda i: (i, 0))
    const = lambda shape: _bs(shape, lambda i: (0,) * len(shape))
    params = [const((1, HEAD_DIM)), const((1, HEAD_DIM)), const((S5_W, S5_W)), const((1, S5_W)), const((1, S5_W))]
    return _pc(
        body, "mixout_bwd", (lp // tm,),
        [tok(d), tok(QW), tok(QW), tok(QW), tok(S5_W)] + params + [const((d, d))],
        [tok(QW), tok(QW), tok(QW), tok(S5_W)] + params + [const((d, d))],
        [_sds((lp, QW))] * 3 + [_sds((lp, S5_W)), _sds((1, HEAD_DIM)), _sds((1, HEAD_DIM)), _sds((S5_W, S5_W)),
                                _sds((1, S5_W)), _sds((1, S5_W)), _sds((d, d))],
    )(dh, osb, odn, z, ys5, g_sb, g_dn, w_glu, b_glu, g_s5, w_out)


def _loss_fwd_bwd(h, g, target, n_real):
    lp, d = h.shape
    tm = _token_tile(lp)

    def body(h_ref, g_ref, t_ref, loss_ref, dh_ref, dg_ref):
        i = pl.program_id(0)

        @pl.when(i == 0)
        def _():
            loss_ref[...] = jnp.zeros_like(loss_ref)
            dg_ref[...] = jnp.zeros_like(dg_ref)

        pos = i * tm + _iota2((tm, 1), 0)
        real = ((pos >= N_META) & (pos < n_real)).astype(f32)
        y, vjp = jax.vjp(_rms, h_ref[...], g_ref[...])
        err = (y - t_ref[...]) * real
        loss_ref[...] += 0.5 * jnp.sum(jnp.mean(err * err, axis=1, keepdims=True))
        dx, dg = vjp(err * (1.0 / d))
        dh_ref[...] = dx
        dg_ref[...] += dg

    tok = _bs((tm, d), lambda i: (i, 0))
    return _pc(body, "loss_fwd_bwd", (lp // tm,), [tok, _bs((1, d), lambda i: (0, 0)), tok],
               [_bs((8, 128), lambda i: (0, 0)), tok, _bs((1, d), lambda i: (0, 0))],
               [_sds((8, 128)), _sds((lp, d)), _sds((1, d))])(h, g, target)


def _row_tile(rows):
    for t in (256, 128, 64, 32, 16, 8):
        if rows % t == 0:
            return t
    raise ValueError(rows)


def _adamw(w, g, m, v):
    shape = w.shape
    w, g, m, v = (a.reshape(-1, shape[-1]) for a in (w, g, m, v))
    rows, width = w.shape
    tr = _row_tile(rows)

    def body(w_ref, g_ref, m_ref, v_ref, d_ref, mo_ref, vo_ref):
        gv = g_ref[...]
        m_new = ADAM_B1 * m_ref[...] + (1.0 - ADAM_B1) * gv
        v_new = ADAM_B2 * v_ref[...] + (1.0 - ADAM_B2) * (gv * gv)
        m_hat = m_new / (1.0 - ADAM_B1 ** ADAM_STEP)
        v_hat = v_new / (1.0 - ADAM_B2 ** ADAM_STEP)
        d_ref[...] = -ADAM_LR * (m_hat / (jnp.sqrt(v_hat) + ADAM_EPS) + ADAM_WD * w_ref[...])
        mo_ref[...] = m_new
        vo_ref[...] = v_new

    blk = _bs((tr, width), lambda i: (i, 0))
    outs = _pc(body, "adamw", (rows // tr,), [blk] * 4, [blk] * 3, [_sds(w.shape)] * 3)(w, g, m, v)
    return tuple(o.reshape(shape) for o in outs)


def _sum_leading(x, name):
    n, rows, _ = x.shape
    tr = _row_tile(rows)

    def body(x_ref, o_ref):
        acc = x_ref[0]
        for k in range(1, n):
            acc = acc + x_ref[k]
        o_ref[...] = acc

    return _pc(body, name, (rows // tr,), [_bs((n, tr, FLAT_W), lambda i: (0, i, 0))],
               _bs((tr, FLAT_W), lambda i: (i, 0)), _sds((rows, FLAT_W)))(x)


def _pair_add(a, b, name):
    rows = a.shape[0]
    tr = _row_tile(rows)

    def body(a_ref, b_ref, o_ref):
        o_ref[...] = a_ref[...] + b_ref[...]

    blk = _bs((tr, FLAT_W), lambda i: (i, 0))
    return _pc(body, name, (rows // tr,), [blk, blk], blk, _sds(a.shape))(a, b)


_CHIP_FLIPS = ((1, 0, 0), (0, 1, 0), (1, 1, 0))
_ALL_FLIPS = tuple((a, b, c) for a in (0, 1) for b in (0, 1) for c in (0, 1) if (a, b, c) != (0, 0, 0))
_CORE_FLIP = ((0, 0, 1),)
_D2D_STREAMS = 4


def _exchange(name, arrays, out_shapes, flips, plan):
    n_in = len(arrays)

    def body(*refs):
        ins, outs = refs[:n_in], refs[n_in:n_in + len(out_shapes)]
        send_sems, recv_sems, local_sems = refs[n_in + len(out_shapes):]
        me = (lax.axis_index("x"), lax.axis_index("y"), lax.axis_index("c"))
        local = [pltpu.make_async_copy(s, d, local_sems.at[n]) for n, (s, d) in enumerate(plan(me, None, ins, outs))]
        for cp in local:
            cp.start()
        sent, k = [], 0
        for f in flips:
            peer = tuple(1 - m if fl else m for m, fl in zip(me, f))
            for s, d in plan(me, peer, ins, outs):
                cp = pltpu.make_async_remote_copy(src_ref=s, dst_ref=d, send_sem=send_sems.at[k],
                                                  recv_sem=recv_sems.at[k], device_id=peer,
                                                  device_id_type=pl.DeviceIdType.MESH)
                cp.start()
                sent.append(cp)
                k += 1
        for cp in sent:
            cp.wait_recv()
        for cp in sent:
            cp.wait_send()
        for cp in local:
            cp.wait()

    me0 = (0, 0, 0)
    n_remote = sum(len(_plan_count(plan, me0, f, arrays, out_shapes)) for f in flips)
    n_local = len(_plan_count(plan, me0, None, arrays, out_shapes))
    hbm = pl.BlockSpec(memory_space=pltpu.HBM)
    return pl.pallas_call(
        body, name=name, in_specs=[hbm] * n_in, out_specs=[hbm] * len(out_shapes), out_shape=list(out_shapes),
        scratch_shapes=[pltpu.SemaphoreType.DMA((n_remote,)), pltpu.SemaphoreType.DMA((n_remote,)),
                        pltpu.SemaphoreType.DMA((max(n_local, 1),))],
        compiler_params=pltpu.CompilerParams(has_side_effects=True))(*arrays)


class _FakeRef:
    def __init__(self):
        self.at = self

    def __getitem__(self, idx):
        return self


def _plan_count(plan, me, flip, arrays, out_shapes):
    peer = None if flip is None else me
    return plan(me, peer, [_FakeRef() for _ in arrays], [_FakeRef() for _ in out_shapes])


def _chip_index(dev):
    return 2 * dev[0] + dev[1]


def _all_gather_chips(name, arrays):
    n = len(arrays)

    def body(*refs):
        ins, outs = refs[:n], refs[n:2 * n]
        send_sems, recv_sems, local_sems = refs[2 * n:]
        x, y, c = lax.axis_index("x"), lax.axis_index("y"), lax.axis_index("c")
        sibling = (x, y, 1 - c)
        chips = [(1 - x, y), (x, 1 - y), (1 - x, 1 - y)]
        mine = 2 * x + y

        def copy(k, src, dst, to):
            return pltpu.make_async_remote_copy(src_ref=src, dst_ref=dst, send_sem=send_sems.at[k],
                                                recv_sem=recv_sems.at[k], device_id=to,
                                                device_id_type=pl.DeviceIdType.MESH)

        local = [pltpu.make_async_copy(ins[a], outs[a].at[mine], local_sems.at[a]) for a in range(n)]
        for cp in local:
            cp.start()
        first = [copy(j * n + a, ins[a].at[c], outs[a].at[mine, c], (*chip, c))
                 for j, chip in enumerate(chips) for a in range(n)]
        for cp in first:
            cp.start()
        passed = []
        for j, chip in enumerate(chips):
            for a in range(n):
                landed = outs[a].at[_chip_index(chip), c]
                copy(j * n + a, landed, landed, sibling).wait_recv()
                cp = copy(3 * n + j * n + a, landed, landed, sibling)
                cp.start()
                passed.append(cp)
        for j, chip in enumerate(chips):
            for a in range(n):
                other = outs[a].at[_chip_index(chip), 1 - c]
                copy(3 * n + j * n + a, other, other, sibling).wait_recv()
        for cp in first + passed:
            cp.wait_send()
        for cp in local:
            cp.wait()

    hbm = pl.BlockSpec(memory_space=pltpu.HBM)
    return pl.pallas_call(
        body, name=name, in_specs=[hbm] * n, out_specs=[hbm] * n,
        out_shape=[_sds((N_CHIPS,) + a.shape, a.dtype) for a in arrays],
        scratch_shapes=[pltpu.SemaphoreType.DMA((6 * n,)), pltpu.SemaphoreType.DMA((6 * n,)),
                        pltpu.SemaphoreType.DMA((n,))],
        compiler_params=pltpu.CompilerParams(has_side_effects=True))(*arrays)


def _all_gather_devices(name, arr):
    def plan(me, peer, ins, outs):
        return [(ins[0], outs[0].at[4 * me[0] + 2 * me[1] + me[2]])]

    return _exchange(name, [arr], [_sds((8,) + arr.shape, arr.dtype)], _ALL_FLIPS, plan)[0]


def _swap_half_with_sibling(name, g):
    step = g.shape[2] // _D2D_STREAMS

    def plan(me, peer, ins, outs):
        if peer is None:
            return []
        return [(ins[0].at[k, 1 - me[2], pl.ds(r * step, step)], outs[0].at[k, pl.ds(r * step, step)])
                for k in range(N_CHIPS) for r in range(_D2D_STREAMS)]

    return _exchange(name, [g], [_sds((N_CHIPS,) + g.shape[2:], g.dtype)], _CORE_FLIP, plan)[0]


def _scatter_to_chips(name, s):
    def plan(me, peer, ins, outs):
        to = me if peer is None else peer
        return [(ins[0].at[_chip_index(to)], outs[0].at[_chip_index(me)])]

    return _exchange(name, [s], [_sds(s.shape, s.dtype)], _CHIP_FLIPS, plan)[0]


def _share_with_sibling(name, r):
    step = r.shape[0] // (4 * _D2D_STREAMS)

    def plan(me, peer, ins, outs):
        return [(ins[0].at[pl.ds(n * step, step)], outs[0].at[me[2], pl.ds(n * step, step)])
                for n in range(4 * _D2D_STREAMS)]

    return _exchange(name, [r], [_sds((2,) + r.shape, r.dtype)], _CORE_FLIP, plan)[0]


def _to_heads(x):
    return x.reshape(x.shape[0], N_HEADS, HEAD_DIM).transpose(1, 0, 2)


def _from_heads(x):
    return x.transpose(1, 0, 2).reshape(x.shape[1], QW)


def _to_heads_t(x):
    return x.T.reshape(N_HEADS, HEAD_DIM, x.shape[0])


def _from_heads_t(x):
    return x.reshape(QW, x.shape[2]).T


def _reorder_w_in(w):
    o = 3 * QW + 3 * QW + QW
    main = jnp.concatenate([w[:, :o], w[:, o + 2 * N_HEADS:]], axis=1)
    ba = jnp.pad(w[:, o:o + 2 * N_HEADS], ((0, 0), (0, 128 - 2 * N_HEADS)))
    return jnp.concatenate([main, ba], axis=1)


def _restore_w_in(w):
    o = 3 * QW + 3 * QW + QW
    return jnp.concatenate([w[:, :o], w[:, PROJ_W - 128:PROJ_W - 128 + 2 * N_HEADS], w[:, o:PROJ_W - 128]], axis=1)


def _slab_embed_b(b):
    x = b.reshape(S5_SLABS, 8, S5_C, S5_P)
    eye = jnp.eye(8, dtype=b.dtype)
    return (x[:, :, :, None, :] * eye[None, :, None, :, None]).reshape(S5_SLABS, 8 * S5_C, 8 * S5_P)


def _slab_extract_b(m):
    x = m.reshape(S5_SLABS, 8, S5_C, 8, S5_P)
    return jnp.stack([x[:, g, :, g, :] for g in range(8)], axis=1).reshape(S5_G, S5_C, S5_P)


def _slab_embed_c(c):
    x = c.reshape(S5_SLABS, 8, S5_C, S5_P).transpose(0, 1, 3, 2)
    eye = jnp.eye(8, dtype=c.dtype)
    return (x[:, :, :, None, :] * eye[None, :, None, :, None]).reshape(S5_SLABS, 8 * S5_P, 8 * S5_C)


def _slab_extract_c(m):
    x = m.reshape(S5_SLABS, 8, S5_P, 8, S5_C)
    return jnp.stack([x[:, g, :, g, :] for g in range(8)], axis=1).transpose(0, 1, 3, 2).reshape(S5_G, S5_C, S5_P)


def _piece_rows(shape):
    return -(-math.prod(shape) // (8 * FLAT_W)) * 8


def _pack_rows(parts, lead, row_align):
    rows = []
    for p in parts:
        flat = p.reshape(lead + (-1,))
        r = _piece_rows(p.shape[len(lead):])
        flat = jnp.pad(flat, [(0, 0)] * len(lead) + [(0, r * FLAT_W - flat.shape[-1])])
        rows.append(flat.reshape(lead + (r, FLAT_W)))
    total = sum(r.shape[-2] for r in rows)
    rows.append(jnp.zeros(lead + ((-total) % row_align, FLAT_W), parts[0].dtype))
    return jnp.concatenate(rows, axis=len(lead))


def _unpack_rows(flat, shapes):
    out, off = [], 0
    for s in shapes:
        r = _piece_rows(s)
        out.append(flat[off:off + r].reshape(-1)[:math.prod(s)].reshape(s))
        off += r
    return out


_SHARDED = ("ffn1_w_gate", "ffn1_w_up", "ffn1_w_down", "w_in", "s5_w_glu", "w_out",
            "ffn2_w_gate", "ffn2_w_up", "ffn2_w_down", "meta_tokens", "dn_conv_w")
_MATMUL_W = _SHARDED[:9]
_WEIGHTS = ("meta_tokens", "ffn1_norm", "ffn1_w_gate", "ffn1_w_up", "ffn1_w_down", "mix_norm", "w_in", "sb_out_norm",
            "dn_conv_w", "dn_a_log", "dn_dt_bias", "dn_out_norm", "s5_a_re", "s5_a_im", "s5_log_dt", "s5_b_re",
            "s5_b_im", "s5_c_re", "s5_c_im", "s5_d", "s5_w_glu", "s5_b_glu", "s5_out_norm", "w_out", "ffn2_norm",
            "ffn2_w_gate", "ffn2_w_up", "ffn2_w_down", "final_norm")
_REPLICATED = tuple(n for n in _WEIGHTS if n not in _SHARDED)


def _chip_major(name, g):
    if name in ("ffn1_w_gate", "ffn1_w_up", "ffn2_w_gate", "ffn2_w_up", "ffn1_w_down", "ffn2_w_down"):
        return g.transpose(1, 0, 2, 3)
    if name == "w_in":
        return g.reshape(2, D_MODEL, N_CHIPS, IN_WIDTH // N_CHIPS).transpose(2, 0, 1, 3)
    if name in ("w_out", "s5_w_glu"):
        return g.reshape(2, N_CHIPS, g.shape[1] // N_CHIPS, g.shape[2]).transpose(1, 0, 2, 3)
    if name == "meta_tokens":
        return g.reshape(N_META, N_CHIPS, D_MODEL // N_CHIPS).transpose(1, 0, 2)
    if name == "dn_conv_w":
        return g.reshape(2, DN_CONV, N_CHIPS, 3 * QW // N_CHIPS).transpose(2, 0, 1, 3)
    raise ValueError(name)


def _layer_forward(h, p):
    lp = h.shape[0]
    h1, *ffn1_saved = _ffn_fwd(h, p["ffn1_norm"], *p["ffn1"])
    q, k, v, dnx, z, u, ba = _inproj_fwd(h1, p["mix_norm"], p["w_in"].astype(MXU_DTYPE))
    qh, kh, vh = _to_heads(q), _to_heads_t(k), _to_heads_t(v)
    osb, sb_w, sb_sig = _sb_fwd(qh, kh, vh)
    dq_, dk_, dv_, bg = _dn_pre_fwd(dnx, ba, p["cw"], p["pv"])
    dqh, dkh, dvh = _to_heads(dq_), _to_heads(dk_), _to_heads(dv_)
    brow = bg[:, :N_HEADS].T.reshape(N_HEADS, lp // DN_CHUNK, 1, DN_CHUNK)
    grow = bg[:, N_HEADS:2 * N_HEADS].T.reshape(N_HEADS, lp // DN_CHUNK, 1, DN_CHUNK)
    odn, states = _dn_scan_fwd(dqh, dkh, dvh, grow, brow)
    ar, ai, bre, bim = _s5_prep_fwd(p["s5_a_re"], p["s5_a_im"], p["s5_log_dt"], p["s5_b_re"], p["s5_b_im"])
    s5t = (ar.reshape(1, S5_N), ai.reshape(1, S5_N), _slab_embed_b(bre), _slab_embed_b(bim),
           _slab_embed_c(p["s5_c_re"]), _slab_embed_c(p["s5_c_im"]), p["s5_d"])
    ys5, cin = _s5_scan_fwd(u, *s5t)
    osb_t, odn_t = _from_heads(osb), _from_heads(odn)
    h2 = _mixout_fwd(h1, osb_t, odn_t, z, ys5, p["sb_out_norm"], p["dn_out_norm"], p["s5_w_glu"], p["s5_b_glu"],
                     p["s5_out_norm"], p["w_out"].astype(MXU_DTYPE))
    h3, *ffn2_saved = _ffn_fwd(h2, p["ffn2_norm"], *p["ffn2"])
    saved = dict(h0=h, h1=h1, h2=h2, ffn1=ffn1_saved, ffn2=ffn2_saved, qh=qh, kh=kh, vh=vh, sb_w=sb_w, sb_sig=sb_sig, dnx=dnx, ba=ba, dqh=dqh, dkh=dkh, dvh=dvh,
                 grow=grow, brow=brow, states=states, odn_t=odn_t, osb_t=osb_t, z=z, u=u, ys5=ys5, cin=cin, s5t=s5t)
    return h3, saved


def _ffn_backward(h, g, w3, dy, fwd_saved):
    xn, gate, up = fwd_saved
    dh, dgn, d_gate, d_up, act = _ffn_bwd_dx(h, g, *w3, dy, gate, up)
    dwg, dwu, dwd = _ffn_bwd_dw(xn, dy, d_gate, d_up, act)
    return dh, dgn, dwg, dwu, dwd


def _layer_backward(dh3, p, s):
    lp = dh3.shape[0]
    g = {}
    dh2, g["ffn2_norm"], g["ffn2_w_gate"], g["ffn2_w_up"], g["ffn2_w_down"] = _ffn_backward(
        s["h2"], p["ffn2_norm"], p["ffn2"], dh3, s["ffn2"])
    (dosb_t, dodn_t, dz, dys5, g["sb_out_norm"], g["dn_out_norm"], g["s5_w_glu"], g["s5_b_glu"], g["s5_out_norm"],
     g["w_out"]) = _mixout_bwd(dh2, s["osb_t"], s["odn_t"], s["z"], s["ys5"], p["sb_out_norm"], p["dn_out_norm"],
                               p["s5_w_glu"], p["s5_b_glu"], p["s5_out_norm"], p["w_out"].astype(MXU_DTYPE))
    du, dab, db8r, db8i, dc8r, dc8i, g["s5_d"] = _s5_scan_bwd(s["u"], dys5, s["cin"], *s["s5t"])
    g["s5_c_re"], g["s5_c_im"] = _slab_extract_c(dc8r), _slab_extract_c(dc8i)
    g["s5_a_re"], g["s5_a_im"], g["s5_log_dt"], g["s5_b_re"], g["s5_b_im"] = _s5_prep_bwd(
        p["s5_a_re"], p["s5_a_im"], p["s5_log_dt"], p["s5_b_re"], p["s5_b_im"],
        dab[0].reshape(S5_G, S5_P), dab[1].reshape(S5_G, S5_P), _slab_extract_b(db8r), _slab_extract_b(db8i))
    ddq, ddk, ddv, dgrow, dbrow = _dn_scan_bwd(s["dqh"], s["dkh"], s["dvh"], s["grow"], s["brow"], s["states"],
                                               _to_heads(dodn_t))
    dbg = jnp.concatenate([dbrow.reshape(N_HEADS, lp).T, dgrow.reshape(N_HEADS, lp).T,
                           jnp.zeros((lp, 128 - 2 * N_HEADS), f32)], axis=1)
    dxs, dba, g["cw"], g["pv"] = _dn_pre_bwd(s["dnx"], s["ba"], p["cw"], p["pv"], _from_heads(ddq), _from_heads(ddk),
                                             _from_heads(ddv), dbg)
    dq, dk_t, dv_t = _sb_bwd(s["qh"], s["kh"], s["vh"], s["sb_w"], s["sb_sig"], _to_heads(dosb_t))
    dh1, g["mix_norm"], g["w_in"] = _inproj_bwd(s["h1"], p["mix_norm"], p["w_in"].astype(MXU_DTYPE), dh2,
                                                _from_heads(dq), _from_heads_t(dk_t), _from_heads_t(dv_t), dxs,
                                                dz, du, dba)
    dh0, g["ffn1_norm"], g["ffn1_w_gate"], g["ffn1_w_up"], g["ffn1_w_down"] = _ffn_backward(
        s["h0"], p["ffn1_norm"], p["ffn1"], dh1, s["ffn1"])
    return dh0, g


def kernel(x, meta_tokens, ffn1_norm, ffn1_w_gate, ffn1_w_up, ffn1_w_down, mix_norm, w_in, sb_out_norm, dn_conv_w, dn_a_log, dn_dt_bias, dn_out_norm, s5_a_re, s5_a_im, s5_log_dt, s5_b_re, s5_b_im, s5_c_re, s5_c_im, s5_d, s5_w_glu, s5_b_glu, s5_out_norm, w_out, ffn2_norm, ffn2_w_gate, ffn2_w_up, ffn2_w_down, final_norm, loss_target, m_meta_tokens, m_ffn1_norm, m_ffn1_w_gate, m_ffn1_w_up, m_ffn1_w_down, m_mix_norm, m_w_in, m_sb_out_norm, m_dn_conv_w, m_dn_a_log, m_dn_dt_bias, m_dn_out_norm, m_s5_a_re, m_s5_a_im, m_s5_log_dt, m_s5_b_re, m_s5_b_im, m_s5_c_re, m_s5_c_im, m_s5_d, m_s5_w_glu, m_s5_b_glu, m_s5_out_norm, m_w_out, m_ffn2_norm, m_ffn2_w_gate, m_ffn2_w_up, m_ffn2_w_down, m_final_norm, v_meta_tokens, v_ffn1_norm, v_ffn1_w_gate, v_ffn1_w_up, v_ffn1_w_down, v_mix_norm, v_w_in, v_sb_out_norm, v_dn_conv_w, v_dn_a_log, v_dn_dt_bias, v_dn_out_norm, v_s5_a_re, v_s5_a_im, v_s5_log_dt, v_s5_b_re, v_s5_b_im, v_s5_c_re, v_s5_c_im, v_s5_d, v_s5_w_glu, v_s5_b_glu, v_s5_out_norm, v_w_out, v_ffn2_norm, v_ffn2_w_gate, v_ffn2_w_up, v_ffn2_w_down, v_final_norm):
    args = dict(locals())
    w = {n: args[n] for n in _WEIGHTS}
    m = {n: args["m_" + n] for n in _WEIGHTS}
    v = {n: args["v_" + n] for n in _WEIGHTS}
    depth = ffn1_norm.shape[0]
    seq = x.shape[1]
    n_real = N_META + seq
    lp = _padded_len(n_real)

    gathered = _all_gather_chips("gather_weights", [w[n].astype(MXU_DTYPE) for n in _MATMUL_W]
                                 + [w["meta_tokens"].reshape(2, N_META // 2, -1), w["dn_conv_w"]])
    full = dict(zip(_MATMUL_W + ("meta_tokens", "dn_conv_w"), gathered))
    meta_full = full["meta_tokens"].reshape(N_CHIPS, N_META, -1).transpose(1, 0, 2).reshape(N_META, D_MODEL)
    conv_full = full["dn_conv_w"].transpose(1, 2, 0, 3).reshape(depth, DN_CONV, 3 * QW)
    w_in_full = full["w_in"].transpose(1, 2, 0, 3).reshape(depth, D_MODEL, IN_WIDTH)
    w_out_full = full["w_out"].transpose(1, 0, 2, 3).reshape(depth, D_MODEL, D_MODEL)
    w_glu_full = full["s5_w_glu"].transpose(1, 0, 2, 3).reshape(depth, S5_W, S5_W)

    layers = []
    for l in range(depth):
        pv = jnp.pad(jnp.stack([dn_a_log[l], dn_dt_bias[l]]), ((0, 6), (N_HEADS, 128 - 2 * N_HEADS)))
        layers.append(dict(
            ffn1_norm=ffn1_norm[l][None], mix_norm=mix_norm[l][None], ffn2_norm=ffn2_norm[l][None],
            ffn1=(full["ffn1_w_gate"][:, l], full["ffn1_w_up"][:, l], full["ffn1_w_down"][:, l]),
            ffn2=(full["ffn2_w_gate"][:, l], full["ffn2_w_up"][:, l], full["ffn2_w_down"][:, l]),
            w_in=_reorder_w_in(w_in_full[l]), w_out=w_out_full[l], s5_w_glu=w_glu_full[l].astype(f32),
            cw=jnp.pad(conv_full[l], ((0, 8 - DN_CONV), (0, 0))), pv=pv,
            sb_out_norm=sb_out_norm[l][None], dn_out_norm=dn_out_norm[l][None],
            s5_a_re=s5_a_re[l], s5_a_im=s5_a_im[l], s5_log_dt=s5_log_dt[l][:, None],
            s5_b_re=s5_b_re[l].transpose(0, 2, 1), s5_b_im=s5_b_im[l].transpose(0, 2, 1),
            s5_c_re=s5_c_re[l], s5_c_im=s5_c_im[l], s5_d=s5_d[l][None], s5_b_glu=s5_b_glu[l][None],
            s5_out_norm=s5_out_norm[l][None]))

    tail = jnp.zeros((lp - n_real, D_MODEL), f32)
    h = jnp.concatenate([meta_full, x[0], tail], axis=0)
    target = jnp.concatenate([jnp.zeros((N_META, D_MODEL), f32), loss_target[0], tail], axis=0)
    saved = []
    for p in layers:
        h, s = _layer_forward(h, p)
        saved.append(s)
    loss_blk, dh, d_final = _loss_fwd_bwd(h, final_norm[None], target, n_real)
    grads = [None] * depth
    for l in reversed(range(depth)):
        dh, grads[l] = _layer_backward(dh, layers[l], saved[l])
    loss = lax.psum(loss_blk[0, 0], ("x", "y", "c"))
    grad_x = dh[N_META:n_real][None]

    stack = lambda name: jnp.stack([grads[l][name] for l in range(depth)])
    gfull = {n: stack(n) for n in ("ffn1_w_gate", "ffn1_w_up", "ffn1_w_down", "s5_w_glu", "w_out", "ffn2_w_gate",
                                   "ffn2_w_up", "ffn2_w_down")}
    gfull["w_in"] = jnp.stack([_restore_w_in(grads[l]["w_in"]) for l in range(depth)])
    gfull["meta_tokens"] = dh[:N_META]
    gfull["dn_conv_w"] = jnp.stack([grads[l]["cw"][:DN_CONV] for l in range(depth)])
    grep = {n: stack(n).reshape(w[n].shape) for n in ("ffn1_norm", "mix_norm", "sb_out_norm", "dn_out_norm", "s5_a_re",
                                                      "s5_a_im", "s5_log_dt", "s5_c_re", "s5_c_im", "s5_d", "s5_b_glu",
                                                      "s5_out_norm", "ffn2_norm")}
    grep["s5_b_re"] = jnp.stack([grads[l]["s5_b_re"].transpose(0, 2, 1) for l in range(depth)])
    grep["s5_b_im"] = jnp.stack([grads[l]["s5_b_im"].transpose(0, 2, 1) for l in range(depth)])
    grep["dn_a_log"] = jnp.stack([grads[l]["pv"][0, N_HEADS:2 * N_HEADS] for l in range(depth)])
    grep["dn_dt_bias"] = jnp.stack([grads[l]["pv"][1, N_HEADS:2 * N_HEADS] for l in range(depth)])
    grep["final_norm"] = d_final[0]

    shard_shapes = [w[n].shape for n in _SHARDED]
    g_big = _pack_rows([_chip_major(n, gfull[n]) for n in _SHARDED], (N_CHIPS,), BIG_ROWS)
    half_rows = g_big.shape[1] // 2
    g_big = g_big.reshape(N_CHIPS, 2, half_rows, FLAT_W)
    c = lax.axis_index("c")
    mine = lax.dynamic_index_in_dim(g_big, c, axis=1, keepdims=False)
    theirs = _swap_half_with_sibling("grad_pair_swap", g_big)
    pair = _pair_add(mine.reshape(-1, FLAT_W), theirs.reshape(-1, FLAT_W), "grad_pair_add")
    arrived = _scatter_to_chips("grad_scatter", pair.reshape(N_CHIPS, half_rows, FLAT_W))
    reduced = _sum_leading(arrived, "grad_chip_sum")
    g_shard = _share_with_sibling("grad_share", reduced).reshape(-1, FLAT_W)

    rep_shapes = [w[n].shape for n in _REPLICATED]
    g_small = _pack_rows([grep[n] for n in _REPLICATED], (), 64)
    g_rep = _sum_leading(_all_gather_devices("grad_small_gather", g_small), "grad_small_sum")

    out = {}
    for n, g_n in zip(_SHARDED, _unpack_rows(g_shard, shard_shapes)):
        out["grad_" + n] = g_n
        out["delta_" + n], out["new_m_" + n], out["new_v_" + n] = _adamw(w[n], g_n, m[n], v[n])
    pack = lambda d: _pack_rows([d[n] for n in _REPLICATED], (), 64)
    delta, m_new, v_new = _adamw(pack(w), g_rep, pack(m), pack(v))
    for kind, flat in (("grad", g_rep), ("delta", delta), ("new_m", m_new), ("new_v", v_new)):
        for n, a in zip(_REPLICATED, _unpack_rows(flat, rep_shapes)):
            out[kind + "_" + n] = a
    return (loss, grad_x, *[out[k + "_" + n] for k in ("grad", "delta", "new_m", "new_v") for n in _WEIGHTS])
```

```python
import functools
import math

import jax
import jax.numpy as jnp
from jax import lax
from jax.experimental import pallas as pl
from jax.experimental.pallas import tpu as pltpu

f32 = jnp.float32
MXU_DTYPE = jnp.bfloat16
HI = lax.Precision.HIGHEST
NN = (((1,), (0,)), ((), ()))
NT = (((1,), (1,)), ((), ()))
TN = (((0,), (0,)), ((), ()))

EPS = 1e-6
D_MODEL = 1024
N_META = 16
HEAD_DIM = 64
N_HEADS = 4
QW = N_HEADS * HEAD_DIM
DN_CONV = 4
DN_CHUNK = 64
S5_W = 512
S5_G = 32
S5_P = 64
S5_C = 16
S5_N = S5_G * S5_P
S5_SLABS = 4
N_CHIPS = 4
PROJ_W = 2432
IN_WIDTH = 2312
W_IN_PACKED = 640
VMEM_LIMIT = 56 * 1024 * 1024

ADAM_LR, ADAM_B1, ADAM_B2, ADAM_EPS, ADAM_WD, ADAM_STEP = 0.001, 0.9, 0.999, 1e-08, 0.01, 10
FLAT_W = 1024
BIG_ROWS = 512


def _dot(a, b, dims=NN):
    return lax.dot_general(a.astype(MXU_DTYPE), b.astype(MXU_DTYPE), dims, preferred_element_type=f32)


def _dotx(a, b, dims=NN):
    return lax.dot_general(a, b, dims, precision=HI, preferred_element_type=f32)


def _split(x):
    if MXU_DTYPE == f32:
        return x, None
    hi = x.astype(MXU_DTYPE)
    return hi, (x - hi.astype(f32)).astype(MXU_DTYPE)


def _dot_split(hi, lo, u01):
    if lo is None:
        return _dotx(hi, u01)
    u = u01.astype(MXU_DTYPE)
    return (lax.dot_general(hi, u, NN, preferred_element_type=f32)
            + lax.dot_general(lo, u, NN, preferred_element_type=f32))


def _dot3(a, b, dims=NN):
    ah, al = _split(a)
    if al is None:
        return _dotx(a, b, dims)
    bh, bl = _split(b)
    d = lambda x, y: lax.dot_general(x, y, dims, preferred_element_type=f32)
    return d(ah, bh) + d(ah, bl) + d(al, bh)


BNN = (((2,), (1,)), ((0,), (0,)))
BNT = (((2,), (2,)), ((0,), (0,)))
BTN = (((1,), (1,)), ((0,), (0,)))


def _with_dot_vjp(dot, kind, batched=False):
    nn, nt, tn = (BNN, BNT, BTN) if batched else (NN, NT, TN)
    dims = {"nn": nn, "nt": nt, "tn": tn}[kind]

    @jax.custom_vjp
    def f(a, b):
        return dot(a, b, dims)

    def fwd(a, b):
        return dot(a, b, dims), (a, b)

    def bwd(res, dy):
        a, b = res
        if kind == "nn":
            return dot(dy, b, nt), dot(a, dy, tn)
        if kind == "nt":
            return dot(dy, b, nn), dot(dy, a, tn)
        return dot(b, dy, nt), dot(a, dy, nn)

    f.defvjp(fwd, bwd)
    return f


_mm = _with_dot_vjp(_dot, "nn")
_bmm = _with_dot_vjp(_dot, "nn", True)
_bmm_nt = _with_dot_vjp(_dot, "nt", True)
_bmm_tn = _with_dot_vjp(_dot, "tn", True)
_bmm3 = _with_dot_vjp(_dot3, "nn", True)


def _rms(x, g):
    return x * lax.rsqrt(jnp.mean(x * x, axis=-1, keepdims=True) + EPS) * g


def _sigmoid(x):
    return 1.0 / (1.0 + jnp.exp(-x))


def _silu(x):
    return x * _sigmoid(x)


def _softplus(x):
    return jnp.maximum(x, 0.0) + jnp.log(1.0 + jnp.exp(-jnp.abs(x)))


def _gelu_tanh(x):
    return 0.5 * x * (1.0 + jnp.tanh(math.sqrt(2.0 / math.pi) * (x + 0.044715 * x * x * x)))


def _iota2(shape, axis):
    return lax.broadcasted_iota(jnp.int32, shape, axis)


def _block_diag_ones(n, blk):
    return ((_iota2((n, n), 0) // blk) == (_iota2((n, n), 1) // blk)).astype(f32)


def _pc(body, name, grid, in_specs, out_specs, out_shape, scratch=(), vmem=VMEM_LIMIT):
    return pl.pallas_call(
        body, name=name, grid=grid, in_specs=in_specs, out_specs=out_specs, out_shape=out_shape,
        scratch_shapes=list(scratch),
        compiler_params=pltpu.CompilerParams(dimension_semantics=("arbitrary",) * len(grid), vmem_limit_bytes=vmem))


def _bs(shape, imap):
    return pl.BlockSpec(shape, imap)


def _sds(shape, dtype=f32):
    return jax.ShapeDtypeStruct(tuple(shape), dtype)


def _token_tile(lp, cap=640):
    for t in (640, 320, 256, 128, 64):
        if t <= cap and lp % t == 0:
            return t
    raise ValueError(lp)


def _padded_len(l):
    return -(-l // 1280) * 1280 if l > 4096 else -(-l // 256) * 256


def _ffn_fwd(h, g, wg, wu, wd):
    lp, d = h.shape
    nch, _, fc = wg.shape
    tm = _token_tile(lp)

    def body(h_ref, g_ref, wg_ref, wu_ref, wd_ref, o_ref, xn_ref, gate_ref, up_ref, xn_s, acc_s):
        j = pl.program_id(1)

        @pl.when(j == 0)
        def _():
            xn_s[...] = _rms(h_ref[...], g_ref[...]).astype(xn_s.dtype)
            acc_s[...] = jnp.zeros_like(acc_s)

        xn = xn_s[...]
        gate = _dot(xn, wg_ref[0])
        up = _dot(xn, wu_ref[0])
        gate_ref[0] = gate.astype(gate_ref.dtype)
        up_ref[0] = up.astype(up_ref.dtype)
        acc_s[...] += _dot(_silu(gate) * up, wd_ref[0])

        @pl.when(j == nch - 1)
        def _():
            o_ref[...] = h_ref[...] + 0.5 * acc_s[...]
            xn_ref[...] = xn_s[...]

    tok = _bs((tm, d), lambda i, j: (i, 0))
    chunk = _bs((1, tm, fc), lambda i, j: (j, i, 0))
    return _pc(
        body, "ffn_fwd", (lp // tm, nch),
        [tok, _bs((1, d), lambda i, j: (0, 0)),
         _bs((1, d, fc), lambda i, j: (j, 0, 0)), _bs((1, d, fc), lambda i, j: (j, 0, 0)),
         _bs((1, fc, d), lambda i, j: (j, 0, 0))],
        [tok, tok, chunk, chunk],
        [_sds((lp, d)), _sds((lp, d), MXU_DTYPE), _sds((nch, lp, fc), MXU_DTYPE), _sds((nch, lp, fc), MXU_DTYPE)],
        scratch=[pltpu.VMEM((tm, d), MXU_DTYPE), pltpu.VMEM((tm, d), f32)])(h, g, wg, wu, wd)


def _ffn_bwd_dx(h, g, wg, wu, wd, dy, gate_saved, up_saved):
    lp, d = h.shape
    nch, _, fc = wg.shape
    tm = _token_tile(lp)

    def body(h_ref, g_ref, wg_ref, wu_ref, wd_ref, dy_ref, gate_ref, up_ref, dh_ref, dgn_ref, dg_ref, du_ref, act_ref,
             dxn_s, dout_s):
        i, j = pl.program_id(0), pl.program_id(1)

        @pl.when((i == 0) & (j == 0))
        def _():
            dgn_ref[...] = jnp.zeros_like(dgn_ref)

        @pl.when(j == 0)
        def _():
            dxn_s[...] = jnp.zeros_like(dxn_s)
            dout_s[...] = (0.5 * dy_ref[...]).astype(dout_s.dtype)

        gate = gate_ref[0].astype(f32)
        up = up_ref[0].astype(f32)
        sig = _sigmoid(gate)
        sl = gate * sig
        dact = _dot(dout_s[...], wd_ref[0], NT)
        d_up = dact * sl
        d_gate = dact * up * sig * (1.0 + gate * (1.0 - sig))
        dg_ref[0] = d_gate.astype(dg_ref.dtype)
        du_ref[0] = d_up.astype(du_ref.dtype)
        act_ref[0] = (sl * up).astype(act_ref.dtype)
        dxn_s[...] += _dot(d_gate, wg_ref[0], NT) + _dot(d_up, wu_ref[0], NT)

        @pl.when(j == nch - 1)
        def _():
            _, vjp = jax.vjp(_rms, h_ref[...], g_ref[...])
            dx, dg = vjp(dxn_s[...])
            dh_ref[...] = dy_ref[...] + dx
            dgn_ref[...] += dg

    tok = _bs((tm, d), lambda i, j: (i, 0))
    chunk = _bs((1, tm, fc), lambda i, j: (j, i, 0))
    return _pc(
        body, "ffn_bwd_dx", (lp // tm, nch),
        [tok, _bs((1, d), lambda i, j: (0, 0)),
         _bs((1, d, fc), lambda i, j: (j, 0, 0)), _bs((1, d, fc), lambda i, j: (j, 0, 0)),
         _bs((1, fc, d), lambda i, j: (j, 0, 0)), tok, chunk, chunk],
        [tok, _bs((1, d), lambda i, j: (0, 0)), chunk, chunk, chunk],
        [_sds((lp, d)), _sds((1, d)),
         _sds((nch, lp, fc), MXU_DTYPE), _sds((nch, lp, fc), MXU_DTYPE), _sds((nch, lp, fc), MXU_DTYPE)],
        scratch=[pltpu.VMEM((tm, d), f32), pltpu.VMEM((tm, d), MXU_DTYPE)],
    )(h, g, wg, wu, wd, dy, gate_saved, up_saved)


def _ffn_bwd_dw(xn, dy, d_gate, d_up, act):
    lp, d = xn.shape
    nch, _, fc = d_gate.shape
    tm = _token_tile(lp)

    def body(xn_ref, dy_ref, dg_ref, du_ref, act_ref, dwg_ref, dwu_ref, dwd_ref):
        @pl.when(pl.program_id(1) == 0)
        def _():
            dwg_ref[...] = jnp.zeros_like(dwg_ref)
            dwu_ref[...] = jnp.zeros_like(dwu_ref)
            dwd_ref[...] = jnp.zeros_like(dwd_ref)

        xn_t = xn_ref[...]
        dwg_ref[0] += _dot(xn_t, dg_ref[0], TN)
        dwu_ref[0] += _dot(xn_t, du_ref[0], TN)
        dwd_ref[0] += _dot(act_ref[0], 0.5 * dy_ref[...], TN)

    tok = _bs((tm, d), lambda j, i: (i, 0))
    chunk = _bs((1, tm, fc), lambda j, i: (j, i, 0))
    return _pc(
        body, "ffn_bwd_dw", (nch, lp // tm), [tok, tok, chunk, chunk, chunk],
        [_bs((1, d, fc), lambda j, i: (j, 0, 0)), _bs((1, d, fc), lambda j, i: (j, 0, 0)),
         _bs((1, fc, d), lambda j, i: (j, 0, 0))],
        [_sds((nch, d, fc)), _sds((nch, d, fc)), _sds((nch, fc, d))])(xn, dy, d_gate, d_up, act)


_PROJ_SPLITS = (QW, QW, QW, 3 * QW, QW, S5_W, 128)


def _inproj_fwd(h, g, w):
    lp, d = h.shape
    tm = _token_tile(lp)

    def body(h_ref, g_ref, w_ref, *outs):
        proj = _dot(_rms(h_ref[...], g_ref[...]), w_ref[...])
        off = 0
        for ref, wd in zip(outs, _PROJ_SPLITS):
            ref[...] = proj[:, off:off + wd]
            off += wd

    return _pc(
        body, "inproj_fwd", (lp // tm,),
        [_bs((tm, d), lambda i: (i, 0)), _bs((1, d), lambda i: (0, 0)), _bs((d, PROJ_W), lambda i: (0, 0))],
        [_bs((tm, wd), lambda i: (i, 0)) for wd in _PROJ_SPLITS],
        [_sds((lp, wd)) for wd in _PROJ_SPLITS])(h, g, w)


def _inproj_bwd(h, g, w, dres, dq, dk, dv, dxs, dz, du, dba):
    lp, d = h.shape
    tm = _token_tile(lp, 320)
    n_tiles = lp // tm

    def body(h_ref, g_ref, w_ref, dres_ref, dq_ref, dk_ref, dv_ref, dxs_ref, nxt_ref, dz_ref, du_ref, dba_ref,
             dh_ref, dgn_ref, dw_ref):
        @pl.when(pl.program_id(0) == 0)
        def _():
            dgn_ref[...] = jnp.zeros_like(dgn_ref)
            dw_ref[...] = jnp.zeros_like(dw_ref)

        row, row8 = _iota2((tm, 1), 0), _iota2((8, 1), 0)
        more = (pl.program_id(0) < n_tiles - 1).astype(f32)
        ddn, tail = dxs_ref[0], jnp.zeros((8, 3 * QW), f32)
        for k in range(1, DN_CONV):
            ddn = ddn + jnp.where(row < tm - k, pltpu.roll(dxs_ref[k], tm - k, 0), 0.0)
            tail = tail + jnp.where(row8 >= 8 - k, pltpu.roll(nxt_ref[k], 8 - k, 0), 0.0)
        ddn = jnp.concatenate([ddn[:tm - 8], ddn[tm - 8:] + more * tail], axis=0)
        dproj = jnp.concatenate(
            [dq_ref[...], dk_ref[...], dv_ref[...], ddn, dz_ref[...], du_ref[...], dba_ref[...]], axis=1)
        xn, vjp = jax.vjp(_rms, h_ref[...], g_ref[...])
        dx, dg = vjp(_dot(dproj, w_ref[...], NT))
        dw_ref[...] += _dot(xn, dproj, TN)
        dh_ref[...] = dres_ref[...] + dx
        dgn_ref[...] += dg

    tok = lambda wd: _bs((tm, wd), lambda i: (i, 0))
    return _pc(
        body, "inproj_bwd", (lp // tm,),
        [tok(d), _bs((1, d), lambda i: (0, 0)), _bs((d, PROJ_W), lambda i: (0, 0)), tok(d),
         tok(QW), tok(QW), tok(QW), _bs((4, tm, 3 * QW), lambda i: (0, i, 0)),
         _bs((4, 8, 3 * QW), lambda i: (0, jnp.minimum((i + 1) * (tm // 8), lp // 8 - 1), 0)),
         tok(QW), tok(S5_W), tok(128)],
        [tok(d), _bs((1, d), lambda i: (0, 0)), _bs((d, PROJ_W), lambda i: (0, 0))],
        [_sds((lp, d)), _sds((1, d)), _sds((d, PROJ_W))])(h, g, w, dres, dq, dk, dv, dxs, dxs, dz, du, dba)


_SB_TQ = 256
_SB_ROWS = 32
_SB_GROUP = 4
_SB_ROWS_BWD = 32
_SB_GROUP_BWD = 4


def _sb_pieces(z, valid):
    t = jnp.exp(-jnp.abs(z))
    sp = jnp.maximum(z, 0.0) + jnp.log(1.0 + t)
    lk = -sp if valid is None else jnp.where(valid, -sp, 0.0)
    return t, sp, lk


def _cat_rows(parts):
    return parts[0] if len(parts) == 1 else jnp.concatenate(parts, axis=0)


def _sb_fwd(q, kt, vt):
    nh, lp, hd = q.shape
    tq = tk = min(_SB_TQ, lp)
    blocks = [slice(r, r + _SB_ROWS) for r in range(0, tq, _SB_ROWS)]

    def body(q_ref, k_ref, v_ref, o_ref, w_hbm, s_hbm, wbuf, sbuf, sems):
        head, qi = pl.program_id(0), pl.program_id(1)
        qv = q_ref[0]
        u_strict = (_iota2((tk, tk), 0) > _iota2((tk, tk), 1)).astype(f32)
        below = _iota2((tq, tk), 1) < _iota2((tq, tk), 0)

        spare = lambda slot, t: lp // tk + slot * _SB_GROUP + t

        def save(slot, t, j):
            return [pltpu.make_async_copy(wbuf.at[slot, t], w_hbm.at[head, qi, j], sems.at[0, slot, t]),
                    pltpu.make_async_copy(sbuf.at[slot, t], s_hbm.at[head, qi, j], sems.at[1, slot, t])]

        def drain(slot):
            for t in range(_SB_GROUP):
                for cp in save(slot, t, spare(slot, t)):
                    cp.wait()

        def idle(slot, t):
            wbuf[slot, t] = jnp.zeros((tq, tk), MXU_DTYPE)
            sbuf[slot, t] = jnp.zeros((tq, tk), MXU_DTYPE)
            for cp in save(slot, t, spare(slot, t)):
                cp.start()

        def tiles(js, carry, slot, masked=False, live=None, first=False):
            if not first:
                drain(slot)
            o_acc, c_after = carry
            kss = [pl.ds(pl.multiple_of(j * tk, tk), tk) for j in js]
            z_alls = [_dot(qv, k_ref[0, :, ks]) * (HEAD_DIM ** -0.5) for ks in kss]
            stage, afters = [], []
            for t, z_all in enumerate(z_alls):
                his, los, logs, sums = [], [], [], []
                for rs in blocks:
                    z = z_all[rs]
                    _, sp, lk = _sb_pieces(z, below[rs] if masked else None)
                    if live is not None:
                        lk = lk * live[t]
                    hi, lo = _split(lk)
                    his.append(hi)
                    los.append(lo)
                    logs.append(z - sp)
                    sums.append(jnp.sum(lk, axis=1, keepdims=True))
                stage.append((logs, _cat_rows(sums)))
                afters.append(_dot_split(_cat_rows(his), None if los[0] is None else _cat_rows(los), u_strict))
            for t, ((logs, sums), after_all) in enumerate(zip(stage, afters)):
                ws, sigs = [], []
                for n, rs in enumerate(blocks):
                    w = jnp.exp(logs[n] + after_all[rs] + c_after[rs])
                    sig = jnp.exp(logs[n])
                    if masked:
                        w, sig = jnp.where(below[rs], w, 0.0), jnp.where(below[rs], sig, 0.0)
                    if live is not None:
                        w = w * live[t]
                    ws.append(w.astype(MXU_DTYPE))
                    sigs.append(sig.astype(MXU_DTYPE))
                w_all = _cat_rows(ws)
                wbuf[slot, t] = w_all
                sbuf[slot, t] = _cat_rows(sigs)
                for cp in save(slot, t, js[t] if live is None else jnp.where(live[t] > 0.0, js[t], spare(slot, t))):
                    cp.start()
                o_acc = o_acc + _dot(w_all, v_ref[0, :, kss[t]], NT)
                c_after = c_after + sums
            for t in range(len(js), _SB_GROUP):
                idle(slot, t)
            return o_acc, c_after

        n_groups, rest = qi // _SB_GROUP, qi % _SB_GROUP
        group = lambda g, c: tiles([qi - 1 - _SB_GROUP * g - n for n in range(_SB_GROUP)], c, (g + 1) % 2)

        def last_group(_, c):
            idx = [rest - 1 - n for n in range(_SB_GROUP)]
            return tiles([jnp.maximum(j, 0) for j in idx], c, (n_groups + 1) % 2,
                         live=[(j >= 0).astype(f32) for j in idx])

        for t in range(_SB_GROUP):
            idle(1, t)
        carry = tiles([qi], (jnp.zeros((tq, hd), f32), jnp.zeros((tq, 1), f32)), 0, masked=True, first=True)
        carry = lax.fori_loop(0, n_groups, group, carry)
        o_acc, _ = lax.fori_loop(0, jnp.minimum(rest, 1), last_group, carry)
        drain(0)
        drain(1)
        o_ref[0] = o_acc

    full_t = _bs((1, hd, lp), lambda h, i: (h, 0, 0))
    hbm = pl.BlockSpec(memory_space=pltpu.HBM)
    return _pc(
        body, "sb_fwd", (nh, lp // tq),
        [_bs((1, tq, hd), lambda h, i: (h, i, 0)), full_t, full_t],
        [_bs((1, tq, hd), lambda h, i: (h, i, 0)), hbm, hbm],
        [_sds((nh, lp, hd))] + [_sds((nh, lp // tq, lp // tk + 2 * _SB_GROUP, tq, tk), MXU_DTYPE)] * 2,
        scratch=[pltpu.VMEM((2, _SB_GROUP, tq, tk), MXU_DTYPE), pltpu.VMEM((2, _SB_GROUP, tq, tk), MXU_DTYPE),
                 pltpu.SemaphoreType.DMA((2, 2, _SB_GROUP))])(q, kt, vt)


def _sb_bwd(q, kt, vt, w_saved, s_saved, do):
    nh, lp, hd = q.shape
    tq = tk = min(_SB_TQ, lp)
    blocks = [slice(r, r + _SB_ROWS_BWD) for r in range(0, tq, _SB_ROWS_BWD)]
    grp = _SB_GROUP_BWD

    def body(q_ref, k_ref, v_ref, w_hbm, s_hbm, do_ref, dq_ref, dk_ref, dv_ref, wbuf, sbuf, sems):
        head, qi = pl.program_id(0), pl.program_id(1)

        @pl.when(qi == 0)
        def _():
            dk_ref[...] = jnp.zeros_like(dk_ref)
            dv_ref[...] = jnp.zeros_like(dv_ref)

        qv, dov = q_ref[0], do_ref[0]
        u_excl = (_iota2((tk, tk), 0) < _iota2((tk, tk), 1)).astype(f32)
        scale = HEAD_DIM ** -0.5

        def loads(js, slot):
            out = []
            for t, j in enumerate(js):
                out += [pltpu.make_async_copy(w_hbm.at[head, qi, j], wbuf.at[slot, t], sems.at[0, slot, t]),
                        pltpu.make_async_copy(s_hbm.at[head, qi, j], sbuf.at[slot, t], sems.at[1, slot, t])]
            return out

        def tiles(js, slot, carry, live=None):
            dq_acc, c_e = carry
            kss = [pl.ds(pl.multiple_of(j * tk, tk), tk) for j in js]
            dw_alls = [_dot(dov, v_ref[0, :, ks]) for ks in kss]
            stage, befores = [], []
            for t, dw_all in enumerate(dw_alls):
                es, ebs, esums = [], [], []
                for rs in blocks:
                    e = wbuf[slot, t, rs].astype(f32) * dw_all[rs]
                    if live is not None:
                        e = e * live[t]
                    es.append(e)
                    ebs.append(e.astype(MXU_DTYPE))
                    esums.append(jnp.sum(e, axis=1, keepdims=True))
                stage.append((es, _cat_rows(esums)))
                befores.append(_dot(_cat_rows(ebs), u_excl))
            for t, ((es, esums), before_all, ks) in enumerate(zip(stage, befores, kss)):
                dzs = []
                for n, rs in enumerate(blocks):
                    sig = sbuf[slot, t, rs].astype(f32)
                    if live is not None:
                        sig = sig * live[t]
                    dz = es[n] * (1.0 - sig) - sig * (c_e[rs] + before_all[rs])
                    dzs.append((dz * scale).astype(MXU_DTYPE))
                dz_all = _cat_rows(dzs)
                w_all = wbuf[slot, t] if live is None else wbuf[slot, t] * live[t].astype(MXU_DTYPE)
                c_e = c_e + esums
                dk_ref[0, :, ks] += _dot(qv, dz_all, TN)
                dv_ref[0, :, ks] += _dot(dov, w_all, TN)
                dq_acc = dq_acc + _dot(dz_all, k_ref[0, :, ks], NT)
            return dq_acc, c_e

        n_tiles = qi + 1
        n_groups, rest = n_tiles // grp, n_tiles % grp
        n_passes = n_groups + jnp.minimum(rest, 1)
        group_js = lambda g: [jnp.minimum(grp * g + t, qi) for t in range(grp)]

        for cp in loads(group_js(0), 0):
            cp.start()

        def fetch_next_and_wait(g):
            slot = g % 2

            @pl.when(g + 1 < n_passes)
            def _():
                for cp in loads(group_js(g + 1), 1 - slot):
                    cp.start()

            for cp in loads(group_js(g), slot):
                cp.wait()
            return slot

        def group(g, carry):
            slot = fetch_next_and_wait(g)
            return tiles(group_js(g), slot, carry)

        def last_group(_, carry):
            slot = fetch_next_and_wait(n_groups)
            live = [(grp * n_groups + t <= qi).astype(f32) for t in range(grp)]
            return tiles(group_js(n_groups), slot, carry, live)

        carry = lax.fori_loop(0, n_groups, group, (jnp.zeros((tq, hd), f32), jnp.zeros((tq, 1), f32)))
        dq_acc, _ = lax.fori_loop(0, jnp.minimum(rest, 1), last_group, carry)
        dq_ref[0] = dq_acc

    tile_spec = _bs((1, tq, hd), lambda h, i: (h, i, 0))
    full_t = _bs((1, hd, lp), lambda h, i: (h, 0, 0))
    hbm = pl.BlockSpec(memory_space=pltpu.HBM)
    return _pc(
        body, "sb_bwd", (nh, lp // tq),
        [tile_spec, full_t, full_t, hbm, hbm, tile_spec],
        [tile_spec, full_t, full_t], [_sds((nh, lp, hd)), _sds((nh, hd, lp)), _sds((nh, hd, lp))],
        scratch=[pltpu.VMEM((2, grp, tq, tk), MXU_DTYPE), pltpu.VMEM((2, grp, tq, tk), MXU_DTYPE),
                 pltpu.SemaphoreType.DMA((2, 2, grp))])(q, kt, vt, w_saved, s_saved, do)


def _dn_pre_tile(xs, ba, cw, pv):
    conv = xs[0] * cw[3:4] + xs[1] * cw[2:3] + xs[2] * cw[1:2] + xs[3] * cw[0:1]
    s = _silu(conv)
    bd = _block_diag_ones(QW, HEAD_DIM)
    sq, sk, sv = s[:, :QW], s[:, QW:2 * QW], s[:, 2 * QW:]
    qn = sq * lax.rsqrt(_dotx(sq * sq, bd) + EPS)
    kn = sk * lax.rsqrt(_dotx(sk * sk, bd) + EPS)
    lane = _iota2(ba.shape, 1)
    beta = _sigmoid(ba)
    g = -jnp.exp(pv[0:1]) * _softplus(ba + pv[1:2])
    bg = jnp.where(lane < N_HEADS, beta, jnp.where(lane < 2 * N_HEADS, g, 0.0))
    return qn, kn, sv, bg


def _dn_shifted(cur, prev, first):
    row = _iota2((cur.shape[0], 1), 0)
    out = [cur]
    for k in range(1, DN_CONV):
        head_rows = jnp.where(first, 0.0, pltpu.roll(prev, k, 0))
        out.append(jnp.where(row >= k, pltpu.roll(cur, k, 0), head_rows))
    return tuple(out)


def _dn_pre_fwd(x, ba, cw, pv):
    lp, w3 = x.shape
    tm = _token_tile(lp, 320)

    def body(x_ref, xp_ref, ba_ref, cw_ref, pv_ref, q_ref, k_ref, v_ref, bg_ref):
        xs = _dn_shifted(x_ref[...], xp_ref[...], pl.program_id(0) == 0)
        qn, kn, sv, bg = _dn_pre_tile(xs, ba_ref[...], cw_ref[...], pv_ref[...])
        q_ref[...], k_ref[...], v_ref[...], bg_ref[...] = qn, kn, sv, bg

    tok = lambda wd: _bs((tm, wd), lambda i: (i, 0))
    return _pc(
        body, "dn_pre_fwd", (lp // tm,),
        [tok(w3), _bs((tm, w3), lambda i: (jnp.maximum(i - 1, 0), 0)), tok(128),
         _bs((8, w3), lambda i: (0, 0)), _bs((8, 128), lambda i: (0, 0))],
        [tok(QW), tok(QW), tok(QW), tok(128)],
        [_sds((lp, QW)), _sds((lp, QW)), _sds((lp, QW)), _sds((lp, 128))])(x, x, ba, cw, pv)


def _dn_pre_bwd(x, ba, cw, pv, dq, dk, dv, dbg):
    lp, w3 = x.shape
    tm = _token_tile(lp, 320)

    def body(x_ref, xp_ref, ba_ref, cw_ref, pv_ref, dq_ref, dk_ref, dv_ref, dbg_ref, dxs_ref, dba_ref, dcw_ref, dpv_ref):
        @pl.when(pl.program_id(0) == 0)
        def _():
            dcw_ref[...] = jnp.zeros_like(dcw_ref)
            dpv_ref[...] = jnp.zeros_like(dpv_ref)

        xs = _dn_shifted(x_ref[...], xp_ref[...], pl.program_id(0) == 0)
        _, vjp = jax.vjp(_dn_pre_tile, xs, ba_ref[...], cw_ref[...], pv_ref[...])
        dxs, dba, dcw, dpv = vjp((dq_ref[...], dk_ref[...], dv_ref[...], dbg_ref[...]))
        for k in range(DN_CONV):
            dxs_ref[k] = dxs[k]
        dba_ref[...] = dba
        dcw_ref[...] += dcw
        dpv_ref[...] += dpv

    tok = lambda wd: _bs((tm, wd), lambda i: (i, 0))
    cw_spec, pv_spec = _bs((8, w3), lambda i: (0, 0)), _bs((8, 128), lambda i: (0, 0))
    return _pc(
        body, "dn_pre_bwd", (lp // tm,),
        [tok(w3), _bs((tm, w3), lambda i: (jnp.maximum(i - 1, 0), 0)), tok(128), cw_spec, pv_spec,
         tok(QW), tok(QW), tok(QW), tok(128)],
        [_bs((4, tm, w3), lambda i: (0, i, 0)), tok(128), cw_spec, pv_spec],
        [_sds((4, lp, w3)), _sds((lp, 128)), _sds((8, w3)), _sds((8, 128))])(x, x, ba, cw, pv, dq, dk, dv, dbg)


def _dn_chunk(state, q, k, v, grow, brow):
    nh, c, _ = q.shape
    ii, jj = _iota2((c, c), 0), _iota2((c, c), 1)
    eye = ii == jj
    col = lambda row: jnp.sum(jnp.where(eye, jnp.broadcast_to(row, (nh, c, c)), 0.0), axis=2, keepdims=True)
    gc_row = _dotx(grow, jnp.broadcast_to((ii <= jj).astype(f32), (nh, c, c)), BNN)
    gc_col, b_col = col(gc_row), col(brow)
    decay = jnp.exp(jnp.where(ii >= jj, gc_col - gc_row, -1e30))
    kb = k * b_col
    p = -jnp.where(ii > jj, _bmm_nt(kb, k) * decay, 0.0)
    t_inv = eye.astype(f32) + p
    for _ in range(5):
        p = _bmm3(p, p)
        t_inv = t_inv + _bmm3(t_inv, p)
    egc = jnp.exp(gc_col)
    u = _bmm(t_inv, v * b_col)
    w = _bmm(t_inv, kb * egc)
    qs = q * (q.shape[2] ** -0.5)
    attn = jnp.where(ii >= jj, _bmm_nt(qs, k) * decay, 0.0)
    v_new = u - _bmm(w, state)
    o = _bmm(qs * egc, state) + _bmm(attn, v_new)
    g_last = gc_row[:, :, c - 1:c]
    new_state = state * jnp.exp(g_last) + _bmm_tn(k * jnp.exp(g_last - gc_col), v_new)
    return new_state, o


def _dn_scan_fwd(q, k, v, grow, brow):
    nh, lp, hd = q.shape
    c = DN_CHUNK
    n = lp // c

    def body(q_ref, k_ref, v_ref, g_ref, b_ref, o_ref, st_ref, state_s):
        @pl.when(pl.program_id(0) == 0)
        def _():
            state_s[...] = jnp.zeros_like(state_s)

        st_ref[:, 0] = state_s[...]
        state, o = _dn_chunk(state_s[...], q_ref[...], k_ref[...], v_ref[...], g_ref[:, 0], b_ref[:, 0])
        state_s[...] = state
        o_ref[...] = o

    seq = _bs((nh, c, hd), lambda i: (0, i, 0))
    row = _bs((nh, 1, 1, c), lambda i: (0, i, 0, 0))
    return _pc(body, "dn_scan_fwd", (n,), [seq, seq, seq, row, row],
               [seq, _bs((nh, 1, hd, hd), lambda i: (0, i, 0, 0))],
               [_sds((nh, lp, hd)), _sds((nh, n, hd, hd))],
               scratch=[pltpu.VMEM((nh, hd, hd), f32)])(q, k, v, grow, brow)


def _dn_scan_bwd(q, k, v, grow, brow, states, do):
    nh, lp, hd = q.shape
    c = DN_CHUNK
    n = lp // c

    def body(q_ref, k_ref, v_ref, g_ref, b_ref, st_ref, do_ref, dq_ref, dk_ref, dv_ref, dg_ref, db_ref, dstate_s):
        @pl.when(pl.program_id(0) == 0)
        def _():
            dstate_s[...] = jnp.zeros_like(dstate_s)

        _, vjp = jax.vjp(_dn_chunk, st_ref[:, 0], q_ref[...], k_ref[...], v_ref[...], g_ref[:, 0], b_ref[:, 0])
        dstate, dq, dk, dv, dg, db = vjp((dstate_s[...], do_ref[...]))
        dstate_s[...] = dstate
        dq_ref[...], dk_ref[...], dv_ref[...] = dq, dk, dv
        dg_ref[:, 0], db_ref[:, 0] = dg, db

    seq = _bs((nh, c, hd), lambda i: (0, n - 1 - i, 0))
    row = _bs((nh, 1, 1, c), lambda i: (0, n - 1 - i, 0, 0))
    return _pc(body, "dn_scan_bwd", (n,),
               [seq, seq, seq, row, row, _bs((nh, 1, hd, hd), lambda i: (0, n - 1 - i, 0, 0)), seq],
               [seq, seq, seq, row, row],
               [_sds((nh, lp, hd))] * 3 + [_sds((nh, n, 1, c))] * 2,
               scratch=[pltpu.VMEM((nh, hd, hd), f32)])(q, k, v, grow, brow, states, do)


def _s5_prep(a_re, a_im, log_dt, b_re, b_im):
    dt = jnp.exp(log_dt)
    mag = jnp.exp(a_re * dt)
    ar, ai = mag * jnp.cos(a_im * dt), mag * jnp.sin(a_im * dt)
    den = a_re * a_re + a_im * a_im
    cr = ((ar - 1.0) * a_re + ai * a_im) / den
    ci = (ai * a_re - (ar - 1.0) * a_im) / den
    cr3, ci3 = cr[:, None, :], ci[:, None, :]
    return ar, ai, cr3 * b_re - ci3 * b_im, cr3 * b_im + ci3 * b_re


def _s5_prep_fwd(a_re, a_im, log_dt, b_re, b_im):
    def body(ar_ref, ai_ref, dt_ref, br_ref, bi_ref, *outs):
        for ref, val in zip(outs, _s5_prep(ar_ref[...], ai_ref[...], dt_ref[...], br_ref[...], bi_ref[...])):
            ref[...] = val

    return pl.pallas_call(body, name="s5_prep_fwd",
                          out_shape=[_sds(a_re.shape), _sds(a_re.shape), _sds(b_re.shape), _sds(b_re.shape)],
                          )(a_re, a_im, log_dt, b_re, b_im)


def _s5_prep_bwd(a_re, a_im, log_dt, b_re, b_im, d_ar, d_ai, d_br, d_bi):
    def body(ar_ref, ai_ref, dt_ref, br_ref, bi_ref, g0, g1, g2, g3, *outs):
        _, vjp = jax.vjp(_s5_prep, ar_ref[...], ai_ref[...], dt_ref[...], br_ref[...], bi_ref[...])
        for ref, val in zip(outs, vjp((g0[...], g1[...], g2[...], g3[...]))):
            ref[...] = val

    return pl.pallas_call(body, name="s5_prep_bwd",
                          out_shape=[_sds(a_re.shape), _sds(a_re.shape), _sds(log_dt.shape), _sds(b_re.shape),
                                     _sds(b_re.shape)])(a_re, a_im, log_dt, b_re, b_im, d_ar, d_ai, d_br, d_bi)


def _s5_block_len(lp):
    return 128 if lp % 128 == 0 else 64


def _s5_powers(ar, ai, tb):
    out = []
    k = 1
    while k < tb:
        out.append((ar, ai))
        ar, ai = ar * ar - ai * ai, 2.0 * ar * ai
        k *= 2
    return out


def _s5_scan_rows(xr, xi, pows, reverse, period=None):
    tb = xr.shape[0]
    span = tb if period is None else period
    row = _iota2((tb, 1), 0)
    if period is not None:
        row = jnp.bitwise_and(row, period - 1)
    k = 1
    for pr, pi in pows:
        if reverse:
            keep = row < span - k
            sr, si = pltpu.roll(xr, tb - k, 0), pltpu.roll(xi, tb - k, 0)
        else:
            keep = row >= k
            sr, si = pltpu.roll(xr, k, 0), pltpu.roll(xi, k, 0)
        sr, si = jnp.where(keep, sr, 0.0), jnp.where(keep, si, 0.0)
        xr, xi = xr + pr * sr - pi * si, xi + pr * si + pi * sr
        k *= 2
    return xr, xi


def _s5_slab_mm(x, w_ref, dims=NN):
    a = x.shape[1] // S5_SLABS
    return jnp.concatenate([_dot(x[:, j * a:(j + 1) * a], w_ref[j], dims) for j in range(S5_SLABS)], axis=1)


def _s5_power_table(ar, ai, tb, reverse):
    at = _iota2((tb, 1), 0) == (tb - 1 if reverse else 0)
    return _s5_scan_rows(jnp.where(at, ar, 0.0), jnp.where(at, ai, 0.0), _s5_powers(ar, ai, tb), reverse)


_S5_GROUP = 8


def _s5_scan_block(xr, xi, ar, ai, carry_r, carry_i, reverse):
    tb = xr.shape[0]
    lr, li = _s5_scan_rows(xr, xi, _s5_powers(ar, ai, _S5_GROUP), reverse, period=_S5_GROUP)
    wr, wi = _s5_power_table(ar, ai, _S5_GROUP, reverse)
    n = tb // _S5_GROUP
    out_r, out_i = [None] * n, [None] * n
    for g in (range(n - 1, -1, -1) if reverse else range(n)):
        rs = slice(_S5_GROUP * g, _S5_GROUP * (g + 1))
        sr = lr[rs] + wr * carry_r - wi * carry_i
        si = li[rs] + wr * carry_i + wi * carry_r
        out_r[g], out_i[g] = sr, si
        edge = slice(0, 1) if reverse else slice(_S5_GROUP - 1, _S5_GROUP)
        carry_r, carry_i = sr[edge], si[edge]
    return jnp.concatenate(out_r, axis=0), jnp.concatenate(out_i, axis=0), carry_r, carry_i


def _s5_scan_fwd(u, ar, ai, b8r, b8i, c8r, c8i, dvec):
    lp = u.shape[0]
    tb = _s5_block_len(lp)
    nblk = lp // tb

    def body(u_ref, ar_ref, ai_ref, b8r_ref, b8i_ref, c8r_ref, c8i_ref, d_ref, y_ref, cin_ref, carry_s):
        @pl.when(pl.program_id(0) == 0)
        def _():
            carry_s[...] = jnp.zeros_like(carry_s)

        cin_ref[0] = carry_s[...]
        uv = u_ref[...]
        sr, si, out_r, out_i = _s5_scan_block(_s5_slab_mm(uv, b8r_ref), _s5_slab_mm(uv, b8i_ref), ar_ref[...],
                                              ai_ref[...], carry_s[0:1], carry_s[1:2], False)
        carry_s[0:1] = out_r
        carry_s[1:2] = out_i
        y_ref[...] = _s5_slab_mm(sr, c8r_ref) - _s5_slab_mm(si, c8i_ref) + d_ref[...] * uv

    const = lambda shape: _bs(shape, lambda i: (0,) * len(shape))
    return _pc(
        body, "s5_scan_fwd", (nblk,),
        [_bs((tb, S5_W), lambda i: (i, 0)), const((1, S5_N)), const((1, S5_N)),
         const((S5_SLABS, 128, 512)), const((S5_SLABS, 128, 512)),
         const((S5_SLABS, 512, 128)), const((S5_SLABS, 512, 128)), const((1, S5_W))],
        [_bs((tb, S5_W), lambda i: (i, 0)), _bs((1, 8, S5_N), lambda i: (i, 0, 0))],
        [_sds((lp, S5_W)), _sds((nblk, 8, S5_N))],
        scratch=[pltpu.VMEM((8, S5_N), f32)])(u, ar, ai, b8r, b8i, c8r, c8i, dvec)


def _s5_scan_bwd(u, dy, cin, ar, ai, b8r, b8i, c8r, c8i, dvec):
    lp = u.shape[0]
    tb = _s5_block_len(lp)
    nblk = lp // tb

    def body(u_ref, dy_ref, cin_ref, ar_ref, ai_ref, b8r_ref, b8i_ref, c8r_ref, c8i_ref, d_ref,
             du_ref, dab_ref, db8r_ref, db8i_ref, dc8r_ref, dc8i_ref, dd_ref, lam_s):
        @pl.when(pl.program_id(0) == 0)
        def _():
            lam_s[...] = jnp.zeros_like(lam_s)
            for ref in (dab_ref, db8r_ref, db8i_ref, dc8r_ref, dc8i_ref, dd_ref):
                ref[...] = jnp.zeros_like(ref)

        uv, dyv = u_ref[...], dy_ref[...]
        a_r, a_i = ar_ref[...], ai_ref[...]
        cin_r, cin_i = cin_ref[0, 0:1], cin_ref[0, 1:2]
        sr, si, _, _ = _s5_scan_block(_s5_slab_mm(uv, b8r_ref), _s5_slab_mm(uv, b8i_ref), a_r, a_i, cin_r, cin_i, False)
        lr, li, top_r, top_i = _s5_scan_block(_s5_slab_mm(dyv, c8r_ref, NT), -_s5_slab_mm(dyv, c8i_ref, NT),
                                              a_r, -a_i, lam_s[0:1], lam_s[1:2], True)
        lam_s[0:1] = top_r
        lam_s[1:2] = top_i
        first = _iota2((tb, 1), 0) == 0
        pr = jnp.where(first, cin_r, pltpu.roll(sr, 1, 0))
        pi = jnp.where(first, cin_i, pltpu.roll(si, 1, 0))
        dab_ref[0:1] += jnp.sum(lr * pr + li * pi, axis=0, keepdims=True)
        dab_ref[1:2] += jnp.sum(li * pr - lr * pi, axis=0, keepdims=True)
        du_ref[...] = _s5_slab_mm(lr, b8r_ref, NT) + _s5_slab_mm(li, b8i_ref, NT) + d_ref[...] * dyv
        dd_ref[...] += jnp.sum(dyv * uv, axis=0, keepdims=True)
        for j in range(S5_SLABS):
            us, dys = uv[:, j * 128:(j + 1) * 128], dyv[:, j * 128:(j + 1) * 128]
            st = slice(j * 512, (j + 1) * 512)
            db8r_ref[j] += _dot(us, lr[:, st], TN)
            db8i_ref[j] += _dot(us, li[:, st], TN)
            dc8r_ref[j] += _dot(sr[:, st], dys, TN)
            dc8i_ref[j] -= _dot(si[:, st], dys, TN)

    const = lambda shape: _bs(shape, lambda i: (0,) * len(shape))
    rev = _bs((tb, S5_W), lambda i: (nblk - 1 - i, 0))
    return _pc(
        body, "s5_scan_bwd", (nblk,),
        [rev, rev, _bs((1, 8, S5_N), lambda i: (nblk - 1 - i, 0, 0)), const((1, S5_N)), const((1, S5_N)),
         const((S5_SLABS, 128, 512)), const((S5_SLABS, 128, 512)),
         const((S5_SLABS, 512, 128)), const((S5_SLABS, 512, 128)), const((1, S5_W))],
        [rev, const((8, S5_N)), const((S5_SLABS, 128, 512)), const((S5_SLABS, 128, 512)),
         const((S5_SLABS, 512, 128)), const((S5_SLABS, 512, 128)), const((1, S5_W))],
        [_sds((lp, S5_W)), _sds((8, S5_N)), _sds((S5_SLABS, 128, 512)), _sds((S5_SLABS, 128, 512)),
         _sds((S5_SLABS, 512, 128)), _sds((S5_SLABS, 512, 128)), _sds((1, S5_W))],
        scratch=[pltpu.VMEM((8, S5_N), f32)])(u, dy, cin, ar, ai, b8r, b8i, c8r, c8i, dvec)


def _mix_tile(osb, odn, z, ys5, g_sb, g_dn, w_glu, b_glu, g_s5):
    bd = _block_diag_ones(QW, HEAD_DIM)
    tile4 = ((_iota2((HEAD_DIM, QW), 1) % HEAD_DIM) == _iota2((HEAD_DIM, QW), 0)).astype(f32)
    seg_rms = lambda x: x * lax.rsqrt(_dotx(x * x, bd) * (1.0 / HEAD_DIM) + EPS)
    sbn = seg_rms(osb) * _dotx(g_sb, tile4)
    dnn = seg_rms(odn) * _dotx(g_dn, tile4) * _silu(z)
    y = _gelu_tanh(ys5)
    glu = y * _sigmoid(_mm(y, w_glu) + b_glu)
    return jnp.concatenate([sbn, dnn, _rms(glu, g_s5)], axis=1)


def _mixout_fwd(h, osb, odn, z, ys5, g_sb, g_dn, w_glu, b_glu, g_s5, w_out):
    lp, d = h.shape
    tm = _token_tile(lp)

    def body(h_ref, osb_ref, odn_ref, z_ref, ys_ref, gsb_ref, gdn_ref, wg_ref, bg_ref, gs5_ref, wo_ref, o_ref):
        mixed = _mix_tile(osb_ref[...], odn_ref[...], z_ref[...], ys_ref[...], gsb_ref[...], gdn_ref[...],
                          wg_ref[...], bg_ref[...], gs5_ref[...])
        o_ref[...] = h_ref[...] + _dot(mixed, wo_ref[...])

    tok = lambda wd: _bs((tm, wd), lambda i: (i, 0))
    const = lambda shape: _bs(shape, lambda i: (0,) * len(shape))
    return _pc(
        body, "mixout_fwd", (lp // tm,),
        [tok(d), tok(QW), tok(QW), tok(QW), tok(S5_W), const((1, HEAD_DIM)), const((1, HEAD_DIM)),
         const((S5_W, S5_W)), const((1, S5_W)), const((1, S5_W)), const((d, d))],
        tok(d), _sds((lp, d)))(h, osb, odn, z, ys5, g_sb, g_dn, w_glu, b_glu, g_s5, w_out)


def _mixout_bwd(dh, osb, odn, z, ys5, g_sb, g_dn, w_glu, b_glu, g_s5, w_out):
    lp, d = dh.shape
    tm = _token_tile(lp)

    def body(dh_ref, osb_ref, odn_ref, z_ref, ys_ref, gsb_ref, gdn_ref, wg_ref, bg_ref, gs5_ref, wo_ref,
             dosb_ref, dodn_ref, dz_ref, dys_ref, dgsb_ref, dgdn_ref, dwg_ref, dbg_ref, dgs5_ref, dwo_ref):
        accs = (dgsb_ref, dgdn_ref, dwg_ref, dbg_ref, dgs5_ref)

        @pl.when(pl.program_id(0) == 0)
        def _():
            for ref in accs + (dwo_ref,):
                ref[...] = jnp.zeros_like(ref)

        mixed, vjp = jax.vjp(_mix_tile, osb_ref[...], odn_ref[...], z_ref[...], ys_ref[...], gsb_ref[...],
                             gdn_ref[...], wg_ref[...], bg_ref[...], gs5_ref[...])
        dhv = dh_ref[...]
        dwo_ref[...] += _dot(mixed, dhv, TN)
        grads = vjp(_dot(dhv, wo_ref[...], NT))
        for ref, val in zip((dosb_ref, dodn_ref, dz_ref, dys_ref), grads[:4]):
            ref[...] = val
        for ref, val in zip(accs, grads[4:]):
            ref[...] += val

    tok = lambda wd: _bs((tm, wd), lambda i: (i, 0))
    const = lambda shape: _bs(shape, lambda i: (0,) * len(shape))
    params = [const((1, HEAD_DIM)), const((1, HEAD_DIM)), const((S5_W, S5_W)), const((1, S5_W)), const((1, S5_W))]
    return _pc(
        body, "mixout_bwd", (lp // tm,),
        [tok(d), tok(QW), tok(QW), tok(QW), tok(S5_W)] + params + [const((d, d))],
        [tok(QW), tok(QW), tok(QW), tok(S5_W)] + params + [const((d, d))],
        [_sds((lp, QW))] * 3 + [_sds((lp, S5_W)), _sds((1, HEAD_DIM)), _sds((1, HEAD_DIM)), _sds((S5_W, S5_W)),
                                _sds((1, S5_W)), _sds((1, S5_W)), _sds((d, d))],
    )(dh, osb, odn, z, ys5, g_sb, g_dn, w_glu, b_glu, g_s5, w_out)


def _loss_fwd_bwd(h, g, target, n_real):
    lp, d = h.shape
    tm = _token_tile(lp)

    def body(h_ref, g_ref, t_ref, loss_ref, dh_ref, dg_ref):
        i = pl.program_id(0)

        @pl.when(i == 0)
        def _():
            loss_ref[...] = jnp.zeros_like(loss_ref)
            dg_ref[...] = jnp.zeros_like(dg_ref)

        pos = i * tm + _iota2((tm, 1), 0)
        real = ((pos >= N_META) & (pos < n_real)).astype(f32)
        y, vjp = jax.vjp(_rms, h_ref[...], g_ref[...])
        err = (y - t_ref[...]) * real
        loss_ref[...] += 0.5 * jnp.sum(jnp.mean(err * err, axis=1, keepdims=True))
        dx, dg = vjp(err * (1.0 / d))
        dh_ref[...] = dx
        dg_ref[...] += dg

    tok = _bs((tm, d), lambda i: (i, 0))
    return _pc(body, "loss_fwd_bwd", (lp // tm,), [tok, _bs((1, d), lambda i: (0, 0)), tok],
               [_bs((8, 128), lambda i: (0, 0)), tok, _bs((1, d), lambda i: (0, 0))],
               [_sds((8, 128)), _sds((lp, d)), _sds((1, d))])(h, g, target)


def _row_tile(rows):
    for t in (256, 128, 64, 32, 16, 8):
        if rows % t == 0:
            return t
    raise ValueError(rows)


def _adamw(w, g, m, v):
    shape = w.shape
    w, g, m, v = (a.reshape(-1, shape[-1]) for a in (w, g, m, v))
    rows, width = w.shape
    tr = _row_tile(rows)

    def body(w_ref, g_ref, m_ref, v_ref, d_ref, mo_ref, vo_ref):
        gv = g_ref[...]
        m_new = ADAM_B1 * m_ref[...] + (1.0 - ADAM_B1) * gv
        v_new = ADAM_B2 * v_ref[...] + (1.0 - ADAM_B2) * (gv * gv)
        m_hat = m_new / (1.0 - ADAM_B1 ** ADAM_STEP)
        v_hat = v_new / (1.0 - ADAM_B2 ** ADAM_STEP)
        d_ref[...] = -ADAM_LR * (m_hat / (jnp.sqrt(v_hat) + ADAM_EPS) + ADAM_WD * w_ref[...])
        mo_ref[...] = m_new
        vo_ref[...] = v_new

    blk = _bs((tr, width), lambda i: (i, 0))
    outs = _pc(body, "adamw", (rows // tr,), [blk] * 4, [blk] * 3, [_sds(w.shape)] * 3)(w, g, m, v)
    return tuple(o.reshape(shape) for o in outs)


def _sum_leading(x, name):
    n, rows, _ = x.shape
    tr = _row_tile(rows)

    def body(x_ref, o_ref):
        acc = x_ref[0]
        for k in range(1, n):
            acc = acc + x_ref[k]
        o_ref[...] = acc

    return _pc(body, name, (rows // tr,), [_bs((n, tr, FLAT_W), lambda i: (0, i, 0))],
               _bs((tr, FLAT_W), lambda i: (i, 0)), _sds((rows, FLAT_W)))(x)


def _pair_add(a, b, name):
    rows = a.shape[0]
    tr = _row_tile(rows)

    def body(a_ref, b_ref, o_ref):
        o_ref[...] = a_ref[...] + b_ref[...]

    blk = _bs((tr, FLAT_W), lambda i: (i, 0))
    return _pc(body, name, (rows // tr,), [blk, blk], blk, _sds(a.shape))(a, b)


_CHIP_FLIPS = ((1, 0, 0), (0, 1, 0), (1, 1, 0))
_ALL_FLIPS = tuple((a, b, c) for a in (0, 1) for b in (0, 1) for c in (0, 1) if (a, b, c) != (0, 0, 0))
_CORE_FLIP = ((0, 0, 1),)
_D2D_STREAMS = 4


def _exchange(name, arrays, out_shapes, flips, plan):
    n_in = len(arrays)

    def body(*refs):
        ins, outs = refs[:n_in], refs[n_in:n_in + len(out_shapes)]
        send_sems, recv_sems, local_sems = refs[n_in + len(out_shapes):]
        me = (lax.axis_index("x"), lax.axis_index("y"), lax.axis_index("c"))
        local = [pltpu.make_async_copy(s, d, local_sems.at[n]) for n, (s, d) in enumerate(plan(me, None, ins, outs))]
        for cp in local:
            cp.start()
        sent, k = [], 0
        for f in flips:
            peer = tuple(1 - m if fl else m for m, fl in zip(me, f))
            for s, d in plan(me, peer, ins, outs):
                cp = pltpu.make_async_remote_copy(src_ref=s, dst_ref=d, send_sem=send_sems.at[k],
                                                  recv_sem=recv_sems.at[k], device_id=peer,
                                                  device_id_type=pl.DeviceIdType.MESH)
                cp.start()
                sent.append(cp)
                k += 1
        for cp in sent:
            cp.wait_recv()
        for cp in sent:
            cp.wait_send()
        for cp in local:
            cp.wait()

    me0 = (0, 0, 0)
    n_remote = sum(len(_plan_count(plan, me0, f, arrays, out_shapes)) for f in flips)
    n_local = len(_plan_count(plan, me0, None, arrays, out_shapes))
    hbm = pl.BlockSpec(memory_space=pltpu.HBM)
    return pl.pallas_call(
        body, name=name, in_specs=[hbm] * n_in, out_specs=[hbm] * len(out_shapes), out_shape=list(out_shapes),
        scratch_shapes=[pltpu.SemaphoreType.DMA((n_remote,)), pltpu.SemaphoreType.DMA((n_remote,)),
                        pltpu.SemaphoreType.DMA((max(n_local, 1),))],
        compiler_params=pltpu.CompilerParams(has_side_effects=True))(*arrays)


class _FakeRef:
    def __init__(self):
        self.at = self

    def __getitem__(self, idx):
        return self


def _plan_count(plan, me, flip, arrays, out_shapes):
    peer = None if flip is None else me
    return plan(me, peer, [_FakeRef() for _ in arrays], [_FakeRef() for _ in out_shapes])


def _chip_index(dev):
    return 2 * dev[0] + dev[1]


def _all_gather_chips(name, arrays):
    n = len(arrays)

    def body(*refs):
        ins, outs = refs[:n], refs[n:2 * n]
        send_sems, recv_sems, local_sems = refs[2 * n:]
        x, y, c = lax.axis_index("x"), lax.axis_index("y"), lax.axis_index("c")
        sibling = (x, y, 1 - c)
        chips = [(1 - x, y), (x, 1 - y), (1 - x, 1 - y)]
        mine = 2 * x + y

        def copy(k, src, dst, to):
            return pltpu.make_async_remote_copy(src_ref=src, dst_ref=dst, send_sem=send_sems.at[k],
                                                recv_sem=recv_sems.at[k], device_id=to,
                                                device_id_type=pl.DeviceIdType.MESH)

        local = [pltpu.make_async_copy(ins[a], outs[a].at[mine], local_sems.at[a]) for a in range(n)]
        for cp in local:
            cp.start()
        first = [copy(j * n + a, ins[a].at[c], outs[a].at[mine, c], (*chip, c))
                 for j, chip in enumerate(chips) for a in range(n)]
        for cp in first:
            cp.start()
        passed = []
        for j, chip in enumerate(chips):
            for a in range(n):
                landed = outs[a].at[_chip_index(chip), c]
                copy(j * n + a, landed, landed, sibling).wait_recv()
                cp = copy(3 * n + j * n + a, landed, landed, sibling)
                cp.start()
                passed.append(cp)
        for j, chip in enumerate(chips):
            for a in range(n):
                other = outs[a].at[_chip_index(chip), 1 - c]
                copy(3 * n + j * n + a, other, other, sibling).wait_recv()
        for cp in first + passed:
            cp.wait_send()
        for cp in local:
            cp.wait()

    hbm = pl.BlockSpec(memory_space=pltpu.HBM)
    return pl.pallas_call(
        body, name=name, in_specs=[hbm] * n, out_specs=[hbm] * n,
        out_shape=[_sds((N_CHIPS,) + a.shape, a.dtype) for a in arrays],
        scratch_shapes=[pltpu.SemaphoreType.DMA((6 * n,)), pltpu.SemaphoreType.DMA((6 * n,)),
                        pltpu.SemaphoreType.DMA((n,))],
        compiler_params=pltpu.CompilerParams(has_side_effects=True))(*arrays)


def _all_gather_devices(name, arr):
    def plan(me, peer, ins, outs):
        return [(ins[0], outs[0].at[4 * me[0] + 2 * me[1] + me[2]])]

    return _exchange(name, [arr], [_sds((8,) + arr.shape, arr.dtype)], _ALL_FLIPS, plan)[0]


def _swap_half_with_sibling(name, g):
    step = g.shape[2] // _D2D_STREAMS

    def plan(me, peer, ins, outs):
        if peer is None:
            return []
        return [(ins[0].at[k, 1 - me[2], pl.ds(r * step, step)], outs[0].at[k, pl.ds(r * step, step)])
                for k in range(N_CHIPS) for r in range(_D2D_STREAMS)]

    return _exchange(name, [g], [_sds((N_CHIPS,) + g.shape[2:], g.dtype)], _CORE_FLIP, plan)[0]


def _scatter_to_chips(name, s):
    def plan(me, peer, ins, outs):
        to = me if peer is None else peer
        return [(ins[0].at[_chip_index(to)], outs[0].at[_chip_index(me)])]

    return _exchange(name, [s], [_sds(s.shape, s.dtype)], _CHIP_FLIPS, plan)[0]


def _share_with_sibling(name, r):
    step = r.shape[0] // (4 * _D2D_STREAMS)

    def plan(me, peer, ins, outs):
        return [(ins[0].at[pl.ds(n * step, step)], outs[0].at[me[2], pl.ds(n * step, step)])
                for n in range(4 * _D2D_STREAMS)]

    return _exchange(name, [r], [_sds((2,) + r.shape, r.dtype)], _CORE_FLIP, plan)[0]


def _to_heads(x):
    return x.reshape(x.shape[0], N_HEADS, HEAD_DIM).transpose(1, 0, 2)


def _from_heads(x):
    return x.transpose(1, 0, 2).reshape(x.shape[1], QW)


def _to_heads_t(x):
    return x.T.reshape(N_HEADS, HEAD_DIM, x.shape[0])


def _from_heads_t(x):
    return x.reshape(QW, x.shape[2]).T


def _reorder_w_in(w):
    o = 3 * QW + 3 * QW + QW
    main = jnp.concatenate([w[:, :o], w[:, o + 2 * N_HEADS:]], axis=1)
    ba = jnp.pad(w[:, o:o + 2 * N_HEADS], ((0, 0), (0, 128 - 2 * N_HEADS)))
    return jnp.concatenate([main, ba], axis=1)


def _restore_w_in(w):
    o = 3 * QW + 3 * QW + QW
    return jnp.concatenate([w[:, :o], w[:, PROJ_W - 128:PROJ_W - 128 + 2 * N_HEADS], w[:, o:PROJ_W - 128]], axis=1)


def _slab_embed_b(b):
    x = b.reshape(S5_SLABS, 8, S5_C, S5_P)
    eye = jnp.eye(8, dtype=b.dtype)
    return (x[:, :, :, None, :] * eye[None, :, None, :, None]).reshape(S5_SLABS, 8 * S5_C, 8 * S5_P)


def _slab_extract_b(m):
    x = m.reshape(S5_SLABS, 8, S5_C, 8, S5_P)
    return jnp.stack([x[:, g, :, g, :] for g in range(8)], axis=1).reshape(S5_G, S5_C, S5_P)


def _slab_embed_c(c):
    x = c.reshape(S5_SLABS, 8, S5_C, S5_P).transpose(0, 1, 3, 2)
    eye = jnp.eye(8, dtype=c.dtype)
    return (x[:, :, :, None, :] * eye[None, :, None, :, None]).reshape(S5_SLABS, 8 * S5_P, 8 * S5_C)


def _slab_extract_c(m):
    x = m.reshape(S5_SLABS, 8, S5_P, 8, S5_C)
    return jnp.stack([x[:, g, :, g, :] for g in range(8)], axis=1).transpose(0, 1, 3, 2).reshape(S5_G, S5_C, S5_P)


def _piece_rows(shape):
    return -(-math.prod(shape) // (8 * FLAT_W)) * 8


def _pack_rows(parts, lead, row_align):
    rows = []
    for p in parts:
        flat = p.reshape(lead + (-1,))
        r = _piece_rows(p.shape[len(lead):])
        flat = jnp.pad(flat, [(0, 0)] * len(lead) + [(0, r * FLAT_W - flat.shape[-1])])
        rows.append(flat.reshape(lead + (r, FLAT_W)))
    total = sum(r.shape[-2] for r in rows)
    rows.append(jnp.zeros(lead + ((-total) % row_align, FLAT_W), parts[0].dtype))
    return jnp.concatenate(rows, axis=len(lead))


def _unpack_rows(flat, shapes):
    out, off = [], 0
    for s in shapes:
        r = _piece_rows(s)
        out.append(flat[off:off + r].reshape(-1)[:math.prod(s)].reshape(s))
        off += r
    return out


_SHARDED = ("ffn1_w_gate", "ffn1_w_up", "ffn1_w_down", "w_in", "s5_w_glu", "w_out",
            "ffn2_w_gate", "ffn2_w_up", "ffn2_w_down", "meta_tokens", "dn_conv_w")
_MATMUL_W = _SHARDED[:9]
_WEIGHTS = ("meta_tokens", "ffn1_norm", "ffn1_w_gate", "ffn1_w_up", "ffn1_w_down", "mix_norm", "w_in", "sb_out_norm",
            "dn_conv_w", "dn_a_log", "dn_dt_bias", "dn_out_norm", "s5_a_re", "s5_a_im", "s5_log_dt", "s5_b_re",
            "s5_b_im", "s5_c_re", "s5_c_im", "s5_d", "s5_w_glu", "s5_b_glu", "s5_out_norm", "w_out", "ffn2_norm",
            "ffn2_w_gate", "ffn2_w_up", "ffn2_w_down", "final_norm")
_REPLICATED = tuple(n for n in _WEIGHTS if n not in _SHARDED)


def _chip_major(name, g):
    if name in ("ffn1_w_gate", "ffn1_w_up", "ffn2_w_gate", "ffn2_w_up", "ffn1_w_down", "ffn2_w_down"):
        return g.transpose(1, 0, 2, 3)
    if name == "w_in":
        g = g.reshape(2, D_MODEL, N_CHIPS, IN_WIDTH // N_CHIPS).transpose(2, 0, 1, 3)
        return jnp.pad(g, ((0, 0), (0, 0), (0, 0), (0, W_IN_PACKED - IN_WIDTH // N_CHIPS)))
    if name in ("w_out", "s5_w_glu"):
        return g.reshape(2, N_CHIPS, g.shape[1] // N_CHIPS, g.shape[2]).transpose(1, 0, 2, 3)
    if name == "meta_tokens":
        return g.reshape(N_META, N_CHIPS, D_MODEL // N_CHIPS).transpose(1, 0, 2)
    if name == "dn_conv_w":
        return g.reshape(2, DN_CONV, N_CHIPS, 3 * QW // N_CHIPS).transpose(2, 0, 1, 3)
    raise ValueError(name)


def _layer_forward(h, p):
    lp = h.shape[0]
    h1, *ffn1_saved = _ffn_fwd(h, p["ffn1_norm"], *p["ffn1"])
    q, k, v, dnx, z, u, ba = _inproj_fwd(h1, p["mix_norm"], p["w_in"].astype(MXU_DTYPE))
    qh, kh, vh = _to_heads(q), _to_heads_t(k), _to_heads_t(v)
    osb, sb_w, sb_sig = _sb_fwd(qh, kh, vh)
    dq_, dk_, dv_, bg = _dn_pre_fwd(dnx, ba, p["cw"], p["pv"])
    dqh, dkh, dvh = _to_heads(dq_), _to_heads(dk_), _to_heads(dv_)
    brow = bg[:, :N_HEADS].T.reshape(N_HEADS, lp // DN_CHUNK, 1, DN_CHUNK)
    grow = bg[:, N_HEADS:2 * N_HEADS].T.reshape(N_HEADS, lp // DN_CHUNK, 1, DN_CHUNK)
    odn, states = _dn_scan_fwd(dqh, dkh, dvh, grow, brow)
    ar, ai, bre, bim = _s5_prep_fwd(p["s5_a_re"], p["s5_a_im"], p["s5_log_dt"], p["s5_b_re"], p["s5_b_im"])
    s5t = (ar.reshape(1, S5_N), ai.reshape(1, S5_N), _slab_embed_b(bre), _slab_embed_b(bim),
           _slab_embed_c(p["s5_c_re"]), _slab_embed_c(p["s5_c_im"]), p["s5_d"])
    ys5, cin = _s5_scan_fwd(u, *s5t)
    osb_t, odn_t = _from_heads(osb), _from_heads(odn)
    h2 = _mixout_fwd(h1, osb_t, odn_t, z, ys5, p["sb_out_norm"], p["dn_out_norm"], p["s5_w_glu"], p["s5_b_glu"],
                     p["s5_out_norm"], p["w_out"].astype(MXU_DTYPE))
    h3, *ffn2_saved = _ffn_fwd(h2, p["ffn2_norm"], *p["ffn2"])
    saved = dict(h0=h, h1=h1, h2=h2, ffn1=ffn1_saved, ffn2=ffn2_saved, qh=qh, kh=kh, vh=vh, sb_w=sb_w, sb_sig=sb_sig, dnx=dnx, ba=ba, dqh=dqh, dkh=dkh, dvh=dvh,
                 grow=grow, brow=brow, states=states, odn_t=odn_t, osb_t=osb_t, z=z, u=u, ys5=ys5, cin=cin, s5t=s5t)
    return h3, saved


def _ffn_backward(h, g, w3, dy, fwd_saved):
    xn, gate, up = fwd_saved
    dh, dgn, d_gate, d_up, act = _ffn_bwd_dx(h, g, *w3, dy, gate, up)
    dwg, dwu, dwd = _ffn_bwd_dw(xn, dy, d_gate, d_up, act)
    return dh, dgn, dwg, dwu, dwd


def _layer_backward(dh3, p, s):
    lp = dh3.shape[0]
    g = {}
    dh2, g["ffn2_norm"], g["ffn2_w_gate"], g["ffn2_w_up"], g["ffn2_w_down"] = _ffn_backward(
        s["h2"], p["ffn2_norm"], p["ffn2"], dh3, s["ffn2"])
    (dosb_t, dodn_t, dz, dys5, g["sb_out_norm"], g["dn_out_norm"], g["s5_w_glu"], g["s5_b_glu"], g["s5_out_norm"],
     g["w_out"]) = _mixout_bwd(dh2, s["osb_t"], s["odn_t"], s["z"], s["ys5"], p["sb_out_norm"], p["dn_out_norm"],
                               p["s5_w_glu"], p["s5_b_glu"], p["s5_out_norm"], p["w_out"].astype(MXU_DTYPE))
    du, dab, db8r, db8i, dc8r, dc8i, g["s5_d"] = _s5_scan_bwd(s["u"], dys5, s["cin"], *s["s5t"])
    g["s5_c_re"], g["s5_c_im"] = _slab_extract_c(dc8r), _slab_extract_c(dc8i)
    g["s5_a_re"], g["s5_a_im"], g["s5_log_dt"], g["s5_b_re"], g["s5_b_im"] = _s5_prep_bwd(
        p["s5_a_re"], p["s5_a_im"], p["s5_log_dt"], p["s5_b_re"], p["s5_b_im"],
        dab[0].reshape(S5_G, S5_P), dab[1].reshape(S5_G, S5_P), _slab_extract_b(db8r), _slab_extract_b(db8i))
    ddq, ddk, ddv, dgrow, dbrow = _dn_scan_bwd(s["dqh"], s["dkh"], s["dvh"], s["grow"], s["brow"], s["states"],
                                               _to_heads(dodn_t))
    dbg = jnp.concatenate([dbrow.reshape(N_HEADS, lp).T, dgrow.reshape(N_HEADS, lp).T,
                           jnp.zeros((lp, 128 - 2 * N_HEADS), f32)], axis=1)
    dxs, dba, g["cw"], g["pv"] = _dn_pre_bwd(s["dnx"], s["ba"], p["cw"], p["pv"], _from_heads(ddq), _from_heads(ddk),
                                             _from_heads(ddv), dbg)
    dq, dk_t, dv_t = _sb_bwd(s["qh"], s["kh"], s["vh"], s["sb_w"], s["sb_sig"], _to_heads(dosb_t))
    dh1, g["mix_norm"], g["w_in"] = _inproj_bwd(s["h1"], p["mix_norm"], p["w_in"].astype(MXU_DTYPE), dh2,
                                                _from_heads(dq), _from_heads_t(dk_t), _from_heads_t(dv_t), dxs,
                                                dz, du, dba)
    dh0, g["ffn1_norm"], g["ffn1_w_gate"], g["ffn1_w_up"], g["ffn1_w_down"] = _ffn_backward(
        s["h0"], p["ffn1_norm"], p["ffn1"], dh1, s["ffn1"])
    return dh0, g


def kernel(x, meta_tokens, ffn1_norm, ffn1_w_gate, ffn1_w_up, ffn1_w_down, mix_norm, w_in, sb_out_norm, dn_conv_w, dn_a_log, dn_dt_bias, dn_out_norm, s5_a_re, s5_a_im, s5_log_dt, s5_b_re, s5_b_im, s5_c_re, s5_c_im, s5_d, s5_w_glu, s5_b_glu, s5_out_norm, w_out, ffn2_norm, ffn2_w_gate, ffn2_w_up, ffn2_w_down, final_norm, loss_target, m_meta_tokens, m_ffn1_norm, m_ffn1_w_gate, m_ffn1_w_up, m_ffn1_w_down, m_mix_norm, m_w_in, m_sb_out_norm, m_dn_conv_w, m_dn_a_log, m_dn_dt_bias, m_dn_out_norm, m_s5_a_re, m_s5_a_im, m_s5_log_dt, m_s5_b_re, m_s5_b_im, m_s5_c_re, m_s5_c_im, m_s5_d, m_s5_w_glu, m_s5_b_glu, m_s5_out_norm, m_w_out, m_ffn2_norm, m_ffn2_w_gate, m_ffn2_w_up, m_ffn2_w_down, m_final_norm, v_meta_tokens, v_ffn1_norm, v_ffn1_w_gate, v_ffn1_w_up, v_ffn1_w_down, v_mix_norm, v_w_in, v_sb_out_norm, v_dn_conv_w, v_dn_a_log, v_dn_dt_bias, v_dn_out_norm, v_s5_a_re, v_s5_a_im, v_s5_log_dt, v_s5_b_re, v_s5_b_im, v_s5_c_re, v_s5_c_im, v_s5_d, v_s5_w_glu, v_s5_b_glu, v_s5_out_norm, v_w_out, v_ffn2_norm, v_ffn2_w_gate, v_ffn2_w_up, v_ffn2_w_down, v_final_norm):
    args = dict(locals())
    w = {n: args[n] for n in _WEIGHTS}
    m = {n: args["m_" + n] for n in _WEIGHTS}
    v = {n: args["v_" + n] for n in _WEIGHTS}
    depth = ffn1_norm.shape[0]
    seq = x.shape[1]
    n_real = N_META + seq
    lp = _padded_len(n_real)

    gathered = _all_gather_chips("gather_weights", [w[n].astype(MXU_DTYPE) for n in _MATMUL_W]
                                 + [w["meta_tokens"].reshape(2, N_META // 2, -1), w["dn_conv_w"]])
    full = dict(zip(_MATMUL_W + ("meta_tokens", "dn_conv_w"), gathered))
    meta_full = full["meta_tokens"].reshape(N_CHIPS, N_META, -1).transpose(1, 0, 2).reshape(N_META, D_MODEL)
    conv_full = full["dn_conv_w"].transpose(1, 2, 0, 3).reshape(depth, DN_CONV, 3 * QW)
    w_in_full = full["w_in"].transpose(1, 2, 0, 3).reshape(depth, D_MODEL, IN_WIDTH)
    w_out_full = full["w_out"].transpose(1, 0, 2, 3).reshape(depth, D_MODEL, D_MODEL)
    w_glu_full = full["s5_w_glu"].transpose(1, 0, 2, 3).reshape(depth, S5_W, S5_W)

    layers = []
    for l in range(depth):
        pv = jnp.pad(jnp.stack([dn_a_log[l], dn_dt_bias[l]]), ((0, 6), (N_HEADS, 128 - 2 * N_HEADS)))
        layers.append(dict(
            ffn1_norm=ffn1_norm[l][None], mix_norm=mix_norm[l][None], ffn2_norm=ffn2_norm[l][None],
            ffn1=(full["ffn1_w_gate"][:, l], full["ffn1_w_up"][:, l], full["ffn1_w_down"][:, l]),
            ffn2=(full["ffn2_w_gate"][:, l], full["ffn2_w_up"][:, l], full["ffn2_w_down"][:, l]),
            w_in=_reorder_w_in(w_in_full[l]), w_out=w_out_full[l], s5_w_glu=w_glu_full[l].astype(f32),
            cw=jnp.pad(conv_full[l], ((0, 8 - DN_CONV), (0, 0))), pv=pv,
            sb_out_norm=sb_out_norm[l][None], dn_out_norm=dn_out_norm[l][None],
            s5_a_re=s5_a_re[l], s5_a_im=s5_a_im[l], s5_log_dt=s5_log_dt[l][:, None],
            s5_b_re=s5_b_re[l].transpose(0, 2, 1), s5_b_im=s5_b_im[l].transpose(0, 2, 1),
            s5_c_re=s5_c_re[l], s5_c_im=s5_c_im[l], s5_d=s5_d[l][None], s5_b_glu=s5_b_glu[l][None],
            s5_out_norm=s5_out_norm[l][None]))

    tail = jnp.zeros((lp - n_real, D_MODEL), f32)
    h = jnp.concatenate([meta_full, x[0], tail], axis=0)
    target = jnp.concatenate([jnp.zeros((N_META, D_MODEL), f32), loss_target[0], tail], axis=0)
    saved = []
    for p in layers:
        h, s = _layer_forward(h, p)
        saved.append(s)
    loss_blk, dh, d_final = _loss_fwd_bwd(h, final_norm[None], target, n_real)
    grads = [None] * depth
    for l in reversed(range(depth)):
        dh, grads[l] = _layer_backward(dh, layers[l], saved[l])
    loss = lax.psum(loss_blk[0, 0], ("x", "y", "c"))
    grad_x = dh[N_META:n_real][None]

    stack = lambda name: jnp.stack([grads[l][name] for l in range(depth)])
    gfull = {n: stack(n) for n in ("ffn1_w_gate", "ffn1_w_up", "ffn1_w_down", "s5_w_glu", "w_out", "ffn2_w_gate",
                                   "ffn2_w_up", "ffn2_w_down")}
    gfull["w_in"] = jnp.stack([_restore_w_in(grads[l]["w_in"]) for l in range(depth)])
    gfull["meta_tokens"] = dh[:N_META]
    gfull["dn_conv_w"] = jnp.stack([grads[l]["cw"][:DN_CONV] for l in range(depth)])
    grep = {n: stack(n).reshape(w[n].shape) for n in ("ffn1_norm", "mix_norm", "sb_out_norm", "dn_out_norm", "s5_a_re",
                                                      "s5_a_im", "s5_log_dt", "s5_c_re", "s5_c_im", "s5_d", "s5_b_glu",
                                                      "s5_out_norm", "ffn2_norm")}
    grep["s5_b_re"] = jnp.stack([grads[l]["s5_b_re"].transpose(0, 2, 1) for l in range(depth)])
    grep["s5_b_im"] = jnp.stack([grads[l]["s5_b_im"].transpose(0, 2, 1) for l in range(depth)])
    grep["dn_a_log"] = jnp.stack([grads[l]["pv"][0, N_HEADS:2 * N_HEADS] for l in range(depth)])
    grep["dn_dt_bias"] = jnp.stack([grads[l]["pv"][1, N_HEADS:2 * N_HEADS] for l in range(depth)])
    grep["final_norm"] = d_final[0]

    shard_shapes = [w[n].shape[:-1] + (W_IN_PACKED,) if n == "w_in" else w[n].shape for n in _SHARDED]
    g_big = _pack_rows([_chip_major(n, gfull[n]) for n in _SHARDED], (N_CHIPS,), BIG_ROWS)
    half_rows = g_big.shape[1] // 2
    g_big = g_big.reshape(N_CHIPS, 2, half_rows, FLAT_W)
    c = lax.axis_index("c")
    mine = lax.dynamic_index_in_dim(g_big, c, axis=1, keepdims=False)
    theirs = _swap_half_with_sibling("grad_pair_swap", g_big)
    pair = _pair_add(mine.reshape(-1, FLAT_W), theirs.reshape(-1, FLAT_W), "grad_pair_add")
    arrived = _scatter_to_chips("grad_scatter", pair.reshape(N_CHIPS, half_rows, FLAT_W))
    reduced = _sum_leading(arrived, "grad_chip_sum")
    g_shard = _share_with_sibling("grad_share", reduced).reshape(-1, FLAT_W)

    rep_shapes = [w[n].shape for n in _REPLICATED]
    g_small = _pack_rows([grep[n] for n in _REPLICATED], (), 64)
    g_rep = _sum_leading(_all_gather_devices("grad_small_gather", g_small), "grad_small_sum")

    out = {}
    for n, g_n in zip(_SHARDED, _unpack_rows(g_shard, shard_shapes)):
        g_n = g_n[..., :w[n].shape[-1]]
        out["grad_" + n] = g_n
        out["delta_" + n], out["new_m_" + n], out["new_v_" + n] = _adamw(w[n], g_n, m[n], v[n])
    pack = lambda d: _pack_rows([d[n] for n in _REPLICATED], (), 64)
    delta, m_new, v_new = _adamw(pack(w), g_rep, pack(m), pack(v))
    for kind, flat in (("grad", g_rep), ("delta", delta), ("new_m", m_new), ("new_v", v_new)):
        for n, a in zip(_REPLICATED, _unpack_rows(flat, rep_shapes)):
            out[kind + "_" + n] = a
    return (loss, grad_x, *[out[k + "_" + n] for k in ("grad", "delta", "new_m", "new_v") for n in _WEIGHTS])
```

```python
import functools
import math

import jax
import jax.numpy as jnp
from jax import lax
from jax.experimental import pallas as pl
from jax.experimental.pallas import tpu as pltpu

f32 = jnp.float32
MXU_DTYPE = jnp.bfloat16
HI = lax.Precision.HIGHEST
NN = (((1,), (0,)), ((), ()))
NT = (((1,), (1,)), ((), ()))
TN = (((0,), (0,)), ((), ()))

EPS = 1e-6
D_MODEL = 1024
N_META = 16
HEAD_DIM = 64
N_HEADS = 4
QW = N_HEADS * HEAD_DIM
DN_CONV = 4
DN_CHUNK = 64
S5_W = 512
S5_G = 32
S5_P = 64
S5_C = 16
S5_N = S5_G * S5_P
S5_SLABS = 4
N_CHIPS = 4
PROJ_W = 2432
IN_WIDTH = 2312
W_IN_PACKED = 640
VMEM_LIMIT = 56 * 1024 * 1024

ADAM_LR, ADAM_B1, ADAM_B2, ADAM_EPS, ADAM_WD, ADAM_STEP = 0.001, 0.9, 0.999, 1e-08, 0.01, 10
FLAT_W = 1024
BIG_ROWS = 512


def _dot(a, b, dims=NN):
    return lax.dot_general(a.astype(MXU_DTYPE), b.astype(MXU_DTYPE), dims, preferred_element_type=f32)


def _dotx(a, b, dims=NN):
    return lax.dot_general(a, b, dims, precision=HI, preferred_element_type=f32)


def _split(x):
    if MXU_DTYPE == f32:
        return x, None
    hi = x.astype(MXU_DTYPE)
    return hi, (x - hi.astype(f32)).astype(MXU_DTYPE)


def _dot_split(hi, lo, u01):
    if lo is None:
        return _dotx(hi, u01)
    u = u01.astype(MXU_DTYPE)
    return (lax.dot_general(hi, u, NN, preferred_element_type=f32)
            + lax.dot_general(lo, u, NN, preferred_element_type=f32))


def _dot3(a, b, dims=NN):
    ah, al = _split(a)
    if al is None:
        return _dotx(a, b, dims)
    bh, bl = _split(b)
    d = lambda x, y: lax.dot_general(x, y, dims, preferred_element_type=f32)
    return d(ah, bh) + d(ah, bl) + d(al, bh)


BNN = (((2,), (1,)), ((0,), (0,)))
BNT = (((2,), (2,)), ((0,), (0,)))
BTN = (((1,), (1,)), ((0,), (0,)))


def _with_dot_vjp(dot, kind, batched=False):
    nn, nt, tn = (BNN, BNT, BTN) if batched else (NN, NT, TN)
    dims = {"nn": nn, "nt": nt, "tn": tn}[kind]

    @jax.custom_vjp
    def f(a, b):
        return dot(a, b, dims)

    def fwd(a, b):
        return dot(a, b, dims), (a, b)

    def bwd(res, dy):
        a, b = res
        if kind == "nn":
            return dot(dy, b, nt), dot(a, dy, tn)
        if kind == "nt":
            return dot(dy, b, nn), dot(dy, a, tn)
        return dot(b, dy, nt), dot(a, dy, nn)

    f.defvjp(fwd, bwd)
    return f


_mm = _with_dot_vjp(_dot, "nn")
_bmm = _with_dot_vjp(_dot, "nn", True)
_bmm_nt = _with_dot_vjp(_dot, "nt", True)
_bmm_tn = _with_dot_vjp(_dot, "tn", True)
_bmm3 = _with_dot_vjp(_dot3, "nn", True)


def _rms(x, g):
    return x * lax.rsqrt(jnp.mean(x * x, axis=-1, keepdims=True) + EPS) * g


def _sigmoid(x):
    return 1.0 / (1.0 + jnp.exp(-x))


def _silu(x):
    return x * _sigmoid(x)


def _softplus(x):
    return jnp.maximum(x, 0.0) + jnp.log(1.0 + jnp.exp(-jnp.abs(x)))


def _gelu_tanh(x):
    return 0.5 * x * (1.0 + jnp.tanh(math.sqrt(2.0 / math.pi) * (x + 0.044715 * x * x * x)))


def _iota2(shape, axis):
    return lax.broadcasted_iota(jnp.int32, shape, axis)


def _block_diag_ones(n, blk):
    return ((_iota2((n, n), 0) // blk) == (_iota2((n, n), 1) // blk)).astype(f32)


def _pc(body, name, grid, in_specs, out_specs, out_shape, scratch=(), vmem=VMEM_LIMIT):
    return pl.pallas_call(
        body, name=name, grid=grid, in_specs=in_specs, out_specs=out_specs, out_shape=out_shape,
        scratch_shapes=list(scratch),
        compiler_params=pltpu.CompilerParams(dimension_semantics=("arbitrary",) * len(grid), vmem_limit_bytes=vmem))


def _bs(shape, imap):
    return pl.BlockSpec(shape, imap)


def _sds(shape, dtype=f32):
    return jax.ShapeDtypeStruct(tuple(shape), dtype)


def _token_tile(lp, cap=640):
    for t in (640, 320, 256, 128, 64):
        if t <= cap and lp % t == 0:
            return t
    raise ValueError(lp)


def _padded_len(l):
    return -(-l // 1280) * 1280 if l > 4096 else -(-l // 256) * 256


def _ffn_fwd(h, g, wg, wu, wd):
    lp, d = h.shape
    nch, _, fc = wg.shape
    tm = _token_tile(lp)

    def body(h_ref, g_ref, wg_ref, wu_ref, wd_ref, o_ref, xn_ref, gate_ref, up_ref, xn_s, acc_s):
        j = pl.program_id(1)

        @pl.when(j == 0)
        def _():
            xn_s[...] = _rms(h_ref[...], g_ref[...]).astype(xn_s.dtype)
            acc_s[...] = jnp.zeros_like(acc_s)

        xn = xn_s[...]
        gate = _dot(xn, wg_ref[0])
        up = _dot(xn, wu_ref[0])
        gate_ref[0] = gate.astype(gate_ref.dtype)
        up_ref[0] = up.astype(up_ref.dtype)
        acc_s[...] += _dot(_silu(gate) * up, wd_ref[0])

        @pl.when(j == nch - 1)
        def _():
            o_ref[...] = h_ref[...] + 0.5 * acc_s[...]
            xn_ref[...] = xn_s[...]

    tok = _bs((tm, d), lambda i, j: (i, 0))
    chunk = _bs((1, tm, fc), lambda i, j: (j, i, 0))
    return _pc(
        body, "ffn_fwd", (lp // tm, nch),
        [tok, _bs((1, d), lambda i, j: (0, 0)),
         _bs((1, d, fc), lambda i, j: (j, 0, 0)), _bs((1, d, fc), lambda i, j: (j, 0, 0)),
         _bs((1, fc, d), lambda i, j: (j, 0, 0))],
        [tok, tok, chunk, chunk],
        [_sds((lp, d)), _sds((lp, d), MXU_DTYPE), _sds((nch, lp, fc), MXU_DTYPE), _sds((nch, lp, fc), MXU_DTYPE)],
        scratch=[pltpu.VMEM((tm, d), MXU_DTYPE), pltpu.VMEM((tm, d), f32)])(h, g, wg, wu, wd)


def _ffn_bwd_dx(h, g, wg, wu, wd, dy, gate_saved, up_saved):
    lp, d = h.shape
    nch, _, fc = wg.shape
    tm = _token_tile(lp, 320)

    def body(h_ref, g_ref, wg_ref, wu_ref, wd_ref, dy_ref, gate_ref, up_ref, dh_ref, dgn_ref, dg_ref, du_ref, act_ref,
             dxn_s, dout_s):
        i, j = pl.program_id(0), pl.program_id(1)

        @pl.when((i == 0) & (j == 0))
        def _():
            dgn_ref[...] = jnp.zeros_like(dgn_ref)

        @pl.when(j == 0)
        def _():
            dxn_s[...] = jnp.zeros_like(dxn_s)
            dout_s[...] = (0.5 * dy_ref[...]).astype(dout_s.dtype)

        gate = gate_ref[0].astype(f32)
        up = up_ref[0].astype(f32)
        sig = _sigmoid(gate)
        sl = gate * sig
        dact = _dot(dout_s[...], wd_ref[0], NT)
        d_up = dact * sl
        d_gate = dact * up * sig * (1.0 + gate * (1.0 - sig))
        dg_ref[0] = d_gate.astype(dg_ref.dtype)
        du_ref[0] = d_up.astype(du_ref.dtype)
        act_ref[0] = (sl * up).astype(act_ref.dtype)
        dxn_s[...] += _dot(d_gate, wg_ref[0], NT) + _dot(d_up, wu_ref[0], NT)

        @pl.when(j == nch - 1)
        def _():
            _, vjp = jax.vjp(_rms, h_ref[...], g_ref[...])
            dx, dg = vjp(dxn_s[...])
            dh_ref[...] = dy_ref[...] + dx
            dgn_ref[...] += dg

    tok = _bs((tm, d), lambda i, j: (i, 0))
    chunk = _bs((1, tm, fc), lambda i, j: (j, i, 0))
    return _pc(
        body, "ffn_bwd_dx", (lp // tm, nch),
        [tok, _bs((1, d), lambda i, j: (0, 0)),
         _bs((1, d, fc), lambda i, j: (j, 0, 0)), _bs((1, d, fc), lambda i, j: (j, 0, 0)),
         _bs((1, fc, d), lambda i, j: (j, 0, 0)), tok, chunk, chunk],
        [tok, _bs((1, d), lambda i, j: (0, 0)), chunk, chunk, chunk],
        [_sds((lp, d)), _sds((1, d)),
         _sds((nch, lp, fc), MXU_DTYPE), _sds((nch, lp, fc), MXU_DTYPE), _sds((nch, lp, fc), MXU_DTYPE)],
        scratch=[pltpu.VMEM((tm, d), f32), pltpu.VMEM((tm, d), MXU_DTYPE)],
    )(h, g, wg, wu, wd, dy, gate_saved, up_saved)


def _ffn_bwd_dw(xn, dy, d_gate, d_up, act):
    lp, d = xn.shape
    nch, _, fc = d_gate.shape
    tm = _token_tile(lp)

    def body(xn_ref, dy_ref, dg_ref, du_ref, act_ref, dwg_ref, dwu_ref, dwd_ref):
        @pl.when(pl.program_id(1) == 0)
        def _():
            dwg_ref[...] = jnp.zeros_like(dwg_ref)
            dwu_ref[...] = jnp.zeros_like(dwu_ref)
            dwd_ref[...] = jnp.zeros_like(dwd_ref)

        xn_t = xn_ref[...]
        dwg_ref[0] += _dot(xn_t, dg_ref[0], TN)
        dwu_ref[0] += _dot(xn_t, du_ref[0], TN)
        dwd_ref[0] += _dot(act_ref[0], 0.5 * dy_ref[...], TN)

    tok = _bs((tm, d), lambda j, i: (i, 0))
    chunk = _bs((1, tm, fc), lambda j, i: (j, i, 0))
    return _pc(
        body, "ffn_bwd_dw", (nch, lp // tm), [tok, tok, chunk, chunk, chunk],
        [_bs((1, d, fc), lambda j, i: (j, 0, 0)), _bs((1, d, fc), lambda j, i: (j, 0, 0)),
         _bs((1, fc, d), lambda j, i: (j, 0, 0))],
        [_sds((nch, d, fc)), _sds((nch, d, fc)), _sds((nch, fc, d))])(xn, dy, d_gate, d_up, act)


_PROJ_SPLITS = (QW, QW, QW, 3 * QW, QW, S5_W, 128)


def _inproj_fwd(h, g, w):
    lp, d = h.shape
    tm = _token_tile(lp)

    def body(h_ref, g_ref, w_ref, *outs):
        proj = _dot(_rms(h_ref[...], g_ref[...]), w_ref[...])
        off = 0
        for ref, wd in zip(outs, _PROJ_SPLITS):
            ref[...] = proj[:, off:off + wd]
            off += wd

    return _pc(
        body, "inproj_fwd", (lp // tm,),
        [_bs((tm, d), lambda i: (i, 0)), _bs((1, d), lambda i: (0, 0)), _bs((d, PROJ_W), lambda i: (0, 0))],
        [_bs((tm, wd), lambda i: (i, 0)) for wd in _PROJ_SPLITS],
        [_sds((lp, wd)) for wd in _PROJ_SPLITS])(h, g, w)


def _inproj_bwd(h, g, w, dres, dq, dk, dv, dxs, dz, du, dba):
    lp, d = h.shape
    tm = _token_tile(lp, 320)
    n_tiles = lp // tm

    def body(h_ref, g_ref, w_ref, dres_ref, dq_ref, dk_ref, dv_ref, dxs_ref, nxt_ref, dz_ref, du_ref, dba_ref,
             dh_ref, dgn_ref, dw_ref):
        @pl.when(pl.program_id(0) == 0)
        def _():
            dgn_ref[...] = jnp.zeros_like(dgn_ref)
            dw_ref[...] = jnp.zeros_like(dw_ref)

        row, row8 = _iota2((tm, 1), 0), _iota2((8, 1), 0)
        more = (pl.program_id(0) < n_tiles - 1).astype(f32)
        ddn, tail = dxs_ref[0], jnp.zeros((8, 3 * QW), f32)
        for k in range(1, DN_CONV):
            ddn = ddn + jnp.where(row < tm - k, pltpu.roll(dxs_ref[k], tm - k, 0), 0.0)
            tail = tail + jnp.where(row8 >= 8 - k, pltpu.roll(nxt_ref[k], 8 - k, 0), 0.0)
        ddn = jnp.concatenate([ddn[:tm - 8], ddn[tm - 8:] + more * tail], axis=0)
        dproj = jnp.concatenate(
            [dq_ref[...], dk_ref[...], dv_ref[...], ddn, dz_ref[...], du_ref[...], dba_ref[...]], axis=1)
        xn, vjp = jax.vjp(_rms, h_ref[...], g_ref[...])
        dx, dg = vjp(_dot(dproj, w_ref[...], NT))
        dw_ref[...] += _dot(xn, dproj, TN)
        dh_ref[...] = dres_ref[...] + dx
        dgn_ref[...] += dg

    tok = lambda wd: _bs((tm, wd), lambda i: (i, 0))
    return _pc(
        body, "inproj_bwd", (lp // tm,),
        [tok(d), _bs((1, d), lambda i: (0, 0)), _bs((d, PROJ_W), lambda i: (0, 0)), tok(d),
         tok(QW), tok(QW), tok(QW), _bs((4, tm, 3 * QW), lambda i: (0, i, 0)),
         _bs((4, 8, 3 * QW), lambda i: (0, jnp.minimum((i + 1) * (tm // 8), lp // 8 - 1), 0)),
         tok(QW), tok(S5_W), tok(128)],
        [tok(d), _bs((1, d), lambda i: (0, 0)), _bs((d, PROJ_W), lambda i: (0, 0))],
        [_sds((lp, d)), _sds((1, d)), _sds((d, PROJ_W))])(h, g, w, dres, dq, dk, dv, dxs, dxs, dz, du, dba)


_SB_TQ = 256
_SB_ROWS = 32
_SB_GROUP = 4
_SB_ROWS_BWD = 32
_SB_GROUP_BWD = 4


def _sb_pieces(z, valid):
    t = jnp.exp(-jnp.abs(z))
    sp = jnp.maximum(z, 0.0) + jnp.log(1.0 + t)
    lk = -sp if valid is None else jnp.where(valid, -sp, 0.0)
    return t, sp, lk


def _cat_rows(parts):
    return parts[0] if len(parts) == 1 else jnp.concatenate(parts, axis=0)


def _sb_fwd(q, kt, vt):
    nh, lp, hd = q.shape
    tq = tk = min(_SB_TQ, lp)
    blocks = [slice(r, r + _SB_ROWS) for r in range(0, tq, _SB_ROWS)]

    def body(q_ref, k_ref, v_ref, o_ref, w_hbm, s_hbm, wbuf, sbuf, sems):
        head, qi = pl.program_id(0), pl.program_id(1)
        qv = q_ref[0]
        u_strict = (_iota2((tk, tk), 0) > _iota2((tk, tk), 1)).astype(f32)
        below = _iota2((tq, tk), 1) < _iota2((tq, tk), 0)

        spare = lambda slot, t: lp // tk + slot * _SB_GROUP + t

        def save(slot, t, j):
            return [pltpu.make_async_copy(wbuf.at[slot, t], w_hbm.at[head, qi, j], sems.at[0, slot, t]),
                    pltpu.make_async_copy(sbuf.at[slot, t], s_hbm.at[head, qi, j], sems.at[1, slot, t])]

        def drain(slot):
            for t in range(_SB_GROUP):
                for cp in save(slot, t, spare(slot, t)):
                    cp.wait()

        def idle(slot, t):
            wbuf[slot, t] = jnp.zeros((tq, tk), MXU_DTYPE)
            sbuf[slot, t] = jnp.zeros((tq, tk), MXU_DTYPE)
            for cp in save(slot, t, spare(slot, t)):
                cp.start()

        def tiles(js, carry, slot, masked=False, live=None, first=False):
            if not first:
                drain(slot)
            o_acc, c_after = carry
            kss = [pl.ds(pl.multiple_of(j * tk, tk), tk) for j in js]
            z_alls = [_dot(qv, k_ref[0, :, ks]) * (HEAD_DIM ** -0.5) for ks in kss]
            stage, afters = [], []
            for t, z_all in enumerate(z_alls):
                his, los, logs, sums = [], [], [], []
                for rs in blocks:
                    z = z_all[rs]
                    _, sp, lk = _sb_pieces(z, below[rs] if masked else None)
                    if live is not None:
                        lk = lk * live[t]
                    hi, lo = _split(lk)
                    his.append(hi)
                    los.append(lo)
                    logs.append(z - sp)
                    sums.append(jnp.sum(lk, axis=1, keepdims=True))
                stage.append((logs, _cat_rows(sums)))
                afters.append(_dot_split(_cat_rows(his), None if los[0] is None else _cat_rows(los), u_strict))
            for t, ((logs, sums), after_all) in enumerate(zip(stage, afters)):
                ws, sigs = [], []
                for n, rs in enumerate(blocks):
                    w = jnp.exp(logs[n] + after_all[rs] + c_after[rs])
                    sig = jnp.exp(logs[n])
                    if masked:
                        w, sig = jnp.where(below[rs], w, 0.0), jnp.where(below[rs], sig, 0.0)
                    if live is not None:
                        w = w * live[t]
                    ws.append(w.astype(MXU_DTYPE))
                    sigs.append(sig.astype(MXU_DTYPE))
                w_all = _cat_rows(ws)
                wbuf[slot, t] = w_all
                sbuf[slot, t] = _cat_rows(sigs)
                for cp in save(slot, t, js[t] if live is None else jnp.where(live[t] > 0.0, js[t], spare(slot, t))):
                    cp.start()
                o_acc = o_acc + _dot(w_all, v_ref[0, :, kss[t]], NT)
                c_after = c_after + sums
            for t in range(len(js), _SB_GROUP):
                idle(slot, t)
            return o_acc, c_after

        n_groups, rest = qi // _SB_GROUP, qi % _SB_GROUP
        group = lambda g, c: tiles([qi - 1 - _SB_GROUP * g - n for n in range(_SB_GROUP)], c, (g + 1) % 2)

        def last_group(_, c):
            idx = [rest - 1 - n for n in range(_SB_GROUP)]
            return tiles([jnp.maximum(j, 0) for j in idx], c, (n_groups + 1) % 2,
                         live=[(j >= 0).astype(f32) for j in idx])

        for t in range(_SB_GROUP):
            idle(1, t)
        carry = tiles([qi], (jnp.zeros((tq, hd), f32), jnp.zeros((tq, 1), f32)), 0, masked=True, first=True)
        carry = lax.fori_loop(0, n_groups, group, carry)
        o_acc, _ = lax.fori_loop(0, jnp.minimum(rest, 1), last_group, carry)
        drain(0)
        drain(1)
        o_ref[0] = o_acc

    full_t = _bs((1, hd, lp), lambda h, i: (h, 0, 0))
    hbm = pl.BlockSpec(memory_space=pltpu.HBM)
    return _pc(
        body, "sb_fwd", (nh, lp // tq),
        [_bs((1, tq, hd), lambda h, i: (h, i, 0)), full_t, full_t],
        [_bs((1, tq, hd), lambda h, i: (h, i, 0)), hbm, hbm],
        [_sds((nh, lp, hd))] + [_sds((nh, lp // tq, lp // tk + 2 * _SB_GROUP, tq, tk), MXU_DTYPE)] * 2,
        scratch=[pltpu.VMEM((2, _SB_GROUP, tq, tk), MXU_DTYPE), pltpu.VMEM((2, _SB_GROUP, tq, tk), MXU_DTYPE),
                 pltpu.SemaphoreType.DMA((2, 2, _SB_GROUP))])(q, kt, vt)


def _sb_bwd(q, kt, vt, w_saved, s_saved, do):
    nh, lp, hd = q.shape
    tq = tk = min(_SB_TQ, lp)
    blocks = [slice(r, r + _SB_ROWS_BWD) for r in range(0, tq, _SB_ROWS_BWD)]
    grp = _SB_GROUP_BWD

    def body(q_ref, k_ref, v_ref, w_hbm, s_hbm, do_ref, dq_ref, dk_ref, dv_ref, wbuf, sbuf, sems):
        head, qi = pl.program_id(0), pl.program_id(1)

        @pl.when(qi == 0)
        def _():
            dk_ref[...] = jnp.zeros_like(dk_ref)
            dv_ref[...] = jnp.zeros_like(dv_ref)

        qv, dov = q_ref[0], do_ref[0]
        u_excl = (_iota2((tk, tk), 0) < _iota2((tk, tk), 1)).astype(f32)
        scale = HEAD_DIM ** -0.5

        def loads(js, slot):
            out = []
            for t, j in enumerate(js):
                out += [pltpu.make_async_copy(w_hbm.at[head, qi, j], wbuf.at[slot, t], sems.at[0, slot, t]),
                        pltpu.make_async_copy(s_hbm.at[head, qi, j], sbuf.at[slot, t], sems.at[1, slot, t])]
            return out

        def tiles(js, slot, carry, live=None):
            dq_acc, c_e = carry
            kss = [pl.ds(pl.multiple_of(j * tk, tk), tk) for j in js]
            dw_alls = [_dot(dov, v_ref[0, :, ks]) for ks in kss]
            stage, befores = [], []
            for t, dw_all in enumerate(dw_alls):
                es, ebs, esums = [], [], []
                for rs in blocks:
                    e = wbuf[slot, t, rs].astype(f32) * dw_all[rs]
                    if live is not None:
                        e = e * live[t]
                    es.append(e)
                    ebs.append(e.astype(MXU_DTYPE))
                    esums.append(jnp.sum(e, axis=1, keepdims=True))
                stage.append((es, _cat_rows(esums)))
                befores.append(_dot(_cat_rows(ebs), u_excl))
            for t, ((es, esums), before_all, ks) in enumerate(zip(stage, befores, kss)):
                dzs = []
                for n, rs in enumerate(blocks):
                    sig = sbuf[slot, t, rs].astype(f32)
                    if live is not None:
                        sig = sig * live[t]
                    dz = es[n] * (1.0 - sig) - sig * (c_e[rs] + before_all[rs])
                    dzs.append((dz * scale).astype(MXU_DTYPE))
                dz_all = _cat_rows(dzs)
                w_all = wbuf[slot, t] if live is None else wbuf[slot, t] * live[t].astype(MXU_DTYPE)
                c_e = c_e + esums
                dk_ref[0, :, ks] += _dot(qv, dz_all, TN)
                dv_ref[0, :, ks] += _dot(dov, w_all, TN)
                dq_acc = dq_acc + _dot(dz_all, k_ref[0, :, ks], NT)
            return dq_acc, c_e

        n_tiles = qi + 1
        n_groups, rest = n_tiles // grp, n_tiles % grp
        n_passes = n_groups + jnp.minimum(rest, 1)
        group_js = lambda g: [jnp.minimum(grp * g + t, qi) for t in range(grp)]

        for cp in loads(group_js(0), 0):
            cp.start()

        def fetch_next_and_wait(g):
            slot = g % 2

            @pl.when(g + 1 < n_passes)
            def _():
                for cp in loads(group_js(g + 1), 1 - slot):
                    cp.start()

            for cp in loads(group_js(g), slot):
                cp.wait()
            return slot

        def group(g, carry):
            slot = fetch_next_and_wait(g)
            return tiles(group_js(g), slot, carry)

        def last_group(_, carry):
            slot = fetch_next_and_wait(n_groups)
            live = [(grp * n_groups + t <= qi).astype(f32) for t in range(grp)]
            return tiles(group_js(n_groups), slot, carry, live)

        carry = lax.fori_loop(0, n_groups, group, (jnp.zeros((tq, hd), f32), jnp.zeros((tq, 1), f32)))
        dq_acc, _ = lax.fori_loop(0, jnp.minimum(rest, 1), last_group, carry)
        dq_ref[0] = dq_acc

    tile_spec = _bs((1, tq, hd), lambda h, i: (h, i, 0))
    full_t = _bs((1, hd, lp), lambda h, i: (h, 0, 0))
    hbm = pl.BlockSpec(memory_space=pltpu.HBM)
    return _pc(
        body, "sb_bwd", (nh, lp // tq),
        [tile_spec, full_t, full_t, hbm, hbm, tile_spec],
        [tile_spec, full_t, full_t], [_sds((nh, lp, hd)), _sds((nh, hd, lp)), _sds((nh, hd, lp))],
        scratch=[pltpu.VMEM((2, grp, tq, tk), MXU_DTYPE), pltpu.VMEM((2, grp, tq, tk), MXU_DTYPE),
                 pltpu.SemaphoreType.DMA((2, 2, grp))])(q, kt, vt, w_saved, s_saved, do)


def _dn_pre_tile(xs, ba, cw, pv):
    conv = xs[0] * cw[3:4] + xs[1] * cw[2:3] + xs[2] * cw[1:2] + xs[3] * cw[0:1]
    s = _silu(conv)
    bd = _block_diag_ones(QW, HEAD_DIM)
    sq, sk, sv = s[:, :QW], s[:, QW:2 * QW], s[:, 2 * QW:]
    qn = sq * lax.rsqrt(_dotx(sq * sq, bd) + EPS)
    kn = sk * lax.rsqrt(_dotx(sk * sk, bd) + EPS)
    lane = _iota2(ba.shape, 1)
    beta = _sigmoid(ba)
    g = -jnp.exp(pv[0:1]) * _softplus(ba + pv[1:2])
    bg = jnp.where(lane < N_HEADS, beta, jnp.where(lane < 2 * N_HEADS, g, 0.0))
    return qn, kn, sv, bg


def _dn_shifted(cur, prev, first):
    row = _iota2((cur.shape[0], 1), 0)
    out = [cur]
    for k in range(1, DN_CONV):
        head_rows = jnp.where(first, 0.0, pltpu.roll(prev, k, 0))
        out.append(jnp.where(row >= k, pltpu.roll(cur, k, 0), head_rows))
    return tuple(out)


def _dn_pre_fwd(x, ba, cw, pv):
    lp, w3 = x.shape
    tm = _token_tile(lp, 320)

    def body(x_ref, xp_ref, ba_ref, cw_ref, pv_ref, q_ref, k_ref, v_ref, bg_ref):
        xs = _dn_shifted(x_ref[...], xp_ref[...], pl.program_id(0) == 0)
        qn, kn, sv, bg = _dn_pre_tile(xs, ba_ref[...], cw_ref[...], pv_ref[...])
        q_ref[...], k_ref[...], v_ref[...], bg_ref[...] = qn, kn, sv, bg

    tok = lambda wd: _bs((tm, wd), lambda i: (i, 0))
    return _pc(
        body, "dn_pre_fwd", (lp // tm,),
        [tok(w3), _bs((tm, w3), lambda i: (jnp.maximum(i - 1, 0), 0)), tok(128),
         _bs((8, w3), lambda i: (0, 0)), _bs((8, 128), lambda i: (0, 0))],
        [tok(QW), tok(QW), tok(QW), tok(128)],
        [_sds((lp, QW)), _sds((lp, QW)), _sds((lp, QW)), _sds((lp, 128))])(x, x, ba, cw, pv)


def _dn_pre_bwd(x, ba, cw, pv, dq, dk, dv, dbg):
    lp, w3 = x.shape
    tm = _token_tile(lp, 320)

    def body(x_ref, xp_ref, ba_ref, cw_ref, pv_ref, dq_ref, dk_ref, dv_ref, dbg_ref, dxs_ref, dba_ref, dcw_ref, dpv_ref):
        @pl.when(pl.program_id(0) == 0)
        def _():
            dcw_ref[...] = jnp.zeros_like(dcw_ref)
            dpv_ref[...] = jnp.zeros_like(dpv_ref)

        xs = _dn_shifted(x_ref[...], xp_ref[...], pl.program_id(0) == 0)
        _, vjp = jax.vjp(_dn_pre_tile, xs, ba_ref[...], cw_ref[...], pv_ref[...])
        dxs, dba, dcw, dpv = vjp((dq_ref[...], dk_ref[...], dv_ref[...], dbg_ref[...]))
        for k in range(DN_CONV):
            dxs_ref[k] = dxs[k]
        dba_ref[...] = dba
        dcw_ref[...] += dcw
        dpv_ref[...] += dpv

    tok = lambda wd: _bs((tm, wd), lambda i: (i, 0))
    cw_spec, pv_spec = _bs((8, w3), lambda i: (0, 0)), _bs((8, 128), lambda i: (0, 0))
    return _pc(
        body, "dn_pre_bwd", (lp // tm,),
        [tok(w3), _bs((tm, w3), lambda i: (jnp.maximum(i - 1, 0), 0)), tok(128), cw_spec, pv_spec,
         tok(QW), tok(QW), tok(QW), tok(128)],
        [_bs((4, tm, w3), lambda i: (0, i, 0)), tok(128), cw_spec, pv_spec],
        [_sds((4, lp, w3)), _sds((lp, 128)), _sds((8, w3)), _sds((8, 128))])(x, x, ba, cw, pv, dq, dk, dv, dbg)


def _dn_chunk(state, q, k, v, grow, brow):
    nh, c, _ = q.shape
    ii, jj = _iota2((c, c), 0), _iota2((c, c), 1)
    eye = ii == jj
    col = lambda row: jnp.sum(jnp.where(eye, jnp.broadcast_to(row, (nh, c, c)), 0.0), axis=2, keepdims=True)
    gc_row = _dotx(grow, jnp.broadcast_to((ii <= jj).astype(f32), (nh, c, c)), BNN)
    gc_col, b_col = col(gc_row), col(brow)
    decay = jnp.exp(jnp.where(ii >= jj, gc_col - gc_row, -1e30))
    kb = k * b_col
    p = -jnp.where(ii > jj, _bmm_nt(kb, k) * decay, 0.0)
    t_inv = eye.astype(f32) + p
    for _ in range(5):
        p = _bmm3(p, p)
        t_inv = t_inv + _bmm3(t_inv, p)
    egc = jnp.exp(gc_col)
    u = _bmm(t_inv, v * b_col)
    w = _bmm(t_inv, kb * egc)
    qs = q * (q.shape[2] ** -0.5)
    attn = jnp.where(ii >= jj, _bmm_nt(qs, k) * decay, 0.0)
    v_new = u - _bmm(w, state)
    o = _bmm(qs * egc, state) + _bmm(attn, v_new)
    g_last = gc_row[:, :, c - 1:c]
    new_state = state * jnp.exp(g_last) + _bmm_tn(k * jnp.exp(g_last - gc_col), v_new)
    return new_state, o


def _dn_scan_fwd(q, k, v, grow, brow):
    nh, lp, hd = q.shape
    c = DN_CHUNK
    n = lp // c

    def body(q_ref, k_ref, v_ref, g_ref, b_ref, o_ref, st_ref, state_s):
        @pl.when(pl.program_id(0) == 0)
        def _():
            state_s[...] = jnp.zeros_like(state_s)

        st_ref[:, 0] = state_s[...]
        state, o = _dn_chunk(state_s[...], q_ref[...], k_ref[...], v_ref[...], g_ref[:, 0], b_ref[:, 0])
        state_s[...] = state
        o_ref[...] = o

    seq = _bs((nh, c, hd), lambda i: (0, i, 0))
    row = _bs((nh, 1, 1, c), lambda i: (0, i, 0, 0))
    return _pc(body, "dn_scan_fwd", (n,), [seq, seq, seq, row, row],
               [seq, _bs((nh, 1, hd, hd), lambda i: (0, i, 0, 0))],
               [_sds((nh, lp, hd)), _sds((nh, n, hd, hd))],
               scratch=[pltpu.VMEM((nh, hd, hd), f32)])(q, k, v, grow, brow)


def _dn_scan_bwd(q, k, v, grow, brow, states, do):
    nh, lp, hd = q.shape
    c = DN_CHUNK
    n = lp // c

    def body(q_ref, k_ref, v_ref, g_ref, b_ref, st_ref, do_ref, dq_ref, dk_ref, dv_ref, dg_ref, db_ref, dstate_s):
        @pl.when(pl.program_id(0) == 0)
        def _():
            dstate_s[...] = jnp.zeros_like(dstate_s)

        _, vjp = jax.vjp(_dn_chunk, st_ref[:, 0], q_ref[...], k_ref[...], v_ref[...], g_ref[:, 0], b_ref[:, 0])
        dstate, dq, dk, dv, dg, db = vjp((dstate_s[...], do_ref[...]))
        dstate_s[...] = dstate
        dq_ref[...], dk_ref[...], dv_ref[...] = dq, dk, dv
        dg_ref[:, 0], db_ref[:, 0] = dg, db

    seq = _bs((nh, c, hd), lambda i: (0, n - 1 - i, 0))
    row = _bs((nh, 1, 1, c), lambda i: (0, n - 1 - i, 0, 0))
    return _pc(body, "dn_scan_bwd", (n,),
               [seq, seq, seq, row, row, _bs((nh, 1, hd, hd), lambda i: (0, n - 1 - i, 0, 0)), seq],
               [seq, seq, seq, row, row],
               [_sds((nh, lp, hd))] * 3 + [_sds((nh, n, 1, c))] * 2,
               scratch=[pltpu.VMEM((nh, hd, hd), f32)])(q, k, v, grow, brow, states, do)


def _s5_prep(a_re, a_im, log_dt, b_re, b_im):
    dt = jnp.exp(log_dt)
    mag = jnp.exp(a_re * dt)
    ar, ai = mag * jnp.cos(a_im * dt), mag * jnp.sin(a_im * dt)
    den = a_re * a_re + a_im * a_im
    cr = ((ar - 1.0) * a_re + ai * a_im) / den
    ci = (ai * a_re - (ar - 1.0) * a_im) / den
    cr3, ci3 = cr[:, None, :], ci[:, None, :]
    return ar, ai, cr3 * b_re - ci3 * b_im, cr3 * b_im + ci3 * b_re


def _s5_prep_fwd(a_re, a_im, log_dt, b_re, b_im):
    def body(ar_ref, ai_ref, dt_ref, br_ref, bi_ref, *outs):
        for ref, val in zip(outs, _s5_prep(ar_ref[...], ai_ref[...], dt_ref[...], br_ref[...], bi_ref[...])):
            ref[...] = val

    return pl.pallas_call(body, name="s5_prep_fwd",
                          out_shape=[_sds(a_re.shape), _sds(a_re.shape), _sds(b_re.shape), _sds(b_re.shape)],
                          )(a_re, a_im, log_dt, b_re, b_im)


def _s5_prep_bwd(a_re, a_im, log_dt, b_re, b_im, d_ar, d_ai, d_br, d_bi):
    def body(ar_ref, ai_ref, dt_ref, br_ref, bi_ref, g0, g1, g2, g3, *outs):
        _, vjp = jax.vjp(_s5_prep, ar_ref[...], ai_ref[...], dt_ref[...], br_ref[...], bi_ref[...])
        for ref, val in zip(outs, vjp((g0[...], g1[...], g2[...], g3[...]))):
            ref[...] = val

    return pl.pallas_call(body, name="s5_prep_bwd",
                          out_shape=[_sds(a_re.shape), _sds(a_re.shape), _sds(log_dt.shape), _sds(b_re.shape),
                                     _sds(b_re.shape)])(a_re, a_im, log_dt, b_re, b_im, d_ar, d_ai, d_br, d_bi)


def _s5_block_len(lp):
    return 128 if lp % 128 == 0 else 64


def _s5_powers(ar, ai, tb):
    out = []
    k = 1
    while k < tb:
        out.append((ar, ai))
        ar, ai = ar * ar - ai * ai, 2.0 * ar * ai
        k *= 2
    return out


def _s5_scan_rows(xr, xi, pows, reverse, period=None):
    tb = xr.shape[0]
    span = tb if period is None else period
    row = _iota2((tb, 1), 0)
    if period is not None:
        row = jnp.bitwise_and(row, period - 1)
    k = 1
    for pr, pi in pows:
        if reverse:
            keep = row < span - k
            sr, si = pltpu.roll(xr, tb - k, 0), pltpu.roll(xi, tb - k, 0)
        else:
            keep = row >= k
            sr, si = pltpu.roll(xr, k, 0), pltpu.roll(xi, k, 0)
        sr, si = jnp.where(keep, sr, 0.0), jnp.where(keep, si, 0.0)
        xr, xi = xr + pr * sr - pi * si, xi + pr * si + pi * sr
        k *= 2
    return xr, xi


def _s5_slab_mm(x, w_ref, dims=NN):
    a = x.shape[1] // S5_SLABS
    return jnp.concatenate([_dot(x[:, j * a:(j + 1) * a], w_ref[j], dims) for j in range(S5_SLABS)], axis=1)


def _s5_power_table(ar, ai, tb, reverse):
    at = _iota2((tb, 1), 0) == (tb - 1 if reverse else 0)
    return _s5_scan_rows(jnp.where(at, ar, 0.0), jnp.where(at, ai, 0.0), _s5_powers(ar, ai, tb), reverse)


_S5_GROUP = 8


def _s5_scan_block(xr, xi, ar, ai, carry_r, carry_i, reverse):
    tb = xr.shape[0]
    lr, li = _s5_scan_rows(xr, xi, _s5_powers(ar, ai, _S5_GROUP), reverse, period=_S5_GROUP)
    wr, wi = _s5_power_table(ar, ai, _S5_GROUP, reverse)
    n = tb // _S5_GROUP
    out_r, out_i = [None] * n, [None] * n
    for g in (range(n - 1, -1, -1) if reverse else range(n)):
        rs = slice(_S5_GROUP * g, _S5_GROUP * (g + 1))
        sr = lr[rs] + wr * carry_r - wi * carry_i
        si = li[rs] + wr * carry_i + wi * carry_r
        out_r[g], out_i[g] = sr, si
        edge = slice(0, 1) if reverse else slice(_S5_GROUP - 1, _S5_GROUP)
        carry_r, carry_i = sr[edge], si[edge]
    return jnp.concatenate(out_r, axis=0), jnp.concatenate(out_i, axis=0), carry_r, carry_i


def _s5_scan_fwd(u, ar, ai, b8r, b8i, c8r, c8i, dvec):
    lp = u.shape[0]
    tb = _s5_block_len(lp)
    nblk = lp // tb

    def body(u_ref, ar_ref, ai_ref, b8r_ref, b8i_ref, c8r_ref, c8i_ref, d_ref, y_ref, cin_ref, carry_s):
        @pl.when(pl.program_id(0) == 0)
        def _():
            carry_s[...] = jnp.zeros_like(carry_s)

        cin_ref[0] = carry_s[...]
        uv = u_ref[...]
        sr, si, out_r, out_i = _s5_scan_block(_s5_slab_mm(uv, b8r_ref), _s5_slab_mm(uv, b8i_ref), ar_ref[...],
                                              ai_ref[...], carry_s[0:1], carry_s[1:2], False)
        carry_s[0:1] = out_r
        carry_s[1:2] = out_i
        y_ref[...] = _s5_slab_mm(sr, c8r_ref) - _s5_slab_mm(si, c8i_ref) + d_ref[...] * uv

    const = lambda shape: _bs(shape, lambda i: (0,) * len(shape))
    return _pc(
        body, "s5_scan_fwd", (nblk,),
        [_bs((tb, S5_W), lambda i: (i, 0)), const((1, S5_N)), const((1, S5_N)),
         const((S5_SLABS, 128, 512)), const((S5_SLABS, 128, 512)),
         const((S5_SLABS, 512, 128)), const((S5_SLABS, 512, 128)), const((1, S5_W))],
        [_bs((tb, S5_W), lambda i: (i, 0)), _bs((1, 8, S5_N), lambda i: (i, 0, 0))],
        [_sds((lp, S5_W)), _sds((nblk, 8, S5_N))],
        scratch=[pltpu.VMEM((8, S5_N), f32)])(u, ar, ai, b8r, b8i, c8r, c8i, dvec)


def _s5_scan_bwd(u, dy, cin, ar, ai, b8r, b8i, c8r, c8i, dvec):
    lp = u.shape[0]
    tb = _s5_block_len(lp)
    nblk = lp // tb

    def body(u_ref, dy_ref, cin_ref, ar_ref, ai_ref, b8r_ref, b8i_ref, c8r_ref, c8i_ref, d_ref,
             du_ref, dab_ref, db8r_ref, db8i_ref, dc8r_ref, dc8i_ref, dd_ref, lam_s):
        @pl.when(pl.program_id(0) == 0)
        def _():
            lam_s[...] = jnp.zeros_like(lam_s)
            for ref in (dab_ref, db8r_ref, db8i_ref, dc8r_ref, dc8i_ref, dd_ref):
                ref[...] = jnp.zeros_like(ref)

        uv, dyv = u_ref[...], dy_ref[...]
        a_r, a_i = ar_ref[...], ai_ref[...]
        cin_r, cin_i = cin_ref[0, 0:1], cin_ref[0, 1:2]
        sr, si, _, _ = _s5_scan_block(_s5_slab_mm(uv, b8r_ref), _s5_slab_mm(uv, b8i_ref), a_r, a_i, cin_r, cin_i, False)
        lr, li, top_r, top_i = _s5_scan_block(_s5_slab_mm(dyv, c8r_ref, NT), -_s5_slab_mm(dyv, c8i_ref, NT),
                                              a_r, -a_i, lam_s[0:1], lam_s[1:2], True)
        lam_s[0:1] = top_r
        lam_s[1:2] = top_i
        first = _iota2((tb, 1), 0) == 0
        pr = jnp.where(first, cin_r, pltpu.roll(sr, 1, 0))
        pi = jnp.where(first, cin_i, pltpu.roll(si, 1, 0))
        dab_ref[0:1] += jnp.sum(lr * pr + li * pi, axis=0, keepdims=True)
        dab_ref[1:2] += jnp.sum(li * pr - lr * pi, axis=0, keepdims=True)
        du_ref[...] = _s5_slab_mm(lr, b8r_ref, NT) + _s5_slab_mm(li, b8i_ref, NT) + d_ref[...] * dyv
        dd_ref[...] += jnp.sum(dyv * uv, axis=0, keepdims=True)
        for j in range(S5_SLABS):
            us, dys = uv[:, j * 128:(j + 1) * 128], dyv[:, j * 128:(j + 1) * 128]
            st = slice(j * 512, (j + 1) * 512)
            db8r_ref[j] += _dot(us, lr[:, st], TN)
            db8i_ref[j] += _dot(us, li[:, st], TN)
            dc8r_ref[j] += _dot(sr[:, st], dys, TN)
            dc8i_ref[j] -= _dot(si[:, st], dys, TN)

    const = lambda shape: _bs(shape, lambda i: (0,) * len(shape))
    rev = _bs((tb, S5_W), lambda i: (nblk - 1 - i, 0))
    return _pc(
        body, "s5_scan_bwd", (nblk,),
        [rev, rev, _bs((1, 8, S5_N), lambda i: (nblk - 1 - i, 0, 0)), const((1, S5_N)), const((1, S5_N)),
         const((S5_SLABS, 128, 512)), const((S5_SLABS, 128, 512)),
         const((S5_SLABS, 512, 128)), const((S5_SLABS, 512, 128)), const((1, S5_W))],
        [rev, const((8, S5_N)), const((S5_SLABS, 128, 512)), const((S5_SLABS, 128, 512)),
         const((S5_SLABS, 512, 128)), const((S5_SLABS, 512, 128)), const((1, S5_W))],
        [_sds((lp, S5_W)), _sds((8, S5_N)), _sds((S5_SLABS, 128, 512)), _sds((S5_SLABS, 128, 512)),
         _sds((S5_SLABS, 512, 128)), _sds((S5_SLABS, 512, 128)), _sds((1, S5_W))],
        scratch=[pltpu.VMEM((8, S5_N), f32)])(u, dy, cin, ar, ai, b8r, b8i, c8r, c8i, dvec)


def _mix_tile(osb, odn, z, ys5, g_sb, g_dn, w_glu, b_glu, g_s5):
    bd = _block_diag_ones(QW, HEAD_DIM)
    tile4 = ((_iota2((HEAD_DIM, QW), 1) % HEAD_DIM) == _iota2((HEAD_DIM, QW), 0)).astype(f32)
    seg_rms = lambda x: x * lax.rsqrt(_dotx(x * x, bd) * (1.0 / HEAD_DIM) + EPS)
    sbn = seg_rms(osb) * _dotx(g_sb, tile4)
    dnn = seg_rms(odn) * _dotx(g_dn, tile4) * _silu(z)
    y = _gelu_tanh(ys5)
    glu = y * _sigmoid(_mm(y, w_glu) + b_glu)
    return jnp.concatenate([sbn, dnn, _rms(glu, g_s5)], axis=1)


def _mixout_fwd(h, osb, odn, z, ys5, g_sb, g_dn, w_glu, b_glu, g_s5, w_out):
    lp, d = h.shape
    tm = _token_tile(lp)

    def body(h_ref, osb_ref, odn_ref, z_ref, ys_ref, gsb_ref, gdn_ref, wg_ref, bg_ref, gs5_ref, wo_ref, o_ref):
        mixed = _mix_tile(osb_ref[...], odn_ref[...], z_ref[...], ys_ref[...], gsb_ref[...], gdn_ref[...],
                          wg_ref[...], bg_ref[...], gs5_ref[...])
        o_ref[...] = h_ref[...] + _dot(mixed, wo_ref[...])

    tok = lambda wd: _bs((tm, wd), lambda i: (i, 0))
    const = lambda shape: _bs(shape, lambda i: (0,) * len(shape))
    return _pc(
        body, "mixout_fwd", (lp // tm,),
        [tok(d), tok(QW), tok(QW), tok(QW), tok(S5_W), const((1, HEAD_DIM)), const((1, HEAD_DIM)),
         const((S5_W, S5_W)), const((1, S5_W)), const((1, S5_W)), const((d, d))],
        tok(d), _sds((lp, d)))(h, osb, odn, z, ys5, g_sb, g_dn, w_glu, b_glu, g_s5, w_out)


def _mixout_bwd(dh, osb, odn, z, ys5, g_sb, g_dn, w_glu, b_glu, g_s5, w_out):
    lp, d = dh.shape
    tm = _token_tile(lp)

    def body(dh_ref, osb_ref, odn_ref, z_ref, ys_ref, gsb_ref, gdn_ref, wg_ref, bg_ref, gs5_ref, wo_ref,
             dosb_ref, dodn_ref, dz_ref, dys_ref, dgsb_ref, dgdn_ref, dwg_ref, dbg_ref, dgs5_ref, dwo_ref):
        accs = (dgsb_ref, dgdn_ref, dwg_ref, dbg_ref, dgs5_ref)

        @pl.when(pl.program_id(0) == 0)
        def _():
            for ref in accs + (dwo_ref,):
                ref[...] = jnp.zeros_like(ref)

        mixed, vjp = jax.vjp(_mix_tile, osb_ref[...], odn_ref[...], z_ref[...], ys_ref[...], gsb_ref[...],
                             gdn_ref[...], wg_ref[...], bg_ref[...], gs5_ref[...])
        dhv = dh_ref[...]
        dwo_ref[...] += _dot(mixed, dhv, TN)
        grads = vjp(_dot(dhv, wo_ref[...], NT))
        for ref, val in zip((dosb_ref, dodn_ref, dz_ref, dys_ref), grads[:4]):
            ref[...] = val
        for ref, val in zip(accs, grads[4:]):
            ref[...] += val

    tok = lambda wd: _bs((tm, wd), lambda i: (i, 0))
    const = lambda shape: _bs(shape, lambda i: (0,) * len(shape))
    params = [const((1, HEAD_DIM)), const((1, HEAD_DIM)), const((S5_W, S5_W)), const((1, S5_W)), const((1, S5_W))]
    return _pc(
        body, "mixout_bwd", (lp // tm,),
        [tok(d), tok(QW), tok(QW), tok(QW), tok(S5_W)] + params + [const((d, d))],
        [tok(QW), tok(QW), tok(QW), tok(S5_W)] + params + [const((d, d))],
        [_sds((lp, QW))] * 3 + [_sds((lp, S5_W)), _sds((1, HEAD_DIM)), _sds((1, HEAD_DIM)), _sds((S5_W, S5_W)),
                                _sds((1, S5_W)), _sds((1, S5_W)), _sds((d, d))],
    )(dh, osb, odn, z, ys5, g_sb, g_dn, w_glu, b_glu, g_s5, w_out)


def _loss_fwd_bwd(h, g, target, n_real):
    lp, d = h.shape
    tm = _token_tile(lp)

    def body(h_ref, g_ref, t_ref, loss_ref, dh_ref, dg_ref):
        i = pl.program_id(0)

        @pl.when(i == 0)
        def _():
            loss_ref[...] = jnp.zeros_like(loss_ref)
            dg_ref[...] = jnp.zeros_like(dg_ref)

        pos = i * tm + _iota2((tm, 1), 0)
        real = ((pos >= N_META) & (pos < n_real)).astype(f32)
        y, vjp = jax.vjp(_rms, h_ref[...], g_ref[...])
        err = (y - t_ref[...]) * real
        loss_ref[...] += 0.5 * jnp.sum(jnp.mean(err * err, axis=1, keepdims=True))
        dx, dg = vjp(err * (1.0 / d))
        dh_ref[...] = dx
        dg_ref[...] += dg

    tok = _bs((tm, d), lambda i: (i, 0))
    return _pc(body, "loss_fwd_bwd", (lp // tm,), [tok, _bs((1, d), lambda i: (0, 0)), tok],
               [_bs((8, 128), lambda i: (0, 0)), tok, _bs((1, d), lambda i: (0, 0))],
               [_sds((8, 128)), _sds((lp, d)), _sds((1, d))])(h, g, target)


def _row_tile(rows):
    for t in (256, 128, 64, 32, 16, 8):
        if rows % t == 0:
            return t
    raise ValueError(rows)


def _adamw(w, g, m, v):
    shape = w.shape
    w, g, m, v = (a.reshape(-1, shape[-1]) for a in (w, g, m, v))
    rows, width = w.shape
    tr = _row_tile(rows)

    def body(w_ref, g_ref, m_ref, v_ref, d_ref, mo_ref, vo_ref):
        gv = g_ref[...]
        m_new = ADAM_B1 * m_ref[...] + (1.0 - ADAM_B1) * gv
        v_new = ADAM_B2 * v_ref[...] + (1.0 - ADAM_B2) * (gv * gv)
        m_hat = m_new / (1.0 - ADAM_B1 ** ADAM_STEP)
        v_hat = v_new / (1.0 - ADAM_B2 ** ADAM_STEP)
        d_ref[...] = -ADAM_LR * (m_hat / (jnp.sqrt(v_hat) + ADAM_EPS) + ADAM_WD * w_ref[...])
        mo_ref[...] = m_new
        vo_ref[...] = v_new

    blk = _bs((tr, width), lambda i: (i, 0))
    outs = _pc(body, "adamw", (rows // tr,), [blk] * 4, [blk] * 3, [_sds(w.shape)] * 3)(w, g, m, v)
    return tuple(o.reshape(shape) for o in outs)


def _sum_leading(x, name):
    n, rows, _ = x.shape
    tr = _row_tile(rows)

    def body(x_ref, o_ref):
        acc = x_ref[0]
        for k in range(1, n):
            acc = acc + x_ref[k]
        o_ref[...] = acc

    return _pc(body, name, (rows // tr,), [_bs((n, tr, FLAT_W), lambda i: (0, i, 0))],
               _bs((tr, FLAT_W), lambda i: (i, 0)), _sds((rows, FLAT_W)))(x)


def _pair_add(a, b, name):
    rows = a.shape[0]
    tr = _row_tile(rows)

    def body(a_ref, b_ref, o_ref):
        o_ref[...] = a_ref[...] + b_ref[...]

    blk = _bs((tr, FLAT_W), lambda i: (i, 0))
    return _pc(body, name, (rows // tr,), [blk, blk], blk, _sds(a.shape))(a, b)


_CHIP_FLIPS = ((1, 0, 0), (0, 1, 0), (1, 1, 0))
_ALL_FLIPS = tuple((a, b, c) for a in (0, 1) for b in (0, 1) for c in (0, 1) if (a, b, c) != (0, 0, 0))
_CORE_FLIP = ((0, 0, 1),)
_D2D_STREAMS = 4


def _exchange(name, arrays, out_shapes, flips, plan):
    n_in = len(arrays)

    def body(*refs):
        ins, outs = refs[:n_in], refs[n_in:n_in + len(out_shapes)]
        send_sems, recv_sems, local_sems = refs[n_in + len(out_shapes):]
        me = (lax.axis_index("x"), lax.axis_index("y"), lax.axis_index("c"))
        local = [pltpu.make_async_copy(s, d, local_sems.at[n]) for n, (s, d) in enumerate(plan(me, None, ins, outs))]
        for cp in local:
            cp.start()
        sent, k = [], 0
        for f in flips:
            peer = tuple(1 - m if fl else m for m, fl in zip(me, f))
            for s, d in plan(me, peer, ins, outs):
                cp = pltpu.make_async_remote_copy(src_ref=s, dst_ref=d, send_sem=send_sems.at[k],
                                                  recv_sem=recv_sems.at[k], device_id=peer,
                                                  device_id_type=pl.DeviceIdType.MESH)
                cp.start()
                sent.append(cp)
                k += 1
        for cp in sent:
            cp.wait_recv()
        for cp in sent:
            cp.wait_send()
        for cp in local:
            cp.wait()

    me0 = (0, 0, 0)
    n_remote = sum(len(_plan_count(plan, me0, f, arrays, out_shapes)) for f in flips)
    n_local = len(_plan_count(plan, me0, None, arrays, out_shapes))
    hbm = pl.BlockSpec(memory_space=pltpu.HBM)
    return pl.pallas_call(
        body, name=name, in_specs=[hbm] * n_in, out_specs=[hbm] * len(out_shapes), out_shape=list(out_shapes),
        scratch_shapes=[pltpu.SemaphoreType.DMA((n_remote,)), pltpu.SemaphoreType.DMA((n_remote,)),
                        pltpu.SemaphoreType.DMA((max(n_local, 1),))],
        compiler_params=pltpu.CompilerParams(has_side_effects=True))(*arrays)


class _FakeRef:
    def __init__(self):
        self.at = self

    def __getitem__(self, idx):
        return self


def _plan_count(plan, me, flip, arrays, out_shapes):
    peer = None if flip is None else me
    return plan(me, peer, [_FakeRef() for _ in arrays], [_FakeRef() for _ in out_shapes])


def _chip_index(dev):
    return 2 * dev[0] + dev[1]


def _all_gather_chips(name, arrays):
    n = len(arrays)

    def body(*refs):
        ins, outs = refs[:n], refs[n:2 * n]
        send_sems, recv_sems, local_sems = refs[2 * n:]
        x, y, c = lax.axis_index("x"), lax.axis_index("y"), lax.axis_index("c")
        sibling = (x, y, 1 - c)
        chips = [(1 - x, y), (x, 1 - y), (1 - x, 1 - y)]
        mine = 2 * x + y

        def copy(k, src, dst, to):
            return pltpu.make_async_remote_copy(src_ref=src, dst_ref=dst, send_sem=send_sems.at[k],
                                                recv_sem=recv_sems.at[k], device_id=to,
                                                device_id_type=pl.DeviceIdType.MESH)

        local = [pltpu.make_async_copy(ins[a], outs[a].at[mine], local_sems.at[a]) for a in range(n)]
        for cp in local:
            cp.start()
        first = [copy(j * n + a, ins[a].at[c], outs[a].at[mine, c], (*chip, c))
                 for j, chip in enumerate(chips) for a in range(n)]
        for cp in first:
            cp.start()
        passed = []
        for j, chip in enumerate(chips):
            for a in range(n):
                landed = outs[a].at[_chip_index(chip), c]
                copy(j * n + a, landed, landed, sibling).wait_recv()
                cp = copy(3 * n + j * n + a, landed, landed, sibling)
                cp.start()
                passed.append(cp)
        for j, chip in enumerate(chips):
            for a in range(n):
                other = outs[a].at[_chip_index(chip), 1 - c]
                copy(3 * n + j * n + a, other, other, sibling).wait_recv()
        for cp in first + passed:
            cp.wait_send()
        for cp in local:
            cp.wait()

    hbm = pl.BlockSpec(memory_space=pltpu.HBM)
    return pl.pallas_call(
        body, name=name, in_specs=[hbm] * n, out_specs=[hbm] * n,
        out_shape=[_sds((N_CHIPS,) + a.shape, a.dtype) for a in arrays],
        scratch_shapes=[pltpu.SemaphoreType.DMA((6 * n,)), pltpu.SemaphoreType.DMA((6 * n,)),
                        pltpu.SemaphoreType.DMA((n,))],
        compiler_params=pltpu.CompilerParams(has_side_effects=True))(*arrays)


def _all_gather_devices(name, arr):
    def plan(me, peer, ins, outs):
        return [(ins[0], outs[0].at[4 * me[0] + 2 * me[1] + me[2]])]

    return _exchange(name, [arr], [_sds((8,) + arr.shape, arr.dtype)], _ALL_FLIPS, plan)[0]


def _swap_half_with_sibling(name, g):
    step = g.shape[2] // _D2D_STREAMS

    def plan(me, peer, ins, outs):
        if peer is None:
            return []
        return [(ins[0].at[k, 1 - me[2], pl.ds(r * step, step)], outs[0].at[k, pl.ds(r * step, step)])
                for k in range(N_CHIPS) for r in range(_D2D_STREAMS)]

    return _exchange(name, [g], [_sds((N_CHIPS,) + g.shape[2:], g.dtype)], _CORE_FLIP, plan)[0]


def _scatter_to_chips(name, s):
    def plan(me, peer, ins, outs):
        to = me if peer is None else peer
        return [(ins[0].at[_chip_index(to)], outs[0].at[_chip_index(me)])]

    return _exchange(name, [s], [_sds(s.shape, s.dtype)], _CHIP_FLIPS, plan)[0]


def _share_with_sibling(name, r):
    step = r.shape[0] // (4 * _D2D_STREAMS)

    def plan(me, peer, ins, outs):
        return [(ins[0].at[pl.ds(n * step, step)], outs[0].at[me[2], pl.ds(n * step, step)])
                for n in range(4 * _D2D_STREAMS)]

    return _exchange(name, [r], [_sds((2,) + r.shape, r.dtype)], _CORE_FLIP, plan)[0]


def _to_heads(x):
    return x.reshape(x.shape[0], N_HEADS, HEAD_DIM).transpose(1, 0, 2)


def _from_heads(x):
    return x.transpose(1, 0, 2).reshape(x.shape[1], QW)


def _to_heads_t(x):
    return x.T.reshape(N_HEADS, HEAD_DIM, x.shape[0])


def _from_heads_t(x):
    return x.reshape(QW, x.shape[2]).T


def _reorder_w_in(w):
    o = 3 * QW + 3 * QW + QW
    main = jnp.concatenate([w[:, :o], w[:, o + 2 * N_HEADS:]], axis=1)
    ba = jnp.pad(w[:, o:o + 2 * N_HEADS], ((0, 0), (0, 128 - 2 * N_HEADS)))
    return jnp.concatenate([main, ba], axis=1)


def _restore_w_in(w):
    o = 3 * QW + 3 * QW + QW
    return jnp.concatenate([w[:, :o], w[:, PROJ_W - 128:PROJ_W - 128 + 2 * N_HEADS], w[:, o:PROJ_W - 128]], axis=1)


def _slab_embed_b(b):
    x = b.reshape(S5_SLABS, 8, S5_C, S5_P)
    eye = jnp.eye(8, dtype=b.dtype)
    return (x[:, :, :, None, :] * eye[None, :, None, :, None]).reshape(S5_SLABS, 8 * S5_C, 8 * S5_P)


def _slab_extract_b(m):
    x = m.reshape(S5_SLABS, 8, S5_C, 8, S5_P)
    return jnp.stack([x[:, g, :, g, :] for g in range(8)], axis=1).reshape(S5_G, S5_C, S5_P)


def _slab_embed_c(c):
    x = c.reshape(S5_SLABS, 8, S5_C, S5_P).transpose(0, 1, 3, 2)
    eye = jnp.eye(8, dtype=c.dtype)
    return (x[:, :, :, None, :] * eye[None, :, None, :, None]).reshape(S5_SLABS, 8 * S5_P, 8 * S5_C)


def _slab_extract_c(m):
    x = m.reshape(S5_SLABS, 8, S5_P, 8, S5_C)
    return jnp.stack([x[:, g, :, g, :] for g in range(8)], axis=1).transpose(0, 1, 3, 2).reshape(S5_G, S5_C, S5_P)


def _piece_rows(shape):
    return -(-math.prod(shape) // (8 * FLAT_W)) * 8


def _pack_rows(parts, lead, row_align):
    rows = []
    for p in parts:
        flat = p.reshape(lead + (-1,))
        r = _piece_rows(p.shape[len(lead):])
        flat = jnp.pad(flat, [(0, 0)] * len(lead) + [(0, r * FLAT_W - flat.shape[-1])])
        rows.append(flat.reshape(lead + (r, FLAT_W)))
    total = sum(r.shape[-2] for r in rows)
    rows.append(jnp.zeros(lead + ((-total) % row_align, FLAT_W), parts[0].dtype))
    return jnp.concatenate(rows, axis=len(lead))


def _unpack_rows(flat, shapes):
    out, off = [], 0
    for s in shapes:
        r = _piece_rows(s)
        out.append(flat[off:off + r].reshape(-1)[:math.prod(s)].reshape(s))
        off += r
    return out


_SHARDED = ("ffn1_w_gate", "ffn1_w_up", "ffn1_w_down", "w_in", "s5_w_glu", "w_out",
            "ffn2_w_gate", "ffn2_w_up", "ffn2_w_down", "meta_tokens", "dn_conv_w")
_MATMUL_W = _SHARDED[:9]
_WEIGHTS = ("meta_tokens", "ffn1_norm", "ffn1_w_gate", "ffn1_w_up", "ffn1_w_down", "mix_norm", "w_in", "sb_out_norm",
            "dn_conv_w", "dn_a_log", "dn_dt_bias", "dn_out_norm", "s5_a_re", "s5_a_im", "s5_log_dt", "s5_b_re",
            "s5_b_im", "s5_c_re", "s5_c_im", "s5_d", "s5_w_glu", "s5_b_glu", "s5_out_norm", "w_out", "ffn2_norm",
            "ffn2_w_gate", "ffn2_w_up", "ffn2_w_down", "final_norm")
_REPLICATED = tuple(n for n in _WEIGHTS if n not in _SHARDED)


def _chip_major(name, g):
    if name in ("ffn1_w_gate", "ffn1_w_up", "ffn2_w_gate", "ffn2_w_up", "ffn1_w_down", "ffn2_w_down"):
        return g.transpose(1, 0, 2, 3)
    if name == "w_in":
        g = g.reshape(2, D_MODEL, N_CHIPS, IN_WIDTH // N_CHIPS).transpose(2, 0, 1, 3)
        return jnp.pad(g, ((0, 0), (0, 0), (0, 0), (0, W_IN_PACKED - IN_WIDTH // N_CHIPS)))
    if name in ("w_out", "s5_w_glu"):
        return g.reshape(2, N_CHIPS, g.shape[1] // N_CHIPS, g.shape[2]).transpose(1, 0, 2, 3)
    if name == "meta_tokens":
        return g.reshape(N_META, N_CHIPS, D_MODEL // N_CHIPS).transpose(1, 0, 2)
    if name == "dn_conv_w":
        return g.reshape(2, DN_CONV, N_CHIPS, 3 * QW // N_CHIPS).transpose(2, 0, 1, 3)
    raise ValueError(name)


def _layer_forward(h, p):
    lp = h.shape[0]
    h1, *ffn1_saved = _ffn_fwd(h, p["ffn1_norm"], *p["ffn1"])
    q, k, v, dnx, z, u, ba = _inproj_fwd(h1, p["mix_norm"], p["w_in"].astype(MXU_DTYPE))
    qh, kh, vh = _to_heads(q), _to_heads_t(k), _to_heads_t(v)
    osb, sb_w, sb_sig = _sb_fwd(qh, kh, vh)
    dq_, dk_, dv_, bg = _dn_pre_fwd(dnx, ba, p["cw"], p["pv"])
    dqh, dkh, dvh = _to_heads(dq_), _to_heads(dk_), _to_heads(dv_)
    brow = bg[:, :N_HEADS].T.reshape(N_HEADS, lp // DN_CHUNK, 1, DN_CHUNK)
    grow = bg[:, N_HEADS:2 * N_HEADS].T.reshape(N_HEADS, lp // DN_CHUNK, 1, DN_CHUNK)
    odn, states = _dn_scan_fwd(dqh, dkh, dvh, grow, brow)
    ar, ai, bre, bim = _s5_prep_fwd(p["s5_a_re"], p["s5_a_im"], p["s5_log_dt"], p["s5_b_re"], p["s5_b_im"])
    s5t = (ar.reshape(1, S5_N), ai.reshape(1, S5_N), _slab_embed_b(bre), _slab_embed_b(bim),
           _slab_embed_c(p["s5_c_re"]), _slab_embed_c(p["s5_c_im"]), p["s5_d"])
    ys5, cin = _s5_scan_fwd(u, *s5t)
    osb_t, odn_t = _from_heads(osb), _from_heads(odn)
    h2 = _mixout_fwd(h1, osb_t, odn_t, z, ys5, p["sb_out_norm"], p["dn_out_norm"], p["s5_w_glu"], p["s5_b_glu"],
                     p["s5_out_norm"], p["w_out"].astype(MXU_DTYPE))
    h3, *ffn2_saved = _ffn_fwd(h2, p["ffn2_norm"], *p["ffn2"])
    saved = dict(h0=h, h1=h1, h2=h2, ffn1=ffn1_saved, ffn2=ffn2_saved, qh=qh, kh=kh, vh=vh, sb_w=sb_w, sb_sig=sb_sig, dnx=dnx, ba=ba, dqh=dqh, dkh=dkh, dvh=dvh,
                 grow=grow, brow=brow, states=states, odn_t=odn_t, osb_t=osb_t, z=z, u=u, ys5=ys5, cin=cin, s5t=s5t)
    return h3, saved


def _ffn_backward(h, g, w3, dy, fwd_saved):
    xn, gate, up = fwd_saved
    dh, dgn, d_gate, d_up, act = _ffn_bwd_dx(h, g, *w3, dy, gate, up)
    dwg, dwu, dwd = _ffn_bwd_dw(xn, dy, d_gate, d_up, act)
    return dh, dgn, dwg, dwu, dwd


def _layer_backward(dh3, p, s):
    lp = dh3.shape[0]
    g = {}
    dh2, g["ffn2_norm"], g["ffn2_w_gate"], g["ffn2_w_up"], g["ffn2_w_down"] = _ffn_backward(
        s["h2"], p["ffn2_norm"], p["ffn2"], dh3, s["ffn2"])
    (dosb_t, dodn_t, dz, dys5, g["sb_out_norm"], g["dn_out_norm"], g["s5_w_glu"], g["s5_b_glu"], g["s5_out_norm"],
     g["w_out"]) = _mixout_bwd(dh2, s["osb_t"], s["odn_t"], s["z"], s["ys5"], p["sb_out_norm"], p["dn_out_norm"],
                               p["s5_w_glu"], p["s5_b_glu"], p["s5_out_norm"], p["w_out"].astype(MXU_DTYPE))
    du, dab, db8r, db8i, dc8r, dc8i, g["s5_d"] = _s5_scan_bwd(s["u"], dys5, s["cin"], *s["s5t"])
    g["s5_c_re"], g["s5_c_im"] = _slab_extract_c(dc8r), _slab_extract_c(dc8i)
    g["s5_a_re"], g["s5_a_im"], g["s5_log_dt"], g["s5_b_re"], g["s5_b_im"] = _s5_prep_bwd(
        p["s5_a_re"], p["s5_a_im"], p["s5_log_dt"], p["s5_b_re"], p["s5_b_im"],
        dab[0].reshape(S5_G, S5_P), dab[1].reshape(S5_G, S5_P), _slab_extract_b(db8r), _slab_extract_b(db8i))
    ddq, ddk, ddv, dgrow, dbrow = _dn_scan_bwd(s["dqh"], s["dkh"], s["dvh"], s["grow"], s["brow"], s["states"],
                                               _to_heads(dodn_t))
    dbg = jnp.concatenate([dbrow.reshape(N_HEADS, lp).T, dgrow.reshape(N_HEADS, lp).T,
                           jnp.zeros((lp, 128 - 2 * N_HEADS), f32)], axis=1)
    dxs, dba, g["cw"], g["pv"] = _dn_pre_bwd(s["dnx"], s["ba"], p["cw"], p["pv"], _from_heads(ddq), _from_heads(ddk),
                                             _from_heads(ddv), dbg)
    dq, dk_t, dv_t = _sb_bwd(s["qh"], s["kh"], s["vh"], s["sb_w"], s["sb_sig"], _to_heads(dosb_t))
    dh1, g["mix_norm"], g["w_in"] = _inproj_bwd(s["h1"], p["mix_norm"], p["w_in"].astype(MXU_DTYPE), dh2,
                                                _from_heads(dq), _from_heads_t(dk_t), _from_heads_t(dv_t), dxs,
                                                dz, du, dba)
    dh0, g["ffn1_norm"], g["ffn1_w_gate"], g["ffn1_w_up"], g["ffn1_w_down"] = _ffn_backward(
        s["h0"], p["ffn1_norm"], p["ffn1"], dh1, s["ffn1"])
    return dh0, g


def kernel(x, meta_tokens, ffn1_norm, ffn1_w_gate, ffn1_w_up, ffn1_w_down, mix_norm, w_in, sb_out_norm, dn_conv_w, dn_a_log, dn_dt_bias, dn_out_norm, s5_a_re, s5_a_im, s5_log_dt, s5_b_re, s5_b_im, s5_c_re, s5_c_im, s5_d, s5_w_glu, s5_b_glu, s5_out_norm, w_out, ffn2_norm, ffn2_w_gate, ffn2_w_up, ffn2_w_down, final_norm, loss_target, m_meta_tokens, m_ffn1_norm, m_ffn1_w_gate, m_ffn1_w_up, m_ffn1_w_down, m_mix_norm, m_w_in, m_sb_out_norm, m_dn_conv_w, m_dn_a_log, m_dn_dt_bias, m_dn_out_norm, m_s5_a_re, m_s5_a_im, m_s5_log_dt, m_s5_b_re, m_s5_b_im, m_s5_c_re, m_s5_c_im, m_s5_d, m_s5_w_glu, m_s5_b_glu, m_s5_out_norm, m_w_out, m_ffn2_norm, m_ffn2_w_gate, m_ffn2_w_up, m_ffn2_w_down, m_final_norm, v_meta_tokens, v_ffn1_norm, v_ffn1_w_gate, v_ffn1_w_up, v_ffn1_w_down, v_mix_norm, v_w_in, v_sb_out_norm, v_dn_conv_w, v_dn_a_log, v_dn_dt_bias, v_dn_out_norm, v_s5_a_re, v_s5_a_im, v_s5_log_dt, v_s5_b_re, v_s5_b_im, v_s5_c_re, v_s5_c_im, v_s5_d, v_s5_w_glu, v_s5_b_glu, v_s5_out_norm, v_w_out, v_ffn2_norm, v_ffn2_w_gate, v_ffn2_w_up, v_ffn2_w_down, v_final_norm):
    args = dict(locals())
    w = {n: args[n] for n in _WEIGHTS}
    m = {n: args["m_" + n] for n in _WEIGHTS}
    v = {n: args["v_" + n] for n in _WEIGHTS}
    depth = ffn1_norm.shape[0]
    seq = x.shape[1]
    n_real = N_META + seq
    lp = _padded_len(n_real)

    gathered = _all_gather_chips("gather_weights", [w[n].astype(MXU_DTYPE) for n in _MATMUL_W]
                                 + [w["meta_tokens"].reshape(2, N_META // 2, -1), w["dn_conv_w"]])
    full = dict(zip(_MATMUL_W + ("meta_tokens", "dn_conv_w"), gathered))
    meta_full = full["meta_tokens"].reshape(N_CHIPS, N_META, -1).transpose(1, 0, 2).reshape(N_META, D_MODEL)
    conv_full = full["dn_conv_w"].transpose(1, 2, 0, 3).reshape(depth, DN_CONV, 3 * QW)
    w_in_full = full["w_in"].transpose(1, 2, 0, 3).reshape(depth, D_MODEL, IN_WIDTH)
    w_out_full = full["w_out"].transpose(1, 0, 2, 3).reshape(depth, D_MODEL, D_MODEL)
    w_glu_full = full["s5_w_glu"].transpose(1, 0, 2, 3).reshape(depth, S5_W, S5_W)

    layers = []
    for l in range(depth):
        pv = jnp.pad(jnp.stack([dn_a_log[l], dn_dt_bias[l]]), ((0, 6), (N_HEADS, 128 - 2 * N_HEADS)))
        layers.append(dict(
            ffn1_norm=ffn1_norm[l][None], mix_norm=mix_norm[l][None], ffn2_norm=ffn2_norm[l][None],
            ffn1=(full["ffn1_w_gate"][:, l], full["ffn1_w_up"][:, l], full["ffn1_w_down"][:, l]),
            ffn2=(full["ffn2_w_gate"][:, l], full["ffn2_w_up"][:, l], full["ffn2_w_down"][:, l]),
            w_in=_reorder_w_in(w_in_full[l]), w_out=w_out_full[l], s5_w_glu=w_glu_full[l].astype(f32),
            cw=jnp.pad(conv_full[l], ((0, 8 - DN_CONV), (0, 0))), pv=pv,
            sb_out_norm=sb_out_norm[l][None], dn_out_norm=dn_out_norm[l][None],
            s5_a_re=s5_a_re[l], s5_a_im=s5_a_im[l], s5_log_dt=s5_log_dt[l][:, None],
            s5_b_re=s5_b_re[l].transpose(0, 2, 1), s5_b_im=s5_b_im[l].transpose(0, 2, 1),
            s5_c_re=s5_c_re[l], s5_c_im=s5_c_im[l], s5_d=s5_d[l][None], s5_b_glu=s5_b_glu[l][None],
            s5_out_norm=s5_out_norm[l][None]))

    tail = jnp.zeros((lp - n_real, D_MODEL), f32)
    h = jnp.concatenate([meta_full, x[0], tail], axis=0)
    target = jnp.concatenate([jnp.zeros((N_META, D_MODEL), f32), loss_target[0], tail], axis=0)
    saved = []
    for p in layers:
        h, s = _layer_forward(h, p)
        saved.append(s)
    loss_blk, dh, d_final = _loss_fwd_bwd(h, final_norm[None], target, n_real)
    grads = [None] * depth
    for l in reversed(range(depth)):
        dh, grads[l] = _layer_backward(dh, layers[l], saved[l])
    loss = lax.psum(loss_blk[0, 0], ("x", "y", "c"))
    grad_x = dh[N_META:n_real][None]

    stack = lambda name: jnp.stack([grads[l][name] for l in range(depth)])
    gfull = {n: stack(n) for n in ("ffn1_w_gate", "ffn1_w_up", "ffn1_w_down", "s5_w_glu", "w_out", "ffn2_w_gate",
                                   "ffn2_w_up", "ffn2_w_down")}
    gfull["w_in"] = jnp.stack([_restore_w_in(grads[l]["w_in"]) for l in range(depth)])
    gfull["meta_tokens"] = dh[:N_META]
    gfull["dn_conv_w"] = jnp.stack([grads[l]["cw"][:DN_CONV] for l in range(depth)])
    grep = {n: stack(n).reshape(w[n].shape) for n in ("ffn1_norm", "mix_norm", "sb_out_norm", "dn_out_norm", "s5_a_re",
                                                      "s5_a_im", "s5_log_dt", "s5_c_re", "s5_c_im", "s5_d", "s5_b_glu",
                                                      "s5_out_norm", "ffn2_norm")}
    grep["s5_b_re"] = jnp.stack([grads[l]["s5_b_re"].transpose(0, 2, 1) for l in range(depth)])
    grep["s5_b_im"] = jnp.stack([grads[l]["s5_b_im"].transpose(0, 2, 1) for l in range(depth)])
    grep["dn_a_log"] = jnp.stack([grads[l]["pv"][0, N_HEADS:2 * N_HEADS] for l in range(depth)])
    grep["dn_dt_bias"] = jnp.stack([grads[l]["pv"][1, N_HEADS:2 * N_HEADS] for l in range(depth)])
    grep["final_norm"] = d_final[0]

    shard_shapes = [w[n].shape[:-1] + (W_IN_PACKED,) if n == "w_in" else w[n].shape for n in _SHARDED]
    g_big = _pack_rows([_chip_major(n, gfull[n]) for n in _SHARDED], (N_CHIPS,), BIG_ROWS)
    half_rows = g_big.shape[1] // 2
    g_big = g_big.reshape(N_CHIPS, 2, half_rows, FLAT_W)
    c = lax.axis_index("c")
    mine = lax.dynamic_index_in_dim(g_big, c, axis=1, keepdims=False)
    theirs = _swap_half_with_sibling("grad_pair_swap", g_big)
    pair = _pair_add(mine.reshape(-1, FLAT_W), theirs.reshape(-1, FLAT_W), "grad_pair_add")
    arrived = _scatter_to_chips("grad_scatter", pair.reshape(N_CHIPS, half_rows, FLAT_W))
    reduced = _sum_leading(arrived, "grad_chip_sum")
    g_shard = _share_with_sibling("grad_share", reduced).reshape(-1, FLAT_W)

    rep_shapes = [w[n].shape for n in _REPLICATED]
    g_small = _pack_rows([grep[n] for n in _REPLICATED], (), 64)
    g_rep = _sum_leading(_all_gather_devices("grad_small_gather", g_small), "grad_small_sum")

    out = {}
    for n, g_n in zip(_SHARDED, _unpack_rows(g_shard, shard_shapes)):
        g_n = g_n[..., :w[n].shape[-1]]
        out["grad_" + n] = g_n
        out["delta_" + n], out["new_m_" + n], out["new_v_" + n] = _adamw(w[n], g_n, m[n], v[n])
    pack = lambda d: _pack_rows([d[n] for n in _REPLICATED], (), 64)
    delta, m_new, v_new = _adamw(pack(w), g_rep, pack(m), pack(v))
    for kind, flat in (("grad", g_rep), ("delta", delta), ("new_m", m_new), ("new_v", v_new)):
        for n, a in zip(_REPLICATED, _unpack_rows(flat, rep_shapes)):
            out[kind + "_" + n] = a
    return (loss, grad_x, *[out[k + "_" + n] for k in ("grad", "delta", "new_m", "new_v") for n in _WEIGHTS])
```
